```python
import math
import jax, jax.numpy as jnp
from jax import lax
import numpy as np

D_MODEL = 1024
BATCH = 32
SEQ = 256
DEPTH = 2
DEC_BATCH = 2
DEC_SEQ = 4096
PAST_LEN = 512

GRID_W = 64
D_CONV = 512
CONV_K = 31
D_RWKV = 512
RWKV_HEAD = 64
N_RWKV_HEADS = D_RWKV // RWKV_HEAD
W_RANK = 64
A_RANK = 64
G_RANK = 128
DECAY_SCALE = math.exp(-0.5)
GN_EPS = 64e-5
N_MLA_HEADS = 8
Q_RANK = 256
KV_RANK = 128
NOPE_DIM = 64
ROPE_DIM = 32
V_DIM = 64
ROPE_BASE = 10000.0
ATTN_SCALE = 1.0 / math.sqrt(NOPE_DIM + ROPE_DIM)
Q_BLOCK = 128
N_BRANCH = 3
SPLIT_SIZES = (2 * D_CONV, D_RWKV, D_RWKV, D_RWKV, W_RANK, A_RANK, G_RANK,
               Q_RANK, KV_RANK, ROPE_DIM, N_BRANCH * D_MODEL)
IN_DIM = sum(SPLIT_SIZES)
D_FF = 2816
N_EXPERTS = 8
TOP_K = 2
E_FF = 2816
N_DENSE = (DEPTH + 1) // 2
N_MOE = DEPTH // 2
EPS = 1e-6

kernel_name = "hybrid_dit_conv_rwkv7_mla_step"


def rms_norm(x, g):
    xf = x.astype(jnp.float32)
    y = xf * lax.rsqrt(jnp.mean(xf * xf, axis=-1, keepdims=True) + EPS)
    return (y * g.astype(jnp.float32)).astype(x.dtype)


def layer_norm(x, g, b):
    xf = x.astype(jnp.float32)
    mu = jnp.mean(xf, axis=-1, keepdims=True)
    var = jnp.mean(jnp.square(xf - mu), axis=-1, keepdims=True)
    return ((xf - mu) * lax.rsqrt(var + EPS) * g + b).astype(x.dtype)


def axial_rope(n_tok):
    rows = n_tok // GRID_W
    row = jnp.repeat(jnp.arange(rows, dtype=jnp.float32), GRID_W)
    col = jnp.tile(jnp.arange(GRID_W, dtype=jnp.float32), rows)
    n_freq = ROPE_DIM // 4
    inv = ROPE_BASE ** (-jnp.arange(n_freq, dtype=jnp.float32) / n_freq)
    ang = jnp.concatenate([row[:, None] * inv, col[:, None] * inv], axis=-1)
    return jnp.cos(ang), jnp.sin(ang)


def apply_rope(x, cos, sin):
    x1, x2 = x[..., 0::2], x[..., 1::2]
    cos = cos.astype(x.dtype)
    sin = sin.astype(x.dtype)
    return jnp.stack([x1 * cos - x2 * sin, x1 * sin + x2 * cos], axis=-1).reshape(x.shape)


def block_attention(q, k, v):
    b, tq, h, dk = q.shape
    dv = v.shape[-1]
    n_blk = tq // Q_BLOCK
    qb = jnp.moveaxis(q.reshape(b, n_blk, Q_BLOCK, h, dk), 1, 0)

    def one_block(qi):
        s = jnp.einsum("bqhd,bkhd->bhqk", qi, k).astype(jnp.float32) * ATTN_SCALE
        p = jax.nn.softmax(s, axis=-1).astype(v.dtype)
        return jnp.einsum("bhqk,bkhd->bqhd", p, v)

    o = lax.map(one_block, qb)
    return jnp.moveaxis(o, 0, 1).reshape(b, tq, h, dv)


def centred_shift(x):
    prev = jnp.pad(x[:, :-1], ((0, 0), (1, 0), (0, 0)))
    nxt = jnp.pad(x[:, 1:], ((0, 0), (0, 1), (0, 0)))
    return 0.5 * (prev + nxt)


def to_heads(t):
    return t.reshape(t.shape[:-1] + (N_RWKV_HEADS, RWKV_HEAD))


def both_dirs(t):
    return jnp.stack([t, jnp.flip(t, 1)], axis=2)


def flip_bwd(t):
    return jnp.stack([t[:, :, 0], jnp.flip(t[:, :, 1], 1)], axis=2)


def wkv_scan(s0, r, w, kk, kka, v, kt):
    def step(S, inp):
        r_t, w_t, kk_t, kka_t, v_t, kt_t = inp
        sa = jnp.einsum("bdhvk,bdhk->bdhv", S, kk_t)
        S = (S * w_t[..., None, :] - sa[..., :, None] * kka_t[..., None, :]
             + v_t[..., :, None] * kt_t[..., None, :])
        return S, jnp.einsum("bdhvk,bdhk->bdhv", S, r_t)

    xs = tuple(jnp.moveaxis(t.astype(jnp.float32), 1, 0) for t in (r, w, kk, kka, v, kt))
    s_fin, o = lax.scan(step, s0.astype(jnp.float32), xs)
    return s_fin, jnp.moveaxis(o, 0, 1)


def conv_branch(u, P, l):
    a, b = jnp.split(u, 2, axis=-1)
    h = a * jax.nn.sigmoid(b)
    h = lax.conv_general_dilated(h, P["conv_w"][l][:, None, :].astype(h.dtype), (1,),
                                 [(CONV_K // 2, CONV_K // 2)],
                                 dimension_numbers=("NWC", "WIO", "NWC"),
                                 feature_group_count=D_CONV) + P["conv_b"][l]
    h = layer_norm(h, P["conv_ln_g"][l], P["conv_ln_b"][l])
    return jax.nn.silu(h) @ P["conv_wo"][l]


def rwkv_branch(r, k, v, dw, da, dg, s0, P, l):
    dt = r.dtype
    bsz, t_len, _ = r.shape
    mu = P["rwkv_mu"][l]
    r = r + mu[0] * (centred_shift(r) - r)
    k = k + mu[1] * (centred_shift(k) - k)
    v = v + mu[2] * (centred_shift(v) - v)
    w = jnp.exp(-DECAY_SCALE * jax.nn.sigmoid(
        P["rwkv_w0"][l] + jnp.einsum("btr,drc->btdc", jnp.tanh(dw), P["rwkv_bw"][l])))
    a = jax.nn.sigmoid(P["rwkv_a0"][l] + jnp.einsum("btr,drc->btdc", da, P["rwkv_ba"][l]))
    g = jax.nn.sigmoid(dg) @ P["rwkv_bg"][l]
    kt = k[:, :, None] * (1.0 + (a - 1.0) * P["rwkv_alpha"][l])
    kk = to_heads(k * P["rwkv_xi"][l]).astype(jnp.float32)
    kk = kk * lax.rsqrt(jnp.sum(kk * kk, axis=-1, keepdims=True) + EPS)
    r_h, v_h = to_heads(r), to_heads(v)
    w_h, a_h, kt_h = to_heads(w), to_heads(a), to_heads(kt)
    kk2 = both_dirs(kk)
    s_fin, o = wkv_scan(s0, both_dirs(r_h), flip_bwd(w_h), kk2, kk2 * flip_bwd(a_h),
                        both_dirs(v_h), flip_bwd(kt_h))
    o = jnp.sum(flip_bwd(o), axis=2)
    mean = jnp.mean(o, axis=-1, keepdims=True)
    var = jnp.mean(jnp.square(o - mean), axis=-1, keepdims=True)
    gn = ((o - mean) * lax.rsqrt(var + GN_EPS)).reshape(bsz, t_len, D_RWKV)
    gn = gn * P["rwkv_gn_g"][l] + P["rwkv_gn_b"][l]
    bonus = jnp.sum(jnp.sum(r_h[:, :, None] * kt_h * to_heads(P["rwkv_rho"][l]), axis=-1,
                            keepdims=True) * v_h[:, :, None], axis=2)
    y = (gn + bonus.reshape(bsz, t_len, D_RWKV).astype(jnp.float32)).astype(dt) * g
    return y @ P["rwkv_wo"][l], s_fin.astype(dt)


def mla_branch(cq, ckv, kr, P, l, rope, ctx_ckv, ctx_kr):
    bsz, t_len, _ = cq.shape
    cq = rms_norm(cq, P["mla_q_norm"][l])
    q = (cq @ P["mla_wuq"][l]).reshape(bsz, t_len, N_MLA_HEADS, NOPE_DIM + ROPE_DIM)
    q_nope, q_rope = q[..., :NOPE_DIM], q[..., NOPE_DIM:]
    ckv = rms_norm(ckv, P["mla_kv_norm"][l])
    if rope is None:
        ckv_all, kr_all = ckv, kr
    else:
        cos, sin = rope
        q_rope = apply_rope(q_rope, cos[:, None], sin[:, None])
        kr_lat = apply_rope(kr, cos, sin)
        ckv_all = jnp.concatenate([ckv, ctx_ckv.astype(ckv.dtype)], axis=1)
        kr_all = jnp.concatenate([kr_lat, ctx_kr.astype(kr.dtype)], axis=1)
    t_k = ckv_all.shape[1]
    k_nope = (ckv_all @ P["mla_wuk"][l]).reshape(bsz, t_k, N_MLA_HEADS, NOPE_DIM)
    v = (ckv_all @ P["mla_wuv"][l]).reshape(bsz, t_k, N_MLA_HEADS, V_DIM)
    k = jnp.concatenate([k_nope, jnp.broadcast_to(kr_all[:, :, None], (bsz, t_k, N_MLA_HEADS, ROPE_DIM))],
                        axis=-1)
    o = block_attention(jnp.concatenate([q_nope, q_rope], axis=-1), k, v)
    return o.reshape(bsz, t_len, N_MLA_HEADS * V_DIM) @ P["mla_wo"][l], ckv, kr


def mixer(h, P, l, rope, ctx):
    bsz, t_len, _ = h.shape
    u = h @ P["w_in"][l] + P["b_in"][l]
    offs = [int(o) for o in np.cumsum(SPLIT_SIZES)[:-1]]
    u_conv, r, k, v, dw, da, dg, cq, ckv, kr, u_gate = jnp.split(u, offs, axis=-1)
    y_conv = conv_branch(u_conv, P, l)
    if ctx is None:
        s0 = jnp.zeros((bsz, 2, N_RWKV_HEADS, RWKV_HEAD, RWKV_HEAD), jnp.float32)
        ctx_ckv, ctx_kr = None, None
    else:
        ctx_ckv, ctx_kr, s0 = ctx
    y_rwkv, s_fin = rwkv_branch(r, k, v, dw, da, dg, s0, P, l)
    y_mla, ckv_n, kr_c = mla_branch(cq, ckv, kr, P, l, rope, ctx_ckv, ctx_kr)
    g = jax.nn.sigmoid(u_gate).reshape(bsz, t_len, N_BRANCH, D_MODEL)
    merged = g[:, :, 0] * y_conv + g[:, :, 1] * y_rwkv + g[:, :, 2] * y_mla
    return merged @ P["w_out"][l], (ckv_n, kr_c, s_fin)


def swiglu(h, w1, w3, w2):
    return (jax.nn.silu(h @ w1) * (h @ w3)) @ w2


def moe_ffn(h, router, w1, w3, w2):
    logits = (h @ router).astype(jnp.float32)
    top_v, top_i = lax.top_k(logits, TOP_K)
    gates = jax.nn.softmax(top_v, axis=-1)
    combine = jnp.sum(jax.nn.one_hot(top_i, N_EXPERTS, dtype=jnp.float32) * gates[..., None], axis=-2)
    out = jnp.zeros_like(h)
    for e in range(N_EXPERTS):
        out = out + combine[..., e:e + 1].astype(h.dtype) * swiglu(h, w1[e], w3[e], w2[e])
    return out


def trunk_layer(x, mod, P, l, rope, ctx):
    sh1, sc1, g1, sh2, sc2, g2 = [t[:, None] for t in jnp.split(mod, 6, axis=-1)]
    h = rms_norm(x, P["norm_mix_pre"][l]) * (1.0 + sc1) + sh1
    y, ctx_out = mixer(h, P, l, rope, ctx)
    x = x + g1 * rms_norm(y, P["norm_mix_post"][l])
    h = rms_norm(x, P["norm_ffn_pre"][l]) * (1.0 + sc2) + sh2
    if l % 2 == 0:
        i = l // 2
        y = swiglu(h, P["ffn_w1"][i], P["ffn_w3"][i], P["ffn_w2"][i])
    else:
        i = l // 2
        y = moe_ffn(h, P["moe_router"][i], P["moe_w1"][i], P["moe_w3"][i], P["moe_w2"][i])
    x = x + g2 * rms_norm(y, P["norm_ffn_post"][l])
    return x, ctx_out


def setup_inputs(seed: int = 0) -> dict:
    key = jax.random.key(seed)
    ks = iter(jax.random.split(key, 64))
    L, D = DEPTH, D_MODEL

    def nrm(shape, scale=1.0):
        return jax.random.normal(next(ks), shape, jnp.float32) * scale

    def gain(shape):
        return 1.0 + 0.05 * nrm(shape)

    return {
        "x_prompt": nrm((BATCH, SEQ, D)),
        "x_sample": nrm((DEC_BATCH, DEC_SEQ, D)),
        "cache_ckv": nrm((DEC_BATCH, L, PAST_LEN, KV_RANK)),
        "cache_krope": nrm((DEC_BATCH, L, PAST_LEN, ROPE_DIM)),
        "state_wkv": nrm((DEC_BATCH, L, 2, N_RWKV_HEADS, RWKV_HEAD, RWKV_HEAD), 0.3),
        "c": nrm((DEC_BATCH, D)),
        "c_ctx": nrm((D,)),
        "ada_w": nrm((L, D, 6 * D), 0.5 * D ** -0.5),
        "ada_b": nrm((L, 6 * D), 0.01),
        "norm_mix_pre": gain((L, D)),
        "norm_mix_post": gain((L, D)),
        "norm_ffn_pre": gain((L, D)),
        "norm_ffn_post": gain((L, D)),
        "w_in": nrm((L, D, IN_DIM), D ** -0.5),
        "b_in": nrm((L, IN_DIM), 0.01),
        "conv_w": nrm((L, CONV_K, D_CONV), CONV_K ** -0.5),
        "conv_b": nrm((L, D_CONV), 0.01),
        "conv_ln_g": gain((L, D_CONV)),
        "conv_ln_b": nrm((L, D_CONV), 0.01),
        "conv_wo": nrm((L, D_CONV, D), D_CONV ** -0.5),
        "rwkv_mu": jax.random.uniform(next(ks), (L, 3, D_RWKV), jnp.float32),
        "rwkv_w0": nrm((L, 2, D_RWKV)),
        "rwkv_bw": nrm((L, 2, W_RANK, D_RWKV), 0.5 * W_RANK ** -0.5),
        "rwkv_a0": nrm((L, 2, D_RWKV), 0.5),
        "rwkv_ba": nrm((L, 2, A_RANK, D_RWKV), 0.5 * A_RANK ** -0.5),
        "rwkv_bg": nrm((L, G_RANK, D_RWKV), G_RANK ** -0.5),
        "rwkv_xi": 0.85 + 0.05 * nrm((L, D_RWKV)),
        "rwkv_alpha": 1.0 + 0.1 * nrm((L, D_RWKV)),
        "rwkv_rho": nrm((L, D_RWKV), 0.1),
        "rwkv_gn_g": gain((L, D_RWKV)),
        "rwkv_gn_b": nrm((L, D_RWKV), 0.01),
        "rwkv_wo": nrm((L, D_RWKV, D), D_RWKV ** -0.5),
        "mla_q_norm": gain((L, Q_RANK)),
        "mla_wuq": nrm((L, Q_RANK, N_MLA_HEADS * (NOPE_DIM + ROPE_DIM)), Q_RANK ** -0.5),
        "mla_kv_norm": gain((L, KV_RANK)),
        "mla_wuk": nrm((L, KV_RANK, N_MLA_HEADS * NOPE_DIM), KV_RANK ** -0.5),
        "mla_wuv": nrm((L, KV_RANK, N_MLA_HEADS * V_DIM), KV_RANK ** -0.5),
        "mla_wo": nrm((L, N_MLA_HEADS * V_DIM, D), (N_MLA_HEADS * V_DIM) ** -0.5),
        "w_out": nrm((L, D, D), D ** -0.5),
        "ffn_w1": nrm((N_DENSE, D, D_FF), D ** -0.5),
        "ffn_w3": nrm((N_DENSE, D, D_FF), D ** -0.5),
        "ffn_w2": nrm((N_DENSE, D_FF, D), D_FF ** -0.5),
        "moe_router": nrm((N_MOE, D, N_EXPERTS), D ** -0.5),
        "moe_w1": nrm((N_MOE, N_EXPERTS, D, E_FF), D ** -0.5),
        "moe_w3": nrm((N_MOE, N_EXPERTS, D, E_FF), D ** -0.5),
        "moe_w2": nrm((N_MOE, N_EXPERTS, E_FF, D), E_FF ** -0.5),
    }


def reference(x_prompt, x_sample, cache_ckv, cache_krope, state_wkv, c, c_ctx,
              ada_w, ada_b, norm_mix_pre, norm_mix_post, norm_ffn_pre, norm_ffn_post,
              w_in, b_in, conv_w, conv_b, conv_ln_g, conv_ln_b, conv_wo,
              rwkv_mu, rwkv_w0, rwkv_bw, rwkv_a0, rwkv_ba, rwkv_bg, rwkv_xi, rwkv_alpha,
              rwkv_rho, rwkv_gn_g, rwkv_gn_b, rwkv_wo,
              mla_q_norm, mla_wuq, mla_kv_norm, mla_wuk, mla_wuv, mla_wo, w_out,
              ffn_w1, ffn_w3, ffn_w2, moe_router, moe_w1, moe_w3, moe_w2):
    P = dict(norm_mix_pre=norm_mix_pre, norm_mix_post=norm_mix_post,
             norm_ffn_pre=norm_ffn_pre, norm_ffn_post=norm_ffn_post,
             w_in=w_in, b_in=b_in, conv_w=conv_w, conv_b=conv_b, conv_ln_g=conv_ln_g,
             conv_ln_b=conv_ln_b, conv_wo=conv_wo, rwkv_mu=rwkv_mu, rwkv_w0=rwkv_w0,
             rwkv_bw=rwkv_bw, rwkv_a0=rwkv_a0, rwkv_ba=rwkv_ba, rwkv_bg=rwkv_bg,
             rwkv_xi=rwkv_xi, rwkv_alpha=rwkv_alpha, rwkv_rho=rwkv_rho,
             rwkv_gn_g=rwkv_gn_g, rwkv_gn_b=rwkv_gn_b, rwkv_wo=rwkv_wo,
             mla_q_norm=mla_q_norm, mla_wuq=mla_wuq, mla_kv_norm=mla_kv_norm,
             mla_wuk=mla_wuk, mla_wuv=mla_wuv, mla_wo=mla_wo, w_out=w_out,
             ffn_w1=ffn_w1, ffn_w3=ffn_w3, ffn_w2=ffn_w2, moe_router=moe_router,
             moe_w1=moe_w1, moe_w3=moe_w3, moe_w2=moe_w2)

    x = x_prompt
    silu_ctx = jax.nn.silu(c_ctx)
    ckv_list, kr_list, st_list = [], [], []
    for l in range(DEPTH):
        mod = (silu_ctx @ ada_w[l] + ada_b[l])[None]
        x, (ckv_l, kr_l, st_l) = trunk_layer(x, mod, P, l, None, None)
        ckv_list.append(ckv_l)
        kr_list.append(kr_l)
        st_list.append(st_l)
    y_prompt = x
    new_ckv = jnp.stack(ckv_list, axis=1)
    new_krope = jnp.stack(kr_list, axis=1)
    new_wkv = jnp.stack(st_list, axis=1)

    rope = axial_rope(x_sample.shape[1])
    x = x_sample
    silu_c = jax.nn.silu(c)
    for l in range(DEPTH):
        mod = silu_c @ ada_w[l] + ada_b[l]
        ctx = (cache_ckv[:, l], cache_krope[:, l], state_wkv[:, l])
        x, _ = trunk_layer(x, mod, P, l, rope, ctx)
    y_sample = x

    return (y_prompt, y_sample, new_ckv, new_krope, new_wkv)
```

```python
import functools
import math

import numpy as np
import jax
import jax.numpy as jnp
from jax import lax
from jax.experimental import pallas as pl
from jax.experimental.pallas import tpu as pltpu

F32 = jnp.float32
BF16 = jnp.bfloat16

D_MODEL = 1024
GRID_W = 64
D_CONV = 512
CONV_K = 31
D_RWKV = 512
RWKV_HEAD = 64
N_RWKV_HEADS = D_RWKV // RWKV_HEAD
W_RANK = 64
A_RANK = 64
G_RANK = 128
DECAY_SCALE = math.exp(-0.5)
GN_EPS = 64e-5
N_MLA_HEADS = 8
Q_RANK = 256
KV_RANK = 128
NOPE_DIM = 64
ROPE_DIM = 32
V_DIM = 64
ROPE_BASE = 10000.0
ATTN_SCALE = 1.0 / math.sqrt(NOPE_DIM + ROPE_DIM)
N_BRANCH = 3
D_FF = 2816
N_EXPERTS = 8
EPS = 1e-6

LANES = 128
SUBLANES = 8
VMEM_LIMIT_BYTES = 56 * 1024 * 1024

TILE = 256
CONV_HALO = 16
SHIFT_HALO = 8

C_CONV, C_G0, C_G1, C_G2 = 0, 1024, 2048, 3072
C_R, C_K, C_V = 4096, 4608, 5120
C_LORA = 5632
C_CQ = 5888
C_CKV = 6144
C_KR = 6272
IN_PAD = 6400


def _cparams(sem):
    return pltpu.CompilerParams(dimension_semantics=sem, vmem_limit_bytes=VMEM_LIMIT_BYTES)


def _sigmoid(x):
    return jax.nn.sigmoid(x)


def _rms(x, g):
    return (x * lax.rsqrt(jnp.mean(x * x, axis=-1, keepdims=True) + EPS)) * g


class Geo:
    def __init__(self, n_ctx, t_ctx, n_lat, t_lat, past):
        assert t_ctx % TILE == 0 and t_lat % TILE == 0
        self.n_ctx, self.t_ctx, self.n_lat, self.t_lat, self.past = n_ctx, t_ctx, n_lat, t_lat, past
        self.ct = t_ctx // TILE
        self.lt = t_lat // TILE
        self.nct = n_ctx * self.ct
        self.nlt = n_lat * self.lt
        self.ntile = self.nct + self.nlt
        self.nc_tok = n_ctx * t_ctx
        self.nl_tok = n_lat * t_lat
        self.ntok = self.nc_tok + self.nl_tok

    def pos(self, i):
        is_ctx = i < self.nct
        p = jnp.where(is_ctx, i % self.ct, (i - self.nct) % self.lt)
        n = jnp.where(is_ctx, self.ct, self.lt)
        return p, n

    def mod_row(self, i):
        return jnp.where(i < self.nct, 0, 1 + (i - self.nct) // self.lt)

    def rope_blk(self, i):
        return jnp.where(i < self.nct, 0, 1 + (i - self.nct) % self.lt)


def _mm_kernel(x_ref, w_ref, b_ref, o_ref, *, pre):
    x = x_ref[...]
    if pre == "silu":
        x = x.astype(F32)
        x = x * _sigmoid(x)
    acc = jnp.dot(x.astype(BF16), w_ref[...], preferred_element_type=F32)
    o_ref[...] = (acc + b_ref[...]).astype(o_ref.dtype)


def matmul_bias(x, w, b, *, tm, tn, pre=None, out_dtype=F32, name="matmul"):
    m, k = x.shape
    n = w.shape[1]
    assert m % tm == 0 and n % tn == 0
    return pl.pallas_call(
        functools.partial(_mm_kernel, pre=pre),
        grid=(n // tn, m // tm),
        in_specs=[
            pl.BlockSpec((tm, k), lambda j, i: (i, 0)),
            pl.BlockSpec((k, tn), lambda j, i: (0, j)),
            pl.BlockSpec((1, tn), lambda j, i: (0, j)),
        ],
        out_specs=pl.BlockSpec((tm, tn), lambda j, i: (i, j)),
        out_shape=jax.ShapeDtypeStruct((m, n), out_dtype),
        compiler_params=_cparams(("arbitrary", "arbitrary")),
        name=name,
    )(x, w, b.reshape(1, n).astype(F32))


def _prenorm_kernel(x_ref, g_ref, sc_ref, sh_ref, o_ref):
    h = _rms(x_ref[...], g_ref[...]) * (1.0 + sc_ref[0]) + sh_ref[0]
    o_ref[...] = h.astype(o_ref.dtype)


def prenorm(x, g, mod, geo, j_shift, j_scale):
    n, d = x.shape
    return pl.pallas_call(
        _prenorm_kernel,
        grid=(geo.ntile,),
        in_specs=[
            pl.BlockSpec((TILE, d), lambda i: (i, 0)),
            pl.BlockSpec((1, d), lambda i: (0, 0)),
            pl.BlockSpec((1, 1, d), lambda i: (geo.mod_row(i) * 6 + j_scale, 0, 0)),
            pl.BlockSpec((1, 1, d), lambda i: (geo.mod_row(i) * 6 + j_shift, 0, 0)),
        ],
        out_specs=pl.BlockSpec((TILE, d), lambda i: (i, 0)),
        out_shape=jax.ShapeDtypeStruct((n, d), BF16),
        compiler_params=_cparams(("arbitrary",)),
        name="prenorm",
    )(x, g.reshape(1, d), mod, mod)


def _conv_kernel(cur_ref, prev_ref, next_ref, gate_ref, cw_ref, cb_ref, lng_ref, lnb_ref,
                 wo_ref, o_ref, hp_ref, *, geo):
    i = pl.program_id(0)
    p, n = geo.pos(i)
    has_prev = p > 0
    has_next = p < n - 1

    def glu(z):
        return z[:, :D_CONV] * _sigmoid(z[:, D_CONV:])

    hp_ref[0:CONV_HALO, :] = jnp.where(has_prev, glu(prev_ref[...]), 0.0)
    hp_ref[CONV_HALO:CONV_HALO + TILE, :] = glu(cur_ref[...])
    hp_ref[CONV_HALO + TILE:, :] = jnp.where(has_next, glu(next_ref[...]), 0.0)

    off = CONV_HALO - CONV_K // 2
    acc = hp_ref[pl.ds(off, TILE), :] * cw_ref[0:1, :]
    for j in range(1, CONV_K):
        acc = acc + hp_ref[pl.ds(off + j, TILE), :] * cw_ref[j:j + 1, :]
    h = acc + cb_ref[...]
    mu = jnp.mean(h, axis=-1, keepdims=True)
    hc = h - mu
    var = jnp.mean(hc * hc, axis=-1, keepdims=True)
    h = hc * lax.rsqrt(var + EPS) * lng_ref[...] + lnb_ref[...]
    h = h * _sigmoid(h)
    y = jnp.dot(h.astype(BF16), wo_ref[...], preferred_element_type=F32)
    o_ref[...] = _sigmoid(gate_ref[...]) * y


def conv_branch(u, cw, cb, lng, lnb, wo, geo):
    n = u.shape[0]
    hb = TILE // CONV_HALO
    nhalo = n // CONV_HALO
    cwp = jnp.zeros((32, D_CONV), F32).at[:CONV_K].set(cw)
    return pl.pallas_call(
        functools.partial(_conv_kernel, geo=geo),
        grid=(geo.ntile,),
        in_specs=[
            pl.BlockSpec((TILE, 2 * D_CONV), lambda i: (i, C_CONV // (2 * D_CONV))),
            pl.BlockSpec((CONV_HALO, 2 * D_CONV), lambda i: (jnp.maximum(i * hb - 1, 0), 0)),
            pl.BlockSpec((CONV_HALO, 2 * D_CONV), lambda i: (jnp.minimum((i + 1) * hb, nhalo - 1), 0)),
            pl.BlockSpec((TILE, D_MODEL), lambda i: (i, C_G0 // D_MODEL)),
            pl.BlockSpec((32, D_CONV), lambda i: (0, 0)),
            pl.BlockSpec((1, D_CONV), lambda i: (0, 0)),
            pl.BlockSpec((1, D_CONV), lambda i: (0, 0)),
            pl.BlockSpec((1, D_CONV), lambda i: (0, 0)),
            pl.BlockSpec((D_CONV, D_MODEL), lambda i: (0, 0)),
        ],
        out_specs=pl.BlockSpec((TILE, D_MODEL), lambda i: (i, 0)),
        out_shape=jax.ShapeDtypeStruct((n, D_MODEL), F32),
        scratch_shapes=[pltpu.VMEM((TILE + 2 * CONV_HALO, D_CONV), F32)],
        compiler_params=_cparams(("arbitrary",)),
        name="conv_branch",
    )(u, u, u, u, cwp, cb.reshape(1, -1), lng.reshape(1, -1), lnb.reshape(1, -1), wo.astype(BF16))


def _seg_sum(x, ones_bd):
    return jnp.dot(x, ones_bd, preferred_element_type=F32, precision=lax.Precision.HIGHEST)


def _rwkv_pre_kernel(r_ref, rp_ref, rn_ref, k_ref, kp_ref, kn_ref, v_ref, vp_ref, vn_ref,
                     lora_ref, mu_ref, w0_ref, bw_ref, a0_ref, ba_ref, bg_ref, xi_ref, al_ref,
                     rho_ref, ones_ref,
                     r_o, kk_o, v_o, w0_o, w1_o, ka0_o, ka1_o, kt0_o, kt1_o, g_o, bonus_o, *, geo):
    i = pl.program_id(0)
    p, n = geo.pos(i)
    has_prev = p > 0
    has_next = p < n - 1
    row = lax.broadcasted_iota(jnp.int32, (TILE, D_RWKV), 0)

    def shifted(c_ref, p_ref, n_ref, mu):
        cur = c_ref[...]
        pv = jnp.where(has_prev, p_ref[SHIFT_HALO - 1:SHIFT_HALO, :], 0.0)
        nx = jnp.where(has_next, n_ref[0:1, :], 0.0)
        prev = jnp.where(row == 0, pv, pltpu.roll(cur, 1, axis=0))
        nxt = jnp.where(row == TILE - 1, nx, pltpu.roll(cur, TILE - 1, axis=0))
        return cur + mu * (0.5 * (prev + nxt) - cur)

    r = shifted(r_ref, rp_ref, rn_ref, mu_ref[0:1, :])
    k = shifted(k_ref, kp_ref, kn_ref, mu_ref[1:2, :])
    v = shifted(v_ref, vp_ref, vn_ref, mu_ref[2:3, :])

    lora = lora_ref[...]
    dw = jnp.tanh(lora[:, :W_RANK]).astype(BF16)
    da = lora[:, W_RANK:W_RANK + A_RANK].astype(BF16)
    dg = _sigmoid(lora[:, W_RANK + A_RANK:]).astype(BF16)
    wl = jnp.dot(dw, bw_ref[...], preferred_element_type=F32)
    al = jnp.dot(da, ba_ref[...], preferred_element_type=F32)
    g = jnp.dot(dg, bg_ref[...], preferred_element_type=F32)

    ones_bd = ones_ref[...]
    kx = k * xi_ref[...]
    kk = kx * lax.rsqrt(_seg_sum(kx * kx, ones_bd) + EPS)
    alpha = al_ref[...]
    rho = rho_ref[...]
    bonus = jnp.zeros((TILE, D_RWKV), F32)
    for d, (w_o, ka_o, kt_o) in enumerate(((w0_o, ka0_o, kt0_o), (w1_o, ka1_o, kt1_o))):
        sl = slice(d * D_RWKV, (d + 1) * D_RWKV)
        w = jnp.exp(-DECAY_SCALE * _sigmoid(w0_ref[d:d + 1, :] + wl[:, sl]))
        a = _sigmoid(a0_ref[d:d + 1, :] + al[:, sl])
        kt = k * (1.0 + (a - 1.0) * alpha)
        w_o[...] = w
        ka_o[...] = kk * a
        kt_o[...] = kt
        bonus = bonus + _seg_sum(r * kt * rho, ones_bd) * v
    r_o[...] = r
    kk_o[...] = kk
    v_o[...] = v
    g_o[...] = g
    bonus_o[...] = bonus


def rwkv_pre(u, P, l, geo, ones_bd):
    n = u.shape[0]
    hb = TILE // SHIFT_HALO
    nhalo = n // SHIFT_HALO

    def trio(c0):
        cb = c0 // D_RWKV
        return [
            pl.BlockSpec((TILE, D_RWKV), lambda i: (i, cb)),
            pl.BlockSpec((SHIFT_HALO, D_RWKV), lambda i: (jnp.maximum(i * hb - 1, 0), cb)),
            pl.BlockSpec((SHIFT_HALO, D_RWKV), lambda i: (jnp.minimum((i + 1) * hb, nhalo - 1), cb)),
        ]

    def full(shape):
        return pl.BlockSpec(shape, lambda i: tuple(0 for _ in shape))

    bw = jnp.concatenate([P["rwkv_bw"][l, 0], P["rwkv_bw"][l, 1]], axis=1).astype(BF16)
    ba = jnp.concatenate([P["rwkv_ba"][l, 0], P["rwkv_ba"][l, 1]], axis=1).astype(BF16)
    out = jax.ShapeDtypeStruct((n, D_RWKV), F32)
    return pl.pallas_call(
        functools.partial(_rwkv_pre_kernel, geo=geo),
        grid=(geo.ntile,),
        in_specs=trio(C_R) + trio(C_K) + trio(C_V) + [
            pl.BlockSpec((TILE, 256), lambda i: (i, C_LORA // 256)),
            full((3, D_RWKV)), full((2, D_RWKV)), full((W_RANK, 2 * D_RWKV)),
            full((2, D_RWKV)), full((A_RANK, 2 * D_RWKV)), full((G_RANK, D_RWKV)),
            full((1, D_RWKV)), full((1, D_RWKV)), full((1, D_RWKV)), full((D_RWKV, D_RWKV)),
        ],
        out_specs=[pl.BlockSpec((TILE, D_RWKV), lambda i: (i, 0))] * 11,
        out_shape=[out] * 11,
        compiler_params=_cparams(("arbitrary",)),
        name="rwkv_pre",
    )(u, u, u, u, u, u, u, u, u, u,
      P["rwkv_mu"][l], P["rwkv_w0"][l], bw, P["rwkv_a0"][l], ba, P["rwkv_bg"][l].astype(BF16),
      P["rwkv_xi"][l].reshape(1, -1), P["rwkv_alpha"][l].reshape(1, -1),
      P["rwkv_rho"][l].reshape(1, -1), ones_bd)


def _scan_kernel(rev_ref, x_ref, v_ref, s0_ref, o_ref, sfin_ref, s_ref, *, tc, nk, kl, vg):
    gb = pl.program_id(0)
    c = pl.program_id(1)
    rev = rev_ref[gb]

    @pl.when(c == 0)
    def _():
        s_ref[...] = s0_ref[0]

    def allred(a):
        sh = LANES // 2
        while sh >= LANES // kl:
            a = a + pltpu.roll(a, sh, axis=1)
            sh //= 2
        return a

    def row(t, a, kh):
        return jnp.broadcast_to(x_ref[0, t, a, pl.ds(kh, 1), :], (SUBLANES, LANES))

    def step(s, carry):
        t = jnp.where(rev == 1, tc - 1 - s, s)
        for v0 in range(0, RWKV_HEAD // SUBLANES, vg):
            acc = [None] * vg
            for kh in range(nk):
                kkb = row(t, 1, kh)
                for j in range(vg):
                    pr = s_ref[v0 + j, kh] * kkb
                    acc[j] = pr if acc[j] is None else acc[j] + pr
            sa = [allred(a) for a in acc]
            vv = [v_ref[0, t, pl.ds((v0 + j) * SUBLANES, SUBLANES), :] for j in range(vg)]
            oacc = [None] * vg
            for kh in range(nk):
                wb, kab, ktb, rb = row(t, 0, kh), row(t, 2, kh), row(t, 3, kh), row(t, 4, kh)
                for j in range(vg):
                    sn = s_ref[v0 + j, kh] * wb - sa[j] * kab + vv[j] * ktb
                    s_ref[v0 + j, kh] = sn
                    pr = sn * rb
                    oacc[j] = pr if oacc[j] is None else oacc[j] + pr
            for j in range(vg):
                o_ref[0, t, pl.ds((v0 + j) * SUBLANES, SUBLANES), :] = allred(oacc[j])
        return carry

    lax.fori_loop(0, tc, step, 0)

    @pl.when(c == pl.num_programs(1) - 1)
    def _():
        sfin_ref[0] = s_ref[...]


def wkv_scan(x, v, s0, rev, *, tc, kl, vg):
    ngb, t_len, _, nk, _ = x.shape
    nch = t_len // tc
    nvh = RWKV_HEAD // SUBLANES

    def tmap(g, c, rev_ref):
        return c + rev_ref[g] * (nch - 1 - 2 * c)

    return pl.pallas_call(
        functools.partial(_scan_kernel, tc=tc, nk=nk, kl=kl, vg=vg),
        grid_spec=pltpu.PrefetchScalarGridSpec(
            num_scalar_prefetch=1,
            grid=(ngb, nch),
            in_specs=[
                pl.BlockSpec((1, tc, 5, nk, LANES), lambda g, c, r: (g, tmap(g, c, r), 0, 0, 0)),
                pl.BlockSpec((1, tc, RWKV_HEAD, LANES), lambda g, c, r: (g, tmap(g, c, r), 0, 0)),
                pl.BlockSpec((1, nvh, nk, SUBLANES, LANES), lambda g, c, r: (g, 0, 0, 0, 0)),
            ],
            out_specs=[
                pl.BlockSpec((1, tc, RWKV_HEAD, LANES), lambda g, c, r: (g, tmap(g, c, r), 0, 0)),
                pl.BlockSpec((1, nvh, nk, SUBLANES, LANES), lambda g, c, r: (g, 0, 0, 0, 0)),
            ],
            scratch_shapes=[pltpu.VMEM((nvh, nk, SUBLANES, LANES), F32)],
        ),
        out_shape=[
            jax.ShapeDtypeStruct((ngb, t_len, RWKV_HEAD, LANES), F32),
            jax.ShapeDtypeStruct((ngb, nvh, nk, SUBLANES, LANES), F32),
        ],
        compiler_params=_cparams(("arbitrary", "arbitrary")),
        name=f"wkv_scan_kl{kl}",
    )(rev, x, v, s0)


def _rwkv_post_kernel(of_ref, ob_ref, bonus_ref, g_ref, gate_ref, gng_ref, gnb_ref, ones_ref,
                      wo_ref, y_ref):
    o = of_ref[0] + ob_ref[0]
    ones_bd = ones_ref[...]
    mean = _seg_sum(o, ones_bd) * (1.0 / RWKV_HEAD)
    oc = o - mean
    var = _seg_sum(oc * oc, ones_bd) * (1.0 / RWKV_HEAD)
    gn = oc * lax.rsqrt(var + GN_EPS) * gng_ref[...] + gnb_ref[...]
    y = (gn + bonus_ref[...]) * g_ref[...]
    y = jnp.dot(y.astype(BF16), wo_ref[...], preferred_element_type=F32)
    y_ref[...] = _sigmoid(gate_ref[...]) * y


def rwkv_post(o2, bonus, g, u, gng, gnb, ones_bd, wo, geo):
    n = bonus.shape[0]

    def full(shape):
        return pl.BlockSpec(shape, lambda i: tuple(0 for _ in shape))

    return pl.pallas_call(
        _rwkv_post_kernel,
        grid=(geo.ntile,),
        in_specs=[
            pl.BlockSpec((1, TILE, D_RWKV), lambda i: (0, i, 0)),
            pl.BlockSpec((1, TILE, D_RWKV), lambda i: (1, i, 0)),
            pl.BlockSpec((TILE, D_RWKV), lambda i: (i, 0)),
            pl.BlockSpec((TILE, D_RWKV), lambda i: (i, 0)),
            pl.BlockSpec((TILE, D_MODEL), lambda i: (i, C_G1 // D_MODEL)),
            full((1, D_RWKV)), full((1, D_RWKV)), full((D_RWKV, D_RWKV)), full((D_RWKV, D_MODEL)),
        ],
        out_specs=pl.BlockSpec((TILE, D_MODEL), lambda i: (i, 0)),
        out_shape=jax.ShapeDtypeStruct((n, D_MODEL), F32),
        compiler_params=_cparams(("arbitrary",)),
        name="rwkv_post",
    )(o2, o2, bonus, g, u, gng.reshape(1, -1), gnb.reshape(1, -1), ones_bd, wo.astype(BF16))


def _mla_pre_kernel(cq_ref, ckv_ref, kr_ref, cos_ref, sin_ref, qg_ref, kvg_ref, wuq_ref, wuk_ref,
                    wuv_ref, qn_o, qr_o, ckvn_o, kn_o, v_o, krr_o):
    nq = N_MLA_HEADS * NOPE_DIM
    nr = N_MLA_HEADS * ROPE_DIM
    cq = _rms(cq_ref[...], qg_ref[...])
    q = jnp.dot(cq.astype(BF16), wuq_ref[...], preferred_element_type=F32)
    cos = cos_ref[...]
    sin = sin_ref[...]
    qn_o[...] = q[:, :nq].astype(qn_o.dtype)
    qr_o[...] = (q[:, nq:nq + nr] * cos + q[:, nq + nr:] * sin).astype(qr_o.dtype)
    ckv = _rms(ckv_ref[...], kvg_ref[...])
    ckvn_o[...] = ckv
    cb = ckv.astype(BF16)
    kn_o[...] = jnp.dot(cb, wuk_ref[...], preferred_element_type=F32).astype(kn_o.dtype)
    v_o[...] = jnp.dot(cb, wuv_ref[...], preferred_element_type=F32).astype(v_o.dtype)
    kr = kr_ref[...]
    krr = kr[:, ROPE_DIM:2 * ROPE_DIM] * cos[:, :ROPE_DIM] + kr[:, 2 * ROPE_DIM:3 * ROPE_DIM] * sin[:, :ROPE_DIM]
    krr_o[...] = jnp.concatenate([krr, jnp.zeros((TILE, LANES - ROPE_DIM), F32)], axis=1).astype(krr_o.dtype)


def mla_pre(u, cos_t, sin_t, qg, kvg, wuq_p, wuk, wuv, geo):
    n = u.shape[0]
    nq = N_MLA_HEADS * NOPE_DIM
    nr = N_MLA_HEADS * ROPE_DIM

    def full(shape):
        return pl.BlockSpec(shape, lambda i: tuple(0 for _ in shape))

    return pl.pallas_call(
        _mla_pre_kernel,
        grid=(geo.ntile,),
        in_specs=[
            pl.BlockSpec((TILE, Q_RANK), lambda i: (i, C_CQ // Q_RANK)),
            pl.BlockSpec((TILE, KV_RANK), lambda i: (i, C_CKV // KV_RANK)),
            pl.BlockSpec((TILE, LANES), lambda i: (i, C_KR // LANES)),
            pl.BlockSpec((TILE, nr), lambda i: (geo.rope_blk(i), 0)),
            pl.BlockSpec((TILE, nr), lambda i: (geo.rope_blk(i), 0)),
            full((1, Q_RANK)), full((1, KV_RANK)),
            full((Q_RANK, nq + 2 * nr)), full((KV_RANK, nq)), full((KV_RANK, N_MLA_HEADS * V_DIM)),
        ],
        out_specs=[
            pl.BlockSpec((TILE, nq), lambda i: (i, 0)),
            pl.BlockSpec((TILE, nr), lambda i: (i, 0)),
            pl.BlockSpec((TILE, KV_RANK), lambda i: (i, 0)),
            pl.BlockSpec((TILE, nq), lambda i: (i, 0)),
            pl.BlockSpec((TILE, N_MLA_HEADS * V_DIM), lambda i: (i, 0)),
            pl.BlockSpec((TILE, LANES), lambda i: (i, 0)),
        ],
        out_shape=[
            jax.ShapeDtypeStruct((n, nq), BF16),
            jax.ShapeDtypeStruct((n, nr), BF16),
            jax.ShapeDtypeStruct((n, KV_RANK), F32),
            jax.ShapeDtypeStruct((n, nq), BF16),
            jax.ShapeDtypeStruct((n, N_MLA_HEADS * V_DIM), BF16),
            jax.ShapeDtypeStruct((n, LANES), BF16),
        ],
        compiler_params=_cparams(("arbitrary",)),
        name="mla_pre",
    )(u, u, u, cos_t, sin_t, qg.reshape(1, -1), kvg.reshape(1, -1), wuq_p, wuk.astype(BF16),
      wuv.astype(BF16))


def _fa_kernel(q_ref, k_ref, v_ref, o_ref, m_ref, l_ref, acc_ref):
    kv = pl.program_id(2)

    @pl.when(kv == 0)
    def _():
        m_ref[...] = jnp.full(m_ref.shape, -jnp.inf, F32)
        l_ref[...] = jnp.zeros(l_ref.shape, F32)
        acc_ref[...] = jnp.zeros(acc_ref.shape, F32)

    s = lax.dot_general(q_ref[0], k_ref[0], (((1,), (1,)), ((), ())),
                        preferred_element_type=F32) * ATTN_SCALE
    m_old = m_ref[...]
    m_new = jnp.maximum(m_old, jnp.max(s, axis=-1, keepdims=True))
    p = jnp.exp(s - m_new)
    alpha = jnp.exp(m_old - m_new)
    l_ref[...] = alpha * l_ref[...] + jnp.sum(p, axis=-1, keepdims=True)
    acc_ref[...] = alpha * acc_ref[...] + jnp.dot(p.astype(BF16), v_ref[0], preferred_element_type=F32)
    m_ref[...] = m_new

    @pl.when(kv == pl.num_programs(2) - 1)
    def _():
        o_ref[0] = (acc_ref[...] / l_ref[...]).astype(o_ref.dtype)


def flash_attention(q, k, v, *, tq, tk):
    bh, t_q, dk = q.shape
    t_k = k.shape[1]
    dv = v.shape[2]
    return pl.pallas_call(
        _fa_kernel,
        grid=(bh, t_q // tq, t_k // tk),
        in_specs=[
            pl.BlockSpec((1, tq, dk), lambda b, i, j: (b, i, 0)),
            pl.BlockSpec((1, tk, dk), lambda b, i, j: (b, j, 0)),
            pl.BlockSpec((1, tk, dv), lambda b, i, j: (b, j, 0)),
        ],
        out_specs=pl.BlockSpec((1, tq, dv), lambda b, i, j: (b, i, 0)),
        out_shape=jax.ShapeDtypeStruct((bh, t_q, dv), BF16),
        scratch_shapes=[pltpu.VMEM((tq, 1), F32), pltpu.VMEM((tq, 1), F32), pltpu.VMEM((tq, dv), F32)],
        compiler_params=_cparams(("arbitrary", "arbitrary", "arbitrary")),
        name="flash_attention",
    )(q, k, v)


def _mla_out_kernel(o_ref, gate_ref, wo_ref, y_ref):
    y = jnp.dot(o_ref[...], wo_ref[...], preferred_element_type=F32)
    y_ref[...] = _sigmoid(gate_ref[...]) * y


def mla_out(o, u, wo, geo):
    n = o.shape[0]
    return pl.pallas_call(
        _mla_out_kernel,
        grid=(geo.ntile,),
        in_specs=[
            pl.BlockSpec((TILE, N_MLA_HEADS * V_DIM), lambda i: (i, 0)),
            pl.BlockSpec((TILE, D_MODEL), lambda i: (i, C_G2 // D_MODEL)),
            pl.BlockSpec((N_MLA_HEADS * V_DIM, D_MODEL), lambda i: (0, 0)),
        ],
        out_specs=pl.BlockSpec((TILE, D_MODEL), lambda i: (i, 0)),
        out_shape=jax.ShapeDtypeStruct((n, D_MODEL), F32),
        compiler_params=_cparams(("arbitrary",)),
        name="mla_out",
    )(o, u, wo.astype(BF16))


def _merge_kernel(x_ref, yc_ref, yr_ref, ym_ref, wout_ref, gpost_ref, g1_ref, gpre_ref, sc_ref,
                  sh_ref, router_ref, x1_o, h2_o, comb_o, *, moe):
    m = yc_ref[...] + yr_ref[...] + ym_ref[...]
    y = jnp.dot(m.astype(BF16), wout_ref[...], preferred_element_type=F32)
    x1 = x_ref[...] + g1_ref[0] * _rms(y, gpost_ref[...])
    x1_o[...] = x1
    h2 = _rms(x1, gpre_ref[...]) * (1.0 + sc_ref[0]) + sh_ref[0]
    h2_o[...] = h2.astype(h2_o.dtype)
    if moe:
        logits = jnp.dot(h2, router_ref[...], preferred_element_type=F32,
                         precision=lax.Precision.HIGHEST)
        lane = lax.broadcasted_iota(jnp.int32, logits.shape, 1)
        neg = jnp.float32(-jnp.inf)
        logits = jnp.where(lane < N_EXPERTS, logits, neg)
        m1 = jnp.max(logits, axis=-1, keepdims=True)
        i1 = jnp.min(jnp.where(logits == m1, lane, LANES), axis=-1, keepdims=True)
        rest = jnp.where(lane == i1, neg, logits)
        m2 = jnp.max(rest, axis=-1, keepdims=True)
        i2 = jnp.min(jnp.where(rest == m2, lane, LANES), axis=-1, keepdims=True)
        e2 = jnp.exp(m2 - m1)
        den = 1.0 + e2
        comb_o[...] = jnp.where(lane == i1, 1.0 / den, 0.0) + jnp.where(lane == i2, e2 / den, 0.0)
    else:
        comb_o[...] = jnp.ones(comb_o.shape, F32)


def merge(x, yc, yr, ym, wout, gpost, gpre, mod, router, geo, moe):
    n, d = x.shape

    def full(shape):
        return pl.BlockSpec(shape, lambda i: tuple(0 for _ in shape))

    def rows():
        return pl.BlockSpec((TILE, d), lambda i: (i, 0))

    def modspec(j):
        return pl.BlockSpec((1, 1, d), lambda i: (geo.mod_row(i) * 6 + j, 0, 0))

    router_p = jnp.zeros((d, LANES), F32).at[:, :N_EXPERTS].set(router)
    return pl.pallas_call(
        functools.partial(_merge_kernel, moe=moe),
        grid=(geo.ntile,),
        in_specs=[rows(), rows(), rows(), rows(), full((d, d)), full((1, d)), modspec(2),
                  full((1, d)), modspec(4), modspec(3), full((d, LANES))],
        out_specs=[rows(), rows(), pl.BlockSpec((TILE, LANES), lambda i: (i, 0))],
        out_shape=[jax.ShapeDtypeStruct((n, d), F32), jax.ShapeDtypeStruct((n, d), BF16),
                   jax.ShapeDtypeStruct((n, LANES), F32)],
        compiler_params=_cparams(("arbitrary",)),
        name="merge",
    )(x, yc, yr, ym, wout.astype(BF16), gpost.reshape(1, d), mod, gpre.reshape(1, d), mod, mod,
      router_p)


def _ffn_kernel(h_ref, comb_ref, w1_ref, w3_ref, w2_ref, o_ref, acc_ref):
    e = pl.program_id(1)
    f = pl.program_id(2)

    @pl.when((e == 0) & (f == 0))
    def _():
        acc_ref[...] = jnp.zeros(acc_ref.shape, F32)

    h = h_ref[...]
    a = jnp.dot(h, w1_ref[0], preferred_element_type=F32)
    b = jnp.dot(h, w3_ref[0], preferred_element_type=F32)
    comb = comb_ref[...]
    lane = lax.broadcasted_iota(jnp.int32, comb.shape, 1)
    ce = jnp.sum(jnp.where(lane == e, comb, 0.0), axis=-1, keepdims=True)
    act = (a * _sigmoid(a) * b) * ce
    acc_ref[...] += jnp.dot(act.astype(BF16), w2_ref[0], preferred_element_type=F32)

    @pl.when((e == pl.num_programs(1) - 1) & (f == pl.num_programs(2) - 1))
    def _():
        o_ref[...] = acc_ref[...]


def ffn(h, comb, w1, w3, w2, *, tm, tf):
    n, d = h.shape
    ne, _, dff = w1.shape
    return pl.pallas_call(
        _ffn_kernel,
        grid=(n // tm, ne, dff // tf),
        in_specs=[
            pl.BlockSpec((tm, d), lambda i, e, f: (i, 0)),
            pl.BlockSpec((tm, LANES), lambda i, e, f: (i, 0)),
            pl.BlockSpec((1, d, tf), lambda i, e, f: (e, 0, f)),
            pl.BlockSpec((1, d, tf), lambda i, e, f: (e, 0, f)),
            pl.BlockSpec((1, tf, d), lambda i, e, f: (e, f, 0)),
        ],
        out_specs=pl.BlockSpec((tm, d), lambda i, e, f: (i, 0)),
        out_shape=jax.ShapeDtypeStruct((n, d), F32),
        scratch_shapes=[pltpu.VMEM((tm, d), F32)],
        compiler_params=_cparams(("arbitrary", "arbitrary", "arbitrary")),
        name="ffn",
    )(h, comb, w1, w3, w2)


def _ffn_post_kernel(x_ref, y_ref, g_ref, g2_ref, o_ref):
    o_ref[...] = x_ref[...] + g2_ref[0] * _rms(y_ref[...], g_ref[...])


def ffn_post(x1, y, gpost, mod, geo):
    n, d = x1.shape
    return pl.pallas_call(
        _ffn_post_kernel,
        grid=(geo.ntile,),
        in_specs=[
            pl.BlockSpec((TILE, d), lambda i: (i, 0)),
            pl.BlockSpec((TILE, d), lambda i: (i, 0)),
            pl.BlockSpec((1, d), lambda i: (0, 0)),
            pl.BlockSpec((1, 1, d), lambda i: (geo.mod_row(i) * 6 + 5, 0, 0)),
        ],
        out_specs=pl.BlockSpec((TILE, d), lambda i: (i, 0)),
        out_shape=jax.ShapeDtypeStruct((n, d), F32),
        compiler_params=_cparams(("arbitrary",)),
        name="ffn_post",
    )(x1, y, gpost.reshape(1, d), mod)


def _scan_layouts(geo, pre, state_l):
    r, kk, v, w0, w1, ka0, ka1, kt0, kt1 = pre
    H, K = N_RWKV_HEADS, RWKV_HEAD
    nc, nl = geo.nc_tok, geo.nl_tok
    bl = LANES // H
    nbh = geo.n_ctx // bl

    def ctx_l(a):
        a = a[:nc].reshape(nbh, bl, geo.t_ctx, H, K)
        return a.transpose(0, 2, 4, 1, 3).reshape(nbh, geo.t_ctx, K, LANES)

    xs = []
    for (w, ka, kt) in ((w0, ka0, kt0), (w1, ka1, kt1)):
        xs.append(jnp.stack([ctx_l(w), ctx_l(kk), ctx_l(ka), ctx_l(kt), ctx_l(r)], axis=2))
    x_ctx = jnp.concatenate(xs, axis=0)
    v_ctx = jnp.tile(ctx_l(v), (2, 1, 1, 1))
    rev_ctx = jnp.concatenate([jnp.zeros((nbh,), jnp.int32), jnp.ones((nbh,), jnp.int32)])

    kl = LANES // (geo.n_lat * 2 * H)
    nk = K // kl

    def lat_pair(a0, a1):
        a0 = a0[nc:].reshape(geo.n_lat, geo.t_lat, H, K)
        a1 = jnp.flip(a1[nc:].reshape(geo.n_lat, geo.t_lat, H, K), axis=1)
        a = jnp.stack([a0, a1], axis=2).reshape(geo.n_lat, geo.t_lat, 2, H, kl, nk)
        return a.transpose(1, 5, 4, 0, 2, 3).reshape(geo.t_lat, nk, LANES)

    x_lat = jnp.stack([lat_pair(w0, w1), lat_pair(kk, kk), lat_pair(ka0, ka1), lat_pair(kt0, kt1),
                       lat_pair(r, r)], axis=1)[None]
    vl = v[nc:].reshape(geo.n_lat, geo.t_lat, H, K)
    vl = jnp.stack([vl, jnp.flip(vl, axis=1)], axis=2)
    vl = vl.transpose(1, 4, 0, 2, 3).reshape(geo.t_lat, K, LANES // kl)
    v_lat = jnp.tile(vl, (1, 1, kl))[None]
    s0 = state_l.reshape(geo.n_lat, 2, H, K // SUBLANES, SUBLANES, kl, nk)
    s0_lat = s0.transpose(3, 6, 4, 5, 0, 1, 2).reshape(1, K // SUBLANES, nk, SUBLANES, LANES)
    return x_ctx, v_ctx, rev_ctx, x_lat, v_lat, s0_lat, kl


def _scan_outputs(geo, o_ctx, sfin_ctx, o_lat, kl):
    H, K = N_RWKV_HEADS, RWKV_HEAD
    bl = LANES // H
    nbh = geo.n_ctx // bl
    oc = o_ctx.reshape(2, nbh, geo.t_ctx, K, bl, H).transpose(0, 1, 4, 2, 5, 3)
    oc = oc.reshape(2, geo.nc_tok, H * K)
    ol = o_lat[0, :, :, :LANES // kl].reshape(geo.t_lat, K, geo.n_lat, 2, H)
    ol = ol.transpose(3, 2, 0, 4, 1)
    ol = jnp.stack([ol[0], jnp.flip(ol[1], axis=1)], axis=0).reshape(2, geo.nl_tok, H * K)
    o2 = jnp.concatenate([oc, ol], axis=1)
    sf = sfin_ctx.reshape(2, nbh, K // SUBLANES, K, SUBLANES, bl, H)
    sf = sf.transpose(1, 5, 0, 6, 2, 4, 3).reshape(geo.n_ctx, 2, H, K, K)
    return o2, sf


def _rope_tables(geo):
    n_freq = ROPE_DIM // 4
    rows = geo.t_lat // GRID_W
    row = jnp.repeat(jnp.arange(rows, dtype=F32), GRID_W)
    col = jnp.tile(jnp.arange(GRID_W, dtype=F32), rows)
    inv = ROPE_BASE ** (-jnp.arange(n_freq, dtype=F32) / n_freq)
    ang = jnp.concatenate([row[:, None] * inv, col[:, None] * inv], axis=-1)
    cos, sin = jnp.cos(ang), jnp.sin(ang)
    cos32 = jnp.concatenate([cos, cos], axis=-1)
    sin32 = jnp.concatenate([-sin, sin], axis=-1)
    cos_t = jnp.concatenate([jnp.ones((TILE, ROPE_DIM), F32), cos32], axis=0)
    sin_t = jnp.concatenate([jnp.zeros((TILE, ROPE_DIM), F32), sin32], axis=0)
    return jnp.tile(cos_t, (1, N_MLA_HEADS)), jnp.tile(sin_t, (1, N_MLA_HEADS))


_DEINT = np.concatenate([np.arange(0, ROPE_DIM, 2), np.arange(1, ROPE_DIM, 2)])
_DEINT_SW = np.concatenate([np.arange(1, ROPE_DIM, 2), np.arange(0, ROPE_DIM, 2)])


def _prep_w_in(w_in, b_in):
    offs = np.cumsum([0, 2 * D_CONV, D_RWKV, D_RWKV, D_RWKV, W_RANK, A_RANK, G_RANK, Q_RANK, KV_RANK,
                      ROPE_DIM, N_BRANCH * D_MODEL])
    o_conv, o_r, o_k, o_v, o_dw, o_da, o_dg, o_cq, o_ckv, o_kr, o_gate, _ = offs
    idx = np.zeros((IN_PAD,), np.int32)
    valid = np.zeros((IN_PAD,), bool)

    def put(dst, src):
        idx[dst:dst + len(src)] = src
        valid[dst:dst + len(src)] = True

    put(C_CONV, np.arange(o_conv, o_conv + 2 * D_CONV))
    for b, c in enumerate((C_G0, C_G1, C_G2)):
        put(c, np.arange(o_gate + b * D_MODEL, o_gate + (b + 1) * D_MODEL))
    put(C_R, np.arange(o_r, o_r + D_RWKV))
    put(C_K, np.arange(o_k, o_k + D_RWKV))
    put(C_V, np.arange(o_v, o_v + D_RWKV))
    put(C_LORA, np.arange(o_dw, o_dw + W_RANK + A_RANK + G_RANK))
    put(C_CQ, np.arange(o_cq, o_cq + Q_RANK))
    put(C_CKV, np.arange(o_ckv, o_ckv + KV_RANK))
    put(C_KR, np.arange(o_kr, o_kr + ROPE_DIM))
    put(C_KR + ROPE_DIM, o_kr + _DEINT)
    put(C_KR + 2 * ROPE_DIM, o_kr + _DEINT_SW)
    w = jnp.where(valid[None, :], jnp.take(w_in, idx, axis=1), 0.0).astype(BF16)
    b = jnp.where(valid, jnp.take(b_in, idx), 0.0)
    return w, b


def _prep_wuq(wuq):
    hd = NOPE_DIM + ROPE_DIM
    nope = np.concatenate([h * hd + np.arange(NOPE_DIM) for h in range(N_MLA_HEADS)])
    rope = np.concatenate([h * hd + NOPE_DIM + _DEINT for h in range(N_MLA_HEADS)])
    rope_sw = np.concatenate([h * hd + NOPE_DIM + _DEINT_SW for h in range(N_MLA_HEADS)])
    return jnp.take(wuq, np.concatenate([nope, rope, rope_sw]), axis=1).astype(BF16)


def _heads(a, nseq, t, width):
    return a.reshape(nseq, t, N_MLA_HEADS, width).transpose(0, 2, 1, 3).reshape(nseq * N_MLA_HEADS, t, width)


def _attention(geo, qn, qr, kn, v, krr, cache_kn, cache_v, cache_kr):
    H = N_MLA_HEADS
    nc = geo.nc_tok
    pad = LANES - NOPE_DIM - ROPE_DIM

    def qk(nope, rope_h, nseq, t):
        a = _heads(nope, nseq, t, NOPE_DIM)
        return jnp.concatenate([a, rope_h, jnp.zeros(a.shape[:2] + (pad,), BF16)], axis=-1)

    def shared(kr, nseq, t):
        return jnp.broadcast_to(kr.reshape(nseq, 1, t, ROPE_DIM), (nseq, H, t, ROPE_DIM)).reshape(nseq * H, t, ROPE_DIM)

    kr32 = krr[:, :ROPE_DIM]
    q_c = qk(qn[:nc], _heads(qr[:nc], geo.n_ctx, geo.t_ctx, ROPE_DIM), geo.n_ctx, geo.t_ctx)
    k_c = qk(kn[:nc], shared(kr32[:nc], geo.n_ctx, geo.t_ctx), geo.n_ctx, geo.t_ctx)
    v_c = _heads(v[:nc], geo.n_ctx, geo.t_ctx, V_DIM)
    o_c = flash_attention(q_c, k_c, v_c, tq=geo.t_ctx, tk=geo.t_ctx)

    q_l = qk(qn[nc:], _heads(qr[nc:], geo.n_lat, geo.t_lat, ROPE_DIM), geo.n_lat, geo.t_lat)
    k_l = qk(kn[nc:], shared(kr32[nc:], geo.n_lat, geo.t_lat), geo.n_lat, geo.t_lat)
    k_p = qk(cache_kn, shared(cache_kr, geo.n_lat, geo.past), geo.n_lat, geo.past)
    k_l = jnp.concatenate([k_l, k_p], axis=1)
    v_l = jnp.concatenate([_heads(v[nc:], geo.n_lat, geo.t_lat, V_DIM),
                           _heads(cache_v, geo.n_lat, geo.past, V_DIM)], axis=1)
    tq = min(512, geo.t_lat)
    tk = math.gcd(512, geo.past)
    o_l = flash_attention(q_l, k_l, v_l, tq=tq, tk=tk)

    def unheads(o, nseq, t):
        return o.reshape(nseq, H, t, V_DIM).transpose(0, 2, 1, 3).reshape(nseq * t, H * V_DIM)

    return jnp.concatenate([unheads(o_c, geo.n_ctx, geo.t_ctx), unheads(o_l, geo.n_lat, geo.t_lat)], axis=0)


def kernel(x_prompt, x_sample, cache_ckv, cache_krope, state_wkv, c, c_ctx, ada_w, ada_b, norm_mix_pre, norm_mix_post, norm_ffn_pre, norm_ffn_post, w_in, b_in, conv_w, conv_b, conv_ln_g, conv_ln_b, conv_wo, rwkv_mu, rwkv_w0, rwkv_bw, rwkv_a0, rwkv_ba, rwkv_bg, rwkv_xi, rwkv_alpha, rwkv_rho, rwkv_gn_g, rwkv_gn_b, rwkv_wo, mla_q_norm, mla_wuq, mla_kv_norm, mla_wuk, mla_wuv, mla_wo, w_out, ffn_w1, ffn_w3, ffn_w2, moe_router, moe_w1, moe_w3, moe_w2):
    P = dict(rwkv_mu=rwkv_mu, rwkv_w0=rwkv_w0, rwkv_bw=rwkv_bw, rwkv_a0=rwkv_a0, rwkv_ba=rwkv_ba,
             rwkv_bg=rwkv_bg, rwkv_xi=rwkv_xi, rwkv_alpha=rwkv_alpha, rwkv_rho=rwkv_rho)
    n_ctx, t_ctx, d = x_prompt.shape
    n_lat, t_lat, _ = x_sample.shape
    depth = ada_w.shape[0]
    past = cache_ckv.shape[2]
    geo = Geo(n_ctx, t_ctx, n_lat, t_lat, past)
    assert d == D_MODEL and (n_ctx * N_RWKV_HEADS) % LANES == 0 and LANES % (n_lat * 2 * N_RWKV_HEADS) == 0

    x = jnp.concatenate([x_prompt.reshape(-1, d), x_sample.reshape(-1, d)], axis=0)

    n_mod = 1 + n_lat
    n_mod_pad = -(-n_mod // SUBLANES) * SUBLANES
    c_all = jnp.zeros((n_mod_pad, d), F32).at[0].set(c_ctx).at[1:n_mod].set(c)
    head_id = np.arange(D_RWKV) // RWKV_HEAD
    ones_bd = jnp.asarray((head_id[:, None] == head_id[None, :]).astype(np.float32))
    cos_t, sin_t = _rope_tables(geo)

    ckv_out, kr_out, st_out = [], [], []
    for l in range(depth):
        mod = matmul_bias(c_all, ada_w[l].astype(BF16), ada_b[l], tm=n_mod_pad, tn=6 * d // 4,
                          pre="silu", name="ada_mod")
        mod = mod[:n_mod].reshape(n_mod * 6, 1, d)

        h = prenorm(x, norm_mix_pre[l], mod, geo, j_shift=0, j_scale=1)
        w_in_p, b_in_p = _prep_w_in(w_in[l], b_in[l])
        u = matmul_bias(h, w_in_p, b_in_p, tm=512, tn=1280, name="in_proj")

        y_conv = conv_branch(u, conv_w[l], conv_b[l], conv_ln_g[l], conv_ln_b[l], conv_wo[l], geo)

        pre = rwkv_pre(u, P, l, geo, ones_bd)
        x_ctx, v_ctx, rev_ctx, x_lat, v_lat, s0_lat, kl = _scan_layouts(geo, pre[:9], state_wkv[:, l])
        s0_ctx = jnp.zeros((x_ctx.shape[0], RWKV_HEAD // SUBLANES, RWKV_HEAD, SUBLANES, LANES), F32)
        o_ctx, sfin_ctx = wkv_scan(x_ctx, v_ctx, s0_ctx, rev_ctx, tc=16, kl=1, vg=4)
        o_lat, _ = wkv_scan(x_lat, v_lat, s0_lat, jnp.zeros((1,), jnp.int32), tc=128, kl=kl, vg=2)
        o2, sfin = _scan_outputs(geo, o_ctx, sfin_ctx, o_lat, kl)
        y_rwkv = rwkv_post(o2, pre[10], pre[9], u, rwkv_gn_g[l], rwkv_gn_b[l], ones_bd, rwkv_wo[l], geo)

        qn, qr, ckvn, kn, v, krr = mla_pre(u, cos_t, sin_t, mla_q_norm[l], mla_kv_norm[l],
                                           _prep_wuq(mla_wuq[l]), mla_wuk[l], mla_wuv[l], geo)
        cache_c = cache_ckv[:, l].reshape(n_lat * past, KV_RANK)
        wukv = jnp.concatenate([mla_wuk[l], mla_wuv[l]], axis=1).astype(BF16)
        ckn_cv = matmul_bias(cache_c, wukv, jnp.zeros((wukv.shape[1],), F32), tm=n_lat * past,
                             tn=wukv.shape[1], out_dtype=BF16, name="cache_kv")
        nq = N_MLA_HEADS * NOPE_DIM
        cache_kr = cache_krope[:, l][..., _DEINT].reshape(n_lat * past, ROPE_DIM).astype(BF16)
        o_att = _attention(geo, qn, qr, kn, v, krr, ckn_cv[:, :nq], ckn_cv[:, nq:], cache_kr)
        y_mla = mla_out(o_att, u, mla_wo[l], geo)

        moe = (l % 2 == 1)
        i = l // 2
        router = moe_router[i] if moe else jnp.zeros((d, N_EXPERTS), F32)
        x1, h2, comb = merge(x, y_conv, y_rwkv, y_mla, w_out[l], norm_mix_post[l], norm_ffn_pre[l],
                             mod, router, geo, moe)
        if moe:
            y = ffn(h2, comb, moe_w1[i].astype(BF16), moe_w3[i].astype(BF16), moe_w2[i].astype(BF16),
                    tm=512, tf=D_FF // 2)
        else:
            y = ffn(h2, comb, ffn_w1[i:i + 1].astype(BF16), ffn_w3[i:i + 1].astype(BF16),
                    ffn_w2[i:i + 1].astype(BF16), tm=512, tf=D_FF // 2)
        x = ffn_post(x1, y, norm_ffn_post[l], mod, geo)

        ckv_out.append(ckvn[:geo.nc_tok].reshape(n_ctx, t_ctx, KV_RANK))
        kr_out.append(u[:geo.nc_tok, C_KR:C_KR + ROPE_DIM].reshape(n_ctx, t_ctx, ROPE_DIM))
        st_out.append(sfin)

    y_prompt = x[:geo.nc_tok].reshape(n_ctx, t_ctx, d)
    y_sample = x[geo.nc_tok:].reshape(n_lat, t_lat, d)
    return (y_prompt, y_sample, jnp.stack(ckv_out, axis=1), jnp.stack(kr_out, axis=1),
            jnp.stack(st_out, axis=1))
```

```python
import functools
import math

import numpy as np
import jax
import jax.numpy as jnp
from jax import lax
from jax.experimental import pallas as pl
from jax.experimental.pallas import tpu as pltpu

F32 = jnp.float32
BF16 = jnp.bfloat16

D_MODEL = 1024
GRID_W = 64
D_CONV = 512
CONV_K = 31
D_RWKV = 512
RWKV_HEAD = 64
N_RWKV_HEADS = D_RWKV // RWKV_HEAD
W_RANK = 64
A_RANK = 64
G_RANK = 128
DECAY_SCALE = math.exp(-0.5)
GN_EPS = 64e-5
N_MLA_HEADS = 8
Q_RANK = 256
KV_RANK = 128
NOPE_DIM = 64
ROPE_DIM = 32
V_DIM = 64
ROPE_BASE = 10000.0
ATTN_SCALE = 1.0 / math.sqrt(NOPE_DIM + ROPE_DIM)
N_BRANCH = 3
D_FF = 2816
N_EXPERTS = 8
EPS = 1e-6

LANES = 128
SUBLANES = 8
VMEM_LIMIT_BYTES = 56 * 1024 * 1024

TILE = 256
CONV_HALO = 16
SHIFT_HALO = 8

C_CONV, C_G0, C_G1, C_G2 = 0, 1024, 2048, 3072
C_R, C_K, C_V = 4096, 4608, 5120
C_LORA = 5632
C_CQ = 5888
C_CKV = 6144
C_KR = 6272
IN_PAD = 6400


def _cparams(sem):
    return pltpu.CompilerParams(dimension_semantics=sem, vmem_limit_bytes=VMEM_LIMIT_BYTES)


def _sigmoid(x):
    return jax.nn.sigmoid(x)


def _rms(x, g):
    return (x * lax.rsqrt(jnp.mean(x * x, axis=-1, keepdims=True) + EPS)) * g


class Geo:
    def __init__(self, n_ctx, t_ctx, n_lat, t_lat, past):
        assert t_ctx % TILE == 0 and t_lat % TILE == 0
        self.n_ctx, self.t_ctx, self.n_lat, self.t_lat, self.past = n_ctx, t_ctx, n_lat, t_lat, past
        self.ct = t_ctx // TILE
        self.lt = t_lat // TILE
        self.nct = n_ctx * self.ct
        self.nlt = n_lat * self.lt
        self.ntile = self.nct + self.nlt
        self.nc_tok = n_ctx * t_ctx
        self.nl_tok = n_lat * t_lat
        self.ntok = self.nc_tok + self.nl_tok

    def pos(self, i):
        is_ctx = i < self.nct
        p = jnp.where(is_ctx, i % self.ct, (i - self.nct) % self.lt)
        n = jnp.where(is_ctx, self.ct, self.lt)
        return p, n

    def mod_row(self, i):
        return jnp.where(i < self.nct, 0, 1 + (i - self.nct) // self.lt)

    def rope_blk(self, i):
        return jnp.where(i < self.nct, 0, 1 + (i - self.nct) % self.lt)


def _mm_kernel(x_ref, w_ref, b_ref, o_ref, *, pre):
    x = x_ref[...]
    if pre == "silu":
        x = x.astype(F32)
        x = x * _sigmoid(x)
    acc = jnp.dot(x.astype(BF16), w_ref[...], preferred_element_type=F32)
    o_ref[...] = (acc + b_ref[...]).astype(o_ref.dtype)


def matmul_bias(x, w, b, *, tm, tn, pre=None, out_dtype=F32, name="matmul"):
    m, k = x.shape
    n = w.shape[1]
    assert m % tm == 0 and n % tn == 0
    return pl.pallas_call(
        functools.partial(_mm_kernel, pre=pre),
        grid=(n // tn, m // tm),
        in_specs=[
            pl.BlockSpec((tm, k), lambda j, i: (i, 0)),
            pl.BlockSpec((k, tn), lambda j, i: (0, j)),
            pl.BlockSpec((1, tn), lambda j, i: (0, j)),
        ],
        out_specs=pl.BlockSpec((tm, tn), lambda j, i: (i, j)),
        out_shape=jax.ShapeDtypeStruct((m, n), out_dtype),
        compiler_params=_cparams(("arbitrary", "arbitrary")),
        name=name,
    )(x, w, b.reshape(1, n).astype(F32))


def _prenorm_kernel(x_ref, g_ref, sc_ref, sh_ref, o_ref):
    h = _rms(x_ref[...], g_ref[...]) * (1.0 + sc_ref[0]) + sh_ref[0]
    o_ref[...] = h.astype(o_ref.dtype)


def prenorm(x, g, mod, geo, j_shift, j_scale):
    n, d = x.shape
    return pl.pallas_call(
        _prenorm_kernel,
        grid=(geo.ntile,),
        in_specs=[
            pl.BlockSpec((TILE, d), lambda i: (i, 0)),
            pl.BlockSpec((1, d), lambda i: (0, 0)),
            pl.BlockSpec((1, 1, d), lambda i: (geo.mod_row(i) * 6 + j_scale, 0, 0)),
            pl.BlockSpec((1, 1, d), lambda i: (geo.mod_row(i) * 6 + j_shift, 0, 0)),
        ],
        out_specs=pl.BlockSpec((TILE, d), lambda i: (i, 0)),
        out_shape=jax.ShapeDtypeStruct((n, d), BF16),
        compiler_params=_cparams(("arbitrary",)),
        name="prenorm",
    )(x, g.reshape(1, d), mod, mod)


def _conv_kernel(cur_ref, prev_ref, next_ref, gate_ref, cw_ref, cb_ref, lng_ref, lnb_ref,
                 wo_ref, o_ref, hp_ref, *, geo):
    i = pl.program_id(0)
    p, n = geo.pos(i)
    has_prev = p > 0
    has_next = p < n - 1

    def glu(z):
        return z[:, :D_CONV] * _sigmoid(z[:, D_CONV:])

    hp_ref[0:CONV_HALO, :] = jnp.where(has_prev, glu(prev_ref[...]), 0.0)
    hp_ref[CONV_HALO:CONV_HALO + TILE, :] = glu(cur_ref[...])
    hp_ref[CONV_HALO + TILE:, :] = jnp.where(has_next, glu(next_ref[...]), 0.0)

    off = CONV_HALO - CONV_K // 2
    acc = hp_ref[pl.ds(off, TILE), :] * cw_ref[0:1, :]
    for j in range(1, CONV_K):
        acc = acc + hp_ref[pl.ds(off + j, TILE), :] * cw_ref[j:j + 1, :]
    h = acc + cb_ref[...]
    mu = jnp.mean(h, axis=-1, keepdims=True)
    hc = h - mu
    var = jnp.mean(hc * hc, axis=-1, keepdims=True)
    h = hc * lax.rsqrt(var + EPS) * lng_ref[...] + lnb_ref[...]
    h = h * _sigmoid(h)
    y = jnp.dot(h.astype(BF16), wo_ref[...], preferred_element_type=F32)
    o_ref[...] = _sigmoid(gate_ref[...]) * y


def conv_branch(u, cw, cb, lng, lnb, wo, geo):
    n = u.shape[0]
    hb = TILE // CONV_HALO
    nhalo = n // CONV_HALO
    cwp = jnp.zeros((32, D_CONV), F32).at[:CONV_K].set(cw)
    return pl.pallas_call(
        functools.partial(_conv_kernel, geo=geo),
        grid=(geo.ntile,),
        in_specs=[
            pl.BlockSpec((TILE, 2 * D_CONV), lambda i: (i, C_CONV // (2 * D_CONV))),
            pl.BlockSpec((CONV_HALO, 2 * D_CONV), lambda i: (jnp.maximum(i * hb - 1, 0), 0)),
            pl.BlockSpec((CONV_HALO, 2 * D_CONV), lambda i: (jnp.minimum((i + 1) * hb, nhalo - 1), 0)),
            pl.BlockSpec((TILE, D_MODEL), lambda i: (i, C_G0 // D_MODEL)),
            pl.BlockSpec((32, D_CONV), lambda i: (0, 0)),
            pl.BlockSpec((1, D_CONV), lambda i: (0, 0)),
            pl.BlockSpec((1, D_CONV), lambda i: (0, 0)),
            pl.BlockSpec((1, D_CONV), lambda i: (0, 0)),
            pl.BlockSpec((D_CONV, D_MODEL), lambda i: (0, 0)),
        ],
        out_specs=pl.BlockSpec((TILE, D_MODEL), lambda i: (i, 0)),
        out_shape=jax.ShapeDtypeStruct((n, D_MODEL), F32),
        scratch_shapes=[pltpu.VMEM((TILE + 2 * CONV_HALO, D_CONV), F32)],
        compiler_params=_cparams(("arbitrary",)),
        name="conv_branch",
    )(u, u, u, u, cwp, cb.reshape(1, -1), lng.reshape(1, -1), lnb.reshape(1, -1), wo.astype(BF16))


def _seg_sum(x, ones_bd):
    return jnp.dot(x, ones_bd, preferred_element_type=F32, precision=lax.Precision.HIGHEST)


def _rwkv_pre_kernel(r_ref, rp_ref, rn_ref, k_ref, kp_ref, kn_ref, v_ref, vp_ref, vn_ref,
                     lora_ref, mu_ref, w0_ref, bw_ref, a0_ref, ba_ref, bg_ref, xi_ref, al_ref,
                     rho_ref, ones_ref,
                     r_o, kk_o, v_o, w0_o, w1_o, ka0_o, ka1_o, kt0_o, kt1_o, g_o, bonus_o, *, geo):
    i = pl.program_id(0)
    p, n = geo.pos(i)
    has_prev = p > 0
    has_next = p < n - 1
    row = lax.broadcasted_iota(jnp.int32, (TILE, D_RWKV), 0)

    def shifted(c_ref, p_ref, n_ref, mu):
        cur = c_ref[...]
        pv = jnp.where(has_prev, p_ref[SHIFT_HALO - 1:SHIFT_HALO, :], 0.0)
        nx = jnp.where(has_next, n_ref[0:1, :], 0.0)
        prev = jnp.where(row == 0, pv, pltpu.roll(cur, 1, axis=0))
        nxt = jnp.where(row == TILE - 1, nx, pltpu.roll(cur, TILE - 1, axis=0))
        return cur + mu * (0.5 * (prev + nxt) - cur)

    r = shifted(r_ref, rp_ref, rn_ref, mu_ref[0:1, :])
    k = shifted(k_ref, kp_ref, kn_ref, mu_ref[1:2, :])
    v = shifted(v_ref, vp_ref, vn_ref, mu_ref[2:3, :])

    lora = lora_ref[...]
    dw = jnp.tanh(lora[:, :W_RANK]).astype(BF16)
    da = lora[:, W_RANK:W_RANK + A_RANK].astype(BF16)
    dg = _sigmoid(lora[:, W_RANK + A_RANK:]).astype(BF16)
    wl = jnp.dot(dw, bw_ref[...], preferred_element_type=F32)
    al = jnp.dot(da, ba_ref[...], preferred_element_type=F32)
    g = jnp.dot(dg, bg_ref[...], preferred_element_type=F32)

    ones_bd = ones_ref[...]
    kx = k * xi_ref[...]
    kk = kx * lax.rsqrt(_seg_sum(kx * kx, ones_bd) + EPS)
    alpha = al_ref[...]
    rho = rho_ref[...]
    bonus = jnp.zeros((TILE, D_RWKV), F32)
    for d, (w_o, ka_o, kt_o) in enumerate(((w0_o, ka0_o, kt0_o), (w1_o, ka1_o, kt1_o))):
        sl = slice(d * D_RWKV, (d + 1) * D_RWKV)
        w = jnp.exp(-DECAY_SCALE * _sigmoid(w0_ref[d:d + 1, :] + wl[:, sl]))
        a = _sigmoid(a0_ref[d:d + 1, :] + al[:, sl])
        kt = k * (1.0 + (a - 1.0) * alpha)
        w_o[...] = w
        ka_o[...] = kk * a
        kt_o[...] = kt
        bonus = bonus + _seg_sum(r * kt * rho, ones_bd) * v
    r_o[...] = r
    kk_o[...] = kk
    v_o[...] = v
    g_o[...] = g
    bonus_o[...] = bonus


def rwkv_pre(u, P, l, geo, ones_bd):
    n = u.shape[0]
    hb = TILE // SHIFT_HALO
    nhalo = n // SHIFT_HALO

    def trio(c0):
        cb = c0 // D_RWKV
        return [
            pl.BlockSpec((TILE, D_RWKV), lambda i: (i, cb)),
            pl.BlockSpec((SHIFT_HALO, D_RWKV), lambda i: (jnp.maximum(i * hb - 1, 0), cb)),
            pl.BlockSpec((SHIFT_HALO, D_RWKV), lambda i: (jnp.minimum((i + 1) * hb, nhalo - 1), cb)),
        ]

    def full(shape):
        return pl.BlockSpec(shape, lambda i: tuple(0 for _ in shape))

    bw = jnp.concatenate([P["rwkv_bw"][l, 0], P["rwkv_bw"][l, 1]], axis=1).astype(BF16)
    ba = jnp.concatenate([P["rwkv_ba"][l, 0], P["rwkv_ba"][l, 1]], axis=1).astype(BF16)
    out = jax.ShapeDtypeStruct((n, D_RWKV), F32)
    return pl.pallas_call(
        functools.partial(_rwkv_pre_kernel, geo=geo),
        grid=(geo.ntile,),
        in_specs=trio(C_R) + trio(C_K) + trio(C_V) + [
            pl.BlockSpec((TILE, 256), lambda i: (i, C_LORA // 256)),
            full((3, D_RWKV)), full((2, D_RWKV)), full((W_RANK, 2 * D_RWKV)),
            full((2, D_RWKV)), full((A_RANK, 2 * D_RWKV)), full((G_RANK, D_RWKV)),
            full((1, D_RWKV)), full((1, D_RWKV)), full((1, D_RWKV)), full((D_RWKV, D_RWKV)),
        ],
        out_specs=[pl.BlockSpec((TILE, D_RWKV), lambda i: (i, 0))] * 11,
        out_shape=[out] * 11,
        compiler_params=_cparams(("arbitrary",)),
        name="rwkv_pre",
    )(u, u, u, u, u, u, u, u, u, u,
      P["rwkv_mu"][l], P["rwkv_w0"][l], bw, P["rwkv_a0"][l], ba, P["rwkv_bg"][l].astype(BF16),
      P["rwkv_xi"][l].reshape(1, -1), P["rwkv_alpha"][l].reshape(1, -1),
      P["rwkv_rho"][l].reshape(1, -1), ones_bd)


N_VBLK = RWKV_HEAD // SUBLANES


def _scan_steps(xrow, vload, ostore, s_ref, tidx, *, tc, nk, kl):
    groups = (tuple(range(0, N_VBLK // 2)), tuple(range(N_VBLK // 2, N_VBLK)))

    def allred(a):
        out = a
        for j in range(1, kl):
            out = out + pltpu.roll(a, j * (LANES // kl), axis=1)
        return out

    def first_sa(grp, t):
        acc = [None] * len(grp)
        for kh in range(nk):
            kkb = xrow(t, 1, kh)
            for j, vb in enumerate(grp):
                pr = s_ref[vb, kh] * kkb
                acc[j] = pr if acc[j] is None else acc[j] + pr
        return tuple(allred(a) for a in acc)

    def fused(grp, s, t, t_next, sa):
        vv = [vload(t, vb) for vb in grp]
        oacc = [None] * len(grp)
        acc = [None] * len(grp)
        for kh in range(nk):
            wb, kab, ktb, rb = xrow(t, 0, kh), xrow(t, 2, kh), xrow(t, 3, kh), xrow(t, 4, kh)
            kkn = xrow(t_next, 1, kh)
            for j, vb in enumerate(grp):
                sn = s_ref[vb, kh] * wb - sa[j] * kab + vv[j] * ktb
                s_ref[vb, kh] = sn
                po = sn * rb
                pa = sn * kkn
                oacc[j] = po if oacc[j] is None else oacc[j] + po
                acc[j] = pa if acc[j] is None else acc[j] + pa
        for j, vb in enumerate(grp):
            ostore(s, t, vb, oacc[j])
        return tuple(allred(a) for a in acc)

    t0 = tidx(0)
    carry0 = tuple(first_sa(g, t0) for g in groups)

    def step(s, carry):
        t = tidx(s)
        t_next = tidx(jnp.minimum(s + 1, tc - 1))
        return tuple(fused(g, s, t, t_next, carry[i]) for i, g in enumerate(groups))

    lax.fori_loop(0, tc, step, carry0)


def _vrows(vb):
    return pl.ds(vb * SUBLANES, SUBLANES)


def _scan_ctx_kernel(rev_ref, x_ref, v_ref, s0_ref, o_ref, sfin_ref, s_ref, *, tc, nk):
    gb = pl.program_id(0)
    c = pl.program_id(1)
    rev = rev_ref[gb]

    @pl.when(c == 0)
    def _():
        s_ref[...] = s0_ref[0]

    def xrow(t, a, kh):
        return jnp.broadcast_to(x_ref[0, t, a, pl.ds(kh, 1), :], (SUBLANES, LANES))

    def vload(t, vb):
        return v_ref[0, t, _vrows(vb), :]

    def ostore(s, t, vb, val):
        o_ref[0, t, _vrows(vb), :] = val

    def tidx(s):
        return jnp.where(rev == 1, tc - 1 - s, s)

    _scan_steps(xrow, vload, ostore, s_ref, tidx, tc=tc, nk=nk, kl=1)

    @pl.when(c == pl.num_programs(1) - 1)
    def _():
        sfin_ref[0] = s_ref[...]


def wkv_scan_ctx(x, v, s0, rev, *, tc):
    ngb, t_len, _, nk, _ = x.shape
    nch = t_len // tc

    def tmap(g, c, rev_ref):
        return c + rev_ref[g] * (nch - 1 - 2 * c)

    return pl.pallas_call(
        functools.partial(_scan_ctx_kernel, tc=tc, nk=nk),
        grid_spec=pltpu.PrefetchScalarGridSpec(
            num_scalar_prefetch=1,
            grid=(ngb, nch),
            in_specs=[
                pl.BlockSpec((1, tc, 5, nk, LANES), lambda g, c, r: (g, tmap(g, c, r), 0, 0, 0)),
                pl.BlockSpec((1, tc, RWKV_HEAD, LANES), lambda g, c, r: (g, tmap(g, c, r), 0, 0)),
                pl.BlockSpec((1, N_VBLK, nk, SUBLANES, LANES), lambda g, c, r: (g, 0, 0, 0, 0)),
            ],
            out_specs=[
                pl.BlockSpec((1, tc, RWKV_HEAD, LANES), lambda g, c, r: (g, tmap(g, c, r), 0, 0)),
                pl.BlockSpec((1, N_VBLK, nk, SUBLANES, LANES), lambda g, c, r: (g, 0, 0, 0, 0)),
            ],
            scratch_shapes=[pltpu.VMEM((N_VBLK, nk, SUBLANES, LANES), F32)],
        ),
        out_shape=[
            jax.ShapeDtypeStruct((ngb, t_len, RWKV_HEAD, LANES), F32),
            jax.ShapeDtypeStruct((ngb, N_VBLK, nk, SUBLANES, LANES), F32),
        ],
        compiler_params=_cparams(("arbitrary", "arbitrary")),
        name="wkv_scan_ctx",
    )(rev, x, v, s0)


def _scan_lat_kernel(xa_ref, xb_ref, va_ref, vb_ref, s0_ref, oa_ref, ob_ref, s_ref, xm_ref, vm_ref,
                     *, tc, nk, kl):
    c = pl.program_id(0)

    @pl.when(c == 0)
    def _():
        s_ref[...] = s0_ref[...]

    def is_bwd(shape):
        lane = lax.broadcasted_iota(jnp.int32, shape, len(shape) - 1)
        return (lane // N_RWKV_HEADS) % 2 == 1

    mx = is_bwd((5, nk, LANES))
    mv = is_bwd((RWKV_HEAD, LANES))

    def merge(s, carry):
        xm_ref[s] = jnp.where(mx, xb_ref[tc - 1 - s], xa_ref[s])
        vm_ref[s] = jnp.where(mv, vb_ref[tc - 1 - s], va_ref[s])
        return carry

    lax.fori_loop(0, tc, merge, 0)

    def xrow(t, a, kh):
        return jnp.broadcast_to(xm_ref[t, a, pl.ds(kh, 1), :], (SUBLANES, LANES))

    def vload(t, vb):
        return vm_ref[t, _vrows(vb), :]

    def ostore(s, t, vb, val):
        oa_ref[s, _vrows(vb), :] = val
        ob_ref[tc - 1 - s, _vrows(vb), :] = val

    _scan_steps(xrow, vload, ostore, s_ref, lambda s: s, tc=tc, nk=nk, kl=kl)


def wkv_scan_lat(x, v, s0, *, tc, kl):
    t_len, _, nk, _ = x.shape
    nch = t_len // tc
    o_sds = jax.ShapeDtypeStruct((t_len, RWKV_HEAD, LANES), F32)
    return pl.pallas_call(
        functools.partial(_scan_lat_kernel, tc=tc, nk=nk, kl=kl),
        grid=(nch,),
        in_specs=[
            pl.BlockSpec((tc, 5, nk, LANES), lambda c: (c, 0, 0, 0)),
            pl.BlockSpec((tc, 5, nk, LANES), lambda c: (nch - 1 - c, 0, 0, 0)),
            pl.BlockSpec((tc, RWKV_HEAD, LANES), lambda c: (c, 0, 0)),
            pl.BlockSpec((tc, RWKV_HEAD, LANES), lambda c: (nch - 1 - c, 0, 0)),
            pl.BlockSpec((N_VBLK, nk, SUBLANES, LANES), lambda c: (0, 0, 0, 0)),
        ],
        out_specs=[
            pl.BlockSpec((tc, RWKV_HEAD, LANES), lambda c: (c, 0, 0)),
            pl.BlockSpec((tc, RWKV_HEAD, LANES), lambda c: (nch - 1 - c, 0, 0)),
        ],
        out_shape=[o_sds, o_sds],
        scratch_shapes=[pltpu.VMEM((N_VBLK, nk, SUBLANES, LANES), F32),
                        pltpu.VMEM((tc, 5, nk, LANES), F32),
                        pltpu.VMEM((tc, RWKV_HEAD, LANES), F32)],
        compiler_params=_cparams(("arbitrary",)),
        name="wkv_scan_lat",
    )(x, x, v, v, s0)


def _rwkv_post_kernel(of_ref, ob_ref, bonus_ref, g_ref, gate_ref, gng_ref, gnb_ref, ones_ref,
                      wo_ref, y_ref):
    o = of_ref[0] + ob_ref[0]
    ones_bd = ones_ref[...]
    mean = _seg_sum(o, ones_bd) * (1.0 / RWKV_HEAD)
    oc = o - mean
    var = _seg_sum(oc * oc, ones_bd) * (1.0 / RWKV_HEAD)
    gn = oc * lax.rsqrt(var + GN_EPS) * gng_ref[...] + gnb_ref[...]
    y = (gn + bonus_ref[...]) * g_ref[...]
    y = jnp.dot(y.astype(BF16), wo_ref[...], preferred_element_type=F32)
    y_ref[...] = _sigmoid(gate_ref[...]) * y


def rwkv_post(o2, bonus, g, u, gng, gnb, ones_bd, wo, geo):
    n = bonus.shape[0]

    def full(shape):
        return pl.BlockSpec(shape, lambda i: tuple(0 for _ in shape))

    return pl.pallas_call(
        _rwkv_post_kernel,
        grid=(geo.ntile,),
        in_specs=[
            pl.BlockSpec((1, TILE, D_RWKV), lambda i: (0, i, 0)),
            pl.BlockSpec((1, TILE, D_RWKV), lambda i: (1, i, 0)),
            pl.BlockSpec((TILE, D_RWKV), lambda i: (i, 0)),
            pl.BlockSpec((TILE, D_RWKV), lambda i: (i, 0)),
            pl.BlockSpec((TILE, D_MODEL), lambda i: (i, C_G1 // D_MODEL)),
            full((1, D_RWKV)), full((1, D_RWKV)), full((D_RWKV, D_RWKV)), full((D_RWKV, D_MODEL)),
        ],
        out_specs=pl.BlockSpec((TILE, D_MODEL), lambda i: (i, 0)),
        out_shape=jax.ShapeDtypeStruct((n, D_MODEL), F32),
        compiler_params=_cparams(("arbitrary",)),
        name="rwkv_post",
    )(o2, o2, bonus, g, u, gng.reshape(1, -1), gnb.reshape(1, -1), ones_bd, wo.astype(BF16))


QK_W = 2 * LANES
Q_SCALE = ATTN_SCALE * math.log2(math.e)


def _mla_pre_kernel(cq_ref, ckv_ref, kr_ref, cos_ref, sin_ref, qg_ref, kvg_ref, wuq_ref, wukt_ref,
                    qf_o, ckvn_o, krr_o):
    nq = N_MLA_HEADS * NOPE_DIM
    nrp = N_MLA_HEADS * LANES
    cq = _rms(cq_ref[...], qg_ref[...])
    q = jnp.dot(cq.astype(BF16), wuq_ref[...], preferred_element_type=F32)
    cos = cos_ref[...]
    sin = sin_ref[...]
    for h in range(N_MLA_HEADS):
        qn = q[:, h * NOPE_DIM:(h + 1) * NOPE_DIM].astype(BF16)
        qa = jnp.dot(qn, wukt_ref[h], preferred_element_type=F32)
        qr = (q[:, nq + h * LANES:nq + (h + 1) * LANES] * cos
              + q[:, nq + nrp + h * LANES:nq + nrp + (h + 1) * LANES] * sin)
        qf_o[:, h * QK_W:h * QK_W + LANES] = (qa * Q_SCALE).astype(qf_o.dtype)
        qf_o[:, h * QK_W + LANES:(h + 1) * QK_W] = (qr * Q_SCALE).astype(qf_o.dtype)
    ckvn_o[...] = _rms(ckv_ref[...], kvg_ref[...])
    kr = kr_ref[...]
    krr = kr[:, ROPE_DIM:2 * ROPE_DIM] * cos[:, :ROPE_DIM] + kr[:, 2 * ROPE_DIM:3 * ROPE_DIM] * sin[:, :ROPE_DIM]
    krr_o[...] = jnp.concatenate([krr, jnp.zeros((TILE, LANES - ROPE_DIM), F32)], axis=1)


def mla_pre(u, cos_t, sin_t, qg, kvg, wuq_p, wukt, geo):
    n = u.shape[0]

    def full(shape):
        return pl.BlockSpec(shape, lambda i: tuple(0 for _ in shape))

    return pl.pallas_call(
        _mla_pre_kernel,
        grid=(geo.ntile,),
        in_specs=[
            pl.BlockSpec((TILE, Q_RANK), lambda i: (i, C_CQ // Q_RANK)),
            pl.BlockSpec((TILE, KV_RANK), lambda i: (i, C_CKV // KV_RANK)),
            pl.BlockSpec((TILE, LANES), lambda i: (i, C_KR // LANES)),
            pl.BlockSpec((TILE, LANES), lambda i: (geo.rope_blk(i), 0)),
            pl.BlockSpec((TILE, LANES), lambda i: (geo.rope_blk(i), 0)),
            full((1, Q_RANK)), full((1, KV_RANK)),
            full(wuq_p.shape), full(wukt.shape),
        ],
        out_specs=[
            pl.BlockSpec((TILE, N_MLA_HEADS * QK_W), lambda i: (i, 0)),
            pl.BlockSpec((TILE, KV_RANK), lambda i: (i, 0)),
            pl.BlockSpec((TILE, LANES), lambda i: (i, 0)),
        ],
        out_shape=[
            jax.ShapeDtypeStruct((n, N_MLA_HEADS * QK_W), BF16),
            jax.ShapeDtypeStruct((n, KV_RANK), F32),
            jax.ShapeDtypeStruct((n, LANES), F32),
        ],
        compiler_params=_cparams(("arbitrary",)),
        name="mla_pre",
    )(u, u, u, cos_t, sin_t, qg.reshape(1, -1), kvg.reshape(1, -1), wuq_p, wukt)


def _attn_kernel(*refs, aliased):
    q_ref, kt_ref, c_ref, wuv_ref, wo_ref, gate_ref = refs[:6]
    y_ref, acc_ref = refs[-2:]
    h = pl.program_id(2)

    @pl.when(h == 0)
    def _():
        acc_ref[...] = jnp.zeros(acc_ref.shape, F32)

    s = jnp.dot(q_ref[...], kt_ref[0], preferred_element_type=F32)
    p = jnp.exp2(s - jnp.max(s, axis=-1, keepdims=True))
    l = jnp.sum(p, axis=-1, keepdims=True)
    pc = jnp.dot(p.astype(BF16), c_ref[0], preferred_element_type=F32) / l
    oh = jnp.dot(pc.astype(BF16), wuv_ref[0], preferred_element_type=F32)
    acc_ref[...] += jnp.dot(oh.astype(BF16), wo_ref[0], preferred_element_type=F32)

    @pl.when(h == pl.num_programs(2) - 1)
    def _():
        y_ref[...] = _sigmoid(gate_ref[...]) * acc_ref[...]


def attention(qf, kct, cv, wuv_h, wo_h, u, y_prev, *, row0, t_seq, tq, name):
    n = qf.shape[0]
    nseq, _, t_k = kct.shape
    qt = t_seq // tq
    rb0 = row0 // tq

    def rows(w, col):
        return pl.BlockSpec((tq, w), lambda s, i, h: (rb0 + s * qt + i, col(h)))

    in_specs = [
        rows(QK_W, lambda h: h),
        pl.BlockSpec((1, QK_W, t_k), lambda s, i, h: (s, 0, 0)),
        pl.BlockSpec((1, t_k, KV_RANK), lambda s, i, h: (s, 0, 0)),
        pl.BlockSpec((1, KV_RANK, V_DIM), lambda s, i, h: (h, 0, 0)),
        pl.BlockSpec((1, V_DIM, D_MODEL), lambda s, i, h: (h, 0, 0)),
        rows(D_MODEL, lambda h: C_G2 // D_MODEL),
    ]
    args = [qf, kct, cv, wuv_h, wo_h, u]
    aliases = {}
    if y_prev is not None:
        in_specs.append(pl.BlockSpec(memory_space=pl.ANY))
        args.append(y_prev)
        aliases = {len(args) - 1: 0}
    return pl.pallas_call(
        functools.partial(_attn_kernel, aliased=y_prev is not None),
        grid=(nseq, qt, N_MLA_HEADS),
        in_specs=in_specs,
        out_specs=rows(D_MODEL, lambda h: 0),
        out_shape=jax.ShapeDtypeStruct((n, D_MODEL), F32),
        scratch_shapes=[pltpu.VMEM((tq, D_MODEL), F32)],
        input_output_aliases=aliases,
        compiler_params=_cparams(("arbitrary", "arbitrary", "arbitrary")),
        name=name,
    )(*args)


def _merge_kernel(x_ref, yc_ref, yr_ref, ym_ref, wout_ref, gpost_ref, g1_ref, gpre_ref, sc_ref,
                  sh_ref, router_ref, x1_o, h2_o, comb_o, *, moe):
    m = yc_ref[...] + yr_ref[...] + ym_ref[...]
    y = jnp.dot(m.astype(BF16), wout_ref[...], preferred_element_type=F32)
    x1 = x_ref[...] + g1_ref[0] * _rms(y, gpost_ref[...])
    x1_o[...] = x1
    h2 = _rms(x1, gpre_ref[...]) * (1.0 + sc_ref[0]) + sh_ref[0]
    h2_o[...] = h2.astype(h2_o.dtype)
    if moe:
        logits = jnp.dot(h2, router_ref[...], preferred_element_type=F32,
                         precision=lax.Precision.HIGHEST)
        lane = lax.broadcasted_iota(jnp.int32, logits.shape, 1)
        neg = jnp.float32(-jnp.inf)
        logits = jnp.where(lane < N_EXPERTS, logits, neg)
        m1 = jnp.max(logits, axis=-1, keepdims=True)
        i1 = jnp.min(jnp.where(logits == m1, lane, LANES), axis=-1, keepdims=True)
        rest = jnp.where(lane == i1, neg, logits)
        m2 = jnp.max(rest, axis=-1, keepdims=True)
        i2 = jnp.min(jnp.where(rest == m2, lane, LANES), axis=-1, keepdims=True)
        e2 = jnp.exp(m2 - m1)
        den = 1.0 + e2
        comb_o[...] = jnp.where(lane == i1, 1.0 / den, 0.0) + jnp.where(lane == i2, e2 / den, 0.0)
    else:
        comb_o[...] = jnp.ones(comb_o.shape, F32)


def merge(x, yc, yr, ym, wout, gpost, gpre, mod, router, geo, moe):
    n, d = x.shape

    def full(shape):
        return pl.BlockSpec(shape, lambda i: tuple(0 for _ in shape))

    def rows():
        return pl.BlockSpec((TILE, d), lambda i: (i, 0))

    def modspec(j):
        return pl.BlockSpec((1, 1, d), lambda i: (geo.mod_row(i) * 6 + j, 0, 0))

    router_p = jnp.zeros((d, LANES), F32).at[:, :N_EXPERTS].set(router)
    return pl.pallas_call(
        functools.partial(_merge_kernel, moe=moe),
        grid=(geo.ntile,),
        in_specs=[rows(), rows(), rows(), rows(), full((d, d)), full((1, d)), modspec(2),
                  full((1, d)), modspec(4), modspec(3), full((d, LANES))],
        out_specs=[rows(), rows(), pl.BlockSpec((TILE, LANES), lambda i: (i, 0))],
        out_shape=[jax.ShapeDtypeStruct((n, d), F32), jax.ShapeDtypeStruct((n, d), BF16),
                   jax.ShapeDtypeStruct((n, LANES), F32)],
        compiler_params=_cparams(("arbitrary",)),
        name="merge",
    )(x, yc, yr, ym, wout.astype(BF16), gpost.reshape(1, d), mod, gpre.reshape(1, d), mod, mod,
      router_p)


def _ffn_kernel(h_ref, comb_ref, w1_ref, w3_ref, w2_ref, o_ref, acc_ref):
    e = pl.program_id(1)
    f = pl.program_id(2)

    @pl.when((e == 0) & (f == 0))
    def _():
        acc_ref[...] = jnp.zeros(acc_ref.shape, F32)

    h = h_ref[...]
    a = jnp.dot(h, w1_ref[0], preferred_element_type=F32)
    b = jnp.dot(h, w3_ref[0], preferred_element_type=F32)
    comb = comb_ref[...]
    lane = lax.broadcasted_iota(jnp.int32, comb.shape, 1)
    ce = jnp.sum(jnp.where(lane == e, comb, 0.0), axis=-1, keepdims=True)
    act = (a * _sigmoid(a) * b) * ce
    acc_ref[...] += jnp.dot(act.astype(BF16), w2_ref[0], preferred_element_type=F32)

    @pl.when((e == pl.num_programs(1) - 1) & (f == pl.num_programs(2) - 1))
    def _():
        o_ref[...] = acc_ref[...]


def ffn(h, comb, w1, w3, w2, *, tm, tf):
    n, d = h.shape
    ne, _, dff = w1.shape
    return pl.pallas_call(
        _ffn_kernel,
        grid=(n // tm, ne, dff // tf),
        in_specs=[
            pl.BlockSpec((tm, d), lambda i, e, f: (i, 0)),
            pl.BlockSpec((tm, LANES), lambda i, e, f: (i, 0)),
            pl.BlockSpec((1, d, tf), lambda i, e, f: (e, 0, f)),
            pl.BlockSpec((1, d, tf), lambda i, e, f: (e, 0, f)),
            pl.BlockSpec((1, tf, d), lambda i, e, f: (e, f, 0)),
        ],
        out_specs=pl.BlockSpec((tm, d), lambda i, e, f: (i, 0)),
        out_shape=jax.ShapeDtypeStruct((n, d), F32),
        scratch_shapes=[pltpu.VMEM((tm, d), F32)],
        compiler_params=_cparams(("arbitrary", "arbitrary", "arbitrary")),
        name="ffn",
    )(h, comb, w1, w3, w2)


def _ffn_post_kernel(x_ref, y_ref, g_ref, g2_ref, o_ref):
    o_ref[...] = x_ref[...] + g2_ref[0] * _rms(y_ref[...], g_ref[...])


def ffn_post(x1, y, gpost, mod, geo):
    n, d = x1.shape
    return pl.pallas_call(
        _ffn_post_kernel,
        grid=(geo.ntile,),
        in_specs=[
            pl.BlockSpec((TILE, d), lambda i: (i, 0)),
            pl.BlockSpec((TILE, d), lambda i: (i, 0)),
            pl.BlockSpec((1, d), lambda i: (0, 0)),
            pl.BlockSpec((1, 1, d), lambda i: (geo.mod_row(i) * 6 + 5, 0, 0)),
        ],
        out_specs=pl.BlockSpec((TILE, d), lambda i: (i, 0)),
        out_shape=jax.ShapeDtypeStruct((n, d), F32),
        compiler_params=_cparams(("arbitrary",)),
        name="ffn_post",
    )(x1, y, gpost.reshape(1, d), mod)


def _scan_layouts(geo, pre, state_l):
    r, kk, v, w0, w1, ka0, ka1, kt0, kt1 = pre
    H, K = N_RWKV_HEADS, RWKV_HEAD
    nc, nl = geo.nc_tok, geo.nl_tok
    bl = LANES // H
    nbh = geo.n_ctx // bl

    def ctx_l(a):
        a = a[:nc].reshape(nbh, bl, geo.t_ctx, H, K)
        return a.transpose(0, 2, 4, 1, 3).reshape(nbh, geo.t_ctx, K, LANES)

    xs = []
    for (w, ka, kt) in ((w0, ka0, kt0), (w1, ka1, kt1)):
        xs.append(jnp.stack([ctx_l(w), ctx_l(kk), ctx_l(ka), ctx_l(kt), ctx_l(r)], axis=2))
    x_ctx = jnp.concatenate(xs, axis=0)
    v_ctx = jnp.tile(ctx_l(v), (2, 1, 1, 1))
    rev_ctx = jnp.concatenate([jnp.zeros((nbh,), jnp.int32), jnp.ones((nbh,), jnp.int32)])

    kl = LANES // (geo.n_lat * 2 * H)
    nk = K // kl

    def lat_pair(a0, a1):
        a0 = a0[nc:].reshape(geo.n_lat, geo.t_lat, H, K)
        a1 = a1[nc:].reshape(geo.n_lat, geo.t_lat, H, K)
        a = jnp.stack([a0, a1], axis=2).reshape(geo.n_lat, geo.t_lat, 2, H, kl, nk)
        return a.transpose(1, 5, 4, 0, 2, 3).reshape(geo.t_lat, nk, LANES)

    x_lat = jnp.stack([lat_pair(w0, w1), lat_pair(kk, kk), lat_pair(ka0, ka1), lat_pair(kt0, kt1),
                       lat_pair(r, r)], axis=1)
    vl = v[nc:].reshape(geo.n_lat, geo.t_lat, H, K)
    vl = jnp.stack([vl, vl], axis=2)
    vl = vl.transpose(1, 4, 0, 2, 3).reshape(geo.t_lat, K, LANES // kl)
    v_lat = jnp.tile(vl, (1, 1, kl))
    s0 = state_l.reshape(geo.n_lat, 2, H, K // SUBLANES, SUBLANES, kl, nk)
    s0_lat = s0.transpose(3, 6, 4, 5, 0, 1, 2).reshape(K // SUBLANES, nk, SUBLANES, LANES)
    return x_ctx, v_ctx, rev_ctx, x_lat, v_lat, s0_lat, kl


def _scan_outputs(geo, o_ctx, sfin_ctx, oa_lat, ob_lat, kl):
    H, K = N_RWKV_HEADS, RWKV_HEAD
    bl = LANES // H
    nbh = geo.n_ctx // bl
    oc = o_ctx.reshape(2, nbh, geo.t_ctx, K, bl, H).transpose(0, 1, 4, 2, 5, 3)
    oc = oc.reshape(2, geo.nc_tok, H * K)

    def lat_dir(o, d):
        o = o.reshape(geo.t_lat, K, kl, geo.n_lat, 2, H)[:, :, :, :, d].sum(axis=2)
        return o.transpose(2, 0, 3, 1).reshape(geo.nl_tok, H * K)

    ol = jnp.stack([lat_dir(oa_lat, 0), lat_dir(ob_lat, 1)], axis=0)
    o2 = jnp.concatenate([oc, ol], axis=1)
    sf = sfin_ctx.reshape(2, nbh, K // SUBLANES, K, SUBLANES, bl, H)
    sf = sf.transpose(1, 5, 0, 6, 2, 4, 3).reshape(geo.n_ctx, 2, H, K, K)
    return o2, sf


def _rope_tables(geo):
    n_freq = ROPE_DIM // 4
    rows = geo.t_lat // GRID_W
    row = jnp.repeat(jnp.arange(rows, dtype=F32), GRID_W)
    col = jnp.tile(jnp.arange(GRID_W, dtype=F32), rows)
    inv = ROPE_BASE ** (-jnp.arange(n_freq, dtype=F32) / n_freq)
    ang = jnp.concatenate([row[:, None] * inv, col[:, None] * inv], axis=-1)
    cos, sin = jnp.cos(ang), jnp.sin(ang)
    cos32 = jnp.concatenate([cos, cos], axis=-1)
    sin32 = jnp.concatenate([-sin, sin], axis=-1)
    cos_t = jnp.concatenate([jnp.ones((TILE, ROPE_DIM), F32), cos32], axis=0)
    sin_t = jnp.concatenate([jnp.zeros((TILE, ROPE_DIM), F32), sin32], axis=0)
    pad = ((0, 0), (0, LANES - ROPE_DIM))
    return jnp.pad(cos_t, pad), jnp.pad(sin_t, pad)


_DEINT = np.concatenate([np.arange(0, ROPE_DIM, 2), np.arange(1, ROPE_DIM, 2)])
_DEINT_SW = np.concatenate([np.arange(1, ROPE_DIM, 2), np.arange(0, ROPE_DIM, 2)])


def _prep_w_in(w_in, b_in):
    offs = np.cumsum([0, 2 * D_CONV, D_RWKV, D_RWKV, D_RWKV, W_RANK, A_RANK, G_RANK, Q_RANK, KV_RANK,
                      ROPE_DIM, N_BRANCH * D_MODEL])
    o_conv, o_r, o_k, o_v, o_dw, o_da, o_dg, o_cq, o_ckv, o_kr, o_gate, _ = offs
    idx = np.zeros((IN_PAD,), np.int32)
    valid = np.zeros((IN_PAD,), bool)

    def put(dst, src):
        idx[dst:dst + len(src)] = src
        valid[dst:dst + len(src)] = True

    put(C_CONV, np.arange(o_conv, o_conv + 2 * D_CONV))
    for b, c in enumerate((C_G0, C_G1, C_G2)):
        put(c, np.arange(o_gate + b * D_MODEL, o_gate + (b + 1) * D_MODEL))
    put(C_R, np.arange(o_r, o_r + D_RWKV))
    put(C_K, np.arange(o_k, o_k + D_RWKV))
    put(C_V, np.arange(o_v, o_v + D_RWKV))
    put(C_LORA, np.arange(o_dw, o_dw + W_RANK + A_RANK + G_RANK))
    put(C_CQ, np.arange(o_cq, o_cq + Q_RANK))
    put(C_CKV, np.arange(o_ckv, o_ckv + KV_RANK))
    put(C_KR, np.arange(o_kr, o_kr + ROPE_DIM))
    put(C_KR + ROPE_DIM, o_kr + _DEINT)
    put(C_KR + 2 * ROPE_DIM, o_kr + _DEINT_SW)
    w = jnp.where(valid[None, :], jnp.take(w_in, idx, axis=1), 0.0).astype(BF16)
    b = jnp.where(valid, jnp.take(b_in, idx), 0.0)
    return w, b


def _prep_wuq(wuq):
    hd = NOPE_DIM + ROPE_DIM
    nq = N_MLA_HEADS * NOPE_DIM
    ncol = nq + 2 * N_MLA_HEADS * LANES
    idx = np.zeros((ncol,), np.int32)
    valid = np.zeros((ncol,), bool)
    for h in range(N_MLA_HEADS):
        idx[h * NOPE_DIM:(h + 1) * NOPE_DIM] = h * hd + np.arange(NOPE_DIM)
        valid[h * NOPE_DIM:(h + 1) * NOPE_DIM] = True
        for blk, perm in enumerate((_DEINT, _DEINT_SW)):
            c0 = nq + blk * N_MLA_HEADS * LANES + h * LANES
            idx[c0:c0 + ROPE_DIM] = h * hd + NOPE_DIM + perm
            valid[c0:c0 + ROPE_DIM] = True
    return jnp.where(valid[None, :], jnp.take(wuq, idx, axis=1), 0.0).astype(BF16)


def _attention_keys(geo, ckvn, krr, cache_c, cache_kr):
    nc = geo.nc_tok
    pad = QK_W - KV_RANK - ROPE_DIM

    def keys(c, kr):
        kc = jnp.concatenate([c, kr, jnp.zeros(c.shape[:2] + (pad,), F32)], axis=-1).astype(BF16)
        return kc.transpose(0, 2, 1), c.astype(BF16)

    kr32 = krr[:, :ROPE_DIM]
    kct_c, cv_c = keys(ckvn[:nc].reshape(geo.n_ctx, geo.t_ctx, KV_RANK),
                       kr32[:nc].reshape(geo.n_ctx, geo.t_ctx, ROPE_DIM))
    c_l = jnp.concatenate([ckvn[nc:].reshape(geo.n_lat, geo.t_lat, KV_RANK), cache_c], axis=1)
    kr_l = jnp.concatenate([kr32[nc:].reshape(geo.n_lat, geo.t_lat, ROPE_DIM), cache_kr], axis=1)
    kct_l, cv_l = keys(c_l, kr_l)
    return kct_c, cv_c, kct_l, cv_l


def kernel(x_prompt, x_sample, cache_ckv, cache_krope, state_wkv, c, c_ctx, ada_w, ada_b, norm_mix_pre, norm_mix_post, norm_ffn_pre, norm_ffn_post, w_in, b_in, conv_w, conv_b, conv_ln_g, conv_ln_b, conv_wo, rwkv_mu, rwkv_w0, rwkv_bw, rwkv_a0, rwkv_ba, rwkv_bg, rwkv_xi, rwkv_alpha, rwkv_rho, rwkv_gn_g, rwkv_gn_b, rwkv_wo, mla_q_norm, mla_wuq, mla_kv_norm, mla_wuk, mla_wuv, mla_wo, w_out, ffn_w1, ffn_w3, ffn_w2, moe_router, moe_w1, moe_w3, moe_w2):
    P = dict(rwkv_mu=rwkv_mu, rwkv_w0=rwkv_w0, rwkv_bw=rwkv_bw, rwkv_a0=rwkv_a0, rwkv_ba=rwkv_ba,
             rwkv_bg=rwkv_bg, rwkv_xi=rwkv_xi, rwkv_alpha=rwkv_alpha, rwkv_rho=rwkv_rho)
    n_ctx, t_ctx, d = x_prompt.shape
    n_lat, t_lat, _ = x_sample.shape
    depth = ada_w.shape[0]
    past = cache_ckv.shape[2]
    geo = Geo(n_ctx, t_ctx, n_lat, t_lat, past)
    assert d == D_MODEL and (n_ctx * N_RWKV_HEADS) % LANES == 0 and LANES % (n_lat * 2 * N_RWKV_HEADS) == 0

    x = jnp.concatenate([x_prompt.reshape(-1, d), x_sample.reshape(-1, d)], axis=0)

    n_mod = 1 + n_lat
    n_mod_pad = -(-n_mod // SUBLANES) * SUBLANES
    c_all = jnp.zeros((n_mod_pad, d), F32).at[0].set(c_ctx).at[1:n_mod].set(c)
    head_id = np.arange(D_RWKV) // RWKV_HEAD
    ones_bd = jnp.asarray((head_id[:, None] == head_id[None, :]).astype(np.float32))
    cos_t, sin_t = _rope_tables(geo)

    ckv_out, kr_out, st_out = [], [], []
    for l in range(depth):
        mod = matmul_bias(c_all, ada_w[l].astype(BF16), ada_b[l], tm=n_mod_pad, tn=6 * d // 4,
                          pre="silu", name="ada_mod")
        mod = mod[:n_mod].reshape(n_mod * 6, 1, d)

        h = prenorm(x, norm_mix_pre[l], mod, geo, j_shift=0, j_scale=1)
        w_in_p, b_in_p = _prep_w_in(w_in[l], b_in[l])
        u = matmul_bias(h, w_in_p, b_in_p, tm=512, tn=1280, name="in_proj")

        y_conv = conv_branch(u, conv_w[l], conv_b[l], conv_ln_g[l], conv_ln_b[l], conv_wo[l], geo)

        pre = rwkv_pre(u, P, l, geo, ones_bd)
        x_ctx, v_ctx, rev_ctx, x_lat, v_lat, s0_lat, kl = _scan_layouts(geo, pre[:9], state_wkv[:, l])
        s0_ctx = jnp.zeros((x_ctx.shape[0], RWKV_HEAD // SUBLANES, RWKV_HEAD, SUBLANES, LANES), F32)
        o_ctx, sfin_ctx = wkv_scan_ctx(x_ctx, v_ctx, s0_ctx, rev_ctx, tc=min(16, t_ctx))
        oa_lat, ob_lat = wkv_scan_lat(x_lat, v_lat, s0_lat, tc=min(64, t_lat), kl=kl)
        o2, sfin = _scan_outputs(geo, o_ctx, sfin_ctx, oa_lat, ob_lat, kl)
        y_rwkv = rwkv_post(o2, pre[10], pre[9], u, rwkv_gn_g[l], rwkv_gn_b[l], ones_bd, rwkv_wo[l], geo)

        wukt = mla_wuk[l].reshape(KV_RANK, N_MLA_HEADS, NOPE_DIM).transpose(1, 2, 0).astype(BF16)
        qf, ckvn, krr = mla_pre(u, cos_t, sin_t, mla_q_norm[l], mla_kv_norm[l],
                                _prep_wuq(mla_wuq[l]), wukt, geo)
        kct_c, cv_c, kct_l, cv_l = _attention_keys(geo, ckvn, krr, cache_ckv[:, l],
                                                   cache_krope[:, l][..., _DEINT])
        wuv_h = mla_wuv[l].reshape(KV_RANK, N_MLA_HEADS, V_DIM).transpose(1, 0, 2).astype(BF16)
        wo_h = mla_wo[l].reshape(N_MLA_HEADS, V_DIM, d).astype(BF16)
        y_mla = attention(qf, kct_l, cv_l, wuv_h, wo_h, u, None, row0=geo.nc_tok, t_seq=t_lat,
                          tq=TILE, name="attention_lat")
        y_mla = attention(qf, kct_c, cv_c, wuv_h, wo_h, u, y_mla, row0=0, t_seq=t_ctx,
                          tq=TILE, name="attention_ctx")

        moe = (l % 2 == 1)
        i = l // 2
        router = moe_router[i] if moe else jnp.zeros((d, N_EXPERTS), F32)
        x1, h2, comb = merge(x, y_conv, y_rwkv, y_mla, w_out[l], norm_mix_post[l], norm_ffn_pre[l],
                             mod, router, geo, moe)
        if moe:
            y = ffn(h2, comb, moe_w1[i].astype(BF16), moe_w3[i].astype(BF16), moe_w2[i].astype(BF16),
                    tm=512, tf=D_FF // 2)
        else:
            y = ffn(h2, comb, ffn_w1[i:i + 1].astype(BF16), ffn_w3[i:i + 1].astype(BF16),
                    ffn_w2[i:i + 1].astype(BF16), tm=512, tf=D_FF // 2)
        x = ffn_post(x1, y, norm_ffn_post[l], mod, geo)

        ckv_out.append(ckvn[:geo.nc_tok].reshape(n_ctx, t_ctx, KV_RANK))
        kr_out.append(u[:geo.nc_tok, C_KR:C_KR + ROPE_DIM].reshape(n_ctx, t_ctx, ROPE_DIM))
        st_out.append(sfin)

    y_prompt = x[:geo.nc_tok].reshape(n_ctx, t_ctx, d)
    y_sample = x[geo.nc_tok:].reshape(n_lat, t_lat, d)
    return (y_prompt, y_sample, jnp.stack(ckv_out, axis=1), jnp.stack(kr_out, axis=1),
            jnp.stack(st_out, axis=1))
```

```python
import functools
import math

import numpy as np
import jax
import jax.numpy as jnp
from jax import lax
from jax.experimental import pallas as pl
from jax.experimental.pallas import tpu as pltpu

F32 = jnp.float32
BF16 = jnp.bfloat16

D_MODEL = 1024
GRID_W = 64
D_CONV = 512
CONV_K = 31
D_RWKV = 512
RWKV_HEAD = 64
N_RWKV_HEADS = D_RWKV // RWKV_HEAD
W_RANK = 64
A_RANK = 64
G_RANK = 128
DECAY_SCALE = math.exp(-0.5)
GN_EPS = 64e-5
N_MLA_HEADS = 8
Q_RANK = 256
KV_RANK = 128
NOPE_DIM = 64
ROPE_DIM = 32
V_DIM = 64
ROPE_BASE = 10000.0
ATTN_SCALE = 1.0 / math.sqrt(NOPE_DIM + ROPE_DIM)
N_BRANCH = 3
D_FF = 2816
N_EXPERTS = 8
EPS = 1e-6

LANES = 128
SUBLANES = 8
VMEM_LIMIT_BYTES = 56 * 1024 * 1024

TILE = 256
CONV_HALO = 16
SHIFT_HALO = 8

C_CONV, C_G0, C_G1, C_G2 = 0, 1024, 2048, 3072
C_R, C_K, C_V = 4096, 4608, 5120
C_LORA = 5632
C_CQ = 5888
C_CKV = 6144
C_KR = 6272
IN_PAD = 6400


def _cparams(sem):
    return pltpu.CompilerParams(dimension_semantics=sem, vmem_limit_bytes=VMEM_LIMIT_BYTES)


def _sigmoid(x):
    return jax.nn.sigmoid(x)


def _rms(x, g):
    return (x * lax.rsqrt(jnp.mean(x * x, axis=-1, keepdims=True) + EPS)) * g


class Geo:
    def __init__(self, n_ctx, t_ctx, n_lat, t_lat, past):
        assert t_ctx % TILE == 0 and t_lat % TILE == 0
        self.n_ctx, self.t_ctx, self.n_lat, self.t_lat, self.past = n_ctx, t_ctx, n_lat, t_lat, past
        self.ct = t_ctx // TILE
        self.lt = t_lat // TILE
        self.nct = n_ctx * self.ct
        self.nlt = n_lat * self.lt
        self.ntile = self.nct + self.nlt
        self.nc_tok = n_ctx * t_ctx
        self.nl_tok = n_lat * t_lat
        self.ntok = self.nc_tok + self.nl_tok

    def pos(self, i):
        is_ctx = i < self.nct
        p = jnp.where(is_ctx, i % self.ct, (i - self.nct) % self.lt)
        n = jnp.where(is_ctx, self.ct, self.lt)
        return p, n

    def mod_row(self, i):
        return jnp.where(i < self.nct, 0, 1 + (i - self.nct) // self.lt)

    def rope_blk(self, i):
        return jnp.where(i < self.nct, 0, 1 + (i - self.nct) % self.lt)


def _mm_kernel(x_ref, w_ref, b_ref, o_ref, *, pre):
    x = x_ref[...]
    if pre == "silu":
        x = x.astype(F32)
        x = x * _sigmoid(x)
    acc = jnp.dot(x.astype(BF16), w_ref[...], preferred_element_type=F32)
    o_ref[...] = (acc + b_ref[...]).astype(o_ref.dtype)


def matmul_bias(x, w, b, *, tm, tn, pre=None, out_dtype=F32, name="matmul"):
    m, k = x.shape
    n = w.shape[1]
    assert m % tm == 0 and n % tn == 0
    return pl.pallas_call(
        functools.partial(_mm_kernel, pre=pre),
        grid=(n // tn, m // tm),
        in_specs=[
            pl.BlockSpec((tm, k), lambda j, i: (i, 0)),
            pl.BlockSpec((k, tn), lambda j, i: (0, j)),
            pl.BlockSpec((1, tn), lambda j, i: (0, j)),
        ],
        out_specs=pl.BlockSpec((tm, tn), lambda j, i: (i, j)),
        out_shape=jax.ShapeDtypeStruct((m, n), out_dtype),
        compiler_params=_cparams(("arbitrary", "arbitrary")),
        name=name,
    )(x, w, b.reshape(1, n).astype(F32))


def _prenorm_kernel(x_ref, g_ref, sc_ref, sh_ref, o_ref):
    h = _rms(x_ref[...], g_ref[...]) * (1.0 + sc_ref[0]) + sh_ref[0]
    o_ref[...] = h.astype(o_ref.dtype)


def prenorm(x, g, mod, geo, j_shift, j_scale):
    n, d = x.shape
    return pl.pallas_call(
        _prenorm_kernel,
        grid=(geo.ntile,),
        in_specs=[
            pl.BlockSpec((TILE, d), lambda i: (i, 0)),
            pl.BlockSpec((1, d), lambda i: (0, 0)),
            pl.BlockSpec((1, 1, d), lambda i: (geo.mod_row(i) * 6 + j_scale, 0, 0)),
            pl.BlockSpec((1, 1, d), lambda i: (geo.mod_row(i) * 6 + j_shift, 0, 0)),
        ],
        out_specs=pl.BlockSpec((TILE, d), lambda i: (i, 0)),
        out_shape=jax.ShapeDtypeStruct((n, d), BF16),
        compiler_params=_cparams(("arbitrary",)),
        name="prenorm",
    )(x, g.reshape(1, d), mod, mod)


def _conv_kernel(cur_ref, prev_ref, next_ref, gate_ref, cw_ref, cb_ref, lng_ref, lnb_ref,
                 wo_ref, o_ref, hp_ref, *, geo):
    i = pl.program_id(0)
    p, n = geo.pos(i)
    has_prev = p > 0
    has_next = p < n - 1

    def glu(z):
        return z[:, :D_CONV] * _sigmoid(z[:, D_CONV:])

    hp_ref[0:CONV_HALO, :] = jnp.where(has_prev, glu(prev_ref[...]), 0.0)
    hp_ref[CONV_HALO:CONV_HALO + TILE, :] = glu(cur_ref[...])
    hp_ref[CONV_HALO + TILE:, :] = jnp.where(has_next, glu(next_ref[...]), 0.0)

    off = CONV_HALO - CONV_K // 2
    acc = hp_ref[pl.ds(off, TILE), :] * cw_ref[0:1, :]
    for j in range(1, CONV_K):
        acc = acc + hp_ref[pl.ds(off + j, TILE), :] * cw_ref[j:j + 1, :]
    h = acc + cb_ref[...]
    mu = jnp.mean(h, axis=-1, keepdims=True)
    hc = h - mu
    var = jnp.mean(hc * hc, axis=-1, keepdims=True)
    h = hc * lax.rsqrt(var + EPS) * lng_ref[...] + lnb_ref[...]
    h = h * _sigmoid(h)
    y = jnp.dot(h.astype(BF16), wo_ref[...], preferred_element_type=F32)
    o_ref[...] = _sigmoid(gate_ref[...]) * y


def conv_branch(u, cw, cb, lng, lnb, wo, geo):
    n = u.shape[0]
    hb = TILE // CONV_HALO
    nhalo = n // CONV_HALO
    cwp = jnp.zeros((32, D_CONV), F32).at[:CONV_K].set(cw)
    return pl.pallas_call(
        functools.partial(_conv_kernel, geo=geo),
        grid=(geo.ntile,),
        in_specs=[
            pl.BlockSpec((TILE, 2 * D_CONV), lambda i: (i, C_CONV // (2 * D_CONV))),
            pl.BlockSpec((CONV_HALO, 2 * D_CONV), lambda i: (jnp.maximum(i * hb - 1, 0), 0)),
            pl.BlockSpec((CONV_HALO, 2 * D_CONV), lambda i: (jnp.minimum((i + 1) * hb, nhalo - 1), 0)),
            pl.BlockSpec((TILE, D_MODEL), lambda i: (i, C_G0 // D_MODEL)),
            pl.BlockSpec((32, D_CONV), lambda i: (0, 0)),
            pl.BlockSpec((1, D_CONV), lambda i: (0, 0)),
            pl.BlockSpec((1, D_CONV), lambda i: (0, 0)),
            pl.BlockSpec((1, D_CONV), lambda i: (0, 0)),
            pl.BlockSpec((D_CONV, D_MODEL), lambda i: (0, 0)),
        ],
        out_specs=pl.BlockSpec((TILE, D_MODEL), lambda i: (i, 0)),
        out_shape=jax.ShapeDtypeStruct((n, D_MODEL), F32),
        scratch_shapes=[pltpu.VMEM((TILE + 2 * CONV_HALO, D_CONV), F32)],
        compiler_params=_cparams(("arbitrary",)),
        name="conv_branch",
    )(u, u, u, u, cwp, cb.reshape(1, -1), lng.reshape(1, -1), lnb.reshape(1, -1), wo.astype(BF16))


def _seg_sum(x, ones_bd):
    return jnp.dot(x, ones_bd, preferred_element_type=F32, precision=lax.Precision.HIGHEST)


def _rwkv_pre_kernel(r_ref, rp_ref, rn_ref, k_ref, kp_ref, kn_ref, v_ref, vp_ref, vn_ref,
                     lora_ref, mu_ref, w0_ref, bw_ref, a0_ref, ba_ref, bg_ref, xi_ref, al_ref,
                     rho_ref, ones_ref,
                     r_o, kk_o, v_o, w0_o, w1_o, ka0_o, ka1_o, kt0_o, kt1_o, g_o, bonus_o, *, geo):
    i = pl.program_id(0)
    p, n = geo.pos(i)
    has_prev = p > 0
    has_next = p < n - 1
    row = lax.broadcasted_iota(jnp.int32, (TILE, D_RWKV), 0)

    def shifted(c_ref, p_ref, n_ref, mu):
        cur = c_ref[...]
        pv = jnp.where(has_prev, p_ref[SHIFT_HALO - 1:SHIFT_HALO, :], 0.0)
        nx = jnp.where(has_next, n_ref[0:1, :], 0.0)
        prev = jnp.where(row == 0, pv, pltpu.roll(cur, 1, axis=0))
        nxt = jnp.where(row == TILE - 1, nx, pltpu.roll(cur, TILE - 1, axis=0))
        return cur + mu * (0.5 * (prev + nxt) - cur)

    r = shifted(r_ref, rp_ref, rn_ref, mu_ref[0:1, :])
    k = shifted(k_ref, kp_ref, kn_ref, mu_ref[1:2, :])
    v = shifted(v_ref, vp_ref, vn_ref, mu_ref[2:3, :])

    lora = lora_ref[...]
    dw = jnp.tanh(lora[:, :W_RANK]).astype(BF16)
    da = lora[:, W_RANK:W_RANK + A_RANK].astype(BF16)
    dg = _sigmoid(lora[:, W_RANK + A_RANK:]).astype(BF16)
    wl = jnp.dot(dw, bw_ref[...], preferred_element_type=F32)
    al = jnp.dot(da, ba_ref[...], preferred_element_type=F32)
    g = jnp.dot(dg, bg_ref[...], preferred_element_type=F32)

    ones_bd = ones_ref[...]
    kx = k * xi_ref[...]
    kk = kx * lax.rsqrt(_seg_sum(kx * kx, ones_bd) + EPS)
    alpha = al_ref[...]
    rho = rho_ref[...]
    bonus = jnp.zeros((TILE, D_RWKV), F32)
    for d, (w_o, ka_o, kt_o) in enumerate(((w0_o, ka0_o, kt0_o), (w1_o, ka1_o, kt1_o))):
        sl = slice(d * D_RWKV, (d + 1) * D_RWKV)
        w = jnp.exp(-DECAY_SCALE * _sigmoid(w0_ref[d:d + 1, :] + wl[:, sl]))
        a = _sigmoid(a0_ref[d:d + 1, :] + al[:, sl])
        kt = k * (1.0 + (a - 1.0) * alpha)
        w_o[...] = w
        ka_o[...] = kk * a
        kt_o[...] = kt
        bonus = bonus + _seg_sum(r * kt * rho, ones_bd) * v
    r_o[...] = r
    kk_o[...] = kk
    v_o[...] = v
    g_o[...] = g
    bonus_o[...] = bonus


def rwkv_pre(u, P, l, geo, ones_bd):
    n = u.shape[0]
    hb = TILE // SHIFT_HALO
    nhalo = n // SHIFT_HALO

    def trio(c0):
        cb = c0 // D_RWKV
        return [
            pl.BlockSpec((TILE, D_RWKV), lambda i: (i, cb)),
            pl.BlockSpec((SHIFT_HALO, D_RWKV), lambda i: (jnp.maximum(i * hb - 1, 0), cb)),
            pl.BlockSpec((SHIFT_HALO, D_RWKV), lambda i: (jnp.minimum((i + 1) * hb, nhalo - 1), cb)),
        ]

    def full(shape):
        return pl.BlockSpec(shape, lambda i: tuple(0 for _ in shape))

    bw = jnp.concatenate([P["rwkv_bw"][l, 0], P["rwkv_bw"][l, 1]], axis=1).astype(BF16)
    ba = jnp.concatenate([P["rwkv_ba"][l, 0], P["rwkv_ba"][l, 1]], axis=1).astype(BF16)
    out = jax.ShapeDtypeStruct((n, D_RWKV), F32)
    return pl.pallas_call(
        functools.partial(_rwkv_pre_kernel, geo=geo),
        grid=(geo.ntile,),
        in_specs=trio(C_R) + trio(C_K) + trio(C_V) + [
            pl.BlockSpec((TILE, 256), lambda i: (i, C_LORA // 256)),
            full((3, D_RWKV)), full((2, D_RWKV)), full((W_RANK, 2 * D_RWKV)),
            full((2, D_RWKV)), full((A_RANK, 2 * D_RWKV)), full((G_RANK, D_RWKV)),
            full((1, D_RWKV)), full((1, D_RWKV)), full((1, D_RWKV)), full((D_RWKV, D_RWKV)),
        ],
        out_specs=[pl.BlockSpec((TILE, D_RWKV), lambda i: (i, 0))] * 11,
        out_shape=[out] * 11,
        compiler_params=_cparams(("arbitrary",)),
        name="rwkv_pre",
    )(u, u, u, u, u, u, u, u, u, u,
      P["rwkv_mu"][l], P["rwkv_w0"][l], bw, P["rwkv_a0"][l], ba, P["rwkv_bg"][l].astype(BF16),
      P["rwkv_xi"][l].reshape(1, -1), P["rwkv_alpha"][l].reshape(1, -1),
      P["rwkv_rho"][l].reshape(1, -1), ones_bd)


N_VBLK = RWKV_HEAD // SUBLANES


def _scan_steps(xrow, vload, ostore, s_ref, tidx, *, tc, nk, kl):
    groups = (tuple(range(0, N_VBLK // 2)), tuple(range(N_VBLK // 2, N_VBLK)))

    def allred(a):
        out = a
        for j in range(1, kl):
            out = out + pltpu.roll(a, j * (LANES // kl), axis=1)
        return out

    def first_sa(grp, t):
        acc = [None] * len(grp)
        for kh in range(nk):
            kkb = xrow(t, 1, kh)
            for j, vb in enumerate(grp):
                pr = s_ref[vb, kh] * kkb
                acc[j] = pr if acc[j] is None else acc[j] + pr
        return tuple(allred(a) for a in acc)

    def fused(grp, s, t, t_next, sa):
        vv = [vload(t, vb) for vb in grp]
        oacc = [None] * len(grp)
        acc = [None] * len(grp)
        for kh in range(nk):
            wb, kab, ktb, rb = xrow(t, 0, kh), xrow(t, 2, kh), xrow(t, 3, kh), xrow(t, 4, kh)
            kkn = xrow(t_next, 1, kh)
            for j, vb in enumerate(grp):
                sn = s_ref[vb, kh] * wb - sa[j] * kab + vv[j] * ktb
                s_ref[vb, kh] = sn
                po = sn * rb
                pa = sn * kkn
                oacc[j] = po if oacc[j] is None else oacc[j] + po
                acc[j] = pa if acc[j] is None else acc[j] + pa
        for j, vb in enumerate(grp):
            ostore(s, t, vb, oacc[j])
        return tuple(allred(a) for a in acc)

    t0 = tidx(0)
    carry0 = tuple(first_sa(g, t0) for g in groups)

    def step(s, carry):
        t = tidx(s)
        t_next = tidx(jnp.minimum(s + 1, tc - 1))
        return tuple(fused(g, s, t, t_next, carry[i]) for i, g in enumerate(groups))

    lax.fori_loop(0, tc, step, carry0)


def _vrows(vb):
    return pl.ds(vb * SUBLANES, SUBLANES)


def _scan_ctx_kernel(rev_ref, x_ref, v_ref, s0_ref, o_ref, sfin_ref, s_ref, *, tc, nk):
    gb = pl.program_id(0)
    c = pl.program_id(1)
    rev = rev_ref[gb]

    @pl.when(c == 0)
    def _():
        s_ref[...] = s0_ref[0]

    def xrow(t, a, kh):
        return jnp.broadcast_to(x_ref[0, t, a, pl.ds(kh, 1), :], (SUBLANES, LANES))

    def vload(t, vb):
        return v_ref[0, t, _vrows(vb), :]

    def ostore(s, t, vb, val):
        o_ref[0, t, _vrows(vb), :] = val

    def tidx(s):
        return jnp.where(rev == 1, tc - 1 - s, s)

    _scan_steps(xrow, vload, ostore, s_ref, tidx, tc=tc, nk=nk, kl=1)

    @pl.when(c == pl.num_programs(1) - 1)
    def _():
        sfin_ref[0] = s_ref[...]


def wkv_scan_ctx(x, v, s0, rev, *, tc):
    ngb, t_len, _, nk, _ = x.shape
    nch = t_len // tc

    def tmap(g, c, rev_ref):
        return c + rev_ref[g] * (nch - 1 - 2 * c)

    return pl.pallas_call(
        functools.partial(_scan_ctx_kernel, tc=tc, nk=nk),
        grid_spec=pltpu.PrefetchScalarGridSpec(
            num_scalar_prefetch=1,
            grid=(ngb, nch),
            in_specs=[
                pl.BlockSpec((1, tc, 5, nk, LANES), lambda g, c, r: (g, tmap(g, c, r), 0, 0, 0)),
                pl.BlockSpec((1, tc, RWKV_HEAD, LANES), lambda g, c, r: (g, tmap(g, c, r), 0, 0)),
                pl.BlockSpec((1, N_VBLK, nk, SUBLANES, LANES), lambda g, c, r: (g, 0, 0, 0, 0)),
            ],
            out_specs=[
                pl.BlockSpec((1, tc, RWKV_HEAD, LANES), lambda g, c, r: (g, tmap(g, c, r), 0, 0)),
                pl.BlockSpec((1, N_VBLK, nk, SUBLANES, LANES), lambda g, c, r: (g, 0, 0, 0, 0)),
            ],
            scratch_shapes=[pltpu.VMEM((N_VBLK, nk, SUBLANES, LANES), F32)],
        ),
        out_shape=[
            jax.ShapeDtypeStruct((ngb, t_len, RWKV_HEAD, LANES), F32),
            jax.ShapeDtypeStruct((ngb, N_VBLK, nk, SUBLANES, LANES), F32),
        ],
        compiler_params=_cparams(("arbitrary", "arbitrary")),
        name="wkv_scan_ctx",
    )(rev, x, v, s0)


def _scan_lat_kernel(xa_ref, xb_ref, va_ref, vb_ref, s0_ref, oa_ref, ob_ref, s_ref, xm_ref, vm_ref,
                     *, tc, nk, kl):
    c = pl.program_id(0)

    @pl.when(c == 0)
    def _():
        s_ref[...] = s0_ref[...]

    def is_bwd(shape):
        lane = lax.broadcasted_iota(jnp.int32, shape, len(shape) - 1)
        return (lane // N_RWKV_HEADS) % 2 == 1

    mx = is_bwd((5, nk, LANES))
    mv = is_bwd((RWKV_HEAD, LANES))

    def merge(s, carry):
        xm_ref[s] = jnp.where(mx, xb_ref[tc - 1 - s], xa_ref[s])
        vm_ref[s] = jnp.where(mv, vb_ref[tc - 1 - s], va_ref[s])
        return carry

    lax.fori_loop(0, tc, merge, 0)

    def xrow(t, a, kh):
        return jnp.broadcast_to(xm_ref[t, a, pl.ds(kh, 1), :], (SUBLANES, LANES))

    def vload(t, vb):
        return vm_ref[t, _vrows(vb), :]

    def ostore(s, t, vb, val):
        oa_ref[s, _vrows(vb), :] = val
        ob_ref[tc - 1 - s, _vrows(vb), :] = val

    _scan_steps(xrow, vload, ostore, s_ref, lambda s: s, tc=tc, nk=nk, kl=kl)


def wkv_scan_lat(x, v, s0, *, tc, kl):
    t_len, _, nk, _ = x.shape
    nch = t_len // tc
    o_sds = jax.ShapeDtypeStruct((t_len, RWKV_HEAD, LANES), F32)
    return pl.pallas_call(
        functools.partial(_scan_lat_kernel, tc=tc, nk=nk, kl=kl),
        grid=(nch,),
        in_specs=[
            pl.BlockSpec((tc, 5, nk, LANES), lambda c: (c, 0, 0, 0)),
            pl.BlockSpec((tc, 5, nk, LANES), lambda c: (nch - 1 - c, 0, 0, 0)),
            pl.BlockSpec((tc, RWKV_HEAD, LANES), lambda c: (c, 0, 0)),
            pl.BlockSpec((tc, RWKV_HEAD, LANES), lambda c: (nch - 1 - c, 0, 0)),
            pl.BlockSpec((N_VBLK, nk, SUBLANES, LANES), lambda c: (0, 0, 0, 0)),
        ],
        out_specs=[
            pl.BlockSpec((tc, RWKV_HEAD, LANES), lambda c: (c, 0, 0)),
            pl.BlockSpec((tc, RWKV_HEAD, LANES), lambda c: (nch - 1 - c, 0, 0)),
        ],
        out_shape=[o_sds, o_sds],
        scratch_shapes=[pltpu.VMEM((N_VBLK, nk, SUBLANES, LANES), F32),
                        pltpu.VMEM((tc, 5, nk, LANES), F32),
                        pltpu.VMEM((tc, RWKV_HEAD, LANES), F32)],
        compiler_params=_cparams(("arbitrary",)),
        name="wkv_scan_lat",
    )(x, x, v, v, s0)


def _rwkv_post_kernel(of_ref, ob_ref, bonus_ref, g_ref, gate_ref, gng_ref, gnb_ref, ones_ref,
                      wo_ref, y_ref):
    o = of_ref[0] + ob_ref[0]
    ones_bd = ones_ref[...]
    mean = _seg_sum(o, ones_bd) * (1.0 / RWKV_HEAD)
    oc = o - mean
    var = _seg_sum(oc * oc, ones_bd) * (1.0 / RWKV_HEAD)
    gn = oc * lax.rsqrt(var + GN_EPS) * gng_ref[...] + gnb_ref[...]
    y = (gn + bonus_ref[...]) * g_ref[...]
    y = jnp.dot(y.astype(BF16), wo_ref[...], preferred_element_type=F32)
    y_ref[...] = _sigmoid(gate_ref[...]) * y


def rwkv_post(o2, bonus, g, u, gng, gnb, ones_bd, wo, geo):
    n = bonus.shape[0]

    def full(shape):
        return pl.BlockSpec(shape, lambda i: tuple(0 for _ in shape))

    return pl.pallas_call(
        _rwkv_post_kernel,
        grid=(geo.ntile,),
        in_specs=[
            pl.BlockSpec((1, TILE, D_RWKV), lambda i: (0, i, 0)),
            pl.BlockSpec((1, TILE, D_RWKV), lambda i: (1, i, 0)),
            pl.BlockSpec((TILE, D_RWKV), lambda i: (i, 0)),
            pl.BlockSpec((TILE, D_RWKV), lambda i: (i, 0)),
            pl.BlockSpec((TILE, D_MODEL), lambda i: (i, C_G1 // D_MODEL)),
            full((1, D_RWKV)), full((1, D_RWKV)), full((D_RWKV, D_RWKV)), full((D_RWKV, D_MODEL)),
        ],
        out_specs=pl.BlockSpec((TILE, D_MODEL), lambda i: (i, 0)),
        out_shape=jax.ShapeDtypeStruct((n, D_MODEL), F32),
        compiler_params=_cparams(("arbitrary",)),
        name="rwkv_post",
    )(o2, o2, bonus, g, u, gng.reshape(1, -1), gnb.reshape(1, -1), ones_bd, wo.astype(BF16))


QK_W = 2 * LANES
Q_SCALE = ATTN_SCALE * math.log2(math.e)


def _mla_pre_kernel(cq_ref, ckv_ref, kr_ref, cos_ref, sin_ref, qg_ref, kvg_ref, wuq_ref, wukt_ref,
                    qf_o, ckvn_o, krr_o):
    nq = N_MLA_HEADS * NOPE_DIM
    nrp = N_MLA_HEADS * LANES
    cq = _rms(cq_ref[...], qg_ref[...])
    q = jnp.dot(cq.astype(BF16), wuq_ref[...], preferred_element_type=F32)
    cos = cos_ref[...]
    sin = sin_ref[...]
    for h in range(N_MLA_HEADS):
        qn = q[:, h * NOPE_DIM:(h + 1) * NOPE_DIM].astype(BF16)
        qa = jnp.dot(qn, wukt_ref[h], preferred_element_type=F32)
        qr = (q[:, nq + h * LANES:nq + (h + 1) * LANES] * cos
              + q[:, nq + nrp + h * LANES:nq + nrp + (h + 1) * LANES] * sin)
        qf_o[:, h * QK_W:h * QK_W + LANES] = (qa * Q_SCALE).astype(qf_o.dtype)
        qf_o[:, h * QK_W + LANES:(h + 1) * QK_W] = (qr * Q_SCALE).astype(qf_o.dtype)
    ckvn_o[...] = _rms(ckv_ref[...], kvg_ref[...])
    kr = kr_ref[...]
    krr = kr[:, ROPE_DIM:2 * ROPE_DIM] * cos[:, :ROPE_DIM] + kr[:, 2 * ROPE_DIM:3 * ROPE_DIM] * sin[:, :ROPE_DIM]
    krr_o[...] = jnp.concatenate([krr, jnp.zeros((TILE, LANES - ROPE_DIM), F32)], axis=1)


def mla_pre(u, cos_t, sin_t, qg, kvg, wuq_p, wukt, geo):
    n = u.shape[0]

    def full(shape):
        return pl.BlockSpec(shape, lambda i: tuple(0 for _ in shape))

    return pl.pallas_call(
        _mla_pre_kernel,
        grid=(geo.ntile,),
        in_specs=[
            pl.BlockSpec((TILE, Q_RANK), lambda i: (i, C_CQ // Q_RANK)),
            pl.BlockSpec((TILE, KV_RANK), lambda i: (i, C_CKV // KV_RANK)),
            pl.BlockSpec((TILE, LANES), lambda i: (i, C_KR // LANES)),
            pl.BlockSpec((TILE, LANES), lambda i: (geo.rope_blk(i), 0)),
            pl.BlockSpec((TILE, LANES), lambda i: (geo.rope_blk(i), 0)),
            full((1, Q_RANK)), full((1, KV_RANK)),
            full(wuq_p.shape), full(wukt.shape),
        ],
        out_specs=[
            pl.BlockSpec((TILE, N_MLA_HEADS * QK_W), lambda i: (i, 0)),
            pl.BlockSpec((TILE, KV_RANK), lambda i: (i, 0)),
            pl.BlockSpec((TILE, LANES), lambda i: (i, 0)),
        ],
        out_shape=[
            jax.ShapeDtypeStruct((n, N_MLA_HEADS * QK_W), BF16),
            jax.ShapeDtypeStruct((n, KV_RANK), F32),
            jax.ShapeDtypeStruct((n, LANES), F32),
        ],
        compiler_params=_cparams(("arbitrary",)),
        name="mla_pre",
    )(u, u, u, cos_t, sin_t, qg.reshape(1, -1), kvg.reshape(1, -1), wuq_p, wukt)


def _attn_kernel(*refs, aliased):
    q_ref, kt_ref, c_ref, wuv_ref, wo_ref, gate_ref = refs[:6]
    y_ref, acc_ref = refs[-2:]
    h = pl.program_id(2)

    @pl.when(h == 0)
    def _():
        acc_ref[...] = jnp.zeros(acc_ref.shape, F32)

    s = jnp.dot(q_ref[...], kt_ref[0], preferred_element_type=F32)
    p = jnp.exp2(s - jnp.max(s, axis=-1, keepdims=True))
    l = jnp.sum(p, axis=-1, keepdims=True)
    pc = jnp.dot(p.astype(BF16), c_ref[0], preferred_element_type=F32) / l
    oh = jnp.dot(pc.astype(BF16), wuv_ref[0], preferred_element_type=F32)
    acc_ref[...] += jnp.dot(oh.astype(BF16), wo_ref[0], preferred_element_type=F32)

    @pl.when(h == pl.num_programs(2) - 1)
    def _():
        y_ref[...] = _sigmoid(gate_ref[...]) * acc_ref[...]


def attention(qf, kct, cv, wuv_h, wo_h, u, y_prev, *, row0, t_seq, tq, name):
    n = qf.shape[0]
    nseq, _, t_k = kct.shape
    qt = t_seq // tq
    rb0 = row0 // tq

    def rows(w, col):
        return pl.BlockSpec((tq, w), lambda s, i, h: (rb0 + s * qt + i, col(h)))

    in_specs = [
        rows(QK_W, lambda h: h),
        pl.BlockSpec((1, QK_W, t_k), lambda s, i, h: (s, 0, 0)),
        pl.BlockSpec((1, t_k, KV_RANK), lambda s, i, h: (s, 0, 0)),
        pl.BlockSpec((1, KV_RANK, V_DIM), lambda s, i, h: (h, 0, 0)),
        pl.BlockSpec((1, V_DIM, D_MODEL), lambda s, i, h: (h, 0, 0)),
        rows(D_MODEL, lambda h: C_G2 // D_MODEL),
    ]
    args = [qf, kct, cv, wuv_h, wo_h, u]
    aliases = {}
    if y_prev is not None:
        in_specs.append(pl.BlockSpec(memory_space=pl.ANY))
        args.append(y_prev)
        aliases = {len(args) - 1: 0}
    return pl.pallas_call(
        functools.partial(_attn_kernel, aliased=y_prev is not None),
        grid=(nseq, qt, N_MLA_HEADS),
        in_specs=in_specs,
        out_specs=rows(D_MODEL, lambda h: 0),
        out_shape=jax.ShapeDtypeStruct((n, D_MODEL), F32),
        scratch_shapes=[pltpu.VMEM((tq, D_MODEL), F32)],
        input_output_aliases=aliases,
        compiler_params=_cparams(("arbitrary", "arbitrary", "arbitrary")),
        name=name,
    )(*args)


R_I1, R_I2, R_G1, R_G2 = 0, 1, 2, 3


def _merge_kernel(x_ref, yc_ref, yr_ref, ym_ref, wout_ref, gpost_ref, g1_ref, gpre_ref, sc_ref,
                  sh_ref, router_ref, x1_o, h2_o, comb_o, *, moe):
    m = yc_ref[...] + yr_ref[...] + ym_ref[...]
    y = jnp.dot(m.astype(BF16), wout_ref[...], preferred_element_type=F32)
    x1 = x_ref[...] + g1_ref[0] * _rms(y, gpost_ref[...])
    x1_o[...] = x1
    h2 = _rms(x1, gpre_ref[...]) * (1.0 + sc_ref[0]) + sh_ref[0]
    h2_o[...] = h2.astype(h2_o.dtype)
    if moe:
        logits = jnp.dot(h2, router_ref[...], preferred_element_type=F32,
                         precision=lax.Precision.HIGHEST)
        lane = lax.broadcasted_iota(jnp.int32, logits.shape, 1)
        neg = jnp.float32(-jnp.inf)
        logits = jnp.where(lane < N_EXPERTS, logits, neg)
        m1 = jnp.max(logits, axis=-1, keepdims=True)
        i1 = jnp.min(jnp.where(logits == m1, lane, LANES), axis=-1, keepdims=True)
        rest = jnp.where(lane == i1, neg, logits)
        m2 = jnp.max(rest, axis=-1, keepdims=True)
        i2 = jnp.min(jnp.where(rest == m2, lane, LANES), axis=-1, keepdims=True)
        e2 = jnp.exp(m2 - m1)
        den = 1.0 + e2
        cols = ((R_I1, i1.astype(F32)), (R_I2, i2.astype(F32)), (R_G1, 1.0 / den), (R_G2, e2 / den))
        route = jnp.zeros(comb_o.shape, F32)
        for col, val in cols:
            route = jnp.where(lane == col, val, route)
        comb_o[...] = route
    else:
        comb_o[...] = jnp.ones(comb_o.shape, F32)


def merge(x, yc, yr, ym, wout, gpost, gpre, mod, router, geo, moe):
    n, d = x.shape

    def full(shape):
        return pl.BlockSpec(shape, lambda i: tuple(0 for _ in shape))

    def rows():
        return pl.BlockSpec((TILE, d), lambda i: (i, 0))

    def modspec(j):
        return pl.BlockSpec((1, 1, d), lambda i: (geo.mod_row(i) * 6 + j, 0, 0))

    router_p = jnp.zeros((d, LANES), F32).at[:, :N_EXPERTS].set(router)
    return pl.pallas_call(
        functools.partial(_merge_kernel, moe=moe),
        grid=(geo.ntile,),
        in_specs=[rows(), rows(), rows(), rows(), full((d, d)), full((1, d)), modspec(2),
                  full((1, d)), modspec(4), modspec(3), full((d, LANES))],
        out_specs=[rows(), rows(), pl.BlockSpec((TILE, LANES), lambda i: (i, 0))],
        out_shape=[jax.ShapeDtypeStruct((n, d), F32), jax.ShapeDtypeStruct((n, d), F32 if moe else BF16),
                   jax.ShapeDtypeStruct((n, LANES), F32)],
        compiler_params=_cparams(("arbitrary",)),
        name="merge",
    )(x, yc, yr, ym, wout.astype(BF16), gpost.reshape(1, d), mod, gpre.reshape(1, d), mod, mod,
      router_p)


def _ffn_kernel(h_ref, comb_ref, w1_ref, w3_ref, w2_ref, o_ref, acc_ref):
    e = pl.program_id(1)
    f = pl.program_id(2)

    @pl.when((e == 0) & (f == 0))
    def _():
        acc_ref[...] = jnp.zeros(acc_ref.shape, F32)

    h = h_ref[...]
    a = jnp.dot(h, w1_ref[0], preferred_element_type=F32)
    b = jnp.dot(h, w3_ref[0], preferred_element_type=F32)
    comb = comb_ref[...]
    lane = lax.broadcasted_iota(jnp.int32, comb.shape, 1)
    ce = jnp.sum(jnp.where(lane == e, comb, 0.0), axis=-1, keepdims=True)
    act = (a * _sigmoid(a) * b) * ce
    acc_ref[...] += jnp.dot(act.astype(BF16), w2_ref[0], preferred_element_type=F32)

    @pl.when((e == pl.num_programs(1) - 1) & (f == pl.num_programs(2) - 1))
    def _():
        o_ref[...] = acc_ref[...]


def ffn(h, comb, w1, w3, w2, *, tm, tf):
    n, d = h.shape
    ne, _, dff = w1.shape
    return pl.pallas_call(
        _ffn_kernel,
        grid=(n // tm, ne, dff // tf),
        in_specs=[
            pl.BlockSpec((tm, d), lambda i, e, f: (i, 0)),
            pl.BlockSpec((tm, LANES), lambda i, e, f: (i, 0)),
            pl.BlockSpec((1, d, tf), lambda i, e, f: (e, 0, f)),
            pl.BlockSpec((1, d, tf), lambda i, e, f: (e, 0, f)),
            pl.BlockSpec((1, tf, d), lambda i, e, f: (e, f, 0)),
        ],
        out_specs=pl.BlockSpec((tm, d), lambda i, e, f: (i, 0)),
        out_shape=jax.ShapeDtypeStruct((n, d), F32),
        scratch_shapes=[pltpu.VMEM((tm, d), F32)],
        compiler_params=_cparams(("arbitrary", "arbitrary", "arbitrary")),
        name="ffn",
    )(h, comb, w1, w3, w2)


MOE_TM = 512


def _row_copy(src_hbm, row, dst_vmem, r, sem):
    return pltpu.make_async_copy(src_hbm.at[pl.ds(row, 1)], dst_vmem.at[pl.ds(r, 1)], sem)


def _gather_into(idx_ref, src_hbm, dst_vmem, sem, nrows):
    def start(r, c):
        _row_copy(src_hbm, idx_ref[0, 0, r], dst_vmem, r, sem).start()
        return c

    def wait(r, c):
        _row_copy(src_hbm, 0, dst_vmem, r, sem).wait()
        return c

    lax.fori_loop(0, nrows, start, 0)
    lax.fori_loop(0, nrows, wait, 0)


def _gather_rows_kernel(idx_ref, src_hbm, o_ref, buf_ref, sem):
    _gather_into(idx_ref, src_hbm, buf_ref, sem, buf_ref.shape[0])
    o_ref[...] = buf_ref[...]


def gather_rows(src, idx, *, rows):
    p = idx.shape[0]
    w = src.shape[1]
    return pl.pallas_call(
        _gather_rows_kernel,
        grid=(p // rows,),
        in_specs=[
            pl.BlockSpec((1, 1, rows), lambda j: (j, 0, 0), memory_space=pltpu.SMEM),
            pl.BlockSpec(memory_space=pl.ANY),
        ],
        out_specs=pl.BlockSpec((rows, w), lambda j: (j, 0)),
        out_shape=jax.ShapeDtypeStruct((p, w), src.dtype),
        scratch_shapes=[pltpu.VMEM((rows, w), src.dtype), pltpu.SemaphoreType.DMA(())],
        compiler_params=_cparams(("arbitrary",)),
        name="moe_gather",
    )(idx.reshape(p // rows, 1, rows), src)


def _ffn_sorted_kernel(te_ref, nused_ref, x_ref, w1_ref, w3_ref, w2_ref, o_ref, acc_ref):
    j = pl.program_id(0)
    f = pl.program_id(1)

    @pl.when(j < nused_ref[0])
    def _():
        @pl.when(f == 0)
        def _():
            acc_ref[...] = jnp.zeros(acc_ref.shape, F32)

        h = x_ref[...].astype(BF16)
        a = jnp.dot(h, w1_ref[0], preferred_element_type=F32)
        b = jnp.dot(h, w3_ref[0], preferred_element_type=F32)
        act = a * _sigmoid(a) * b
        acc_ref[...] += jnp.dot(act.astype(BF16), w2_ref[0], preferred_element_type=F32)

        @pl.when(f == pl.num_programs(1) - 1)
        def _():
            o_ref[...] = acc_ref[...]


def ffn_sorted(xs, te, nused, w1, w3, w2, *, tm, tf):
    p, d = xs.shape
    dff = w1.shape[2]
    return pl.pallas_call(
        _ffn_sorted_kernel,
        grid_spec=pltpu.PrefetchScalarGridSpec(
            num_scalar_prefetch=2,
            grid=(p // tm, dff // tf),
            in_specs=[
                pl.BlockSpec((tm, d), lambda j, f, te, nu: (j, 0)),
                pl.BlockSpec((1, d, tf), lambda j, f, te, nu: (te[j], 0, f)),
                pl.BlockSpec((1, d, tf), lambda j, f, te, nu: (te[j], 0, f)),
                pl.BlockSpec((1, tf, d), lambda j, f, te, nu: (te[j], f, 0)),
            ],
            out_specs=pl.BlockSpec((tm, d), lambda j, f, te, nu: (j, 0)),
            scratch_shapes=[pltpu.VMEM((tm, d), F32)],
        ),
        out_shape=jax.ShapeDtypeStruct((p, d), F32),
        compiler_params=_cparams(("arbitrary", "arbitrary")),
        name="ffn_sorted",
    )(te, nused, xs, w1, w3, w2)


def _moe_combine_kernel(d0_ref, d1_ref, ys_hbm, route_ref, x_ref, g_ref, g2_ref, o_ref,
                        y0_ref, y1_ref, sem0, sem1):
    def start(r, c):
        _row_copy(ys_hbm, d0_ref[0, 0, r], y0_ref, r, sem0).start()
        _row_copy(ys_hbm, d1_ref[0, 0, r], y1_ref, r, sem1).start()
        return c

    def wait(r, c):
        _row_copy(ys_hbm, 0, y0_ref, r, sem0).wait()
        _row_copy(ys_hbm, 0, y1_ref, r, sem1).wait()
        return c

    lax.fori_loop(0, TILE, start, 0)
    lax.fori_loop(0, TILE, wait, 0)
    route = route_ref[...]
    lane = lax.broadcasted_iota(jnp.int32, route.shape, 1)
    ga = jnp.sum(jnp.where(lane == R_G1, route, 0.0), axis=-1, keepdims=True)
    gb = jnp.sum(jnp.where(lane == R_G2, route, 0.0), axis=-1, keepdims=True)
    y = ga * y0_ref[...] + gb * y1_ref[...]
    o_ref[...] = x_ref[...] + g2_ref[0] * _rms(y, g_ref[...])


def moe_combine_post(x1, ys, dest, route, gpost, mod, geo):
    n, d = x1.shape
    dd = dest.reshape(2, n // TILE, 1, TILE)
    return pl.pallas_call(
        _moe_combine_kernel,
        grid=(geo.ntile,),
        in_specs=[
            pl.BlockSpec((1, 1, TILE), lambda i: (i, 0, 0), memory_space=pltpu.SMEM),
            pl.BlockSpec((1, 1, TILE), lambda i: (i, 0, 0), memory_space=pltpu.SMEM),
            pl.BlockSpec(memory_space=pl.ANY),
            pl.BlockSpec((TILE, LANES), lambda i: (i, 0)),
            pl.BlockSpec((TILE, d), lambda i: (i, 0)),
            pl.BlockSpec((1, d), lambda i: (0, 0)),
            pl.BlockSpec((1, 1, d), lambda i: (geo.mod_row(i) * 6 + 5, 0, 0)),
        ],
        out_specs=pl.BlockSpec((TILE, d), lambda i: (i, 0)),
        out_shape=jax.ShapeDtypeStruct((n, d), F32),
        scratch_shapes=[pltpu.VMEM((TILE, d), F32), pltpu.VMEM((TILE, d), F32),
                        pltpu.SemaphoreType.DMA(()), pltpu.SemaphoreType.DMA(())],
        compiler_params=_cparams(("arbitrary",)),
        name="moe_combine",
    )(dd[0], dd[1], ys, route, x1, gpost.reshape(1, d), mod)


def _route_plan(route, tm):
    n = route.shape[0]
    e_flat = jnp.concatenate([route[:, R_I1], route[:, R_I2]]).astype(jnp.int32)
    onehot = (e_flat[:, None] == jnp.arange(N_EXPERTS, dtype=jnp.int32)[None, :]).astype(jnp.int32)
    csum = jnp.cumsum(onehot, axis=0)
    rank = jnp.take_along_axis(csum, e_flat[:, None], axis=1)[:, 0] - 1
    counts = csum[-1]
    ptiles = (counts + tm - 1) // tm
    tile_end = jnp.cumsum(ptiles)
    gstart = (tile_end - ptiles) * tm
    dest = jnp.take(gstart, e_flat) + rank
    p_max = 2 * n + N_EXPERTS * tm
    tok = jnp.tile(jnp.arange(n, dtype=jnp.int32), 2)
    row_src = jnp.zeros((p_max,), jnp.int32).at[dest].set(tok)
    tiles = jnp.arange(p_max // tm, dtype=jnp.int32)
    te = jnp.minimum(jnp.searchsorted(tile_end, tiles, side="right"), N_EXPERTS - 1).astype(jnp.int32)
    return row_src, dest.astype(jnp.int32).reshape(2, n), te, tile_end[-1:].astype(jnp.int32)


def _ffn_post_kernel(x_ref, y_ref, g_ref, g2_ref, o_ref):
    o_ref[...] = x_ref[...] + g2_ref[0] * _rms(y_ref[...], g_ref[...])


def ffn_post(x1, y, gpost, mod, geo):
    n, d = x1.shape
    return pl.pallas_call(
        _ffn_post_kernel,
        grid=(geo.ntile,),
        in_specs=[
            pl.BlockSpec((TILE, d), lambda i: (i, 0)),
            pl.BlockSpec((TILE, d), lambda i: (i, 0)),
            pl.BlockSpec((1, d), lambda i: (0, 0)),
            pl.BlockSpec((1, 1, d), lambda i: (geo.mod_row(i) * 6 + 5, 0, 0)),
        ],
        out_specs=pl.BlockSpec((TILE, d), lambda i: (i, 0)),
        out_shape=jax.ShapeDtypeStruct((n, d), F32),
        compiler_params=_cparams(("arbitrary",)),
        name="ffn_post",
    )(x1, y, gpost.reshape(1, d), mod)


def _scan_layouts(geo, pre, state_l):
    r, kk, v, w0, w1, ka0, ka1, kt0, kt1 = pre
    H, K = N_RWKV_HEADS, RWKV_HEAD
    nc, nl = geo.nc_tok, geo.nl_tok
    bl = LANES // H
    nbh = geo.n_ctx // bl

    def ctx_l(a):
        a = a[:nc].reshape(nbh, bl, geo.t_ctx, H, K)
        return a.transpose(0, 2, 4, 1, 3).reshape(nbh, geo.t_ctx, K, LANES)

    xs = []
    for (w, ka, kt) in ((w0, ka0, kt0), (w1, ka1, kt1)):
        xs.append(jnp.stack([ctx_l(w), ctx_l(kk), ctx_l(ka), ctx_l(kt), ctx_l(r)], axis=2))
    x_ctx = jnp.concatenate(xs, axis=0)
    v_ctx = jnp.tile(ctx_l(v), (2, 1, 1, 1))
    rev_ctx = jnp.concatenate([jnp.zeros((nbh,), jnp.int32), jnp.ones((nbh,), jnp.int32)])

    kl = LANES // (geo.n_lat * 2 * H)
    nk = K // kl

    def lat_pair(a0, a1):
        a0 = a0[nc:].reshape(geo.n_lat, geo.t_lat, H, K)
        a1 = a1[nc:].reshape(geo.n_lat, geo.t_lat, H, K)
        a = jnp.stack([a0, a1], axis=2).reshape(geo.n_lat, geo.t_lat, 2, H, kl, nk)
        return a.transpose(1, 5, 4, 0, 2, 3).reshape(geo.t_lat, nk, LANES)

    x_lat = jnp.stack([lat_pair(w0, w1), lat_pair(kk, kk), lat_pair(ka0, ka1), lat_pair(kt0, kt1),
                       lat_pair(r, r)], axis=1)
    vl = v[nc:].reshape(geo.n_lat, geo.t_lat, H, K)
    vl = jnp.stack([vl, vl], axis=2)
    vl = vl.transpose(1, 4, 0, 2, 3).reshape(geo.t_lat, K, LANES // kl)
    v_lat = jnp.tile(vl, (1, 1, kl))
    s0 = state_l.reshape(geo.n_lat, 2, H, K // SUBLANES, SUBLANES, kl, nk)
    s0_lat = s0.transpose(3, 6, 4, 5, 0, 1, 2).reshape(K // SUBLANES, nk, SUBLANES, LANES)
    return x_ctx, v_ctx, rev_ctx, x_lat, v_lat, s0_lat, kl


def _scan_outputs(geo, o_ctx, sfin_ctx, oa_lat, ob_lat, kl):
    H, K = N_RWKV_HEADS, RWKV_HEAD
    bl = LANES // H
    nbh = geo.n_ctx // bl
    oc = o_ctx.reshape(2, nbh, geo.t_ctx, K, bl, H).transpose(0, 1, 4, 2, 5, 3)
    oc = oc.reshape(2, geo.nc_tok, H * K)

    def lat_dir(o, d):
        o = o.reshape(geo.t_lat, K, kl, geo.n_lat, 2, H)[:, :, :, :, d].sum(axis=2)
        return o.transpose(2, 0, 3, 1).reshape(geo.nl_tok, H * K)

    ol = jnp.stack([lat_dir(oa_lat, 0), lat_dir(ob_lat, 1)], axis=0)
    o2 = jnp.concatenate([oc, ol], axis=1)
    sf = sfin_ctx.reshape(2, nbh, K // SUBLANES, K, SUBLANES, bl, H)
    sf = sf.transpose(1, 5, 0, 6, 2, 4, 3).reshape(geo.n_ctx, 2, H, K, K)
    return o2, sf


def _rope_tables(geo):
    n_freq = ROPE_DIM // 4
    rows = geo.t_lat // GRID_W
    row = jnp.repeat(jnp.arange(rows, dtype=F32), GRID_W)
    col = jnp.tile(jnp.arange(GRID_W, dtype=F32), rows)
    inv = ROPE_BASE ** (-jnp.arange(n_freq, dtype=F32) / n_freq)
    ang = jnp.concatenate([row[:, None] * inv, col[:, None] * inv], axis=-1)
    cos, sin = jnp.cos(ang), jnp.sin(ang)
    cos32 = jnp.concatenate([cos, cos], axis=-1)
    sin32 = jnp.concatenate([-sin, sin], axis=-1)
    cos_t = jnp.concatenate([jnp.ones((TILE, ROPE_DIM), F32), cos32], axis=0)
    sin_t = jnp.concatenate([jnp.zeros((TILE, ROPE_DIM), F32), sin32], axis=0)
    pad = ((0, 0), (0, LANES - ROPE_DIM))
    return jnp.pad(cos_t, pad), jnp.pad(sin_t, pad)


_DEINT = np.concatenate([np.arange(0, ROPE_DIM, 2), np.arange(1, ROPE_DIM, 2)])
_DEINT_SW = np.concatenate([np.arange(1, ROPE_DIM, 2), np.arange(0, ROPE_DIM, 2)])


def _prep_w_in(w_in, b_in):
    offs = np.cumsum([0, 2 * D_CONV, D_RWKV, D_RWKV, D_RWKV, W_RANK, A_RANK, G_RANK, Q_RANK, KV_RANK,
                      ROPE_DIM, N_BRANCH * D_MODEL])
    o_conv, o_r, o_k, o_v, o_dw, o_da, o_dg, o_cq, o_ckv, o_kr, o_gate, _ = offs
    idx = np.zeros((IN_PAD,), np.int32)
    valid = np.zeros((IN_PAD,), bool)

    def put(dst, src):
        idx[dst:dst + len(src)] = src
        valid[dst:dst + len(src)] = True

    put(C_CONV, np.arange(o_conv, o_conv + 2 * D_CONV))
    for b, c in enumerate((C_G0, C_G1, C_G2)):
        put(c, np.arange(o_gate + b * D_MODEL, o_gate + (b + 1) * D_MODEL))
    put(C_R, np.arange(o_r, o_r + D_RWKV))
    put(C_K, np.arange(o_k, o_k + D_RWKV))
    put(C_V, np.arange(o_v, o_v + D_RWKV))
    put(C_LORA, np.arange(o_dw, o_dw + W_RANK + A_RANK + G_RANK))
    put(C_CQ, np.arange(o_cq, o_cq + Q_RANK))
    put(C_CKV, np.arange(o_ckv, o_ckv + KV_RANK))
    put(C_KR, np.arange(o_kr, o_kr + ROPE_DIM))
    put(C_KR + ROPE_DIM, o_kr + _DEINT)
    put(C_KR + 2 * ROPE_DIM, o_kr + _DEINT_SW)
    w = jnp.where(valid[None, :], jnp.take(w_in, idx, axis=1), 0.0).astype(BF16)
    b = jnp.where(valid, jnp.take(b_in, idx), 0.0)
    return w, b


def _prep_wuq(wuq):
    hd = NOPE_DIM + ROPE_DIM
    nq = N_MLA_HEADS * NOPE_DIM
    ncol = nq + 2 * N_MLA_HEADS * LANES
    idx = np.zeros((ncol,), np.int32)
    valid = np.zeros((ncol,), bool)
    for h in range(N_MLA_HEADS):
        idx[h * NOPE_DIM:(h + 1) * NOPE_DIM] = h * hd + np.arange(NOPE_DIM)
        valid[h * NOPE_DIM:(h + 1) * NOPE_DIM] = True
        for blk, perm in enumerate((_DEINT, _DEINT_SW)):
            c0 = nq + blk * N_MLA_HEADS * LANES + h * LANES
            idx[c0:c0 + ROPE_DIM] = h * hd + NOPE_DIM + perm
            valid[c0:c0 + ROPE_DIM] = True
    return jnp.where(valid[None, :], jnp.take(wuq, idx, axis=1), 0.0).astype(BF16)


def _attention_keys(geo, ckvn, krr, cache_c, cache_kr):
    nc = geo.nc_tok
    pad = QK_W - KV_RANK - ROPE_DIM

    def keys(c, kr):
        kc = jnp.concatenate([c, kr, jnp.zeros(c.shape[:2] + (pad,), F32)], axis=-1).astype(BF16)
        return kc.transpose(0, 2, 1), c.astype(BF16)

    kr32 = krr[:, :ROPE_DIM]
    kct_c, cv_c = keys(ckvn[:nc].reshape(geo.n_ctx, geo.t_ctx, KV_RANK),
                       kr32[:nc].reshape(geo.n_ctx, geo.t_ctx, ROPE_DIM))
    c_l = jnp.concatenate([ckvn[nc:].reshape(geo.n_lat, geo.t_lat, KV_RANK), cache_c], axis=1)
    kr_l = jnp.concatenate([kr32[nc:].reshape(geo.n_lat, geo.t_lat, ROPE_DIM), cache_kr], axis=1)
    kct_l, cv_l = keys(c_l, kr_l)
    return kct_c, cv_c, kct_l, cv_l


def kernel(x_prompt, x_sample, cache_ckv, cache_krope, state_wkv, c, c_ctx, ada_w, ada_b, norm_mix_pre, norm_mix_post, norm_ffn_pre, norm_ffn_post, w_in, b_in, conv_w, conv_b, conv_ln_g, conv_ln_b, conv_wo, rwkv_mu, rwkv_w0, rwkv_bw, rwkv_a0, rwkv_ba, rwkv_bg, rwkv_xi, rwkv_alpha, rwkv_rho, rwkv_gn_g, rwkv_gn_b, rwkv_wo, mla_q_norm, mla_wuq, mla_kv_norm, mla_wuk, mla_wuv, mla_wo, w_out, ffn_w1, ffn_w3, ffn_w2, moe_router, moe_w1, moe_w3, moe_w2):
    P = dict(rwkv_mu=rwkv_mu, rwkv_w0=rwkv_w0, rwkv_bw=rwkv_bw, rwkv_a0=rwkv_a0, rwkv_ba=rwkv_ba,
             rwkv_bg=rwkv_bg, rwkv_xi=rwkv_xi, rwkv_alpha=rwkv_alpha, rwkv_rho=rwkv_rho)
    n_ctx, t_ctx, d = x_prompt.shape
    n_lat, t_lat, _ = x_sample.shape
    depth = ada_w.shape[0]
    past = cache_ckv.shape[2]
    geo = Geo(n_ctx, t_ctx, n_lat, t_lat, past)
    assert d == D_MODEL and (n_ctx * N_RWKV_HEADS) % LANES == 0 and LANES % (n_lat * 2 * N_RWKV_HEADS) == 0

    x = jnp.concatenate([x_prompt.reshape(-1, d), x_sample.reshape(-1, d)], axis=0)

    n_mod = 1 + n_lat
    n_mod_pad = -(-n_mod // SUBLANES) * SUBLANES
    c_all = jnp.zeros((n_mod_pad, d), F32).at[0].set(c_ctx).at[1:n_mod].set(c)
    head_id = np.arange(D_RWKV) // RWKV_HEAD
    ones_bd = jnp.asarray((head_id[:, None] == head_id[None, :]).astype(np.float32))
    cos_t, sin_t = _rope_tables(geo)

    ckv_out, kr_out, st_out = [], [], []
    for l in range(depth):
        mod = matmul_bias(c_all, ada_w[l].astype(BF16), ada_b[l], tm=n_mod_pad, tn=6 * d // 4,
                          pre="silu", name="ada_mod")
        mod = mod[:n_mod].reshape(n_mod * 6, 1, d)

        h = prenorm(x, norm_mix_pre[l], mod, geo, j_shift=0, j_scale=1)
        w_in_p, b_in_p = _prep_w_in(w_in[l], b_in[l])
        u = matmul_bias(h, w_in_p, b_in_p, tm=512, tn=1280, name="in_proj")

        y_conv = conv_branch(u, conv_w[l], conv_b[l], conv_ln_g[l], conv_ln_b[l], conv_wo[l], geo)

        pre = rwkv_pre(u, P, l, geo, ones_bd)
        x_ctx, v_ctx, rev_ctx, x_lat, v_lat, s0_lat, kl = _scan_layouts(geo, pre[:9], state_wkv[:, l])
        s0_ctx = jnp.zeros((x_ctx.shape[0], RWKV_HEAD // SUBLANES, RWKV_HEAD, SUBLANES, LANES), F32)
        o_ctx, sfin_ctx = wkv_scan_ctx(x_ctx, v_ctx, s0_ctx, rev_ctx, tc=min(16, t_ctx))
        oa_lat, ob_lat = wkv_scan_lat(x_lat, v_lat, s0_lat, tc=min(64, t_lat), kl=kl)
        o2, sfin = _scan_outputs(geo, o_ctx, sfin_ctx, oa_lat, ob_lat, kl)
        y_rwkv = rwkv_post(o2, pre[10], pre[9], u, rwkv_gn_g[l], rwkv_gn_b[l], ones_bd, rwkv_wo[l], geo)

        wukt = mla_wuk[l].reshape(KV_RANK, N_MLA_HEADS, NOPE_DIM).transpose(1, 2, 0).astype(BF16)
        qf, ckvn, krr = mla_pre(u, cos_t, sin_t, mla_q_norm[l], mla_kv_norm[l],
                                _prep_wuq(mla_wuq[l]), wukt, geo)
        kct_c, cv_c, kct_l, cv_l = _attention_keys(geo, ckvn, krr, cache_ckv[:, l],
                                                   cache_krope[:, l][..., _DEINT])
        wuv_h = mla_wuv[l].reshape(KV_RANK, N_MLA_HEADS, V_DIM).transpose(1, 0, 2).astype(BF16)
        wo_h = mla_wo[l].reshape(N_MLA_HEADS, V_DIM, d).astype(BF16)
        y_mla = attention(qf, kct_l, cv_l, wuv_h, wo_h, u, None, row0=geo.nc_tok, t_seq=t_lat,
                          tq=TILE, name="attention_lat")
        y_mla = attention(qf, kct_c, cv_c, wuv_h, wo_h, u, y_mla, row0=0, t_seq=t_ctx,
                          tq=TILE, name="attention_ctx")

        moe = (l % 2 == 1)
        i = l // 2
        router = moe_router[i] if moe else jnp.zeros((d, N_EXPERTS), F32)
        x1, h2, comb = merge(x, y_conv, y_rwkv, y_mla, w_out[l], norm_mix_post[l], norm_ffn_pre[l],
                             mod, router, geo, moe)
        if moe:
            row_src, dest, te, nused = _route_plan(comb, MOE_TM)
            xs = gather_rows(h2, row_src, rows=TILE)
            ys = ffn_sorted(xs, te, nused, moe_w1[i].astype(BF16), moe_w3[i].astype(BF16),
                            moe_w2[i].astype(BF16), tm=MOE_TM, tf=D_FF // 2)
            x = moe_combine_post(x1, ys, dest, comb, norm_ffn_post[l], mod, geo)
        else:
            y = ffn(h2, comb, ffn_w1[i:i + 1].astype(BF16), ffn_w3[i:i + 1].astype(BF16),
                    ffn_w2[i:i + 1].astype(BF16), tm=512, tf=D_FF // 2)
            x = ffn_post(x1, y, norm_ffn_post[l], mod, geo)

        ckv_out.append(ckvn[:geo.nc_tok].reshape(n_ctx, t_ctx, KV_RANK))
        kr_out.append(u[:geo.nc_tok, C_KR:C_KR + ROPE_DIM].reshape(n_ctx, t_ctx, ROPE_DIM))
        st_out.append(sfin)

    y_prompt = x[:geo.nc_tok].reshape(n_ctx, t_ctx, d)
    y_sample = x[geo.nc_tok:].reshape(n_lat, t_lat, d)
    return (y_prompt, y_sample, jnp.stack(ckv_out, axis=1), jnp.stack(kr_out, axis=1),
            jnp.stack(st_out, axis=1))
```

```python
import functools
import math

import numpy as np
import jax
import jax.numpy as jnp
from jax import lax
from jax.experimental import pallas as pl
from jax.experimental.pallas import tpu as pltpu

F32 = jnp.float32
BF16 = jnp.bfloat16

D_MODEL = 1024
GRID_W = 64
D_CONV = 512
CONV_K = 31
D_RWKV = 512
RWKV_HEAD = 64
N_RWKV_HEADS = D_RWKV // RWKV_HEAD
W_RANK = 64
A_RANK = 64
G_RANK = 128
DECAY_SCALE = math.exp(-0.5)
GN_EPS = 64e-5
N_MLA_HEADS = 8
Q_RANK = 256
KV_RANK = 128
NOPE_DIM = 64
ROPE_DIM = 32
V_DIM = 64
ROPE_BASE = 10000.0
ATTN_SCALE = 1.0 / math.sqrt(NOPE_DIM + ROPE_DIM)
N_BRANCH = 3
D_FF = 2816
N_EXPERTS = 8
EPS = 1e-6

LANES = 128
SUBLANES = 8
VMEM_LIMIT_BYTES = 56 * 1024 * 1024

TILE = 256
CONV_HALO = 16
SHIFT_HALO = 8

C_CONV, C_G0, C_G1, C_G2 = 0, 1024, 2048, 3072
C_R, C_K, C_V = 4096, 4608, 5120
C_LORA = 5632
C_CQ = 5888
C_CKV = 6144
C_KR = 6272
IN_PAD = 6400


def _cparams(sem):
    return pltpu.CompilerParams(dimension_semantics=sem, vmem_limit_bytes=VMEM_LIMIT_BYTES)


def _sigmoid(x):
    return jax.nn.sigmoid(x)


def _rms(x, g):
    return (x * lax.rsqrt(jnp.mean(x * x, axis=-1, keepdims=True) + EPS)) * g


class Geo:
    def __init__(self, n_ctx, t_ctx, n_lat, t_lat, past):
        assert t_ctx % TILE == 0 and t_lat % TILE == 0
        self.n_ctx, self.t_ctx, self.n_lat, self.t_lat, self.past = n_ctx, t_ctx, n_lat, t_lat, past
        self.ct = t_ctx // TILE
        self.lt = t_lat // TILE
        self.nct = n_ctx * self.ct
        self.nlt = n_lat * self.lt
        self.ntile = self.nct + self.nlt
        self.nc_tok = n_ctx * t_ctx
        self.nl_tok = n_lat * t_lat
        self.ntok = self.nc_tok + self.nl_tok

    def pos(self, i):
        is_ctx = i < self.nct
        p = jnp.where(is_ctx, i % self.ct, (i - self.nct) % self.lt)
        n = jnp.where(is_ctx, self.ct, self.lt)
        return p, n

    def mod_row(self, i):
        return jnp.where(i < self.nct, 0, 1 + (i - self.nct) // self.lt)

    def rope_blk(self, i):
        return jnp.where(i < self.nct, 0, 1 + (i - self.nct) % self.lt)


def _mm_kernel(x_ref, w_ref, b_ref, o_ref, *, pre):
    x = x_ref[...]
    if pre == "silu":
        x = x.astype(F32)
        x = x * _sigmoid(x)
    acc = jnp.dot(x.astype(BF16), w_ref[...], preferred_element_type=F32)
    o_ref[...] = (acc + b_ref[...]).astype(o_ref.dtype)


def matmul_bias(x, w, b, *, tm, tn, pre=None, out_dtype=F32, name="matmul"):
    m, k = x.shape
    n = w.shape[1]
    assert m % tm == 0 and n % tn == 0
    return pl.pallas_call(
        functools.partial(_mm_kernel, pre=pre),
        grid=(n // tn, m // tm),
        in_specs=[
            pl.BlockSpec((tm, k), lambda j, i: (i, 0)),
            pl.BlockSpec((k, tn), lambda j, i: (0, j)),
            pl.BlockSpec((1, tn), lambda j, i: (0, j)),
        ],
        out_specs=pl.BlockSpec((tm, tn), lambda j, i: (i, j)),
        out_shape=jax.ShapeDtypeStruct((m, n), out_dtype),
        compiler_params=_cparams(("arbitrary", "arbitrary")),
        name=name,
    )(x, w, b.reshape(1, n).astype(F32))


def _prenorm_kernel(x_ref, g_ref, sc_ref, sh_ref, o_ref):
    h = _rms(x_ref[...], g_ref[...]) * (1.0 + sc_ref[0]) + sh_ref[0]
    o_ref[...] = h.astype(o_ref.dtype)


def prenorm(x, g, mod, geo, j_shift, j_scale):
    n, d = x.shape
    return pl.pallas_call(
        _prenorm_kernel,
        grid=(geo.ntile,),
        in_specs=[
            pl.BlockSpec((TILE, d), lambda i: (i, 0)),
            pl.BlockSpec((1, d), lambda i: (0, 0)),
            pl.BlockSpec((1, 1, d), lambda i: (geo.mod_row(i) * 6 + j_scale, 0, 0)),
            pl.BlockSpec((1, 1, d), lambda i: (geo.mod_row(i) * 6 + j_shift, 0, 0)),
        ],
        out_specs=pl.BlockSpec((TILE, d), lambda i: (i, 0)),
        out_shape=jax.ShapeDtypeStruct((n, d), BF16),
        compiler_params=_cparams(("arbitrary",)),
        name="prenorm",
    )(x, g.reshape(1, d), mod, mod)


def _conv_kernel(cur_ref, prev_ref, next_ref, gate_ref, cw_ref, cb_ref, lng_ref, lnb_ref,
                 wo_ref, o_ref, hp_ref, *, geo):
    i = pl.program_id(0)
    p, n = geo.pos(i)
    has_prev = p > 0
    has_next = p < n - 1

    def glu(z):
        return z[:, :D_CONV] * _sigmoid(z[:, D_CONV:])

    hp_ref[0:CONV_HALO, :] = jnp.where(has_prev, glu(prev_ref[...]), 0.0)
    hp_ref[CONV_HALO:CONV_HALO + TILE, :] = glu(cur_ref[...])
    hp_ref[CONV_HALO + TILE:, :] = jnp.where(has_next, glu(next_ref[...]), 0.0)

    off = CONV_HALO - CONV_K // 2
    acc = hp_ref[pl.ds(off, TILE), :] * cw_ref[0:1, :]
    for j in range(1, CONV_K):
        acc = acc + hp_ref[pl.ds(off + j, TILE), :] * cw_ref[j:j + 1, :]
    h = acc + cb_ref[...]
    mu = jnp.mean(h, axis=-1, keepdims=True)
    hc = h - mu
    var = jnp.mean(hc * hc, axis=-1, keepdims=True)
    h = hc * lax.rsqrt(var + EPS) * lng_ref[...] + lnb_ref[...]
    h = h * _sigmoid(h)
    y = jnp.dot(h.astype(BF16), wo_ref[...], preferred_element_type=F32)
    o_ref[...] = _sigmoid(gate_ref[...]) * y


def conv_branch(u, cw, cb, lng, lnb, wo, geo):
    n = u.shape[0]
    hb = TILE // CONV_HALO
    nhalo = n // CONV_HALO
    cwp = jnp.zeros((32, D_CONV), F32).at[:CONV_K].set(cw)
    return pl.pallas_call(
        functools.partial(_conv_kernel, geo=geo),
        grid=(geo.ntile,),
        in_specs=[
            pl.BlockSpec((TILE, 2 * D_CONV), lambda i: (i, C_CONV // (2 * D_CONV))),
            pl.BlockSpec((CONV_HALO, 2 * D_CONV), lambda i: (jnp.maximum(i * hb - 1, 0), 0)),
            pl.BlockSpec((CONV_HALO, 2 * D_CONV), lambda i: (jnp.minimum((i + 1) * hb, nhalo - 1), 0)),
            pl.BlockSpec((TILE, D_MODEL), lambda i: (i, C_G0 // D_MODEL)),
            pl.BlockSpec((32, D_CONV), lambda i: (0, 0)),
            pl.BlockSpec((1, D_CONV), lambda i: (0, 0)),
            pl.BlockSpec((1, D_CONV), lambda i: (0, 0)),
            pl.BlockSpec((1, D_CONV), lambda i: (0, 0)),
            pl.BlockSpec((D_CONV, D_MODEL), lambda i: (0, 0)),
        ],
        out_specs=pl.BlockSpec((TILE, D_MODEL), lambda i: (i, 0)),
        out_shape=jax.ShapeDtypeStruct((n, D_MODEL), F32),
        scratch_shapes=[pltpu.VMEM((TILE + 2 * CONV_HALO, D_CONV), F32)],
        compiler_params=_cparams(("arbitrary",)),
        name="conv_branch",
    )(u, u, u, u, cwp, cb.reshape(1, -1), lng.reshape(1, -1), lnb.reshape(1, -1), wo.astype(BF16))


def _seg_sum(x, ones_bd):
    return jnp.dot(x, ones_bd, preferred_element_type=F32, precision=lax.Precision.HIGHEST)


def _rwkv_pre_kernel(r_ref, rp_ref, rn_ref, k_ref, kp_ref, kn_ref, v_ref, vp_ref, vn_ref,
                     lora_ref, mu_ref, w0_ref, bw_ref, a0_ref, ba_ref, bg_ref, xi_ref, al_ref,
                     rho_ref, ones_ref,
                     r_o, kk_o, v_o, w0_o, w1_o, ka0_o, ka1_o, kt0_o, kt1_o, g_o, bonus_o, *, geo, tile0):
    i = pl.program_id(0) + tile0
    p, n = geo.pos(i)
    has_prev = p > 0
    has_next = p < n - 1
    row = lax.broadcasted_iota(jnp.int32, (TILE, D_RWKV), 0)

    def shifted(c_ref, p_ref, n_ref, mu):
        cur = c_ref[...]
        pv = jnp.where(has_prev, p_ref[SHIFT_HALO - 1:SHIFT_HALO, :], 0.0)
        nx = jnp.where(has_next, n_ref[0:1, :], 0.0)
        prev = jnp.where(row == 0, pv, pltpu.roll(cur, 1, axis=0))
        nxt = jnp.where(row == TILE - 1, nx, pltpu.roll(cur, TILE - 1, axis=0))
        return cur + mu * (0.5 * (prev + nxt) - cur)

    r = shifted(r_ref, rp_ref, rn_ref, mu_ref[0:1, :])
    k = shifted(k_ref, kp_ref, kn_ref, mu_ref[1:2, :])
    v = shifted(v_ref, vp_ref, vn_ref, mu_ref[2:3, :])

    lora = lora_ref[...]
    dw = jnp.tanh(lora[:, :W_RANK]).astype(BF16)
    da = lora[:, W_RANK:W_RANK + A_RANK].astype(BF16)
    dg = _sigmoid(lora[:, W_RANK + A_RANK:]).astype(BF16)
    wl = jnp.dot(dw, bw_ref[...], preferred_element_type=F32)
    al = jnp.dot(da, ba_ref[...], preferred_element_type=F32)
    g = jnp.dot(dg, bg_ref[...], preferred_element_type=F32)

    ones_bd = ones_ref[...]
    kx = k * xi_ref[...]
    kk = kx * lax.rsqrt(_seg_sum(kx * kx, ones_bd) + EPS)
    alpha = al_ref[...]
    rho = rho_ref[...]
    bonus = jnp.zeros((TILE, D_RWKV), F32)
    for d, (w_o, ka_o, kt_o) in enumerate(((w0_o, ka0_o, kt0_o), (w1_o, ka1_o, kt1_o))):
        sl = slice(d * D_RWKV, (d + 1) * D_RWKV)
        w = jnp.exp(-DECAY_SCALE * _sigmoid(w0_ref[d:d + 1, :] + wl[:, sl]))
        a = _sigmoid(a0_ref[d:d + 1, :] + al[:, sl])
        kt = k * (1.0 + (a - 1.0) * alpha)
        w_o[...] = w
        ka_o[...] = kk * a
        kt_o[...] = kt
        bonus = bonus + _seg_sum(r * kt * rho, ones_bd) * v
    r_o[...] = r
    kk_o[...] = kk
    v_o[...] = v
    g_o[...] = g
    bonus_o[...] = bonus


def rwkv_pre(u, P, l, geo, ones_bd, tile0, ntiles):
    hb = TILE // SHIFT_HALO
    nhalo = u.shape[0] // SHIFT_HALO

    def trio(c0):
        cb = c0 // D_RWKV
        return [
            pl.BlockSpec((TILE, D_RWKV), lambda i: (i + tile0, cb)),
            pl.BlockSpec((SHIFT_HALO, D_RWKV), lambda i: (jnp.maximum((i + tile0) * hb - 1, 0), cb)),
            pl.BlockSpec((SHIFT_HALO, D_RWKV), lambda i: (jnp.minimum((i + tile0 + 1) * hb, nhalo - 1), cb)),
        ]

    def full(shape):
        return pl.BlockSpec(shape, lambda i: tuple(0 for _ in shape))

    bw = jnp.concatenate([P["rwkv_bw"][l, 0], P["rwkv_bw"][l, 1]], axis=1).astype(BF16)
    ba = jnp.concatenate([P["rwkv_ba"][l, 0], P["rwkv_ba"][l, 1]], axis=1).astype(BF16)
    out = jax.ShapeDtypeStruct((ntiles * TILE, D_RWKV), F32)
    return pl.pallas_call(
        functools.partial(_rwkv_pre_kernel, geo=geo, tile0=tile0),
        grid=(ntiles,),
        in_specs=trio(C_R) + trio(C_K) + trio(C_V) + [
            pl.BlockSpec((TILE, 256), lambda i: (i + tile0, C_LORA // 256)),
            full((3, D_RWKV)), full((2, D_RWKV)), full((W_RANK, 2 * D_RWKV)),
            full((2, D_RWKV)), full((A_RANK, 2 * D_RWKV)), full((G_RANK, D_RWKV)),
            full((1, D_RWKV)), full((1, D_RWKV)), full((1, D_RWKV)), full((D_RWKV, D_RWKV)),
        ],
        out_specs=[pl.BlockSpec((TILE, D_RWKV), lambda i: (i, 0))] * 11,
        out_shape=[out] * 11,
        compiler_params=_cparams(("arbitrary",)),
        name="rwkv_pre",
    )(u, u, u, u, u, u, u, u, u, u,
      P["rwkv_mu"][l], P["rwkv_w0"][l], bw, P["rwkv_a0"][l], ba, P["rwkv_bg"][l].astype(BF16),
      P["rwkv_xi"][l].reshape(1, -1), P["rwkv_alpha"][l].reshape(1, -1),
      P["rwkv_rho"][l].reshape(1, -1), ones_bd)


N_VBLK = RWKV_HEAD // SUBLANES


def _scan_steps(xrow, vload, ostore, s_ref, tidx, *, tc, nk, kl):
    groups = (tuple(range(0, N_VBLK // 2)), tuple(range(N_VBLK // 2, N_VBLK)))

    def allred(a):
        out = a
        for j in range(1, kl):
            out = out + pltpu.roll(a, j * (LANES // kl), axis=1)
        return out

    def first_sa(grp, t):
        acc = [None] * len(grp)
        for kh in range(nk):
            kkb = xrow(t, 1, kh)
            for j, vb in enumerate(grp):
                pr = s_ref[vb, kh] * kkb
                acc[j] = pr if acc[j] is None else acc[j] + pr
        return tuple(allred(a) for a in acc)

    def fused(grp, s, t, t_next, sa):
        vv = [vload(t, vb) for vb in grp]
        oacc = [None] * len(grp)
        acc = [None] * len(grp)
        for kh in range(nk):
            wb, kab, ktb, rb = xrow(t, 0, kh), xrow(t, 2, kh), xrow(t, 3, kh), xrow(t, 4, kh)
            kkn = xrow(t_next, 1, kh)
            for j, vb in enumerate(grp):
                sn = s_ref[vb, kh] * wb - sa[j] * kab + vv[j] * ktb
                s_ref[vb, kh] = sn
                po = sn * rb
                pa = sn * kkn
                oacc[j] = po if oacc[j] is None else oacc[j] + po
                acc[j] = pa if acc[j] is None else acc[j] + pa
        for j, vb in enumerate(grp):
            ostore(s, t, vb, oacc[j])
        return tuple(allred(a) for a in acc)

    t0 = tidx(0)
    carry0 = tuple(first_sa(g, t0) for g in groups)

    def step(s, carry):
        t = tidx(s)
        t_next = tidx(jnp.minimum(s + 1, tc - 1))
        return tuple(fused(g, s, t, t_next, carry[i]) for i, g in enumerate(groups))

    lax.fori_loop(0, tc, step, carry0)


def _vrows(vb):
    return pl.ds(vb * SUBLANES, SUBLANES)


def _scan_ctx_kernel(w_ref, kk_ref, ka_ref, kt_ref, r_ref, v_ref, o_ref, sfin_ref, s_ref, *, tc, nk, rev):
    c = pl.program_id(1)
    x_refs = (w_ref, kk_ref, ka_ref, kt_ref, r_ref)

    @pl.when(c == 0)
    def _():
        s_ref[...] = jnp.zeros(s_ref.shape, F32)

    def xrow(t, a, kh):
        return jnp.broadcast_to(x_refs[a][0, t, pl.ds(kh, 1), :], (SUBLANES, LANES))

    def vload(t, vb):
        return v_ref[0, t, _vrows(vb), :]

    def ostore(s, t, vb, val):
        o_ref[0, t, _vrows(vb), :] = val

    def tidx(s):
        return tc - 1 - s if rev else s

    _scan_steps(xrow, vload, ostore, s_ref, tidx, tc=tc, nk=nk, kl=1)

    @pl.when(c == pl.num_programs(1) - 1)
    def _():
        sfin_ref[0] = s_ref[...]


def wkv_scan_ctx(w, kk, ka, kt, r, v, *, tc, rev):
    ngb, t_len, nk, _ = w.shape
    nch = t_len // tc
    blk = pl.BlockSpec((1, tc, nk, LANES), lambda g, c: (g, nch - 1 - c if rev else c, 0, 0))
    st = pl.BlockSpec((1, N_VBLK, nk, SUBLANES, LANES), lambda g, c: (g, 0, 0, 0, 0))
    return pl.pallas_call(
        functools.partial(_scan_ctx_kernel, tc=tc, nk=nk, rev=rev),
        grid=(ngb, nch),
        in_specs=[blk] * 6,
        out_specs=[blk, st],
        out_shape=[
            jax.ShapeDtypeStruct((ngb, t_len, RWKV_HEAD, LANES), F32),
            jax.ShapeDtypeStruct((ngb, N_VBLK, nk, SUBLANES, LANES), F32),
        ],
        scratch_shapes=[pltpu.VMEM((N_VBLK, nk, SUBLANES, LANES), F32)],
        compiler_params=_cparams(("arbitrary", "arbitrary")),
        name="wkv_scan_ctx",
    )(w, kk, ka, kt, r, v)


def _scan_lat_kernel(*refs, tc, nk, kl):
    xa_refs, xb_refs = refs[0:5], refs[5:10]
    va_ref, vb_ref, s0_ref, oa_ref, ob_ref, s_ref, xm_ref, vm_ref = refs[10:]
    c = pl.program_id(0)

    @pl.when(c == 0)
    def _():
        s_ref[...] = s0_ref[...]

    def is_bwd(shape):
        lane = lax.broadcasted_iota(jnp.int32, shape, len(shape) - 1)
        return (lane // N_RWKV_HEADS) % 2 == 1

    mx = is_bwd((nk, LANES))
    mv = is_bwd((RWKV_HEAD, LANES))

    def merge(s, carry):
        for a in range(5):
            xm_ref[a, s] = jnp.where(mx, xb_refs[a][tc - 1 - s], xa_refs[a][s])
        vm_ref[s] = jnp.where(mv, vb_ref[tc - 1 - s], va_ref[s])
        return carry

    lax.fori_loop(0, tc, merge, 0)

    def xrow(t, a, kh):
        return jnp.broadcast_to(xm_ref[a, t, pl.ds(kh, 1), :], (SUBLANES, LANES))

    def vload(t, vb):
        return vm_ref[t, _vrows(vb), :]

    def ostore(s, t, vb, val):
        oa_ref[s, _vrows(vb), :] = val
        ob_ref[tc - 1 - s, _vrows(vb), :] = val

    _scan_steps(xrow, vload, ostore, s_ref, lambda s: s, tc=tc, nk=nk, kl=kl)


def wkv_scan_lat(xs, v, s0, *, tc, kl):
    t_len, nk, _ = xs[0].shape
    nch = t_len // tc
    o_sds = jax.ShapeDtypeStruct((t_len, RWKV_HEAD, LANES), F32)
    xa = pl.BlockSpec((tc, nk, LANES), lambda c: (c, 0, 0))
    xb = pl.BlockSpec((tc, nk, LANES), lambda c: (nch - 1 - c, 0, 0))
    va = pl.BlockSpec((tc, RWKV_HEAD, LANES), lambda c: (c, 0, 0))
    vb = pl.BlockSpec((tc, RWKV_HEAD, LANES), lambda c: (nch - 1 - c, 0, 0))
    return pl.pallas_call(
        functools.partial(_scan_lat_kernel, tc=tc, nk=nk, kl=kl),
        grid=(nch,),
        in_specs=[xa] * 5 + [xb] * 5 + [va, vb,
                                        pl.BlockSpec((N_VBLK, nk, SUBLANES, LANES), lambda c: (0, 0, 0, 0))],
        out_specs=[va, vb],
        out_shape=[o_sds, o_sds],
        scratch_shapes=[pltpu.VMEM((N_VBLK, nk, SUBLANES, LANES), F32),
                        pltpu.VMEM((5, tc, nk, LANES), F32),
                        pltpu.VMEM((tc, RWKV_HEAD, LANES), F32)],
        compiler_params=_cparams(("arbitrary",)),
        name="wkv_scan_lat",
    )(*xs, *xs, v, v, s0)


def _rwkv_post_kernel(of_ref, ob_ref, bonus_ref, g_ref, gate_ref, gng_ref, gnb_ref, ones_ref,
                      wo_ref, *rest):
    y_ref = rest[-1]
    o = of_ref[...] + ob_ref[...]
    ones_bd = ones_ref[...]
    mean = _seg_sum(o, ones_bd) * (1.0 / RWKV_HEAD)
    oc = o - mean
    var = _seg_sum(oc * oc, ones_bd) * (1.0 / RWKV_HEAD)
    gn = oc * lax.rsqrt(var + GN_EPS) * gng_ref[...] + gnb_ref[...]
    y = (gn + bonus_ref[...]) * g_ref[...]
    y = jnp.dot(y.astype(BF16), wo_ref[...], preferred_element_type=F32)
    y_ref[...] = _sigmoid(gate_ref[...]) * y


def rwkv_post(o_f, o_b, bonus, g, u, gng, gnb, ones_bd, wo, y_prev, tile0):
    ntiles = bonus.shape[0] // TILE

    def full(shape):
        return pl.BlockSpec(shape, lambda i: tuple(0 for _ in shape))

    part = pl.BlockSpec((TILE, D_RWKV), lambda i: (i, 0))
    in_specs = [part, part, part, part,
                pl.BlockSpec((TILE, D_MODEL), lambda i: (i + tile0, C_G1 // D_MODEL)),
                full((1, D_RWKV)), full((1, D_RWKV)), full((D_RWKV, D_RWKV)), full((D_RWKV, D_MODEL))]
    args = [o_f, o_b, bonus, g, u, gng.reshape(1, -1), gnb.reshape(1, -1), ones_bd, wo.astype(BF16)]
    aliases = {}
    if y_prev is not None:
        in_specs.append(pl.BlockSpec(memory_space=pl.ANY))
        args.append(y_prev)
        aliases = {len(args) - 1: 0}
    return pl.pallas_call(
        _rwkv_post_kernel,
        grid=(ntiles,),
        in_specs=in_specs,
        out_specs=pl.BlockSpec((TILE, D_MODEL), lambda i: (i + tile0, 0)),
        out_shape=jax.ShapeDtypeStruct((u.shape[0], D_MODEL), F32),
        input_output_aliases=aliases,
        compiler_params=_cparams(("arbitrary",)),
        name="rwkv_post",
    )(*args)


QK_W = 2 * LANES
Q_SCALE = ATTN_SCALE * math.log2(math.e)


def _mla_pre_kernel(cq_ref, ckv_ref, kr_ref, cos_ref, sin_ref, qg_ref, kvg_ref, wuq_ref, wukt_ref,
                    qf_o, ckvn_o, krr_o):
    nq = N_MLA_HEADS * NOPE_DIM
    nrp = N_MLA_HEADS * LANES
    cq = _rms(cq_ref[...], qg_ref[...])
    q = jnp.dot(cq.astype(BF16), wuq_ref[...], preferred_element_type=F32)
    cos = cos_ref[...]
    sin = sin_ref[...]
    for h in range(N_MLA_HEADS):
        qn = q[:, h * NOPE_DIM:(h + 1) * NOPE_DIM].astype(BF16)
        qa = jnp.dot(qn, wukt_ref[h], preferred_element_type=F32)
        qr = (q[:, nq + h * LANES:nq + (h + 1) * LANES] * cos
              + q[:, nq + nrp + h * LANES:nq + nrp + (h + 1) * LANES] * sin)
        qf_o[:, h * QK_W:h * QK_W + LANES] = (qa * Q_SCALE).astype(qf_o.dtype)
        qf_o[:, h * QK_W + LANES:(h + 1) * QK_W] = (qr * Q_SCALE).astype(qf_o.dtype)
    ckvn_o[...] = _rms(ckv_ref[...], kvg_ref[...])
    kr = kr_ref[...]
    krr = kr[:, ROPE_DIM:2 * ROPE_DIM] * cos[:, :ROPE_DIM] + kr[:, 2 * ROPE_DIM:3 * ROPE_DIM] * sin[:, :ROPE_DIM]
    krr_o[...] = jnp.concatenate([krr, jnp.zeros((TILE, LANES - ROPE_DIM), F32)], axis=1)


def mla_pre(u, cos_t, sin_t, qg, kvg, wuq_p, wukt, geo):
    n = u.shape[0]

    def full(shape):
        return pl.BlockSpec(shape, lambda i: tuple(0 for _ in shape))

    return pl.pallas_call(
        _mla_pre_kernel,
        grid=(geo.ntile,),
        in_specs=[
            pl.BlockSpec((TILE, Q_RANK), lambda i: (i, C_CQ // Q_RANK)),
            pl.BlockSpec((TILE, KV_RANK), lambda i: (i, C_CKV // KV_RANK)),
            pl.BlockSpec((TILE, LANES), lambda i: (i, C_KR // LANES)),
            pl.BlockSpec((TILE, LANES), lambda i: (geo.rope_blk(i), 0)),
            pl.BlockSpec((TILE, LANES), lambda i: (geo.rope_blk(i), 0)),
            full((1, Q_RANK)), full((1, KV_RANK)),
            full(wuq_p.shape), full(wukt.shape),
        ],
        out_specs=[
            pl.BlockSpec((TILE, N_MLA_HEADS * QK_W), lambda i: (i, 0)),
            pl.BlockSpec((TILE, KV_RANK), lambda i: (i, 0)),
            pl.BlockSpec((TILE, LANES), lambda i: (i, 0)),
        ],
        out_shape=[
            jax.ShapeDtypeStruct((n, N_MLA_HEADS * QK_W), BF16),
            jax.ShapeDtypeStruct((n, KV_RANK), F32),
            jax.ShapeDtypeStruct((n, LANES), F32),
        ],
        compiler_params=_cparams(("arbitrary",)),
        name="mla_pre",
    )(u, u, u, cos_t, sin_t, qg.reshape(1, -1), kvg.reshape(1, -1), wuq_p, wukt)


def _attn_kernel(*refs, aliased):
    q_ref, kt_ref, c_ref, wuv_ref, wo_ref, gate_ref = refs[:6]
    y_ref, acc_ref = refs[-2:]
    h = pl.program_id(2)

    @pl.when(h == 0)
    def _():
        acc_ref[...] = jnp.zeros(acc_ref.shape, F32)

    s = jnp.dot(q_ref[...], kt_ref[0], preferred_element_type=F32)
    p = jnp.exp2(s - jnp.max(s, axis=-1, keepdims=True))
    l = jnp.sum(p, axis=-1, keepdims=True)
    pc = jnp.dot(p.astype(BF16), c_ref[0], preferred_element_type=F32) / l
    oh = jnp.dot(pc.astype(BF16), wuv_ref[0], preferred_element_type=F32)
    acc_ref[...] += jnp.dot(oh.astype(BF16), wo_ref[0], preferred_element_type=F32)

    @pl.when(h == pl.num_programs(2) - 1)
    def _():
        y_ref[...] = _sigmoid(gate_ref[...]) * acc_ref[...]


def attention(qf, kct, cv, wuv_h, wo_h, u, y_prev, *, row0, t_seq, tq, name):
    n = qf.shape[0]
    nseq, _, t_k = kct.shape
    qt = t_seq // tq
    rb0 = row0 // tq

    def rows(w, col):
        return pl.BlockSpec((tq, w), lambda s, i, h: (rb0 + s * qt + i, col(h)))

    in_specs = [
        rows(QK_W, lambda h: h),
        pl.BlockSpec((1, QK_W, t_k), lambda s, i, h: (s, 0, 0)),
        pl.BlockSpec((1, t_k, KV_RANK), lambda s, i, h: (s, 0, 0)),
        pl.BlockSpec((1, KV_RANK, V_DIM), lambda s, i, h: (h, 0, 0)),
        pl.BlockSpec((1, V_DIM, D_MODEL), lambda s, i, h: (h, 0, 0)),
        rows(D_MODEL, lambda h: C_G2 // D_MODEL),
    ]
    args = [qf, kct, cv, wuv_h, wo_h, u]
    aliases = {}
    if y_prev is not None:
        in_specs.append(pl.BlockSpec(memory_space=pl.ANY))
        args.append(y_prev)
        aliases = {len(args) - 1: 0}
    return pl.pallas_call(
        functools.partial(_attn_kernel, aliased=y_prev is not None),
        grid=(nseq, qt, N_MLA_HEADS),
        in_specs=in_specs,
        out_specs=rows(D_MODEL, lambda h: 0),
        out_shape=jax.ShapeDtypeStruct((n, D_MODEL), F32),
        scratch_shapes=[pltpu.VMEM((tq, D_MODEL), F32)],
        input_output_aliases=aliases,
        compiler_params=_cparams(("arbitrary", "arbitrary", "arbitrary")),
        name=name,
    )(*args)


R_I1, R_I2, R_G1, R_G2 = 0, 1, 2, 3


def _merge_kernel(x_ref, yc_ref, yr_ref, ym_ref, wout_ref, gpost_ref, g1_ref, gpre_ref, sc_ref,
                  sh_ref, router_ref, x1_o, h2_o, comb_o, *, moe):
    m = yc_ref[...] + yr_ref[...] + ym_ref[...]
    y = jnp.dot(m.astype(BF16), wout_ref[...], preferred_element_type=F32)
    x1 = x_ref[...] + g1_ref[0] * _rms(y, gpost_ref[...])
    x1_o[...] = x1
    h2 = _rms(x1, gpre_ref[...]) * (1.0 + sc_ref[0]) + sh_ref[0]
    h2_o[...] = h2.astype(h2_o.dtype)
    if moe:
        logits = jnp.dot(h2, router_ref[...], preferred_element_type=F32,
                         precision=lax.Precision.HIGHEST)
        lane = lax.broadcasted_iota(jnp.int32, logits.shape, 1)
        neg = jnp.float32(-jnp.inf)
        logits = jnp.where(lane < N_EXPERTS, logits, neg)
        m1 = jnp.max(logits, axis=-1, keepdims=True)
        i1 = jnp.min(jnp.where(logits == m1, lane, LANES), axis=-1, keepdims=True)
        rest = jnp.where(lane == i1, neg, logits)
        m2 = jnp.max(rest, axis=-1, keepdims=True)
        i2 = jnp.min(jnp.where(rest == m2, lane, LANES), axis=-1, keepdims=True)
        e2 = jnp.exp(m2 - m1)
        den = 1.0 + e2
        cols = ((R_I1, i1.astype(F32)), (R_I2, i2.astype(F32)), (R_G1, 1.0 / den), (R_G2, e2 / den))
        route = jnp.zeros(comb_o.shape, F32)
        for col, val in cols:
            route = jnp.where(lane == col, val, route)
        comb_o[...] = route
    else:
        comb_o[...] = jnp.ones(comb_o.shape, F32)


def merge(x, yc, yr, ym, wout, gpost, gpre, mod, router, geo, moe):
    n, d = x.shape

    def full(shape):
        return pl.BlockSpec(shape, lambda i: tuple(0 for _ in shape))

    def rows():
        return pl.BlockSpec((TILE, d), lambda i: (i, 0))

    def modspec(j):
        return pl.BlockSpec((1, 1, d), lambda i: (geo.mod_row(i) * 6 + j, 0, 0))

    router_p = jnp.zeros((d, LANES), F32).at[:, :N_EXPERTS].set(router)
    return pl.pallas_call(
        functools.partial(_merge_kernel, moe=moe),
        grid=(geo.ntile,),
        in_specs=[rows(), rows(), rows(), rows(), full((d, d)), full((1, d)), modspec(2),
                  full((1, d)), modspec(4), modspec(3), full((d, LANES))],
        out_specs=[rows(), rows(), pl.BlockSpec((TILE, LANES), lambda i: (i, 0))],
        out_shape=[jax.ShapeDtypeStruct((n, d), F32), jax.ShapeDtypeStruct((n, d), F32 if moe else BF16),
                   jax.ShapeDtypeStruct((n, LANES), F32)],
        compiler_params=_cparams(("arbitrary",)),
        name="merge",
    )(x, yc, yr, ym, wout.astype(BF16), gpost.reshape(1, d), mod, gpre.reshape(1, d), mod, mod,
      router_p)


def _ffn_kernel(h_ref, comb_ref, w1_ref, w3_ref, w2_ref, o_ref, acc_ref):
    e = pl.program_id(1)
    f = pl.program_id(2)

    @pl.when((e == 0) & (f == 0))
    def _():
        acc_ref[...] = jnp.zeros(acc_ref.shape, F32)

    h = h_ref[...]
    a = jnp.dot(h, w1_ref[0], preferred_element_type=F32)
    b = jnp.dot(h, w3_ref[0], preferred_element_type=F32)
    comb = comb_ref[...]
    lane = lax.broadcasted_iota(jnp.int32, comb.shape, 1)
    ce = jnp.sum(jnp.where(lane == e, comb, 0.0), axis=-1, keepdims=True)
    act = (a * _sigmoid(a) * b) * ce
    acc_ref[...] += jnp.dot(act.astype(BF16), w2_ref[0], preferred_element_type=F32)

    @pl.when((e == pl.num_programs(1) - 1) & (f == pl.num_programs(2) - 1))
    def _():
        o_ref[...] = acc_ref[...]


def ffn(h, comb, w1, w3, w2, *, tm, tf):
    n, d = h.shape
    ne, _, dff = w1.shape
    return pl.pallas_call(
        _ffn_kernel,
        grid=(n // tm, ne, dff // tf),
        in_specs=[
            pl.BlockSpec((tm, d), lambda i, e, f: (i, 0)),
            pl.BlockSpec((tm, LANES), lambda i, e, f: (i, 0)),
            pl.BlockSpec((1, d, tf), lambda i, e, f: (e, 0, f)),
            pl.BlockSpec((1, d, tf), lambda i, e, f: (e, 0, f)),
            pl.BlockSpec((1, tf, d), lambda i, e, f: (e, f, 0)),
        ],
        out_specs=pl.BlockSpec((tm, d), lambda i, e, f: (i, 0)),
        out_shape=jax.ShapeDtypeStruct((n, d), F32),
        scratch_shapes=[pltpu.VMEM((tm, d), F32)],
        compiler_params=_cparams(("arbitrary", "arbitrary", "arbitrary")),
        name="ffn",
    )(h, comb, w1, w3, w2)


MOE_TM = 512


def _row_copy(src_hbm, row, dst_vmem, r, sem):
    return pltpu.make_async_copy(src_hbm.at[pl.ds(row, 1)], dst_vmem.at[pl.ds(r, 1)], sem)


DMA_UNROLL = 8


def _gather_into(idx_ref, src_hbm, dst_vmem, sem, nrows):
    def start(i, c):
        for k in range(DMA_UNROLL):
            r = i * DMA_UNROLL + k
            _row_copy(src_hbm, idx_ref[0, 0, r], dst_vmem, r, sem).start(priority=k % 2)
        return c

    def wait(i, c):
        for k in range(DMA_UNROLL):
            _row_copy(src_hbm, 0, dst_vmem, i * DMA_UNROLL + k, sem).wait()
        return c

    lax.fori_loop(0, nrows // DMA_UNROLL, start, 0)
    lax.fori_loop(0, nrows // DMA_UNROLL, wait, 0)


def _gather_rows_kernel(idx_ref, src_hbm, o_ref, buf_ref, sem):
    _gather_into(idx_ref, src_hbm, buf_ref, sem, buf_ref.shape[0])
    o_ref[...] = buf_ref[...]


def gather_rows(src, idx, *, rows):
    p = idx.shape[0]
    w = src.shape[1]
    return pl.pallas_call(
        _gather_rows_kernel,
        grid=(p // rows,),
        in_specs=[
            pl.BlockSpec((1, 1, rows), lambda j: (j, 0, 0), memory_space=pltpu.SMEM),
            pl.BlockSpec(memory_space=pl.ANY),
        ],
        out_specs=pl.BlockSpec((rows, w), lambda j: (j, 0)),
        out_shape=jax.ShapeDtypeStruct((p, w), src.dtype),
        scratch_shapes=[pltpu.VMEM((rows, w), src.dtype), pltpu.SemaphoreType.DMA(())],
        compiler_params=_cparams(("arbitrary",)),
        name="moe_gather",
    )(idx.reshape(p // rows, 1, rows), src)


def _ffn_sorted_kernel(te_ref, nused_ref, x_ref, w1_ref, w3_ref, w2_ref, o_ref, acc_ref):
    j = pl.program_id(0)
    f = pl.program_id(1)

    @pl.when(j < nused_ref[0])
    def _():
        @pl.when(f == 0)
        def _():
            acc_ref[...] = jnp.zeros(acc_ref.shape, F32)

        h = x_ref[...].astype(BF16)
        a = jnp.dot(h, w1_ref[0], preferred_element_type=F32)
        b = jnp.dot(h, w3_ref[0], preferred_element_type=F32)
        act = a * _sigmoid(a) * b
        acc_ref[...] += jnp.dot(act.astype(BF16), w2_ref[0], preferred_element_type=F32)

        @pl.when(f == pl.num_programs(1) - 1)
        def _():
            o_ref[...] = acc_ref[...]


def ffn_sorted(xs, te, nused, w1, w3, w2, *, tm, tf):
    p, d = xs.shape
    dff = w1.shape[2]
    return pl.pallas_call(
        _ffn_sorted_kernel,
        grid_spec=pltpu.PrefetchScalarGridSpec(
            num_scalar_prefetch=2,
            grid=(p // tm, dff // tf),
            in_specs=[
                pl.BlockSpec((tm, d), lambda j, f, te, nu: (j, 0)),
                pl.BlockSpec((1, d, tf), lambda j, f, te, nu: (te[j], 0, f)),
                pl.BlockSpec((1, d, tf), lambda j, f, te, nu: (te[j], 0, f)),
                pl.BlockSpec((1, tf, d), lambda j, f, te, nu: (te[j], f, 0)),
            ],
            out_specs=pl.BlockSpec((tm, d), lambda j, f, te, nu: (j, 0)),
            scratch_shapes=[pltpu.VMEM((tm, d), F32)],
        ),
        out_shape=jax.ShapeDtypeStruct((p, d), F32),
        compiler_params=_cparams(("arbitrary", "arbitrary")),
        name="ffn_sorted",
    )(te, nused, xs, w1, w3, w2)


def _moe_combine_kernel(d0_ref, d1_ref, ys_hbm, route_ref, x_ref, g_ref, g2_ref, o_ref,
                        y0_ref, y1_ref, sem0, sem1):
    def start(i, c):
        for k in range(DMA_UNROLL):
            r = i * DMA_UNROLL + k
            _row_copy(ys_hbm, d0_ref[0, 0, r], y0_ref, r, sem0).start(priority=0)
            _row_copy(ys_hbm, d1_ref[0, 0, r], y1_ref, r, sem1).start(priority=1)
        return c

    def wait(i, c):
        for k in range(DMA_UNROLL):
            r = i * DMA_UNROLL + k
            _row_copy(ys_hbm, 0, y0_ref, r, sem0).wait()
            _row_copy(ys_hbm, 0, y1_ref, r, sem1).wait()
        return c

    lax.fori_loop(0, TILE // DMA_UNROLL, start, 0)
    lax.fori_loop(0, TILE // DMA_UNROLL, wait, 0)
    route = route_ref[...]
    lane = lax.broadcasted_iota(jnp.int32, route.shape, 1)
    ga = jnp.sum(jnp.where(lane == R_G1, route, 0.0), axis=-1, keepdims=True)
    gb = jnp.sum(jnp.where(lane == R_G2, route, 0.0), axis=-1, keepdims=True)
    y = ga * y0_ref[...] + gb * y1_ref[...]
    o_ref[...] = x_ref[...] + g2_ref[0] * _rms(y, g_ref[...])


def moe_combine_post(x1, ys, dest, route, gpost, mod, geo):
    n, d = x1.shape
    dd = dest.reshape(2, n // TILE, 1, TILE)
    return pl.pallas_call(
        _moe_combine_kernel,
        grid=(geo.ntile,),
        in_specs=[
            pl.BlockSpec((1, 1, TILE), lambda i: (i, 0, 0), memory_space=pltpu.SMEM),
            pl.BlockSpec((1, 1, TILE), lambda i: (i, 0, 0), memory_space=pltpu.SMEM),
            pl.BlockSpec(memory_space=pl.ANY),
            pl.BlockSpec((TILE, LANES), lambda i: (i, 0)),
            pl.BlockSpec((TILE, d), lambda i: (i, 0)),
            pl.BlockSpec((1, d), lambda i: (0, 0)),
            pl.BlockSpec((1, 1, d), lambda i: (geo.mod_row(i) * 6 + 5, 0, 0)),
        ],
        out_specs=pl.BlockSpec((TILE, d), lambda i: (i, 0)),
        out_shape=jax.ShapeDtypeStruct((n, d), F32),
        scratch_shapes=[pltpu.VMEM((TILE, d), F32), pltpu.VMEM((TILE, d), F32),
                        pltpu.SemaphoreType.DMA(()), pltpu.SemaphoreType.DMA(())],
        compiler_params=_cparams(("arbitrary",)),
        name="moe_combine",
    )(dd[0], dd[1], ys, route, x1, gpost.reshape(1, d), mod)


def _route_plan(route, tm):
    n = route.shape[0]
    e_flat = jnp.concatenate([route[:, R_I1], route[:, R_I2]]).astype(jnp.int32)
    onehot = (e_flat[:, None] == jnp.arange(N_EXPERTS, dtype=jnp.int32)[None, :]).astype(jnp.int32)
    csum = jnp.cumsum(onehot, axis=0)
    rank = jnp.take_along_axis(csum, e_flat[:, None], axis=1)[:, 0] - 1
    counts = csum[-1]
    ptiles = (counts + tm - 1) // tm
    tile_end = jnp.cumsum(ptiles)
    gstart = (tile_end - ptiles) * tm
    dest = jnp.take(gstart, e_flat) + rank
    p_max = 2 * n + N_EXPERTS * tm
    tok = jnp.tile(jnp.arange(n, dtype=jnp.int32), 2)
    row_src = jnp.zeros((p_max,), jnp.int32).at[dest].set(tok)
    tiles = jnp.arange(p_max // tm, dtype=jnp.int32)
    te = jnp.minimum(jnp.searchsorted(tile_end, tiles, side="right"), N_EXPERTS - 1).astype(jnp.int32)
    return row_src, dest.astype(jnp.int32).reshape(2, n), te, tile_end[-1:].astype(jnp.int32)


def _ffn_post_kernel(x_ref, y_ref, g_ref, g2_ref, o_ref):
    o_ref[...] = x_ref[...] + g2_ref[0] * _rms(y_ref[...], g_ref[...])


def ffn_post(x1, y, gpost, mod, geo):
    n, d = x1.shape
    return pl.pallas_call(
        _ffn_post_kernel,
        grid=(geo.ntile,),
        in_specs=[
            pl.BlockSpec((TILE, d), lambda i: (i, 0)),
            pl.BlockSpec((TILE, d), lambda i: (i, 0)),
            pl.BlockSpec((1, d), lambda i: (0, 0)),
            pl.BlockSpec((1, 1, d), lambda i: (geo.mod_row(i) * 6 + 5, 0, 0)),
        ],
        out_specs=pl.BlockSpec((TILE, d), lambda i: (i, 0)),
        out_shape=jax.ShapeDtypeStruct((n, d), F32),
        compiler_params=_cparams(("arbitrary",)),
        name="ffn_post",
    )(x1, y, gpost.reshape(1, d), mod)


def _ctx_scan(geo, pre):
    r, kk, v, w0, w1, ka0, ka1, kt0, kt1 = pre
    H, K = N_RWKV_HEADS, RWKV_HEAD
    bl = LANES // H
    nbh = geo.n_ctx // bl

    def to_scan(a):
        a = a.reshape(nbh, bl, geo.t_ctx, H, K)
        return a.transpose(0, 2, 4, 1, 3).reshape(nbh, geo.t_ctx, K, LANES)

    def from_scan(o):
        o = o.reshape(nbh, geo.t_ctx, K, bl, H).transpose(0, 3, 1, 4, 2)
        return o.reshape(geo.nc_tok, H * K)

    def state(sf):
        sf = sf.reshape(nbh, K // SUBLANES, K, SUBLANES, bl, H).transpose(0, 4, 5, 1, 3, 2)
        return sf.reshape(geo.n_ctx, H, K, K)

    kk_s, r_s, v_s = to_scan(kk), to_scan(r), to_scan(v)
    tc = min(16, geo.t_ctx)
    o_f, sf_f = wkv_scan_ctx(to_scan(w0), kk_s, to_scan(ka0), to_scan(kt0), r_s, v_s, tc=tc, rev=False)
    o_b, sf_b = wkv_scan_ctx(to_scan(w1), kk_s, to_scan(ka1), to_scan(kt1), r_s, v_s, tc=tc, rev=True)
    return from_scan(o_f), from_scan(o_b), jnp.stack([state(sf_f), state(sf_b)], axis=1)


def _lat_scan(geo, pre, state_l):
    r, kk, v, w0, w1, ka0, ka1, kt0, kt1 = pre
    H, K = N_RWKV_HEADS, RWKV_HEAD
    kl = LANES // (geo.n_lat * 2 * H)
    nk = K // kl

    def pair(a0, a1):
        a0 = a0.reshape(geo.n_lat, geo.t_lat, H, K)
        a1 = a1.reshape(geo.n_lat, geo.t_lat, H, K)
        a = jnp.stack([a0, a1], axis=2).reshape(geo.n_lat, geo.t_lat, 2, H, kl, nk)
        return a.transpose(1, 5, 4, 0, 2, 3).reshape(geo.t_lat, nk, LANES)

    xs = (pair(w0, w1), pair(kk, kk), pair(ka0, ka1), pair(kt0, kt1), pair(r, r))
    vl = v.reshape(geo.n_lat, geo.t_lat, H, K)
    vl = jnp.stack([vl, vl], axis=2)
    vl = vl.transpose(1, 4, 0, 2, 3).reshape(geo.t_lat, K, LANES // kl)
    v_lat = jnp.tile(vl, (1, 1, kl))
    s0 = state_l.reshape(geo.n_lat, 2, H, K // SUBLANES, SUBLANES, kl, nk)
    s0 = s0.transpose(3, 6, 4, 5, 0, 1, 2).reshape(K // SUBLANES, nk, SUBLANES, LANES)
    oa, ob = wkv_scan_lat(xs, v_lat, s0, tc=min(64, geo.t_lat), kl=kl)

    def direction(o, d):
        o = o.reshape(geo.t_lat, K, kl, geo.n_lat, 2, H)[:, :, :, :, d].sum(axis=2)
        return o.transpose(2, 0, 3, 1).reshape(geo.nl_tok, H * K)

    return direction(oa, 0), direction(ob, 1)


def _rope_tables(geo):
    n_freq = ROPE_DIM // 4
    rows = geo.t_lat // GRID_W
    row = jnp.repeat(jnp.arange(rows, dtype=F32), GRID_W)
    col = jnp.tile(jnp.arange(GRID_W, dtype=F32), rows)
    inv = ROPE_BASE ** (-jnp.arange(n_freq, dtype=F32) / n_freq)
    ang = jnp.concatenate([row[:, None] * inv, col[:, None] * inv], axis=-1)
    cos, sin = jnp.cos(ang), jnp.sin(ang)
    cos32 = jnp.concatenate([cos, cos], axis=-1)
    sin32 = jnp.concatenate([-sin, sin], axis=-1)
    cos_t = jnp.concatenate([jnp.ones((TILE, ROPE_DIM), F32), cos32], axis=0)
    sin_t = jnp.concatenate([jnp.zeros((TILE, ROPE_DIM), F32), sin32], axis=0)
    pad = ((0, 0), (0, LANES - ROPE_DIM))
    return jnp.pad(cos_t, pad), jnp.pad(sin_t, pad)


_DEINT = np.concatenate([np.arange(0, ROPE_DIM, 2), np.arange(1, ROPE_DIM, 2)])
_DEINT_SW = np.concatenate([np.arange(1, ROPE_DIM, 2), np.arange(0, ROPE_DIM, 2)])


def _prep_w_in(w_in, b_in):
    offs = np.cumsum([0, 2 * D_CONV, D_RWKV, D_RWKV, D_RWKV, W_RANK, A_RANK, G_RANK, Q_RANK, KV_RANK,
                      ROPE_DIM, N_BRANCH * D_MODEL])
    o_conv, o_r, o_k, o_v, o_dw, o_da, o_dg, o_cq, o_ckv, o_kr, o_gate, _ = offs
    idx = np.zeros((IN_PAD,), np.int32)
    valid = np.zeros((IN_PAD,), bool)

    def put(dst, src):
        idx[dst:dst + len(src)] = src
        valid[dst:dst + len(src)] = True

    put(C_CONV, np.arange(o_conv, o_conv + 2 * D_CONV))
    for b, c in enumerate((C_G0, C_G1, C_G2)):
        put(c, np.arange(o_gate + b * D_MODEL, o_gate + (b + 1) * D_MODEL))
    put(C_R, np.arange(o_r, o_r + D_RWKV))
    put(C_K, np.arange(o_k, o_k + D_RWKV))
    put(C_V, np.arange(o_v, o_v + D_RWKV))
    put(C_LORA, np.arange(o_dw, o_dw + W_RANK + A_RANK + G_RANK))
    put(C_CQ, np.arange(o_cq, o_cq + Q_RANK))
    put(C_CKV, np.arange(o_ckv, o_ckv + KV_RANK))
    put(C_KR, np.arange(o_kr, o_kr + ROPE_DIM))
    put(C_KR + ROPE_DIM, o_kr + _DEINT)
    put(C_KR + 2 * ROPE_DIM, o_kr + _DEINT_SW)
    w = jnp.where(valid[None, :], jnp.take(w_in, idx, axis=1), 0.0).astype(BF16)
    b = jnp.where(valid, jnp.take(b_in, idx), 0.0)
    return w, b


def _prep_wuq(wuq):
    hd = NOPE_DIM + ROPE_DIM
    nq = N_MLA_HEADS * NOPE_DIM
    ncol = nq + 2 * N_MLA_HEADS * LANES
    idx = np.zeros((ncol,), np.int32)
    valid = np.zeros((ncol,), bool)
    for h in range(N_MLA_HEADS):
        idx[h * NOPE_DIM:(h + 1) * NOPE_DIM] = h * hd + np.arange(NOPE_DIM)
        valid[h * NOPE_DIM:(h + 1) * NOPE_DIM] = True
        for blk, perm in enumerate((_DEINT, _DEINT_SW)):
            c0 = nq + blk * N_MLA_HEADS * LANES + h * LANES
            idx[c0:c0 + ROPE_DIM] = h * hd + NOPE_DIM + perm
            valid[c0:c0 + ROPE_DIM] = True
    return jnp.where(valid[None, :], jnp.take(wuq, idx, axis=1), 0.0).astype(BF16)


def _attention_keys(geo, ckvn, krr, cache_c, cache_kr):
    nc = geo.nc_tok
    pad = QK_W - KV_RANK - ROPE_DIM

    def keys(c, kr):
        kc = jnp.concatenate([c, kr, jnp.zeros(c.shape[:2] + (pad,), F32)], axis=-1).astype(BF16)
        return kc.transpose(0, 2, 1), c.astype(BF16)

    kr32 = krr[:, :ROPE_DIM]
    kct_c, cv_c = keys(ckvn[:nc].reshape(geo.n_ctx, geo.t_ctx, KV_RANK),
                       kr32[:nc].reshape(geo.n_ctx, geo.t_ctx, ROPE_DIM))
    c_l = jnp.concatenate([ckvn[nc:].reshape(geo.n_lat, geo.t_lat, KV_RANK), cache_c], axis=1)
    kr_l = jnp.concatenate([kr32[nc:].reshape(geo.n_lat, geo.t_lat, ROPE_DIM), cache_kr], axis=1)
    kct_l, cv_l = keys(c_l, kr_l)
    return kct_c, cv_c, kct_l, cv_l


def kernel(x_prompt, x_sample, cache_ckv, cache_krope, state_wkv, c, c_ctx, ada_w, ada_b, norm_mix_pre, norm_mix_post, norm_ffn_pre, norm_ffn_post, w_in, b_in, conv_w, conv_b, conv_ln_g, conv_ln_b, conv_wo, rwkv_mu, rwkv_w0, rwkv_bw, rwkv_a0, rwkv_ba, rwkv_bg, rwkv_xi, rwkv_alpha, rwkv_rho, rwkv_gn_g, rwkv_gn_b, rwkv_wo, mla_q_norm, mla_wuq, mla_kv_norm, mla_wuk, mla_wuv, mla_wo, w_out, ffn_w1, ffn_w3, ffn_w2, moe_router, moe_w1, moe_w3, moe_w2):
    P = dict(rwkv_mu=rwkv_mu, rwkv_w0=rwkv_w0, rwkv_bw=rwkv_bw, rwkv_a0=rwkv_a0, rwkv_ba=rwkv_ba,
             rwkv_bg=rwkv_bg, rwkv_xi=rwkv_xi, rwkv_alpha=rwkv_alpha, rwkv_rho=rwkv_rho)
    n_ctx, t_ctx, d = x_prompt.shape
    n_lat, t_lat, _ = x_sample.shape
    depth = ada_w.shape[0]
    past = cache_ckv.shape[2]
    geo = Geo(n_ctx, t_ctx, n_lat, t_lat, past)
    assert d == D_MODEL and (n_ctx * N_RWKV_HEADS) % LANES == 0 and LANES % (n_lat * 2 * N_RWKV_HEADS) == 0

    x = jnp.concatenate([x_prompt.reshape(-1, d), x_sample.reshape(-1, d)], axis=0)

    n_mod = 1 + n_lat
    n_mod_pad = -(-n_mod // SUBLANES) * SUBLANES
    c_all = jnp.zeros((n_mod_pad, d), F32).at[0].set(c_ctx).at[1:n_mod].set(c)
    head_id = np.arange(D_RWKV) // RWKV_HEAD
    ones_bd = jnp.asarray((head_id[:, None] == head_id[None, :]).astype(np.float32))
    cos_t, sin_t = _rope_tables(geo)

    ckv_out, kr_out, st_out = [], [], []
    for l in range(depth):
        mod = matmul_bias(c_all, ada_w[l].astype(BF16), ada_b[l], tm=n_mod_pad, tn=6 * d // 4,
                          pre="silu", name="ada_mod")
        mod = mod[:n_mod].reshape(n_mod * 6, 1, d)

        h = prenorm(x, norm_mix_pre[l], mod, geo, j_shift=0, j_scale=1)
        w_in_p, b_in_p = _prep_w_in(w_in[l], b_in[l])
        u = matmul_bias(h, w_in_p, b_in_p, tm=512, tn=1280, name="in_proj")

        y_conv = conv_branch(u, conv_w[l], conv_b[l], conv_ln_g[l], conv_ln_b[l], conv_wo[l], geo)

        pre_c = rwkv_pre(u, P, l, geo, ones_bd, 0, geo.nct)
        pre_l = rwkv_pre(u, P, l, geo, ones_bd, geo.nct, geo.nlt)
        oc_f, oc_b, sfin = _ctx_scan(geo, pre_c[:9])
        ol_f, ol_b = _lat_scan(geo, pre_l[:9], state_wkv[:, l])
        y_rwkv = rwkv_post(ol_f, ol_b, pre_l[10], pre_l[9], u, rwkv_gn_g[l], rwkv_gn_b[l], ones_bd,
                           rwkv_wo[l], None, geo.nct)
        y_rwkv = rwkv_post(oc_f, oc_b, pre_c[10], pre_c[9], u, rwkv_gn_g[l], rwkv_gn_b[l], ones_bd,
                           rwkv_wo[l], y_rwkv, 0)

        wukt = mla_wuk[l].reshape(KV_RANK, N_MLA_HEADS, NOPE_DIM).transpose(1, 2, 0).astype(BF16)
        qf, ckvn, krr = mla_pre(u, cos_t, sin_t, mla_q_norm[l], mla_kv_norm[l],
                                _prep_wuq(mla_wuq[l]), wukt, geo)
        kct_c, cv_c, kct_l, cv_l = _attention_keys(geo, ckvn, krr, cache_ckv[:, l],
                                                   cache_krope[:, l][..., _DEINT])
        wuv_h = mla_wuv[l].reshape(KV_RANK, N_MLA_HEADS, V_DIM).transpose(1, 0, 2).astype(BF16)
        wo_h = mla_wo[l].reshape(N_MLA_HEADS, V_DIM, d).astype(BF16)
        y_mla = attention(qf, kct_l, cv_l, wuv_h, wo_h, u, None, row0=geo.nc_tok, t_seq=t_lat,
                          tq=TILE, name="attention_lat")
        y_mla = attention(qf, kct_c, cv_c, wuv_h, wo_h, u, y_mla, row0=0, t_seq=t_ctx,
                          tq=TILE, name="attention_ctx")

        moe = (l % 2 == 1)
        i = l // 2
        router = moe_router[i] if moe else jnp.zeros((d, N_EXPERTS), F32)
        x1, h2, comb = merge(x, y_conv, y_rwkv, y_mla, w_out[l], norm_mix_post[l], norm_ffn_pre[l],
                             mod, router, geo, moe)
        if moe:
            row_src, dest, te, nused = _route_plan(comb, MOE_TM)
            xs = gather_rows(h2, row_src, rows=TILE)
            ys = ffn_sorted(xs, te, nused, moe_w1[i].astype(BF16), moe_w3[i].astype(BF16),
                            moe_w2[i].astype(BF16), tm=MOE_TM, tf=D_FF // 2)
            x = moe_combine_post(x1, ys, dest, comb, norm_ffn_post[l], mod, geo)
        else:
            y = ffn(h2, comb, ffn_w1[i:i + 1].astype(BF16), ffn_w3[i:i + 1].astype(BF16),
                    ffn_w2[i:i + 1].astype(BF16), tm=512, tf=D_FF // 2)
            x = ffn_post(x1, y, norm_ffn_post[l], mod, geo)

        ckv_out.append(ckvn[:geo.nc_tok].reshape(n_ctx, t_ctx, KV_RANK))
        kr_out.append(u[:geo.nc_tok, C_KR:C_KR + ROPE_DIM].reshape(n_ctx, t_ctx, ROPE_DIM))
        st_out.append(sfin)

    y_prompt = x[:geo.nc_tok].reshape(n_ctx, t_ctx, d)
    y_sample = x[geo.nc_tok:].reshape(n_lat, t_lat, d)
    return (y_prompt, y_sample, jnp.stack(ckv_out, axis=1), jnp.stack(kr_out, axis=1),
            jnp.stack(st_out, axis=1))
```

```python
import functools
import math

import numpy as np
import jax
import jax.numpy as jnp
from jax import lax
from jax.experimental import pallas as pl
from jax.experimental.pallas import tpu as pltpu

F32 = jnp.float32
BF16 = jnp.bfloat16

D_MODEL = 1024
GRID_W = 64
D_CONV = 512
CONV_K = 31
D_RWKV = 512
RWKV_HEAD = 64
N_RWKV_HEADS = D_RWKV // RWKV_HEAD
W_RANK = 64
A_RANK = 64
G_RANK = 128
DECAY_SCALE = math.exp(-0.5)
GN_EPS = 64e-5
N_MLA_HEADS = 8
Q_RANK = 256
KV_RANK = 128
NOPE_DIM = 64
ROPE_DIM = 32
V_DIM = 64
ROPE_BASE = 10000.0
ATTN_SCALE = 1.0 / math.sqrt(NOPE_DIM + ROPE_DIM)
N_BRANCH = 3
D_FF = 2816
N_EXPERTS = 8
EPS = 1e-6

LANES = 128
SUBLANES = 8
VMEM_LIMIT_BYTES = 56 * 1024 * 1024

TILE = 256
CONV_HALO = 16
SHIFT_HALO = 8

C_CONV, C_G0, C_G1, C_G2 = 0, 1024, 2048, 3072
C_R, C_K, C_V = 4096, 4608, 5120
C_LORA = 5632
C_CQ = 5888
C_CKV = 6144
C_KR = 6272
IN_PAD = 6400


def _cparams(sem):
    return pltpu.CompilerParams(dimension_semantics=sem, vmem_limit_bytes=VMEM_LIMIT_BYTES)


def _sigmoid(x):
    return jax.nn.sigmoid(x)


def _rms(x, g):
    return (x * lax.rsqrt(jnp.mean(x * x, axis=-1, keepdims=True) + EPS)) * g


class Geo:
    def __init__(self, n_ctx, t_ctx, n_lat, t_lat, past):
        assert t_ctx % TILE == 0 and t_lat % TILE == 0
        self.n_ctx, self.t_ctx, self.n_lat, self.t_lat, self.past = n_ctx, t_ctx, n_lat, t_lat, past
        self.ct = t_ctx // TILE
        self.lt = t_lat // TILE
        self.nct = n_ctx * self.ct
        self.nlt = n_lat * self.lt
        self.ntile = self.nct + self.nlt
        self.nc_tok = n_ctx * t_ctx
        self.nl_tok = n_lat * t_lat
        self.ntok = self.nc_tok + self.nl_tok

    def pos(self, i):
        is_ctx = i < self.nct
        p = jnp.where(is_ctx, i % self.ct, (i - self.nct) % self.lt)
        n = jnp.where(is_ctx, self.ct, self.lt)
        return p, n

    def mod_row(self, i):
        return jnp.where(i < self.nct, 0, 1 + (i - self.nct) // self.lt)

    def rope_blk(self, i):
        return jnp.where(i < self.nct, 0, 1 + (i - self.nct) % self.lt)


def _mm_kernel(x_ref, w_ref, b_ref, o_ref, *, pre):
    x = x_ref[...]
    if pre == "silu":
        x = x.astype(F32)
        x = x * _sigmoid(x)
    acc = jnp.dot(x.astype(BF16), w_ref[...], preferred_element_type=F32)
    o_ref[...] = (acc + b_ref[...]).astype(o_ref.dtype)


def matmul_bias(x, w, b, *, tm, tn, pre=None, out_dtype=F32, name="matmul"):
    m, k = x.shape
    n = w.shape[1]
    assert m % tm == 0 and n % tn == 0
    return pl.pallas_call(
        functools.partial(_mm_kernel, pre=pre),
        grid=(n // tn, m // tm),
        in_specs=[
            pl.BlockSpec((tm, k), lambda j, i: (i, 0)),
            pl.BlockSpec((k, tn), lambda j, i: (0, j)),
            pl.BlockSpec((1, tn), lambda j, i: (0, j)),
        ],
        out_specs=pl.BlockSpec((tm, tn), lambda j, i: (i, j)),
        out_shape=jax.ShapeDtypeStruct((m, n), out_dtype),
        compiler_params=_cparams(("arbitrary", "arbitrary")),
        name=name,
    )(x, w, b.reshape(1, n).astype(F32))


def _prenorm_kernel(x_ref, g_ref, sc_ref, sh_ref, o_ref):
    h = _rms(x_ref[...], g_ref[...]) * (1.0 + sc_ref[0]) + sh_ref[0]
    o_ref[...] = h.astype(o_ref.dtype)


def prenorm(x, g, mod, geo, j_shift, j_scale):
    n, d = x.shape
    return pl.pallas_call(
        _prenorm_kernel,
        grid=(geo.ntile,),
        in_specs=[
            pl.BlockSpec((TILE, d), lambda i: (i, 0)),
            pl.BlockSpec((1, d), lambda i: (0, 0)),
            pl.BlockSpec((1, 1, d), lambda i: (geo.mod_row(i) * 6 + j_scale, 0, 0)),
            pl.BlockSpec((1, 1, d), lambda i: (geo.mod_row(i) * 6 + j_shift, 0, 0)),
        ],
        out_specs=pl.BlockSpec((TILE, d), lambda i: (i, 0)),
        out_shape=jax.ShapeDtypeStruct((n, d), BF16),
        compiler_params=_cparams(("arbitrary",)),
        name="prenorm",
    )(x, g.reshape(1, d), mod, mod)


def _conv_kernel(cur_ref, prev_ref, next_ref, gate_ref, cw_ref, cb_ref, lng_ref, lnb_ref,
                 wo_ref, o_ref, hp_ref, *, geo):
    i = pl.program_id(0)
    p, n = geo.pos(i)
    has_prev = p > 0
    has_next = p < n - 1

    def glu(z):
        return z[:, :D_CONV] * _sigmoid(z[:, D_CONV:])

    hp_ref[0:CONV_HALO, :] = jnp.where(has_prev, glu(prev_ref[...]), 0.0)
    hp_ref[CONV_HALO:CONV_HALO + TILE, :] = glu(cur_ref[...])
    hp_ref[CONV_HALO + TILE:, :] = jnp.where(has_next, glu(next_ref[...]), 0.0)

    off = CONV_HALO - CONV_K // 2
    acc = hp_ref[pl.ds(off, TILE), :] * cw_ref[0:1, :]
    for j in range(1, CONV_K):
        acc = acc + hp_ref[pl.ds(off + j, TILE), :] * cw_ref[j:j + 1, :]
    h = acc + cb_ref[...]
    mu = jnp.mean(h, axis=-1, keepdims=True)
    hc = h - mu
    var = jnp.mean(hc * hc, axis=-1, keepdims=True)
    h = hc * lax.rsqrt(var + EPS) * lng_ref[...] + lnb_ref[...]
    h = h * _sigmoid(h)
    y = jnp.dot(h.astype(BF16), wo_ref[...], preferred_element_type=F32)
    o_ref[...] = _sigmoid(gate_ref[...]) * y


def conv_branch(u, cw, cb, lng, lnb, wo, geo):
    n = u.shape[0]
    hb = TILE // CONV_HALO
    nhalo = n // CONV_HALO
    cwp = jnp.zeros((32, D_CONV), F32).at[:CONV_K].set(cw)
    return pl.pallas_call(
        functools.partial(_conv_kernel, geo=geo),
        grid=(geo.ntile,),
        in_specs=[
            pl.BlockSpec((TILE, 2 * D_CONV), lambda i: (i, C_CONV // (2 * D_CONV))),
            pl.BlockSpec((CONV_HALO, 2 * D_CONV), lambda i: (jnp.maximum(i * hb - 1, 0), 0)),
            pl.BlockSpec((CONV_HALO, 2 * D_CONV), lambda i: (jnp.minimum((i + 1) * hb, nhalo - 1), 0)),
            pl.BlockSpec((TILE, D_MODEL), lambda i: (i, C_G0 // D_MODEL)),
            pl.BlockSpec((32, D_CONV), lambda i: (0, 0)),
            pl.BlockSpec((1, D_CONV), lambda i: (0, 0)),
            pl.BlockSpec((1, D_CONV), lambda i: (0, 0)),
            pl.BlockSpec((1, D_CONV), lambda i: (0, 0)),
            pl.BlockSpec((D_CONV, D_MODEL), lambda i: (0, 0)),
        ],
        out_specs=pl.BlockSpec((TILE, D_MODEL), lambda i: (i, 0)),
        out_shape=jax.ShapeDtypeStruct((n, D_MODEL), F32),
        scratch_shapes=[pltpu.VMEM((TILE + 2 * CONV_HALO, D_CONV), F32)],
        compiler_params=_cparams(("arbitrary",)),
        name="conv_branch",
    )(u, u, u, u, cwp, cb.reshape(1, -1), lng.reshape(1, -1), lnb.reshape(1, -1), wo.astype(BF16))


def _seg_sum(x, ones_bd):
    return jnp.dot(x, ones_bd, preferred_element_type=F32, precision=lax.Precision.HIGHEST)


def _rwkv_pre_kernel(r_ref, rp_ref, rn_ref, k_ref, kp_ref, kn_ref, v_ref, vp_ref, vn_ref,
                     lora_ref, mu_ref, w0_ref, bw_ref, a0_ref, ba_ref, bg_ref, xi_ref, al_ref,
                     rho_ref, ones_ref,
                     r_o, kk_o, v_o, w0_o, w1_o, ka0_o, ka1_o, kt0_o, kt1_o, g_o, bonus_o, *, geo, tile0):
    i = pl.program_id(0) + tile0
    p, n = geo.pos(i)
    has_prev = p > 0
    has_next = p < n - 1
    row = lax.broadcasted_iota(jnp.int32, (TILE, D_RWKV), 0)

    def shifted(c_ref, p_ref, n_ref, mu):
        cur = c_ref[...]
        pv = jnp.where(has_prev, p_ref[SHIFT_HALO - 1:SHIFT_HALO, :], 0.0)
        nx = jnp.where(has_next, n_ref[0:1, :], 0.0)
        prev = jnp.where(row == 0, pv, pltpu.roll(cur, 1, axis=0))
        nxt = jnp.where(row == TILE - 1, nx, pltpu.roll(cur, TILE - 1, axis=0))
        return cur + mu * (0.5 * (prev + nxt) - cur)

    r = shifted(r_ref, rp_ref, rn_ref, mu_ref[0:1, :])
    k = shifted(k_ref, kp_ref, kn_ref, mu_ref[1:2, :])
    v = shifted(v_ref, vp_ref, vn_ref, mu_ref[2:3, :])

    lora = lora_ref[...]
    dw = jnp.tanh(lora[:, :W_RANK]).astype(BF16)
    da = lora[:, W_RANK:W_RANK + A_RANK].astype(BF16)
    dg = _sigmoid(lora[:, W_RANK + A_RANK:]).astype(BF16)
    wl = jnp.dot(dw, bw_ref[...], preferred_element_type=F32)
    al = jnp.dot(da, ba_ref[...], preferred_element_type=F32)
    g = jnp.dot(dg, bg_ref[...], preferred_element_type=F32)

    ones_bd = ones_ref[...]
    kx = k * xi_ref[...]
    kk = kx * lax.rsqrt(_seg_sum(kx * kx, ones_bd) + EPS)
    alpha = al_ref[...]
    rho = rho_ref[...]
    bonus = jnp.zeros((TILE, D_RWKV), F32)
    for d, (w_o, ka_o, kt_o) in enumerate(((w0_o, ka0_o, kt0_o), (w1_o, ka1_o, kt1_o))):
        sl = slice(d * D_RWKV, (d + 1) * D_RWKV)
        w = jnp.exp(-DECAY_SCALE * _sigmoid(w0_ref[d:d + 1, :] + wl[:, sl]))
        a = _sigmoid(a0_ref[d:d + 1, :] + al[:, sl])
        kt = k * (1.0 + (a - 1.0) * alpha)
        w_o[...] = w
        ka_o[...] = kk * a
        kt_o[...] = kt
        bonus = bonus + _seg_sum(r * kt * rho, ones_bd) * v
    r_o[...] = r
    kk_o[...] = kk
    v_o[...] = v
    g_o[...] = g
    bonus_o[...] = bonus


def rwkv_pre(u, P, l, geo, ones_bd, tile0, ntiles):
    hb = TILE // SHIFT_HALO
    nhalo = u.shape[0] // SHIFT_HALO

    def trio(c0):
        cb = c0 // D_RWKV
        return [
            pl.BlockSpec((TILE, D_RWKV), lambda i: (i + tile0, cb)),
            pl.BlockSpec((SHIFT_HALO, D_RWKV), lambda i: (jnp.maximum((i + tile0) * hb - 1, 0), cb)),
            pl.BlockSpec((SHIFT_HALO, D_RWKV), lambda i: (jnp.minimum((i + tile0 + 1) * hb, nhalo - 1), cb)),
        ]

    def full(shape):
        return pl.BlockSpec(shape, lambda i: tuple(0 for _ in shape))

    bw = jnp.concatenate([P["rwkv_bw"][l, 0], P["rwkv_bw"][l, 1]], axis=1).astype(BF16)
    ba = jnp.concatenate([P["rwkv_ba"][l, 0], P["rwkv_ba"][l, 1]], axis=1).astype(BF16)
    out = jax.ShapeDtypeStruct((ntiles * TILE, D_RWKV), F32)
    return pl.pallas_call(
        functools.partial(_rwkv_pre_kernel, geo=geo, tile0=tile0),
        grid=(ntiles,),
        in_specs=trio(C_R) + trio(C_K) + trio(C_V) + [
            pl.BlockSpec((TILE, 256), lambda i: (i + tile0, C_LORA // 256)),
            full((3, D_RWKV)), full((2, D_RWKV)), full((W_RANK, 2 * D_RWKV)),
            full((2, D_RWKV)), full((A_RANK, 2 * D_RWKV)), full((G_RANK, D_RWKV)),
            full((1, D_RWKV)), full((1, D_RWKV)), full((1, D_RWKV)), full((D_RWKV, D_RWKV)),
        ],
        out_specs=[pl.BlockSpec((TILE, D_RWKV), lambda i: (i, 0))] * 11,
        out_shape=[out] * 11,
        compiler_params=_cparams(("arbitrary",)),
        name="rwkv_pre",
    )(u, u, u, u, u, u, u, u, u, u,
      P["rwkv_mu"][l], P["rwkv_w0"][l], bw, P["rwkv_a0"][l], ba, P["rwkv_bg"][l].astype(BF16),
      P["rwkv_xi"][l].reshape(1, -1), P["rwkv_alpha"][l].reshape(1, -1),
      P["rwkv_rho"][l].reshape(1, -1), ones_bd)


N_VBLK = RWKV_HEAD // SUBLANES


def _scan_steps(xrow, vload, ostore, s_ref, tidx, *, tc, nk, kl, ngroups=2, unroll=1):
    gsz = N_VBLK // ngroups
    groups = tuple(tuple(range(g * gsz, (g + 1) * gsz)) for g in range(ngroups))

    def allred(a):
        out = a
        for j in range(1, kl):
            out = out + pltpu.roll(a, j * (LANES // kl), axis=1)
        return out

    def first_sa(grp, t):
        acc = [None] * len(grp)
        for kh in range(nk):
            kkb = xrow(t, 1, kh)
            for j, vb in enumerate(grp):
                pr = s_ref[vb, kh] * kkb
                acc[j] = pr if acc[j] is None else acc[j] + pr
        return tuple(allred(a) for a in acc)

    def fused(grp, s, t, t_next, sa):
        vv = [vload(t, vb) for vb in grp]
        oacc = [None] * len(grp)
        acc = [None] * len(grp)
        for kh in range(nk):
            wb, kab, ktb, rb = xrow(t, 0, kh), xrow(t, 2, kh), xrow(t, 3, kh), xrow(t, 4, kh)
            kkn = xrow(t_next, 1, kh)
            for j, vb in enumerate(grp):
                sn = s_ref[vb, kh] * wb - sa[j] * kab + vv[j] * ktb
                s_ref[vb, kh] = sn
                po = sn * rb
                pa = sn * kkn
                oacc[j] = po if oacc[j] is None else oacc[j] + po
                acc[j] = pa if acc[j] is None else acc[j] + pa
        for j, vb in enumerate(grp):
            ostore(s, t, vb, oacc[j])
        return tuple(allred(a) for a in acc)

    t0 = tidx(0)
    carry0 = tuple(first_sa(g, t0) for g in groups)

    def step(s, carry):
        t = tidx(s)
        t_next = tidx(jnp.minimum(s + 1, tc - 1))
        return tuple(fused(g, s, t, t_next, carry[i]) for i, g in enumerate(groups))

    lax.fori_loop(0, tc, step, carry0, unroll=unroll)


def _vrows(vb):
    return pl.ds(vb * SUBLANES, SUBLANES)


def _scan_ctx_kernel(w_ref, kk_ref, ka_ref, kt_ref, r_ref, v_ref, o_ref, sfin_ref, s_ref, *, tc, nk, rev):
    c = pl.program_id(1)
    x_refs = (w_ref, kk_ref, ka_ref, kt_ref, r_ref)

    @pl.when(c == 0)
    def _():
        s_ref[...] = jnp.zeros(s_ref.shape, F32)

    def xrow(t, a, kh):
        return jnp.broadcast_to(x_refs[a][0, t, pl.ds(kh, 1), :], (SUBLANES, LANES))

    def vload(t, vb):
        return v_ref[0, t, _vrows(vb), :]

    def ostore(s, t, vb, val):
        o_ref[0, t, _vrows(vb), :] = val

    def tidx(s):
        return tc - 1 - s if rev else s

    _scan_steps(xrow, vload, ostore, s_ref, tidx, tc=tc, nk=nk, kl=1)

    @pl.when(c == pl.num_programs(1) - 1)
    def _():
        sfin_ref[0] = s_ref[...]


def wkv_scan_ctx(w, kk, ka, kt, r, v, *, tc, rev):
    ngb, t_len, nk, _ = w.shape
    nch = t_len // tc
    blk = pl.BlockSpec((1, tc, nk, LANES), lambda g, c: (g, nch - 1 - c if rev else c, 0, 0))
    st = pl.BlockSpec((1, N_VBLK, nk, SUBLANES, LANES), lambda g, c: (g, 0, 0, 0, 0))
    return pl.pallas_call(
        functools.partial(_scan_ctx_kernel, tc=tc, nk=nk, rev=rev),
        grid=(ngb, nch),
        in_specs=[blk] * 6,
        out_specs=[blk, st],
        out_shape=[
            jax.ShapeDtypeStruct((ngb, t_len, RWKV_HEAD, LANES), F32),
            jax.ShapeDtypeStruct((ngb, N_VBLK, nk, SUBLANES, LANES), F32),
        ],
        scratch_shapes=[pltpu.VMEM((N_VBLK, nk, SUBLANES, LANES), F32)],
        compiler_params=_cparams(("arbitrary", "arbitrary")),
        name="wkv_scan_ctx",
    )(w, kk, ka, kt, r, v)


def _scan_lat_kernel(*refs, tc, nk, kl):
    xa_refs, xb_refs = refs[0:5], refs[5:10]
    va_ref, vb_ref, s0_ref, oa_ref, ob_ref, s_ref, xm_ref, vm_ref = refs[10:]
    c = pl.program_id(0)

    @pl.when(c == 0)
    def _():
        s_ref[...] = s0_ref[...]

    def is_bwd(shape):
        lane = lax.broadcasted_iota(jnp.int32, shape, len(shape) - 1)
        return (lane // N_RWKV_HEADS) % 2 == 1

    mx = is_bwd((nk, LANES))
    mv = is_bwd((RWKV_HEAD, LANES))

    def merge(s, carry):
        for a in range(5):
            xm_ref[a, s] = jnp.where(mx, xb_refs[a][tc - 1 - s], xa_refs[a][s])
        vm_ref[s] = jnp.where(mv, vb_ref[tc - 1 - s], va_ref[s])
        return carry

    lax.fori_loop(0, tc, merge, 0)

    def xrow(t, a, kh):
        return jnp.broadcast_to(xm_ref[a, t, pl.ds(kh, 1), :], (SUBLANES, LANES))

    def vload(t, vb):
        return vm_ref[t, _vrows(vb), :]

    def ostore(s, t, vb, val):
        oa_ref[s, _vrows(vb), :] = val
        ob_ref[tc - 1 - s, _vrows(vb), :] = val

    _scan_steps(xrow, vload, ostore, s_ref, lambda s: s, tc=tc, nk=nk, kl=kl, ngroups=2, unroll=2)


def wkv_scan_lat(xs, v, s0, *, tc, kl):
    t_len, nk, _ = xs[0].shape
    nch = t_len // tc
    o_sds = jax.ShapeDtypeStruct((t_len, RWKV_HEAD, LANES), F32)
    xa = pl.BlockSpec((tc, nk, LANES), lambda c: (c, 0, 0))
    xb = pl.BlockSpec((tc, nk, LANES), lambda c: (nch - 1 - c, 0, 0))
    va = pl.BlockSpec((tc, RWKV_HEAD, LANES), lambda c: (c, 0, 0))
    vb = pl.BlockSpec((tc, RWKV_HEAD, LANES), lambda c: (nch - 1 - c, 0, 0))
    return pl.pallas_call(
        functools.partial(_scan_lat_kernel, tc=tc, nk=nk, kl=kl),
        grid=(nch,),
        in_specs=[xa] * 5 + [xb] * 5 + [va, vb,
                                        pl.BlockSpec((N_VBLK, nk, SUBLANES, LANES), lambda c: (0, 0, 0, 0))],
        out_specs=[va, vb],
        out_shape=[o_sds, o_sds],
        scratch_shapes=[pltpu.VMEM((N_VBLK, nk, SUBLANES, LANES), F32),
                        pltpu.VMEM((5, tc, nk, LANES), F32),
                        pltpu.VMEM((tc, RWKV_HEAD, LANES), F32)],
        compiler_params=_cparams(("arbitrary",)),
        name="wkv_scan_lat",
    )(*xs, *xs, v, v, s0)


def _rwkv_post_kernel(of_ref, ob_ref, bonus_ref, g_ref, gate_ref, gng_ref, gnb_ref, ones_ref,
                      wo_ref, *rest):
    y_ref = rest[-1]
    o = of_ref[...] + ob_ref[...]
    ones_bd = ones_ref[...]
    mean = _seg_sum(o, ones_bd) * (1.0 / RWKV_HEAD)
    oc = o - mean
    var = _seg_sum(oc * oc, ones_bd) * (1.0 / RWKV_HEAD)
    gn = oc * lax.rsqrt(var + GN_EPS) * gng_ref[...] + gnb_ref[...]
    y = (gn + bonus_ref[...]) * g_ref[...]
    y = jnp.dot(y.astype(BF16), wo_ref[...], preferred_element_type=F32)
    y_ref[...] = _sigmoid(gate_ref[...]) * y


def rwkv_post(o_f, o_b, bonus, g, u, gng, gnb, ones_bd, wo, y_prev, tile0):
    ntiles = bonus.shape[0] // TILE

    def full(shape):
        return pl.BlockSpec(shape, lambda i: tuple(0 for _ in shape))

    part = pl.BlockSpec((TILE, D_RWKV), lambda i: (i, 0))
    in_specs = [part, part, part, part,
                pl.BlockSpec((TILE, D_MODEL), lambda i: (i + tile0, C_G1 // D_MODEL)),
                full((1, D_RWKV)), full((1, D_RWKV)), full((D_RWKV, D_RWKV)), full((D_RWKV, D_MODEL))]
    args = [o_f, o_b, bonus, g, u, gng.reshape(1, -1), gnb.reshape(1, -1), ones_bd, wo.astype(BF16)]
    aliases = {}
    if y_prev is not None:
        in_specs.append(pl.BlockSpec(memory_space=pl.ANY))
        args.append(y_prev)
        aliases = {len(args) - 1: 0}
    return pl.pallas_call(
        _rwkv_post_kernel,
        grid=(ntiles,),
        in_specs=in_specs,
        out_specs=pl.BlockSpec((TILE, D_MODEL), lambda i: (i + tile0, 0)),
        out_shape=jax.ShapeDtypeStruct((u.shape[0], D_MODEL), F32),
        input_output_aliases=aliases,
        compiler_params=_cparams(("arbitrary",)),
        name="rwkv_post",
    )(*args)


QK_W = 2 * LANES
Q_SCALE = ATTN_SCALE * math.log2(math.e)


def _mla_pre_kernel(cq_ref, ckv_ref, kr_ref, cos_ref, sin_ref, qg_ref, kvg_ref, wuq_ref, wukt_ref,
                    qf_o, ckvn_o, krr_o):
    nq = N_MLA_HEADS * NOPE_DIM
    nrp = N_MLA_HEADS * LANES
    cq = _rms(cq_ref[...], qg_ref[...])
    q = jnp.dot(cq.astype(BF16), wuq_ref[...], preferred_element_type=F32)
    cos = cos_ref[...]
    sin = sin_ref[...]
    for h in range(N_MLA_HEADS):
        qn = q[:, h * NOPE_DIM:(h + 1) * NOPE_DIM].astype(BF16)
        qa = jnp.dot(qn, wukt_ref[h], preferred_element_type=F32)
        qr = (q[:, nq + h * LANES:nq + (h + 1) * LANES] * cos
              + q[:, nq + nrp + h * LANES:nq + nrp + (h + 1) * LANES] * sin)
        qf_o[:, h * QK_W:h * QK_W + LANES] = (qa * Q_SCALE).astype(qf_o.dtype)
        qf_o[:, h * QK_W + LANES:(h + 1) * QK_W] = (qr * Q_SCALE).astype(qf_o.dtype)
    ckvn_o[...] = _rms(ckv_ref[...], kvg_ref[...])
    kr = kr_ref[...]
    krr = kr[:, ROPE_DIM:2 * ROPE_DIM] * cos[:, :ROPE_DIM] + kr[:, 2 * ROPE_DIM:3 * ROPE_DIM] * sin[:, :ROPE_DIM]
    krr_o[...] = jnp.concatenate([krr, jnp.zeros((TILE, LANES - ROPE_DIM), F32)], axis=1)


def mla_pre(u, cos_t, sin_t, qg, kvg, wuq_p, wukt, geo):
    n = u.shape[0]

    def full(shape):
        return pl.BlockSpec(shape, lambda i: tuple(0 for _ in shape))

    return pl.pallas_call(
        _mla_pre_kernel,
        grid=(geo.ntile,),
        in_specs=[
            pl.BlockSpec((TILE, Q_RANK), lambda i: (i, C_CQ // Q_RANK)),
            pl.BlockSpec((TILE, KV_RANK), lambda i: (i, C_CKV // KV_RANK)),
            pl.BlockSpec((TILE, LANES), lambda i: (i, C_KR // LANES)),
            pl.BlockSpec((TILE, LANES), lambda i: (geo.rope_blk(i), 0)),
            pl.BlockSpec((TILE, LANES), lambda i: (geo.rope_blk(i), 0)),
            full((1, Q_RANK)), full((1, KV_RANK)),
            full(wuq_p.shape), full(wukt.shape),
        ],
        out_specs=[
            pl.BlockSpec((TILE, N_MLA_HEADS * QK_W), lambda i: (i, 0)),
            pl.BlockSpec((TILE, KV_RANK), lambda i: (i, 0)),
            pl.BlockSpec((TILE, LANES), lambda i: (i, 0)),
        ],
        out_shape=[
            jax.ShapeDtypeStruct((n, N_MLA_HEADS * QK_W), BF16),
            jax.ShapeDtypeStruct((n, KV_RANK), F32),
            jax.ShapeDtypeStruct((n, LANES), F32),
        ],
        compiler_params=_cparams(("arbitrary",)),
        name="mla_pre",
    )(u, u, u, cos_t, sin_t, qg.reshape(1, -1), kvg.reshape(1, -1), wuq_p, wukt)


ATT_KEY_CHUNK = 512
ATT_HEADS_PER_STEP = 4


def _attn_kernel(q_ref, kt_ref, c_ref, wuv_ref, wo_ref, gate_ref, *rest, ck):
    y_ref, pc_ref = rest[-2:]
    hg = pl.program_id(2)
    tq = q_ref.shape[0]
    t_k = kt_ref.shape[2]
    hp = ATT_HEADS_PER_STEP
    q = jnp.concatenate([q_ref[:, j * QK_W:(j + 1) * QK_W] for j in range(hp)], axis=0)
    m = jnp.full((hp * tq, 1), -jnp.inf, F32)
    l = jnp.zeros((hp * tq, 1), F32)
    acc = jnp.zeros((hp * tq, KV_RANK), F32)
    for c0 in range(0, t_k, ck):
        s = jnp.dot(q, kt_ref[0, :, c0:c0 + ck], preferred_element_type=F32)
        m_new = jnp.maximum(m, jnp.max(s, axis=-1, keepdims=True))
        alpha = jnp.exp2(m - m_new)
        p = jnp.exp2(s - m_new)
        l = alpha * l + jnp.sum(p, axis=-1, keepdims=True)
        acc = alpha * acc + jnp.dot(p.astype(BF16), c_ref[0, c0:c0 + ck, :], preferred_element_type=F32)
        m = m_new
    pc = (acc / l).astype(BF16)
    for j in range(hp):
        pc_ref[hg * hp + j] = pc[j * tq:(j + 1) * tq]

    @pl.when(hg == pl.num_programs(2) - 1)
    def _():
        pcs = jnp.concatenate([pc_ref[i] for i in range(N_MLA_HEADS)], axis=1)
        oh = jnp.dot(pcs, wuv_ref[...], preferred_element_type=F32)
        y = jnp.dot(oh.astype(BF16), wo_ref[...], preferred_element_type=F32)
        y_ref[...] = _sigmoid(gate_ref[...]) * y


def attention(qf, kct, cv, wuv_bd, wo, u, y_prev, *, row0, t_seq, tq, name):
    n = qf.shape[0]
    nseq, _, t_k = kct.shape
    qt = t_seq // tq
    rb0 = row0 // tq
    ck = math.gcd(ATT_KEY_CHUNK, t_k)

    def rows(w, col):
        return pl.BlockSpec((tq, w), lambda s, i, h: (rb0 + s * qt + i, col(h)))

    in_specs = [
        rows(ATT_HEADS_PER_STEP * QK_W, lambda h: h),
        pl.BlockSpec((1, QK_W, t_k), lambda s, i, h: (s, 0, 0)),
        pl.BlockSpec((1, t_k, KV_RANK), lambda s, i, h: (s, 0, 0)),
        pl.BlockSpec(wuv_bd.shape, lambda s, i, h: (0, 0)),
        pl.BlockSpec(wo.shape, lambda s, i, h: (0, 0)),
        rows(D_MODEL, lambda h: C_G2 // D_MODEL),
    ]
    args = [qf, kct, cv, wuv_bd, wo, u]
    aliases = {}
    if y_prev is not None:
        in_specs.append(pl.BlockSpec(memory_space=pl.ANY))
        args.append(y_prev)
        aliases = {len(args) - 1: 0}
    return pl.pallas_call(
        functools.partial(_attn_kernel, ck=ck),
        grid=(nseq, qt, N_MLA_HEADS // ATT_HEADS_PER_STEP),
        in_specs=in_specs,
        out_specs=rows(D_MODEL, lambda h: 0),
        out_shape=jax.ShapeDtypeStruct((n, D_MODEL), F32),
        scratch_shapes=[pltpu.VMEM((N_MLA_HEADS, tq, KV_RANK), BF16)],
        input_output_aliases=aliases,
        compiler_params=_cparams(("arbitrary", "arbitrary", "arbitrary")),
        name=name,
    )(*args)


R_I1, R_I2, R_G1, R_G2 = 0, 1, 2, 3


def _merge_kernel(x_ref, yc_ref, yr_ref, ym_ref, wout_ref, gpost_ref, g1_ref, gpre_ref, sc_ref,
                  sh_ref, router_ref, x1_o, h2_o, comb_o, *, moe):
    m = yc_ref[...] + yr_ref[...] + ym_ref[...]
    y = jnp.dot(m.astype(BF16), wout_ref[...], preferred_element_type=F32)
    x1 = x_ref[...] + g1_ref[0] * _rms(y, gpost_ref[...])
    x1_o[...] = x1
    h2 = _rms(x1, gpre_ref[...]) * (1.0 + sc_ref[0]) + sh_ref[0]
    h2_o[...] = h2.astype(h2_o.dtype)
    if moe:
        logits = jnp.dot(h2, router_ref[...], preferred_element_type=F32,
                         precision=lax.Precision.HIGHEST)
        lane = lax.broadcasted_iota(jnp.int32, logits.shape, 1)
        neg = jnp.float32(-jnp.inf)
        logits = jnp.where(lane < N_EXPERTS, logits, neg)
        m1 = jnp.max(logits, axis=-1, keepdims=True)
        i1 = jnp.min(jnp.where(logits == m1, lane, LANES), axis=-1, keepdims=True)
        rest = jnp.where(lane == i1, neg, logits)
        m2 = jnp.max(rest, axis=-1, keepdims=True)
        i2 = jnp.min(jnp.where(rest == m2, lane, LANES), axis=-1, keepdims=True)
        e2 = jnp.exp(m2 - m1)
        den = 1.0 + e2
        cols = ((R_I1, i1.astype(F32)), (R_I2, i2.astype(F32)), (R_G1, 1.0 / den), (R_G2, e2 / den))
        route = jnp.zeros(comb_o.shape, F32)
        for col, val in cols:
            route = jnp.where(lane == col, val, route)
        comb_o[...] = route
    else:
        comb_o[...] = jnp.ones(comb_o.shape, F32)


def merge(x, yc, yr, ym, wout, gpost, gpre, mod, router, geo, moe):
    n, d = x.shape

    def full(shape):
        return pl.BlockSpec(shape, lambda i: tuple(0 for _ in shape))

    def rows():
        return pl.BlockSpec((TILE, d), lambda i: (i, 0))

    def modspec(j):
        return pl.BlockSpec((1, 1, d), lambda i: (geo.mod_row(i) * 6 + j, 0, 0))

    router_p = jnp.zeros((d, LANES), F32).at[:, :N_EXPERTS].set(router)
    return pl.pallas_call(
        functools.partial(_merge_kernel, moe=moe),
        grid=(geo.ntile,),
        in_specs=[rows(), rows(), rows(), rows(), full((d, d)), full((1, d)), modspec(2),
                  full((1, d)), modspec(4), modspec(3), full((d, LANES))],
        out_specs=[rows(), rows(), pl.BlockSpec((TILE, LANES), lambda i: (i, 0))],
        out_shape=[jax.ShapeDtypeStruct((n, d), F32), jax.ShapeDtypeStruct((n, d), F32 if moe else BF16),
                   jax.ShapeDtypeStruct((n, LANES), F32)],
        compiler_params=_cparams(("arbitrary",)),
        name="merge",
    )(x, yc, yr, ym, wout.astype(BF16), gpost.reshape(1, d), mod, gpre.reshape(1, d), mod, mod,
      router_p)


def _ffn_kernel(h_ref, comb_ref, w1_ref, w3_ref, w2_ref, o_ref, acc_ref):
    e = pl.program_id(1)
    f = pl.program_id(2)

    @pl.when((e == 0) & (f == 0))
    def _():
        acc_ref[...] = jnp.zeros(acc_ref.shape, F32)

    h = h_ref[...]
    a = jnp.dot(h, w1_ref[0], preferred_element_type=F32)
    b = jnp.dot(h, w3_ref[0], preferred_element_type=F32)
    comb = comb_ref[...]
    lane = lax.broadcasted_iota(jnp.int32, comb.shape, 1)
    ce = jnp.sum(jnp.where(lane == e, comb, 0.0), axis=-1, keepdims=True)
    act = (a * _sigmoid(a) * b) * ce
    acc_ref[...] += jnp.dot(act.astype(BF16), w2_ref[0], preferred_element_type=F32)

    @pl.when((e == pl.num_programs(1) - 1) & (f == pl.num_programs(2) - 1))
    def _():
        o_ref[...] = acc_ref[...]


def ffn(h, comb, w1, w3, w2, *, tm, tf):
    n, d = h.shape
    ne, _, dff = w1.shape
    return pl.pallas_call(
        _ffn_kernel,
        grid=(n // tm, ne, dff // tf),
        in_specs=[
            pl.BlockSpec((tm, d), lambda i, e, f: (i, 0)),
            pl.BlockSpec((tm, LANES), lambda i, e, f: (i, 0)),
            pl.BlockSpec((1, d, tf), lambda i, e, f: (e, 0, f)),
            pl.BlockSpec((1, d, tf), lambda i, e, f: (e, 0, f)),
            pl.BlockSpec((1, tf, d), lambda i, e, f: (e, f, 0)),
        ],
        out_specs=pl.BlockSpec((tm, d), lambda i, e, f: (i, 0)),
        out_shape=jax.ShapeDtypeStruct((n, d), F32),
        scratch_shapes=[pltpu.VMEM((tm, d), F32)],
        compiler_params=_cparams(("arbitrary", "arbitrary", "arbitrary")),
        name="ffn",
    )(h, comb, w1, w3, w2)


MOE_TM = 512


def _row_copy(src_hbm, row, dst_vmem, r, sem):
    return pltpu.make_async_copy(src_hbm.at[pl.ds(row, 1)], dst_vmem.at[pl.ds(r, 1)], sem)


DMA_UNROLL = 8


def _gather_into(idx_ref, src_hbm, dst_vmem, sem, nrows):
    def start(i, c):
        for k in range(DMA_UNROLL):
            r = i * DMA_UNROLL + k
            _row_copy(src_hbm, idx_ref[0, 0, r], dst_vmem, r, sem).start(priority=k % 2)
        return c

    def wait(i, c):
        for k in range(DMA_UNROLL):
            _row_copy(src_hbm, 0, dst_vmem, i * DMA_UNROLL + k, sem).wait()
        return c

    lax.fori_loop(0, nrows // DMA_UNROLL, start, 0)
    lax.fori_loop(0, nrows // DMA_UNROLL, wait, 0)


def _gather_rows_kernel(idx_ref, src_hbm, o_ref, buf_ref, sem):
    _gather_into(idx_ref, src_hbm, buf_ref, sem, buf_ref.shape[0])
    o_ref[...] = buf_ref[...]


def gather_rows(src, idx, *, rows):
    p = idx.shape[0]
    w = src.shape[1]
    return pl.pallas_call(
        _gather_rows_kernel,
        grid=(p // rows,),
        in_specs=[
            pl.BlockSpec((1, 1, rows), lambda j: (j, 0, 0), memory_space=pltpu.SMEM),
            pl.BlockSpec(memory_space=pl.ANY),
        ],
        out_specs=pl.BlockSpec((rows, w), lambda j: (j, 0)),
        out_shape=jax.ShapeDtypeStruct((p, w), src.dtype),
        scratch_shapes=[pltpu.VMEM((rows, w), src.dtype), pltpu.SemaphoreType.DMA(())],
        compiler_params=_cparams(("arbitrary",)),
        name="moe_gather",
    )(idx.reshape(p // rows, 1, rows), src)


def _ffn_sorted_kernel(te_ref, nused_ref, x_ref, w1_ref, w3_ref, w2_ref, o_ref, acc_ref):
    j = pl.program_id(0)
    f = pl.program_id(1)

    @pl.when(j < nused_ref[0])
    def _():
        @pl.when(f == 0)
        def _():
            acc_ref[...] = jnp.zeros(acc_ref.shape, F32)

        h = x_ref[...].astype(BF16)
        a = jnp.dot(h, w1_ref[0], preferred_element_type=F32)
        b = jnp.dot(h, w3_ref[0], preferred_element_type=F32)
        act = a * _sigmoid(a) * b
        acc_ref[...] += jnp.dot(act.astype(BF16), w2_ref[0], preferred_element_type=F32)

        @pl.when(f == pl.num_programs(1) - 1)
        def _():
            o_ref[...] = acc_ref[...]


def ffn_sorted(xs, te, nused, w1, w3, w2, *, tm, tf):
    p, d = xs.shape
    dff = w1.shape[2]
    return pl.pallas_call(
        _ffn_sorted_kernel,
        grid_spec=pltpu.PrefetchScalarGridSpec(
            num_scalar_prefetch=2,
            grid=(p // tm, dff // tf),
            in_specs=[
                pl.BlockSpec((tm, d), lambda j, f, te, nu: (j, 0)),
                pl.BlockSpec((1, d, tf), lambda j, f, te, nu: (te[j], 0, f)),
                pl.BlockSpec((1, d, tf), lambda j, f, te, nu: (te[j], 0, f)),
                pl.BlockSpec((1, tf, d), lambda j, f, te, nu: (te[j], f, 0)),
            ],
            out_specs=pl.BlockSpec((tm, d), lambda j, f, te, nu: (j, 0)),
            scratch_shapes=[pltpu.VMEM((tm, d), F32)],
        ),
        out_shape=jax.ShapeDtypeStruct((p, d), F32),
        compiler_params=_cparams(("arbitrary", "arbitrary")),
        name="ffn_sorted",
    )(te, nused, xs, w1, w3, w2)


def _moe_combine_kernel(d0_ref, d1_ref, ys_hbm, route_ref, x_ref, g_ref, g2_ref, o_ref,
                        y0_ref, y1_ref, sem0, sem1):
    def start(i, c):
        for k in range(DMA_UNROLL):
            r = i * DMA_UNROLL + k
            _row_copy(ys_hbm, d0_ref[0, 0, r], y0_ref, r, sem0).start(priority=0)
            _row_copy(ys_hbm, d1_ref[0, 0, r], y1_ref, r, sem1).start(priority=1)
        return c

    def wait(i, c):
        for k in range(DMA_UNROLL):
            r = i * DMA_UNROLL + k
            _row_copy(ys_hbm, 0, y0_ref, r, sem0).wait()
            _row_copy(ys_hbm, 0, y1_ref, r, sem1).wait()
        return c

    lax.fori_loop(0, TILE // DMA_UNROLL, start, 0)
    lax.fori_loop(0, TILE // DMA_UNROLL, wait, 0)
    route = route_ref[...]
    lane = lax.broadcasted_iota(jnp.int32, route.shape, 1)
    ga = jnp.sum(jnp.where(lane == R_G1, route, 0.0), axis=-1, keepdims=True)
    gb = jnp.sum(jnp.where(lane == R_G2, route, 0.0), axis=-1, keepdims=True)
    y = ga * y0_ref[...] + gb * y1_ref[...]
    o_ref[...] = x_ref[...] + g2_ref[0] * _rms(y, g_ref[...])


def moe_combine_post(x1, ys, dest, route, gpost, mod, geo):
    n, d = x1.shape
    dd = dest.reshape(2, n // TILE, 1, TILE)
    return pl.pallas_call(
        _moe_combine_kernel,
        grid=(geo.ntile,),
        in_specs=[
            pl.BlockSpec((1, 1, TILE), lambda i: (i, 0, 0), memory_space=pltpu.SMEM),
            pl.BlockSpec((1, 1, TILE), lambda i: (i, 0, 0), memory_space=pltpu.SMEM),
            pl.BlockSpec(memory_space=pl.ANY),
            pl.BlockSpec((TILE, LANES), lambda i: (i, 0)),
            pl.BlockSpec((TILE, d), lambda i: (i, 0)),
            pl.BlockSpec((1, d), lambda i: (0, 0)),
            pl.BlockSpec((1, 1, d), lambda i: (geo.mod_row(i) * 6 + 5, 0, 0)),
        ],
        out_specs=pl.BlockSpec((TILE, d), lambda i: (i, 0)),
        out_shape=jax.ShapeDtypeStruct((n, d), F32),
        scratch_shapes=[pltpu.VMEM((TILE, d), F32), pltpu.VMEM((TILE, d), F32),
                        pltpu.SemaphoreType.DMA(()), pltpu.SemaphoreType.DMA(())],
        compiler_params=_cparams(("arbitrary",)),
        name="moe_combine",
    )(dd[0], dd[1], ys, route, x1, gpost.reshape(1, d), mod)


def _route_plan(route, tm):
    n = route.shape[0]
    e_flat = jnp.concatenate([route[:, R_I1], route[:, R_I2]]).astype(jnp.int32)
    onehot = (e_flat[:, None] == jnp.arange(N_EXPERTS, dtype=jnp.int32)[None, :]).astype(jnp.int32)
    csum = jnp.cumsum(onehot, axis=0)
    rank = jnp.take_along_axis(csum, e_flat[:, None], axis=1)[:, 0] - 1
    counts = csum[-1]
    ptiles = (counts + tm - 1) // tm
    tile_end = jnp.cumsum(ptiles)
    gstart = (tile_end - ptiles) * tm
    dest = jnp.take(gstart, e_flat) + rank
    p_max = 2 * n + N_EXPERTS * tm
    tok = jnp.tile(jnp.arange(n, dtype=jnp.int32), 2)
    row_src = jnp.zeros((p_max,), jnp.int32).at[dest].set(tok)
    tiles = jnp.arange(p_max // tm, dtype=jnp.int32)
    te = jnp.minimum(jnp.searchsorted(tile_end, tiles, side="right"), N_EXPERTS - 1).astype(jnp.int32)
    return row_src, dest.astype(jnp.int32).reshape(2, n), te, tile_end[-1:].astype(jnp.int32)


def _ffn_post_kernel(x_ref, y_ref, g_ref, g2_ref, o_ref):
    o_ref[...] = x_ref[...] + g2_ref[0] * _rms(y_ref[...], g_ref[...])


def ffn_post(x1, y, gpost, mod, geo):
    n, d = x1.shape
    return pl.pallas_call(
        _ffn_post_kernel,
        grid=(geo.ntile,),
        in_specs=[
            pl.BlockSpec((TILE, d), lambda i: (i, 0)),
            pl.BlockSpec((TILE, d), lambda i: (i, 0)),
            pl.BlockSpec((1, d), lambda i: (0, 0)),
            pl.BlockSpec((1, 1, d), lambda i: (geo.mod_row(i) * 6 + 5, 0, 0)),
        ],
        out_specs=pl.BlockSpec((TILE, d), lambda i: (i, 0)),
        out_shape=jax.ShapeDtypeStruct((n, d), F32),
        compiler_params=_cparams(("arbitrary",)),
        name="ffn_post",
    )(x1, y, gpost.reshape(1, d), mod)


def _ctx_scan(geo, pre):
    r, kk, v, w0, w1, ka0, ka1, kt0, kt1 = pre
    H, K = N_RWKV_HEADS, RWKV_HEAD
    bl = LANES // H
    nbh = geo.n_ctx // bl

    def to_scan(a):
        a = a.reshape(nbh, bl, geo.t_ctx, H, K)
        return a.transpose(0, 2, 4, 1, 3).reshape(nbh, geo.t_ctx, K, LANES)

    def from_scan(o):
        o = o.reshape(nbh, geo.t_ctx, K, bl, H).transpose(0, 3, 1, 4, 2)
        return o.reshape(geo.nc_tok, H * K)

    def state(sf):
        sf = sf.reshape(nbh, K // SUBLANES, K, SUBLANES, bl, H).transpose(0, 4, 5, 1, 3, 2)
        return sf.reshape(geo.n_ctx, H, K, K)

    kk_s, r_s, v_s = to_scan(kk), to_scan(r), to_scan(v)
    tc = min(16, geo.t_ctx)
    o_f, sf_f = wkv_scan_ctx(to_scan(w0), kk_s, to_scan(ka0), to_scan(kt0), r_s, v_s, tc=tc, rev=False)
    o_b, sf_b = wkv_scan_ctx(to_scan(w1), kk_s, to_scan(ka1), to_scan(kt1), r_s, v_s, tc=tc, rev=True)
    return from_scan(o_f), from_scan(o_b), jnp.stack([state(sf_f), state(sf_b)], axis=1)


def _lat_scan(geo, pre, state_l):
    r, kk, v, w0, w1, ka0, ka1, kt0, kt1 = pre
    H, K = N_RWKV_HEADS, RWKV_HEAD
    kl = LANES // (geo.n_lat * 2 * H)
    nk = K // kl

    def pair(a0, a1):
        a0 = a0.reshape(geo.n_lat, geo.t_lat, H, K)
        a1 = a1.reshape(geo.n_lat, geo.t_lat, H, K)
        a = jnp.stack([a0, a1], axis=2).reshape(geo.n_lat, geo.t_lat, 2, H, kl, nk)
        return a.transpose(1, 5, 4, 0, 2, 3).reshape(geo.t_lat, nk, LANES)

    xs = (pair(w0, w1), pair(kk, kk), pair(ka0, ka1), pair(kt0, kt1), pair(r, r))
    vl = v.reshape(geo.n_lat, geo.t_lat, H, K)
    vl = jnp.stack([vl, vl], axis=2)
    vl = vl.transpose(1, 4, 0, 2, 3).reshape(geo.t_lat, K, LANES // kl)
    v_lat = jnp.tile(vl, (1, 1, kl))
    s0 = state_l.reshape(geo.n_lat, 2, H, K // SUBLANES, SUBLANES, kl, nk)
    s0 = s0.transpose(3, 6, 4, 5, 0, 1, 2).reshape(K // SUBLANES, nk, SUBLANES, LANES)
    oa, ob = wkv_scan_lat(xs, v_lat, s0, tc=min(64, geo.t_lat), kl=kl)

    def direction(o, d):
        o = o.reshape(geo.t_lat, K, kl, geo.n_lat, 2, H)[:, :, :, :, d].sum(axis=2)
        return o.transpose(2, 0, 3, 1).reshape(geo.nl_tok, H * K)

    return direction(oa, 0), direction(ob, 1)


def _rope_tables(geo):
    n_freq = ROPE_DIM // 4
    rows = geo.t_lat // GRID_W
    row = jnp.repeat(jnp.arange(rows, dtype=F32), GRID_W)
    col = jnp.tile(jnp.arange(GRID_W, dtype=F32), rows)
    inv = ROPE_BASE ** (-jnp.arange(n_freq, dtype=F32) / n_freq)
    ang = jnp.concatenate([row[:, None] * inv, col[:, None] * inv], axis=-1)
    cos, sin = jnp.cos(ang), jnp.sin(ang)
    cos32 = jnp.concatenate([cos, cos], axis=-1)
    sin32 = jnp.concatenate([-sin, sin], axis=-1)
    cos_t = jnp.concatenate([jnp.ones((TILE, ROPE_DIM), F32), cos32], axis=0)
    sin_t = jnp.concatenate([jnp.zeros((TILE, ROPE_DIM), F32), sin32], axis=0)
    pad = ((0, 0), (0, LANES - ROPE_DIM))
    return jnp.pad(cos_t, pad), jnp.pad(sin_t, pad)


_DEINT = np.concatenate([np.arange(0, ROPE_DIM, 2), np.arange(1, ROPE_DIM, 2)])
_DEINT_SW = np.concatenate([np.arange(1, ROPE_DIM, 2), np.arange(0, ROPE_DIM, 2)])


def _prep_w_in(w_in, b_in):
    offs = np.cumsum([0, 2 * D_CONV, D_RWKV, D_RWKV, D_RWKV, W_RANK, A_RANK, G_RANK, Q_RANK, KV_RANK,
                      ROPE_DIM, N_BRANCH * D_MODEL])
    o_conv, o_r, o_k, o_v, o_dw, o_da, o_dg, o_cq, o_ckv, o_kr, o_gate, _ = offs
    idx = np.zeros((IN_PAD,), np.int32)
    valid = np.zeros((IN_PAD,), bool)

    def put(dst, src):
        idx[dst:dst + len(src)] = src
        valid[dst:dst + len(src)] = True

    put(C_CONV, np.arange(o_conv, o_conv + 2 * D_CONV))
    for b, c in enumerate((C_G0, C_G1, C_G2)):
        put(c, np.arange(o_gate + b * D_MODEL, o_gate + (b + 1) * D_MODEL))
    put(C_R, np.arange(o_r, o_r + D_RWKV))
    put(C_K, np.arange(o_k, o_k + D_RWKV))
    put(C_V, np.arange(o_v, o_v + D_RWKV))
    put(C_LORA, np.arange(o_dw, o_dw + W_RANK + A_RANK + G_RANK))
    put(C_CQ, np.arange(o_cq, o_cq + Q_RANK))
    put(C_CKV, np.arange(o_ckv, o_ckv + KV_RANK))
    put(C_KR, np.arange(o_kr, o_kr + ROPE_DIM))
    put(C_KR + ROPE_DIM, o_kr + _DEINT)
    put(C_KR + 2 * ROPE_DIM, o_kr + _DEINT_SW)
    w = jnp.where(valid[None, :], jnp.take(w_in, idx, axis=1), 0.0).astype(BF16)
    b = jnp.where(valid, jnp.take(b_in, idx), 0.0)
    return w, b


def _prep_wuq(wuq):
    hd = NOPE_DIM + ROPE_DIM
    nq = N_MLA_HEADS * NOPE_DIM
    ncol = nq + 2 * N_MLA_HEADS * LANES
    idx = np.zeros((ncol,), np.int32)
    valid = np.zeros((ncol,), bool)
    for h in range(N_MLA_HEADS):
        idx[h * NOPE_DIM:(h + 1) * NOPE_DIM] = h * hd + np.arange(NOPE_DIM)
        valid[h * NOPE_DIM:(h + 1) * NOPE_DIM] = True
        for blk, perm in enumerate((_DEINT, _DEINT_SW)):
            c0 = nq + blk * N_MLA_HEADS * LANES + h * LANES
            idx[c0:c0 + ROPE_DIM] = h * hd + NOPE_DIM + perm
            valid[c0:c0 + ROPE_DIM] = True
    return jnp.where(valid[None, :], jnp.take(wuq, idx, axis=1), 0.0).astype(BF16)


def _block_diag_heads(wuv):
    w3 = wuv.reshape(KV_RANK, N_MLA_HEADS, V_DIM)
    eye = jnp.eye(N_MLA_HEADS, dtype=wuv.dtype)
    bd = w3.transpose(1, 0, 2)[:, :, None, :] * eye[:, None, :, None]
    return bd.reshape(N_MLA_HEADS * KV_RANK, N_MLA_HEADS * V_DIM).astype(BF16)


def _attention_keys(geo, ckvn, krr, cache_c, cache_kr):
    nc = geo.nc_tok
    pad = QK_W - KV_RANK - ROPE_DIM

    def keys(c, kr):
        kc = jnp.concatenate([c, kr, jnp.zeros(c.shape[:2] + (pad,), F32)], axis=-1).astype(BF16)
        return kc.transpose(0, 2, 1), c.astype(BF16)

    kr32 = krr[:, :ROPE_DIM]
    kct_c, cv_c = keys(ckvn[:nc].reshape(geo.n_ctx, geo.t_ctx, KV_RANK),
                       kr32[:nc].reshape(geo.n_ctx, geo.t_ctx, ROPE_DIM))
    c_l = jnp.concatenate([ckvn[nc:].reshape(geo.n_lat, geo.t_lat, KV_RANK), cache_c], axis=1)
    kr_l = jnp.concatenate([kr32[nc:].reshape(geo.n_lat, geo.t_lat, ROPE_DIM), cache_kr], axis=1)
    kct_l, cv_l = keys(c_l, kr_l)
    return kct_c, cv_c, kct_l, cv_l


def kernel(x_prompt, x_sample, cache_ckv, cache_krope, state_wkv, c, c_ctx, ada_w, ada_b, norm_mix_pre, norm_mix_post, norm_ffn_pre, norm_ffn_post, w_in, b_in, conv_w, conv_b, conv_ln_g, conv_ln_b, conv_wo, rwkv_mu, rwkv_w0, rwkv_bw, rwkv_a0, rwkv_ba, rwkv_bg, rwkv_xi, rwkv_alpha, rwkv_rho, rwkv_gn_g, rwkv_gn_b, rwkv_wo, mla_q_norm, mla_wuq, mla_kv_norm, mla_wuk, mla_wuv, mla_wo, w_out, ffn_w1, ffn_w3, ffn_w2, moe_router, moe_w1, moe_w3, moe_w2):
    P = dict(rwkv_mu=rwkv_mu, rwkv_w0=rwkv_w0, rwkv_bw=rwkv_bw, rwkv_a0=rwkv_a0, rwkv_ba=rwkv_ba,
             rwkv_bg=rwkv_bg, rwkv_xi=rwkv_xi, rwkv_alpha=rwkv_alpha, rwkv_rho=rwkv_rho)
    n_ctx, t_ctx, d = x_prompt.shape
    n_lat, t_lat, _ = x_sample.shape
    depth = ada_w.shape[0]
    past = cache_ckv.shape[2]
    geo = Geo(n_ctx, t_ctx, n_lat, t_lat, past)
    assert d == D_MODEL and (n_ctx * N_RWKV_HEADS) % LANES == 0 and LANES % (n_lat * 2 * N_RWKV_HEADS) == 0

    x = jnp.concatenate([x_prompt.reshape(-1, d), x_sample.reshape(-1, d)], axis=0)

    n_mod = 1 + n_lat
    n_mod_pad = -(-n_mod // SUBLANES) * SUBLANES
    c_all = jnp.zeros((n_mod_pad, d), F32).at[0].set(c_ctx).at[1:n_mod].set(c)
    head_id = np.arange(D_RWKV) // RWKV_HEAD
    ones_bd = jnp.asarray((head_id[:, None] == head_id[None, :]).astype(np.float32))
    cos_t, sin_t = _rope_tables(geo)

    ckv_out, kr_out, st_out = [], [], []
    for l in range(depth):
        mod = matmul_bias(c_all, ada_w[l].astype(BF16), ada_b[l], tm=n_mod_pad, tn=6 * d // 4,
                          pre="silu", name="ada_mod")
        mod = mod[:n_mod].reshape(n_mod * 6, 1, d)

        h = prenorm(x, norm_mix_pre[l], mod, geo, j_shift=0, j_scale=1)
        w_in_p, b_in_p = _prep_w_in(w_in[l], b_in[l])
        u = matmul_bias(h, w_in_p, b_in_p, tm=math.gcd(1024, geo.ntok), tn=1280, name="in_proj")

        y_conv = conv_branch(u, conv_w[l], conv_b[l], conv_ln_g[l], conv_ln_b[l], conv_wo[l], geo)

        pre_c = rwkv_pre(u, P, l, geo, ones_bd, 0, geo.nct)
        pre_l = rwkv_pre(u, P, l, geo, ones_bd, geo.nct, geo.nlt)
        oc_f, oc_b, sfin = _ctx_scan(geo, pre_c[:9])
        ol_f, ol_b = _lat_scan(geo, pre_l[:9], state_wkv[:, l])
        y_rwkv = rwkv_post(ol_f, ol_b, pre_l[10], pre_l[9], u, rwkv_gn_g[l], rwkv_gn_b[l], ones_bd,
                           rwkv_wo[l], None, geo.nct)
        y_rwkv = rwkv_post(oc_f, oc_b, pre_c[10], pre_c[9], u, rwkv_gn_g[l], rwkv_gn_b[l], ones_bd,
                           rwkv_wo[l], y_rwkv, 0)

        wukt = mla_wuk[l].reshape(KV_RANK, N_MLA_HEADS, NOPE_DIM).transpose(1, 2, 0).astype(BF16)
        qf, ckvn, krr = mla_pre(u, cos_t, sin_t, mla_q_norm[l], mla_kv_norm[l],
                                _prep_wuq(mla_wuq[l]), wukt, geo)
        kct_c, cv_c, kct_l, cv_l = _attention_keys(geo, ckvn, krr, cache_ckv[:, l],
                                                   cache_krope[:, l][..., _DEINT])
        wuv_bd = _block_diag_heads(mla_wuv[l])
        wo_b = mla_wo[l].astype(BF16)
        y_mla = attention(qf, kct_l, cv_l, wuv_bd, wo_b, u, None, row0=geo.nc_tok, t_seq=t_lat,
                          tq=TILE, name="attention_lat")
        y_mla = attention(qf, kct_c, cv_c, wuv_bd, wo_b, u, y_mla, row0=0, t_seq=t_ctx,
                          tq=TILE, name="attention_ctx")

        moe = (l % 2 == 1)
        i = l // 2
        router = moe_router[i] if moe else jnp.zeros((d, N_EXPERTS), F32)
        x1, h2, comb = merge(x, y_conv, y_rwkv, y_mla, w_out[l], norm_mix_post[l], norm_ffn_pre[l],
                             mod, router, geo, moe)
        if moe:
            row_src, dest, te, nused = _route_plan(comb, MOE_TM)
            xs = gather_rows(h2, row_src, rows=TILE)
            ys = ffn_sorted(xs, te, nused, moe_w1[i].astype(BF16), moe_w3[i].astype(BF16),
                            moe_w2[i].astype(BF16), tm=MOE_TM, tf=D_FF // 2)
            x = moe_combine_post(x1, ys, dest, comb, norm_ffn_post[l], mod, geo)
        else:
            y = ffn(h2, comb, ffn_w1[i:i + 1].astype(BF16), ffn_w3[i:i + 1].astype(BF16),
                    ffn_w2[i:i + 1].astype(BF16), tm=512, tf=D_FF // 2)
            x = ffn_post(x1, y, norm_ffn_post[l], mod, geo)

        ckv_out.append(ckvn[:geo.nc_tok].reshape(n_ctx, t_ctx, KV_RANK))
        kr_out.append(u[:geo.nc_tok, C_KR:C_KR + ROPE_DIM].reshape(n_ctx, t_ctx, ROPE_DIM))
        st_out.append(sfin)

    y_prompt = x[:geo.nc_tok].reshape(n_ctx, t_ctx, d)
    y_sample = x[geo.nc_tok:].reshape(n_lat, t_lat, d)
    return (y_prompt, y_sample, jnp.stack(ckv_out, axis=1), jnp.stack(kr_out, axis=1),
            jnp.stack(st_out, axis=1))
```

```python
import functools
import math

import numpy as np
import jax
import jax.numpy as jnp
from jax import lax
from jax.experimental import pallas as pl
from jax.experimental.pallas import tpu as pltpu

F32 = jnp.float32
BF16 = jnp.bfloat16

D_MODEL = 1024
GRID_W = 64
D_CONV = 512
CONV_K = 31
D_RWKV = 512
RWKV_HEAD = 64
N_RWKV_HEADS = D_RWKV // RWKV_HEAD
W_RANK = 64
A_RANK = 64
G_RANK = 128
DECAY_SCALE = math.exp(-0.5)
GN_EPS = 64e-5
N_MLA_HEADS = 8
Q_RANK = 256
KV_RANK = 128
NOPE_DIM = 64
ROPE_DIM = 32
V_DIM = 64
ROPE_BASE = 10000.0
ATTN_SCALE = 1.0 / math.sqrt(NOPE_DIM + ROPE_DIM)
N_BRANCH = 3
D_FF = 2816
N_EXPERTS = 8
EPS = 1e-6

LANES = 128
SUBLANES = 8
VMEM_LIMIT_BYTES = 56 * 1024 * 1024

TILE = 256
CONV_HALO = 16
SHIFT_HALO = 8

C_CONV, C_G0, C_G1, C_G2 = 0, 1024, 2048, 3072
C_R, C_K, C_V = 4096, 4608, 5120
C_LORA = 5632
C_CQ = 5888
C_CKV = 6144
C_KR = 6272
IN_PAD = 6400


def _cparams(sem):
    return pltpu.CompilerParams(dimension_semantics=sem, vmem_limit_bytes=VMEM_LIMIT_BYTES)


def _sigmoid(x):
    return jax.nn.sigmoid(x)


def _rms(x, g):
    return (x * lax.rsqrt(jnp.mean(x * x, axis=-1, keepdims=True) + EPS)) * g


class Geo:
    def __init__(self, n_ctx, t_ctx, n_lat, t_lat, past):
        assert t_ctx % TILE == 0 and t_lat % TILE == 0
        self.n_ctx, self.t_ctx, self.n_lat, self.t_lat, self.past = n_ctx, t_ctx, n_lat, t_lat, past
        self.ct = t_ctx // TILE
        self.lt = t_lat // TILE
        self.nct = n_ctx * self.ct
        self.nlt = n_lat * self.lt
        self.ntile = self.nct + self.nlt
        self.nc_tok = n_ctx * t_ctx
        self.nl_tok = n_lat * t_lat
        self.ntok = self.nc_tok + self.nl_tok

    def pos(self, i):
        is_ctx = i < self.nct
        p = jnp.where(is_ctx, i % self.ct, (i - self.nct) % self.lt)
        n = jnp.where(is_ctx, self.ct, self.lt)
        return p, n

    def mod_row(self, i):
        return jnp.where(i < self.nct, 0, 1 + (i - self.nct) // self.lt)

    def rope_blk(self, i):
        return jnp.where(i < self.nct, 0, 1 + (i - self.nct) % self.lt)


def _mm_kernel(x_ref, w_ref, b_ref, o_ref, *, pre):
    x = x_ref[...]
    if pre == "silu":
        x = x.astype(F32)
        x = x * _sigmoid(x)
    acc = jnp.dot(x.astype(BF16), w_ref[...], preferred_element_type=F32)
    o_ref[...] = (acc + b_ref[...]).astype(o_ref.dtype)


def matmul_bias(x, w, b, *, tm, tn, pre=None, out_dtype=F32, name="matmul"):
    m, k = x.shape
    n = w.shape[1]
    assert m % tm == 0 and n % tn == 0
    return pl.pallas_call(
        functools.partial(_mm_kernel, pre=pre),
        grid=(n // tn, m // tm),
        in_specs=[
            pl.BlockSpec((tm, k), lambda j, i: (i, 0)),
            pl.BlockSpec((k, tn), lambda j, i: (0, j)),
            pl.BlockSpec((1, tn), lambda j, i: (0, j)),
        ],
        out_specs=pl.BlockSpec((tm, tn), lambda j, i: (i, j)),
        out_shape=jax.ShapeDtypeStruct((m, n), out_dtype),
        compiler_params=_cparams(("arbitrary", "arbitrary")),
        name=name,
    )(x, w, b.reshape(1, n).astype(F32))


def _prenorm_kernel(x_ref, g_ref, sc_ref, sh_ref, o_ref):
    h = _rms(x_ref[...], g_ref[...]) * (1.0 + sc_ref[0]) + sh_ref[0]
    o_ref[...] = h.astype(o_ref.dtype)


def prenorm(x, g, mod, geo, j_shift, j_scale):
    n, d = x.shape
    return pl.pallas_call(
        _prenorm_kernel,
        grid=(geo.ntile,),
        in_specs=[
            pl.BlockSpec((TILE, d), lambda i: (i, 0)),
            pl.BlockSpec((1, d), lambda i: (0, 0)),
            pl.BlockSpec((1, 1, d), lambda i: (geo.mod_row(i) * 6 + j_scale, 0, 0)),
            pl.BlockSpec((1, 1, d), lambda i: (geo.mod_row(i) * 6 + j_shift, 0, 0)),
        ],
        out_specs=pl.BlockSpec((TILE, d), lambda i: (i, 0)),
        out_shape=jax.ShapeDtypeStruct((n, d), BF16),
        compiler_params=_cparams(("arbitrary",)),
        name="prenorm",
    )(x, g.reshape(1, d), mod, mod)


def _conv_kernel(cur_ref, prev_ref, next_ref, gate_ref, cw_ref, cb_ref, lng_ref, lnb_ref,
                 wo_ref, o_ref, hp_ref, *, geo):
    i = pl.program_id(0)
    p, n = geo.pos(i)
    has_prev = p > 0
    has_next = p < n - 1

    def glu(z):
        return z[:, :D_CONV] * _sigmoid(z[:, D_CONV:])

    hp_ref[0, 0:CONV_HALO, :] = jnp.where(has_prev, glu(prev_ref[...]), 0.0)
    hp_ref[0, CONV_HALO:CONV_HALO + TILE, :] = glu(cur_ref[...])
    hp_ref[0, CONV_HALO + TILE:, :] = jnp.where(has_next, glu(next_ref[...]), 0.0)
    nrow = TILE + 2 * CONV_HALO - SUBLANES
    for b in range(1, SUBLANES):
        hp_ref[b, 0:nrow, :] = hp_ref[0, pl.ds(b, nrow), :]

    off = CONV_HALO - CONV_K // 2
    acc = None
    for j in range(CONV_K):
        q = off + j
        term = hp_ref[q % SUBLANES, pl.ds(q - q % SUBLANES, TILE), :] * cw_ref[j:j + 1, :]
        acc = term if acc is None else acc + term
    h = acc + cb_ref[...]
    mu = jnp.mean(h, axis=-1, keepdims=True)
    hc = h - mu
    var = jnp.mean(hc * hc, axis=-1, keepdims=True)
    h = hc * lax.rsqrt(var + EPS) * lng_ref[...] + lnb_ref[...]
    h = h * _sigmoid(h)
    y = jnp.dot(h.astype(BF16), wo_ref[...], preferred_element_type=F32)
    o_ref[...] = _sigmoid(gate_ref[...]) * y


def conv_branch(u, cw, cb, lng, lnb, wo, geo):
    n = u.shape[0]
    hb = TILE // CONV_HALO
    nhalo = n // CONV_HALO
    cwp = jnp.zeros((32, D_CONV), F32).at[:CONV_K].set(cw)
    return pl.pallas_call(
        functools.partial(_conv_kernel, geo=geo),
        grid=(geo.ntile,),
        in_specs=[
            pl.BlockSpec((TILE, 2 * D_CONV), lambda i: (i, C_CONV // (2 * D_CONV))),
            pl.BlockSpec((CONV_HALO, 2 * D_CONV), lambda i: (jnp.maximum(i * hb - 1, 0), 0)),
            pl.BlockSpec((CONV_HALO, 2 * D_CONV), lambda i: (jnp.minimum((i + 1) * hb, nhalo - 1), 0)),
            pl.BlockSpec((TILE, D_MODEL), lambda i: (i, C_G0 // D_MODEL)),
            pl.BlockSpec((32, D_CONV), lambda i: (0, 0)),
            pl.BlockSpec((1, D_CONV), lambda i: (0, 0)),
            pl.BlockSpec((1, D_CONV), lambda i: (0, 0)),
            pl.BlockSpec((1, D_CONV), lambda i: (0, 0)),
            pl.BlockSpec((D_CONV, D_MODEL), lambda i: (0, 0)),
        ],
        out_specs=pl.BlockSpec((TILE, D_MODEL), lambda i: (i, 0)),
        out_shape=jax.ShapeDtypeStruct((n, D_MODEL), F32),
        scratch_shapes=[pltpu.VMEM((SUBLANES, TILE + 2 * CONV_HALO, D_CONV), F32)],
        compiler_params=_cparams(("arbitrary",)),
        name="conv_branch",
    )(u, u, u, u, cwp, cb.reshape(1, -1), lng.reshape(1, -1), lnb.reshape(1, -1), wo.astype(BF16))


def _seg_sum(x, ones_bd):
    return jnp.dot(x, ones_bd, preferred_element_type=F32, precision=lax.Precision.HIGHEST)


def _rwkv_pre_kernel(r_ref, rp_ref, rn_ref, k_ref, kp_ref, kn_ref, v_ref, vp_ref, vn_ref,
                     lora_ref, mu_ref, w0_ref, bw_ref, a0_ref, ba_ref, bg_ref, xi_ref, al_ref,
                     rho_ref, ones_ref,
                     r_o, kk_o, v_o, w0_o, w1_o, ka0_o, ka1_o, kt0_o, kt1_o, g_o, bonus_o, *, geo, tile0):
    i = pl.program_id(0) + tile0
    p, n = geo.pos(i)
    has_prev = p > 0
    has_next = p < n - 1
    row = lax.broadcasted_iota(jnp.int32, (TILE, D_RWKV), 0)

    def shifted(c_ref, p_ref, n_ref, mu):
        cur = c_ref[...]
        pv = jnp.where(has_prev, p_ref[SHIFT_HALO - 1:SHIFT_HALO, :], 0.0)
        nx = jnp.where(has_next, n_ref[0:1, :], 0.0)
        prev = jnp.where(row == 0, pv, pltpu.roll(cur, 1, axis=0))
        nxt = jnp.where(row == TILE - 1, nx, pltpu.roll(cur, TILE - 1, axis=0))
        return cur + mu * (0.5 * (prev + nxt) - cur)

    r = shifted(r_ref, rp_ref, rn_ref, mu_ref[0:1, :])
    k = shifted(k_ref, kp_ref, kn_ref, mu_ref[1:2, :])
    v = shifted(v_ref, vp_ref, vn_ref, mu_ref[2:3, :])

    lora = lora_ref[...]
    dw = jnp.tanh(lora[:, :W_RANK]).astype(BF16)
    da = lora[:, W_RANK:W_RANK + A_RANK].astype(BF16)
    dg = _sigmoid(lora[:, W_RANK + A_RANK:]).astype(BF16)
    wl = jnp.dot(dw, bw_ref[...], preferred_element_type=F32)
    al = jnp.dot(da, ba_ref[...], preferred_element_type=F32)
    g = jnp.dot(dg, bg_ref[...], preferred_element_type=F32)

    ones_bd = ones_ref[...]
    kx = k * xi_ref[...]
    kk = kx * lax.rsqrt(_seg_sum(kx * kx, ones_bd) + EPS)
    alpha = al_ref[...]
    rho = rho_ref[...]
    bonus = jnp.zeros((TILE, D_RWKV), F32)
    for d, (w_o, ka_o, kt_o) in enumerate(((w0_o, ka0_o, kt0_o), (w1_o, ka1_o, kt1_o))):
        sl = slice(d * D_RWKV, (d + 1) * D_RWKV)
        w = jnp.exp(-DECAY_SCALE * _sigmoid(w0_ref[d:d + 1, :] + wl[:, sl]))
        a = _sigmoid(a0_ref[d:d + 1, :] + al[:, sl])
        kt = k * (1.0 + (a - 1.0) * alpha)
        w_o[...] = w
        ka_o[...] = kk * a
        kt_o[...] = kt
        bonus = bonus + _seg_sum(r * kt * rho, ones_bd) * v
    r_o[...] = r
    kk_o[...] = kk
    v_o[...] = v
    g_o[...] = g
    bonus_o[...] = bonus


def rwkv_pre(u, P, l, geo, ones_bd, tile0, ntiles, seq_block=None):
    hb = TILE // SHIFT_HALO
    nhalo = u.shape[0] // SHIFT_HALO

    def trio(c0):
        cb = c0 // D_RWKV
        return [
            pl.BlockSpec((TILE, D_RWKV), lambda i: (i + tile0, cb)),
            pl.BlockSpec((SHIFT_HALO, D_RWKV), lambda i: (jnp.maximum((i + tile0) * hb - 1, 0), cb)),
            pl.BlockSpec((SHIFT_HALO, D_RWKV), lambda i: (jnp.minimum((i + tile0 + 1) * hb, nhalo - 1), cb)),
        ]

    def full(shape):
        return pl.BlockSpec(shape, lambda i: tuple(0 for _ in shape))

    bw = jnp.concatenate([P["rwkv_bw"][l, 0], P["rwkv_bw"][l, 1]], axis=1).astype(BF16)
    ba = jnp.concatenate([P["rwkv_ba"][l, 0], P["rwkv_ba"][l, 1]], axis=1).astype(BF16)
    out = jax.ShapeDtypeStruct((ntiles * TILE, D_RWKV), F32)
    tok_spec = pl.BlockSpec((TILE, D_RWKV), lambda i: (i, 0))
    if seq_block is None:
        scan_spec, scan_out = tok_spec, out
    else:
        bl, ts = seq_block
        scan_out = jax.ShapeDtypeStruct((ntiles * TILE // bl, bl * D_RWKV), F32)
        scan_spec = pl.BlockSpec((TILE, D_RWKV),
                                 lambda i: ((i // ts // bl) * ts + i % ts, (i // ts) % bl))
    return pl.pallas_call(
        functools.partial(_rwkv_pre_kernel, geo=geo, tile0=tile0),
        grid=(ntiles,),
        in_specs=trio(C_R) + trio(C_K) + trio(C_V) + [
            pl.BlockSpec((TILE, 256), lambda i: (i + tile0, C_LORA // 256)),
            full((3, D_RWKV)), full((2, D_RWKV)), full((W_RANK, 2 * D_RWKV)),
            full((2, D_RWKV)), full((A_RANK, 2 * D_RWKV)), full((G_RANK, D_RWKV)),
            full((1, D_RWKV)), full((1, D_RWKV)), full((1, D_RWKV)), full((D_RWKV, D_RWKV)),
        ],
        out_specs=[scan_spec] * 9 + [tok_spec] * 2,
        out_shape=[scan_out] * 9 + [out] * 2,
        compiler_params=_cparams(("arbitrary",)),
        name="rwkv_pre",
    )(u, u, u, u, u, u, u, u, u, u,
      P["rwkv_mu"][l], P["rwkv_w0"][l], bw, P["rwkv_a0"][l], ba, P["rwkv_bg"][l].astype(BF16),
      P["rwkv_xi"][l].reshape(1, -1), P["rwkv_alpha"][l].reshape(1, -1),
      P["rwkv_rho"][l].reshape(1, -1), ones_bd)


N_VBLK = RWKV_HEAD // SUBLANES


def _scan_steps(xrow, vload, ostore, s_ref, tidx, *, tc, nk, kl, ngroups=2, unroll=1):
    gsz = N_VBLK // ngroups
    groups = tuple(tuple(range(g * gsz, (g + 1) * gsz)) for g in range(ngroups))

    def allred(a):
        out = a
        for j in range(1, kl):
            out = out + pltpu.roll(a, j * (LANES // kl), axis=1)
        return out

    def first_sa(grp, t):
        acc = [None] * len(grp)
        for kh in range(nk):
            kkb = xrow(t, 1, kh)
            for j, vb in enumerate(grp):
                pr = s_ref[vb, kh] * kkb
                acc[j] = pr if acc[j] is None else acc[j] + pr
        return tuple(allred(a) for a in acc)

    def fused(grp, s, t, t_next, sa):
        vv = [vload(t, vb) for vb in grp]
        oacc = [None] * len(grp)
        acc = [None] * len(grp)
        for kh in range(nk):
            wb, kab, ktb, rb = xrow(t, 0, kh), xrow(t, 2, kh), xrow(t, 3, kh), xrow(t, 4, kh)
            kkn = xrow(t_next, 1, kh)
            for j, vb in enumerate(grp):
                sn = s_ref[vb, kh] * wb - sa[j] * kab + vv[j] * ktb
                s_ref[vb, kh] = sn
                po = sn * rb
                pa = sn * kkn
                oacc[j] = po if oacc[j] is None else oacc[j] + po
                acc[j] = pa if acc[j] is None else acc[j] + pa
        for j, vb in enumerate(grp):
            ostore(s, t, vb, oacc[j])
        return tuple(allred(a) for a in acc)

    t0 = tidx(0)
    carry0 = tuple(first_sa(g, t0) for g in groups)

    def step(s, carry):
        t = tidx(s)
        t_next = tidx(jnp.minimum(s + 1, tc - 1))
        return tuple(fused(g, s, t, t_next, carry[i]) for i, g in enumerate(groups))

    lax.fori_loop(0, tc, step, carry0, unroll=unroll)


def _vrows(vb):
    return pl.ds(vb * SUBLANES, SUBLANES)


def _scan_ctx_kernel(w_ref, kk_ref, ka_ref, kt_ref, r_ref, v_ref, o_ref, sfin_ref, s_ref, *, tc, nk, rev):
    c = pl.program_id(1)
    x_refs = (w_ref, kk_ref, ka_ref, kt_ref, r_ref)

    @pl.when(c == 0)
    def _():
        s_ref[...] = jnp.zeros(s_ref.shape, F32)

    def xrow(t, a, kh):
        return jnp.broadcast_to(x_refs[a][0, t, pl.ds(kh, 1), :], (SUBLANES, LANES))

    def vload(t, vb):
        return v_ref[0, t, _vrows(vb), :]

    def ostore(s, t, vb, val):
        o_ref[0, t, _vrows(vb), :] = val

    def tidx(s):
        return tc - 1 - s if rev else s

    _scan_steps(xrow, vload, ostore, s_ref, tidx, tc=tc, nk=nk, kl=1)

    @pl.when(c == pl.num_programs(1) - 1)
    def _():
        sfin_ref[0] = s_ref[...]


def wkv_scan_ctx(w, kk, ka, kt, r, v, *, tc, rev):
    ngb, t_len, nk, _ = w.shape
    nch = t_len // tc
    blk = pl.BlockSpec((1, tc, nk, LANES), lambda g, c: (g, nch - 1 - c if rev else c, 0, 0))
    st = pl.BlockSpec((1, N_VBLK, nk, SUBLANES, LANES), lambda g, c: (g, 0, 0, 0, 0))
    return pl.pallas_call(
        functools.partial(_scan_ctx_kernel, tc=tc, nk=nk, rev=rev),
        grid=(ngb, nch),
        in_specs=[blk] * 6,
        out_specs=[blk, st],
        out_shape=[
            jax.ShapeDtypeStruct((ngb, t_len, RWKV_HEAD, LANES), F32),
            jax.ShapeDtypeStruct((ngb, N_VBLK, nk, SUBLANES, LANES), F32),
        ],
        scratch_shapes=[pltpu.VMEM((N_VBLK, nk, SUBLANES, LANES), F32)],
        compiler_params=_cparams(("arbitrary", "arbitrary")),
        name="wkv_scan_ctx",
    )(w, kk, ka, kt, r, v)


def _scan_lat_kernel(*refs, tc, nk, kl):
    xa_refs, xb_refs = refs[0:5], refs[5:10]
    va_ref, vb_ref, s0_ref, oa_ref, ob_ref, s_ref, xm_ref, vm_ref = refs[10:]
    c = pl.program_id(0)

    @pl.when(c == 0)
    def _():
        s_ref[...] = s0_ref[...]

    def is_bwd(shape):
        lane = lax.broadcasted_iota(jnp.int32, shape, len(shape) - 1)
        return (lane // N_RWKV_HEADS) % 2 == 1

    mx = is_bwd((nk, LANES))
    mv = is_bwd((RWKV_HEAD, LANES))

    def merge(s, carry):
        for a in range(5):
            xm_ref[a, s] = jnp.where(mx, xb_refs[a][tc - 1 - s], xa_refs[a][s])
        vm_ref[s] = jnp.where(mv, vb_ref[tc - 1 - s], va_ref[s])
        return carry

    lax.fori_loop(0, tc, merge, 0)

    def xrow(t, a, kh):
        return jnp.broadcast_to(xm_ref[a, t, pl.ds(kh, 1), :], (SUBLANES, LANES))

    def vload(t, vb):
        return vm_ref[t, _vrows(vb), :]

    def ostore(s, t, vb, val):
        oa_ref[s, _vrows(vb), :] = val
        ob_ref[tc - 1 - s, _vrows(vb), :] = val

    _scan_steps(xrow, vload, ostore, s_ref, lambda s: s, tc=tc, nk=nk, kl=kl, ngroups=2, unroll=2)


def wkv_scan_lat(xs, v, s0, *, tc, kl):
    t_len, nk, _ = xs[0].shape
    nch = t_len // tc
    o_sds = jax.ShapeDtypeStruct((t_len, RWKV_HEAD, LANES), F32)
    xa = pl.BlockSpec((tc, nk, LANES), lambda c: (c, 0, 0))
    xb = pl.BlockSpec((tc, nk, LANES), lambda c: (nch - 1 - c, 0, 0))
    va = pl.BlockSpec((tc, RWKV_HEAD, LANES), lambda c: (c, 0, 0))
    vb = pl.BlockSpec((tc, RWKV_HEAD, LANES), lambda c: (nch - 1 - c, 0, 0))
    return pl.pallas_call(
        functools.partial(_scan_lat_kernel, tc=tc, nk=nk, kl=kl),
        grid=(nch,),
        in_specs=[xa] * 5 + [xb] * 5 + [va, vb,
                                        pl.BlockSpec((N_VBLK, nk, SUBLANES, LANES), lambda c: (0, 0, 0, 0))],
        out_specs=[va, vb],
        out_shape=[o_sds, o_sds],
        scratch_shapes=[pltpu.VMEM((N_VBLK, nk, SUBLANES, LANES), F32),
                        pltpu.VMEM((5, tc, nk, LANES), F32),
                        pltpu.VMEM((tc, RWKV_HEAD, LANES), F32)],
        compiler_params=_cparams(("arbitrary",)),
        name="wkv_scan_lat",
    )(*xs, *xs, v, v, s0)


def _rwkv_post_kernel(of_ref, ob_ref, bonus_ref, g_ref, gate_ref, gng_ref, gnb_ref, ones_ref,
                      wo_ref, *rest):
    y_ref = rest[-1]
    o = of_ref[...] + ob_ref[...]
    ones_bd = ones_ref[...]
    mean = _seg_sum(o, ones_bd) * (1.0 / RWKV_HEAD)
    oc = o - mean
    var = _seg_sum(oc * oc, ones_bd) * (1.0 / RWKV_HEAD)
    gn = oc * lax.rsqrt(var + GN_EPS) * gng_ref[...] + gnb_ref[...]
    y = (gn + bonus_ref[...]) * g_ref[...]
    y = jnp.dot(y.astype(BF16), wo_ref[...], preferred_element_type=F32)
    y_ref[...] = _sigmoid(gate_ref[...]) * y


def rwkv_post(o_f, o_b, bonus, g, u, gng, gnb, ones_bd, wo, y_prev, tile0, seq_block=None):
    ntiles = bonus.shape[0] // TILE

    def full(shape):
        return pl.BlockSpec(shape, lambda i: tuple(0 for _ in shape))

    part = pl.BlockSpec((TILE, D_RWKV), lambda i: (i, 0))
    if seq_block is None:
        opart = part
    else:
        bl, ts = seq_block
        opart = pl.BlockSpec((TILE, D_RWKV), lambda i: ((i // ts // bl) * ts + i % ts, (i // ts) % bl))
    in_specs = [opart, opart, part, part,
                pl.BlockSpec((TILE, D_MODEL), lambda i: (i + tile0, C_G1 // D_MODEL)),
                full((1, D_RWKV)), full((1, D_RWKV)), full((D_RWKV, D_RWKV)), full((D_RWKV, D_MODEL))]
    args = [o_f, o_b, bonus, g, u, gng.reshape(1, -1), gnb.reshape(1, -1), ones_bd, wo.astype(BF16)]
    aliases = {}
    if y_prev is not None:
        in_specs.append(pl.BlockSpec(memory_space=pl.ANY))
        args.append(y_prev)
        aliases = {len(args) - 1: 0}
    return pl.pallas_call(
        _rwkv_post_kernel,
        grid=(ntiles,),
        in_specs=in_specs,
        out_specs=pl.BlockSpec((TILE, D_MODEL), lambda i: (i + tile0, 0)),
        out_shape=jax.ShapeDtypeStruct((u.shape[0], D_MODEL), F32),
        input_output_aliases=aliases,
        compiler_params=_cparams(("arbitrary",)),
        name="rwkv_post",
    )(*args)


QK_W = 2 * LANES
Q_SCALE = ATTN_SCALE * math.log2(math.e)


def _mla_pre_kernel(cq_ref, ckv_ref, kr_ref, cos_ref, sin_ref, qg_ref, kvg_ref, wuq_ref, wukt_ref,
                    qf_o, ckvn_o, krr_o):
    nq = N_MLA_HEADS * NOPE_DIM
    nrp = N_MLA_HEADS * LANES
    cq = _rms(cq_ref[...], qg_ref[...])
    q = jnp.dot(cq.astype(BF16), wuq_ref[...], preferred_element_type=F32)
    cos = cos_ref[...]
    sin = sin_ref[...]
    for h in range(N_MLA_HEADS):
        qn = q[:, h * NOPE_DIM:(h + 1) * NOPE_DIM].astype(BF16)
        qa = jnp.dot(qn, wukt_ref[h], preferred_element_type=F32)
        qr = (q[:, nq + h * LANES:nq + (h + 1) * LANES] * cos
              + q[:, nq + nrp + h * LANES:nq + nrp + (h + 1) * LANES] * sin)
        qf_o[:, h * QK_W:h * QK_W + LANES] = (qa * Q_SCALE).astype(qf_o.dtype)
        qf_o[:, h * QK_W + LANES:(h + 1) * QK_W] = (qr * Q_SCALE).astype(qf_o.dtype)
    ckvn_o[...] = _rms(ckv_ref[...], kvg_ref[...])
    kr = kr_ref[...]
    krr = kr[:, ROPE_DIM:2 * ROPE_DIM] * cos[:, :ROPE_DIM] + kr[:, 2 * ROPE_DIM:3 * ROPE_DIM] * sin[:, :ROPE_DIM]
    krr_o[...] = jnp.concatenate([krr, jnp.zeros((TILE, LANES - ROPE_DIM), F32)], axis=1)


def mla_pre(u, cos_t, sin_t, qg, kvg, wuq_p, wukt, geo):
    n = u.shape[0]

    def full(shape):
        return pl.BlockSpec(shape, lambda i: tuple(0 for _ in shape))

    return pl.pallas_call(
        _mla_pre_kernel,
        grid=(geo.ntile,),
        in_specs=[
            pl.BlockSpec((TILE, Q_RANK), lambda i: (i, C_CQ // Q_RANK)),
            pl.BlockSpec((TILE, KV_RANK), lambda i: (i, C_CKV // KV_RANK)),
            pl.BlockSpec((TILE, LANES), lambda i: (i, C_KR // LANES)),
            pl.BlockSpec((TILE, LANES), lambda i: (geo.rope_blk(i), 0)),
            pl.BlockSpec((TILE, LANES), lambda i: (geo.rope_blk(i), 0)),
            full((1, Q_RANK)), full((1, KV_RANK)),
            full(wuq_p.shape), full(wukt.shape),
        ],
        out_specs=[
            pl.BlockSpec((TILE, N_MLA_HEADS * QK_W), lambda i: (i, 0)),
            pl.BlockSpec((TILE, KV_RANK), lambda i: (i, 0)),
            pl.BlockSpec((TILE, LANES), lambda i: (i, 0)),
        ],
        out_shape=[
            jax.ShapeDtypeStruct((n, N_MLA_HEADS * QK_W), BF16),
            jax.ShapeDtypeStruct((n, KV_RANK), F32),
            jax.ShapeDtypeStruct((n, LANES), F32),
        ],
        compiler_params=_cparams(("arbitrary",)),
        name="mla_pre",
    )(u, u, u, cos_t, sin_t, qg.reshape(1, -1), kvg.reshape(1, -1), wuq_p, wukt)


ATT_KEY_CHUNK = 512
ATT_HEADS_PER_STEP = 4


def _attn_kernel(q_ref, kt_ref, c_ref, wuv_ref, wo_ref, gate_ref, *rest, ck):
    y_ref, pc_ref = rest[-2:]
    hg = pl.program_id(2)
    tq = q_ref.shape[0]
    t_k = kt_ref.shape[2]
    hp = ATT_HEADS_PER_STEP
    q = jnp.concatenate([q_ref[:, j * QK_W:(j + 1) * QK_W] for j in range(hp)], axis=0)
    m = jnp.full((hp * tq, 1), -jnp.inf, F32)
    l = jnp.zeros((hp * tq, 1), F32)
    acc = jnp.zeros((hp * tq, KV_RANK), F32)
    for c0 in range(0, t_k, ck):
        s = jnp.dot(q, kt_ref[0, :, c0:c0 + ck], preferred_element_type=F32)
        m_new = jnp.maximum(m, jnp.max(s, axis=-1, keepdims=True))
        alpha = jnp.exp2(m - m_new)
        p = jnp.exp2(s - m_new)
        l = alpha * l + jnp.sum(p, axis=-1, keepdims=True)
        acc = alpha * acc + jnp.dot(p.astype(BF16), c_ref[0, c0:c0 + ck, :], preferred_element_type=F32)
        m = m_new
    pc = (acc / l).astype(BF16)
    for j in range(hp):
        pc_ref[hg * hp + j] = pc[j * tq:(j + 1) * tq]

    @pl.when(hg == pl.num_programs(2) - 1)
    def _():
        pcs = jnp.concatenate([pc_ref[i] for i in range(N_MLA_HEADS)], axis=1)
        oh = jnp.dot(pcs, wuv_ref[...], preferred_element_type=F32)
        y = jnp.dot(oh.astype(BF16), wo_ref[...], preferred_element_type=F32)
        y_ref[...] = _sigmoid(gate_ref[...]) * y


def attention(qf, kct, cv, wuv_bd, wo, u, y_prev, *, row0, t_seq, tq, name):
    n = qf.shape[0]
    nseq, _, t_k = kct.shape
    qt = t_seq // tq
    rb0 = row0 // tq
    ck = math.gcd(ATT_KEY_CHUNK, t_k)

    def rows(w, col):
        return pl.BlockSpec((tq, w), lambda s, i, h: (rb0 + s * qt + i, col(h)))

    in_specs = [
        rows(ATT_HEADS_PER_STEP * QK_W, lambda h: h),
        pl.BlockSpec((1, QK_W, t_k), lambda s, i, h: (s, 0, 0)),
        pl.BlockSpec((1, t_k, KV_RANK), lambda s, i, h: (s, 0, 0)),
        pl.BlockSpec(wuv_bd.shape, lambda s, i, h: (0, 0)),
        pl.BlockSpec(wo.shape, lambda s, i, h: (0, 0)),
        rows(D_MODEL, lambda h: C_G2 // D_MODEL),
    ]
    args = [qf, kct, cv, wuv_bd, wo, u]
    aliases = {}
    if y_prev is not None:
        in_specs.append(pl.BlockSpec(memory_space=pl.ANY))
        args.append(y_prev)
        aliases = {len(args) - 1: 0}
    return pl.pallas_call(
        functools.partial(_attn_kernel, ck=ck),
        grid=(nseq, qt, N_MLA_HEADS // ATT_HEADS_PER_STEP),
        in_specs=in_specs,
        out_specs=rows(D_MODEL, lambda h: 0),
        out_shape=jax.ShapeDtypeStruct((n, D_MODEL), F32),
        scratch_shapes=[pltpu.VMEM((N_MLA_HEADS, tq, KV_RANK), BF16)],
        input_output_aliases=aliases,
        compiler_params=_cparams(("arbitrary", "arbitrary", "arbitrary")),
        name=name,
    )(*args)


R_I1, R_I2, R_G1, R_G2 = 0, 1, 2, 3


def _merge_kernel(x_ref, yc_ref, yr_ref, ym_ref, wout_ref, gpost_ref, g1_ref, gpre_ref, sc_ref,
                  sh_ref, router_ref, x1_o, h2_o, comb_o, *, moe):
    m = yc_ref[...] + yr_ref[...] + ym_ref[...]
    y = jnp.dot(m.astype(BF16), wout_ref[...], preferred_element_type=F32)
    x1 = x_ref[...] + g1_ref[0] * _rms(y, gpost_ref[...])
    x1_o[...] = x1
    h2 = _rms(x1, gpre_ref[...]) * (1.0 + sc_ref[0]) + sh_ref[0]
    h2_o[...] = h2.astype(h2_o.dtype)
    if moe:
        logits = jnp.dot(h2, router_ref[...], preferred_element_type=F32,
                         precision=lax.Precision.HIGHEST)
        lane = lax.broadcasted_iota(jnp.int32, logits.shape, 1)
        neg = jnp.float32(-jnp.inf)
        logits = jnp.where(lane < N_EXPERTS, logits, neg)
        m1 = jnp.max(logits, axis=-1, keepdims=True)
        i1 = jnp.min(jnp.where(logits == m1, lane, LANES), axis=-1, keepdims=True)
        rest = jnp.where(lane == i1, neg, logits)
        m2 = jnp.max(rest, axis=-1, keepdims=True)
        i2 = jnp.min(jnp.where(rest == m2, lane, LANES), axis=-1, keepdims=True)
        e2 = jnp.exp(m2 - m1)
        den = 1.0 + e2
        cols = ((R_I1, i1.astype(F32)), (R_I2, i2.astype(F32)), (R_G1, 1.0 / den), (R_G2, e2 / den))
        route = jnp.zeros(comb_o.shape, F32)
        for col, val in cols:
            route = jnp.where(lane == col, val, route)
        comb_o[...] = route
    else:
        comb_o[...] = jnp.ones(comb_o.shape, F32)


def merge(x, yc, yr, ym, wout, gpost, gpre, mod, router, geo, moe):
    n, d = x.shape

    def full(shape):
        return pl.BlockSpec(shape, lambda i: tuple(0 for _ in shape))

    def rows():
        return pl.BlockSpec((TILE, d), lambda i: (i, 0))

    def modspec(j):
        return pl.BlockSpec((1, 1, d), lambda i: (geo.mod_row(i) * 6 + j, 0, 0))

    router_p = jnp.zeros((d, LANES), F32).at[:, :N_EXPERTS].set(router)
    return pl.pallas_call(
        functools.partial(_merge_kernel, moe=moe),
        grid=(geo.ntile,),
        in_specs=[rows(), rows(), rows(), rows(), full((d, d)), full((1, d)), modspec(2),
                  full((1, d)), modspec(4), modspec(3), full((d, LANES))],
        out_specs=[rows(), rows(), pl.BlockSpec((TILE, LANES), lambda i: (i, 0))],
        out_shape=[jax.ShapeDtypeStruct((n, d), F32), jax.ShapeDtypeStruct((n, d), F32 if moe else BF16),
                   jax.ShapeDtypeStruct((n, LANES), F32)],
        compiler_params=_cparams(("arbitrary",)),
        name="merge",
    )(x, yc, yr, ym, wout.astype(BF16), gpost.reshape(1, d), mod, gpre.reshape(1, d), mod, mod,
      router_p)


def _ffn_kernel(h_ref, comb_ref, w1_ref, w3_ref, w2_ref, o_ref, acc_ref):
    e = pl.program_id(1)
    f = pl.program_id(2)

    @pl.when((e == 0) & (f == 0))
    def _():
        acc_ref[...] = jnp.zeros(acc_ref.shape, F32)

    h = h_ref[...]
    a = jnp.dot(h, w1_ref[0], preferred_element_type=F32)
    b = jnp.dot(h, w3_ref[0], preferred_element_type=F32)
    comb = comb_ref[...]
    lane = lax.broadcasted_iota(jnp.int32, comb.shape, 1)
    ce = jnp.sum(jnp.where(lane == e, comb, 0.0), axis=-1, keepdims=True)
    act = (a * _sigmoid(a) * b) * ce
    acc_ref[...] += jnp.dot(act.astype(BF16), w2_ref[0], preferred_element_type=F32)

    @pl.when((e == pl.num_programs(1) - 1) & (f == pl.num_programs(2) - 1))
    def _():
        o_ref[...] = acc_ref[...]


def ffn(h, comb, w1, w3, w2, *, tm, tf):
    n, d = h.shape
    ne, _, dff = w1.shape
    return pl.pallas_call(
        _ffn_kernel,
        grid=(n // tm, ne, dff // tf),
        in_specs=[
            pl.BlockSpec((tm, d), lambda i, e, f: (i, 0)),
            pl.BlockSpec((tm, LANES), lambda i, e, f: (i, 0)),
            pl.BlockSpec((1, d, tf), lambda i, e, f: (e, 0, f)),
            pl.BlockSpec((1, d, tf), lambda i, e, f: (e, 0, f)),
            pl.BlockSpec((1, tf, d), lambda i, e, f: (e, f, 0)),
        ],
        out_specs=pl.BlockSpec((tm, d), lambda i, e, f: (i, 0)),
        out_shape=jax.ShapeDtypeStruct((n, d), F32),
        scratch_shapes=[pltpu.VMEM((tm, d), F32)],
        compiler_params=_cparams(("arbitrary", "arbitrary", "arbitrary")),
        name="ffn",
    )(h, comb, w1, w3, w2)


MOE_TM = 512


def _row_copy(src_hbm, row, dst_vmem, r, sem):
    return pltpu.make_async_copy(src_hbm.at[pl.ds(row, 1)], dst_vmem.at[pl.ds(r, 1)], sem)


DMA_UNROLL = 8


def _gather_into(idx_ref, src_hbm, dst_vmem, sem, nrows):
    def start(i, c):
        for k in range(DMA_UNROLL):
            r = i * DMA_UNROLL + k
            _row_copy(src_hbm, idx_ref[0, 0, r], dst_vmem, r, sem).start(priority=k % 2)
        return c

    def wait(i, c):
        for k in range(DMA_UNROLL):
            _row_copy(src_hbm, 0, dst_vmem, i * DMA_UNROLL + k, sem).wait()
        return c

    lax.fori_loop(0, nrows // DMA_UNROLL, start, 0)
    lax.fori_loop(0, nrows // DMA_UNROLL, wait, 0)


def _gather_rows_kernel(idx_ref, src_hbm, o_ref, buf_ref, sem):
    _gather_into(idx_ref, src_hbm, buf_ref, sem, buf_ref.shape[0])
    o_ref[...] = buf_ref[...]


def gather_rows(src, idx, *, rows):
    p = idx.shape[0]
    w = src.shape[1]
    return pl.pallas_call(
        _gather_rows_kernel,
        grid=(p // rows,),
        in_specs=[
            pl.BlockSpec((1, 1, rows), lambda j: (j, 0, 0), memory_space=pltpu.SMEM),
            pl.BlockSpec(memory_space=pl.ANY),
        ],
        out_specs=pl.BlockSpec((rows, w), lambda j: (j, 0)),
        out_shape=jax.ShapeDtypeStruct((p, w), src.dtype),
        scratch_shapes=[pltpu.VMEM((rows, w), src.dtype), pltpu.SemaphoreType.DMA(())],
        compiler_params=_cparams(("arbitrary",)),
        name="moe_gather",
    )(idx.reshape(p // rows, 1, rows), src)


def _ffn_sorted_kernel(te_ref, nused_ref, x_ref, w1_ref, w3_ref, w2_ref, o_ref, acc_ref):
    j = pl.program_id(0)
    f = pl.program_id(1)

    @pl.when(j < nused_ref[0])
    def _():
        @pl.when(f == 0)
        def _():
            acc_ref[...] = jnp.zeros(acc_ref.shape, F32)

        h = x_ref[...].astype(BF16)
        a = jnp.dot(h, w1_ref[0], preferred_element_type=F32)
        b = jnp.dot(h, w3_ref[0], preferred_element_type=F32)
        act = a * _sigmoid(a) * b
        acc_ref[...] += jnp.dot(act.astype(BF16), w2_ref[0], preferred_element_type=F32)

        @pl.when(f == pl.num_programs(1) - 1)
        def _():
            o_ref[...] = acc_ref[...]


def ffn_sorted(xs, te, nused, w1, w3, w2, *, tm, tf):
    p, d = xs.shape
    dff = w1.shape[2]
    return pl.pallas_call(
        _ffn_sorted_kernel,
        grid_spec=pltpu.PrefetchScalarGridSpec(
            num_scalar_prefetch=2,
            grid=(p // tm, dff // tf),
            in_specs=[
                pl.BlockSpec((tm, d), lambda j, f, te, nu: (j, 0)),
                pl.BlockSpec((1, d, tf), lambda j, f, te, nu: (te[j], 0, f)),
                pl.BlockSpec((1, d, tf), lambda j, f, te, nu: (te[j], 0, f)),
                pl.BlockSpec((1, tf, d), lambda j, f, te, nu: (te[j], f, 0)),
            ],
            out_specs=pl.BlockSpec((tm, d), lambda j, f, te, nu: (j, 0)),
            scratch_shapes=[pltpu.VMEM((tm, d), F32)],
        ),
        out_shape=jax.ShapeDtypeStruct((p, d), F32),
        compiler_params=_cparams(("arbitrary", "arbitrary")),
        name="ffn_sorted",
    )(te, nused, xs, w1, w3, w2)


def _moe_combine_kernel(d0_ref, d1_ref, ys_hbm, route_ref, x_ref, g_ref, g2_ref, o_ref,
                        y0_ref, y1_ref, sem0, sem1):
    def start(i, c):
        for k in range(DMA_UNROLL):
            r = i * DMA_UNROLL + k
            _row_copy(ys_hbm, d0_ref[0, 0, r], y0_ref, r, sem0).start(priority=0)
            _row_copy(ys_hbm, d1_ref[0, 0, r], y1_ref, r, sem1).start(priority=1)
        return c

    def wait(i, c):
        for k in range(DMA_UNROLL):
            r = i * DMA_UNROLL + k
            _row_copy(ys_hbm, 0, y0_ref, r, sem0).wait()
            _row_copy(ys_hbm, 0, y1_ref, r, sem1).wait()
        return c

    lax.fori_loop(0, TILE // DMA_UNROLL, start, 0)
    lax.fori_loop(0, TILE // DMA_UNROLL, wait, 0)
    route = route_ref[...]
    lane = lax.broadcasted_iota(jnp.int32, route.shape, 1)
    ga = jnp.sum(jnp.where(lane == R_G1, route, 0.0), axis=-1, keepdims=True)
    gb = jnp.sum(jnp.where(lane == R_G2, route, 0.0), axis=-1, keepdims=True)
    y = ga * y0_ref[...] + gb * y1_ref[...]
    o_ref[...] = x_ref[...] + g2_ref[0] * _rms(y, g_ref[...])


def moe_combine_post(x1, ys, dest, route, gpost, mod, geo):
    n, d = x1.shape
    dd = dest.reshape(2, n // TILE, 1, TILE)
    return pl.pallas_call(
        _moe_combine_kernel,
        grid=(geo.ntile,),
        in_specs=[
            pl.BlockSpec((1, 1, TILE), lambda i: (i, 0, 0), memory_space=pltpu.SMEM),
            pl.BlockSpec((1, 1, TILE), lambda i: (i, 0, 0), memory_space=pltpu.SMEM),
            pl.BlockSpec(memory_space=pl.ANY),
            pl.BlockSpec((TILE, LANES), lambda i: (i, 0)),
            pl.BlockSpec((TILE, d), lambda i: (i, 0)),
            pl.BlockSpec((1, d), lambda i: (0, 0)),
            pl.BlockSpec((1, 1, d), lambda i: (geo.mod_row(i) * 6 + 5, 0, 0)),
        ],
        out_specs=pl.BlockSpec((TILE, d), lambda i: (i, 0)),
        out_shape=jax.ShapeDtypeStruct((n, d), F32),
        scratch_shapes=[pltpu.VMEM((TILE, d), F32), pltpu.VMEM((TILE, d), F32),
                        pltpu.SemaphoreType.DMA(()), pltpu.SemaphoreType.DMA(())],
        compiler_params=_cparams(("arbitrary",)),
        name="moe_combine",
    )(dd[0], dd[1], ys, route, x1, gpost.reshape(1, d), mod)


def _route_plan(route, tm):
    n = route.shape[0]
    e_flat = jnp.concatenate([route[:, R_I1], route[:, R_I2]]).astype(jnp.int32)
    onehot = (e_flat[:, None] == jnp.arange(N_EXPERTS, dtype=jnp.int32)[None, :]).astype(jnp.int32)
    csum = jnp.cumsum(onehot, axis=0)
    rank = jnp.take_along_axis(csum, e_flat[:, None], axis=1)[:, 0] - 1
    counts = csum[-1]
    ptiles = (counts + tm - 1) // tm
    tile_end = jnp.cumsum(ptiles)
    gstart = (tile_end - ptiles) * tm
    dest = jnp.take(gstart, e_flat) + rank
    p_max = 2 * n + N_EXPERTS * tm
    tok = jnp.tile(jnp.arange(n, dtype=jnp.int32), 2)
    row_src = jnp.zeros((p_max,), jnp.int32).at[dest].set(tok)
    tiles = jnp.arange(p_max // tm, dtype=jnp.int32)
    te = jnp.minimum(jnp.searchsorted(tile_end, tiles, side="right"), N_EXPERTS - 1).astype(jnp.int32)
    return row_src, dest.astype(jnp.int32).reshape(2, n), te, tile_end[-1:].astype(jnp.int32)


def _ffn_post_kernel(x_ref, y_ref, g_ref, g2_ref, o_ref):
    o_ref[...] = x_ref[...] + g2_ref[0] * _rms(y_ref[...], g_ref[...])


def ffn_post(x1, y, gpost, mod, geo):
    n, d = x1.shape
    return pl.pallas_call(
        _ffn_post_kernel,
        grid=(geo.ntile,),
        in_specs=[
            pl.BlockSpec((TILE, d), lambda i: (i, 0)),
            pl.BlockSpec((TILE, d), lambda i: (i, 0)),
            pl.BlockSpec((1, d), lambda i: (0, 0)),
            pl.BlockSpec((1, 1, d), lambda i: (geo.mod_row(i) * 6 + 5, 0, 0)),
        ],
        out_specs=pl.BlockSpec((TILE, d), lambda i: (i, 0)),
        out_shape=jax.ShapeDtypeStruct((n, d), F32),
        compiler_params=_cparams(("arbitrary",)),
        name="ffn_post",
    )(x1, y, gpost.reshape(1, d), mod)


def _ctx_scan(geo, pre):
    r, kk, v, w0, w1, ka0, ka1, kt0, kt1 = pre
    H, K = N_RWKV_HEADS, RWKV_HEAD
    bl = LANES // H
    nbh = geo.n_ctx // bl

    def to_scan(a):
        return jnp.swapaxes(a.reshape(nbh, geo.t_ctx, LANES, K), 2, 3)

    def from_scan(o):
        return jnp.swapaxes(o, 2, 3).reshape(nbh * geo.t_ctx, bl * H * K)

    def state(sf):
        sf = sf.reshape(nbh, K // SUBLANES, K, SUBLANES, bl, H).transpose(0, 4, 5, 1, 3, 2)
        return sf.reshape(geo.n_ctx, H, K, K)

    kk_s, r_s, v_s = to_scan(kk), to_scan(r), to_scan(v)
    tc = min(32, geo.t_ctx)
    o_f, sf_f = wkv_scan_ctx(to_scan(w0), kk_s, to_scan(ka0), to_scan(kt0), r_s, v_s, tc=tc, rev=False)
    o_b, sf_b = wkv_scan_ctx(to_scan(w1), kk_s, to_scan(ka1), to_scan(kt1), r_s, v_s, tc=tc, rev=True)
    return from_scan(o_f), from_scan(o_b), jnp.stack([state(sf_f), state(sf_b)], axis=1)


def _lat_scan(geo, pre, state_l):
    r, kk, v, w0, w1, ka0, ka1, kt0, kt1 = pre
    H, K = N_RWKV_HEADS, RWKV_HEAD
    kl = LANES // (geo.n_lat * 2 * H)
    nk = K // kl

    def pair(a0, a1):
        a0 = a0.reshape(geo.n_lat, geo.t_lat, H, K)
        a1 = a1.reshape(geo.n_lat, geo.t_lat, H, K)
        a = jnp.stack([a0, a1], axis=2).reshape(geo.n_lat, geo.t_lat, 2, H, kl, nk)
        return a.transpose(1, 5, 4, 0, 2, 3).reshape(geo.t_lat, nk, LANES)

    xs = (pair(w0, w1), pair(kk, kk), pair(ka0, ka1), pair(kt0, kt1), pair(r, r))
    vl = v.reshape(geo.n_lat, geo.t_lat, H, K)
    vl = jnp.stack([vl, vl], axis=2)
    vl = vl.transpose(1, 4, 0, 2, 3).reshape(geo.t_lat, K, LANES // kl)
    v_lat = jnp.tile(vl, (1, 1, kl))
    s0 = state_l.reshape(geo.n_lat, 2, H, K // SUBLANES, SUBLANES, kl, nk)
    s0 = s0.transpose(3, 6, 4, 5, 0, 1, 2).reshape(K // SUBLANES, nk, SUBLANES, LANES)
    oa, ob = wkv_scan_lat(xs, v_lat, s0, tc=min(64, geo.t_lat), kl=kl)

    def direction(o, d):
        o = o.reshape(geo.t_lat, K, kl, geo.n_lat, 2, H)[:, :, :, :, d].sum(axis=2)
        return o.transpose(2, 0, 3, 1).reshape(geo.nl_tok, H * K)

    return direction(oa, 0), direction(ob, 1)


def _rope_tables(geo):
    n_freq = ROPE_DIM // 4
    rows = geo.t_lat // GRID_W
    row = jnp.repeat(jnp.arange(rows, dtype=F32), GRID_W)
    col = jnp.tile(jnp.arange(GRID_W, dtype=F32), rows)
    inv = ROPE_BASE ** (-jnp.arange(n_freq, dtype=F32) / n_freq)
    ang = jnp.concatenate([row[:, None] * inv, col[:, None] * inv], axis=-1)
    cos, sin = jnp.cos(ang), jnp.sin(ang)
    cos32 = jnp.concatenate([cos, cos], axis=-1)
    sin32 = jnp.concatenate([-sin, sin], axis=-1)
    cos_t = jnp.concatenate([jnp.ones((TILE, ROPE_DIM), F32), cos32], axis=0)
    sin_t = jnp.concatenate([jnp.zeros((TILE, ROPE_DIM), F32), sin32], axis=0)
    pad = ((0, 0), (0, LANES - ROPE_DIM))
    return jnp.pad(cos_t, pad), jnp.pad(sin_t, pad)


_DEINT = np.concatenate([np.arange(0, ROPE_DIM, 2), np.arange(1, ROPE_DIM, 2)])
_DEINT_SW = np.concatenate([np.arange(1, ROPE_DIM, 2), np.arange(0, ROPE_DIM, 2)])


def _prep_w_in(w_in, b_in):
    offs = np.cumsum([0, 2 * D_CONV, D_RWKV, D_RWKV, D_RWKV, W_RANK, A_RANK, G_RANK, Q_RANK, KV_RANK,
                      ROPE_DIM, N_BRANCH * D_MODEL])
    o_conv, o_r, o_k, o_v, o_dw, o_da, o_dg, o_cq, o_ckv, o_kr, o_gate, _ = offs
    idx = np.zeros((IN_PAD,), np.int32)
    valid = np.zeros((IN_PAD,), bool)

    def put(dst, src):
        idx[dst:dst + len(src)] = src
        valid[dst:dst + len(src)] = True

    put(C_CONV, np.arange(o_conv, o_conv + 2 * D_CONV))
    for b, c in enumerate((C_G0, C_G1, C_G2)):
        put(c, np.arange(o_gate + b * D_MODEL, o_gate + (b + 1) * D_MODEL))
    put(C_R, np.arange(o_r, o_r + D_RWKV))
    put(C_K, np.arange(o_k, o_k + D_RWKV))
    put(C_V, np.arange(o_v, o_v + D_RWKV))
    put(C_LORA, np.arange(o_dw, o_dw + W_RANK + A_RANK + G_RANK))
    put(C_CQ, np.arange(o_cq, o_cq + Q_RANK))
    put(C_CKV, np.arange(o_ckv, o_ckv + KV_RANK))
    put(C_KR, np.arange(o_kr, o_kr + ROPE_DIM))
    put(C_KR + ROPE_DIM, o_kr + _DEINT)
    put(C_KR + 2 * ROPE_DIM, o_kr + _DEINT_SW)
    w = jnp.where(valid[None, :], jnp.take(w_in, idx, axis=1), 0.0).astype(BF16)
    b = jnp.where(valid, jnp.take(b_in, idx), 0.0)
    return w, b


def _prep_wuq(wuq):
    hd = NOPE_DIM + ROPE_DIM
    nq = N_MLA_HEADS * NOPE_DIM
    ncol = nq + 2 * N_MLA_HEADS * LANES
    idx = np.zeros((ncol,), np.int32)
    valid = np.zeros((ncol,), bool)
    for h in range(N_MLA_HEADS):
        idx[h * NOPE_DIM:(h + 1) * NOPE_DIM] = h * hd + np.arange(NOPE_DIM)
        valid[h * NOPE_DIM:(h + 1) * NOPE_DIM] = True
        for blk, perm in enumerate((_DEINT, _DEINT_SW)):
            c0 = nq + blk * N_MLA_HEADS * LANES + h * LANES
            idx[c0:c0 + ROPE_DIM] = h * hd + NOPE_DIM + perm
            valid[c0:c0 + ROPE_DIM] = True
    return jnp.where(valid[None, :], jnp.take(wuq, idx, axis=1), 0.0).astype(BF16)


def _block_diag_heads(wuv):
    w3 = wuv.reshape(KV_RANK, N_MLA_HEADS, V_DIM)
    eye = jnp.eye(N_MLA_HEADS, dtype=wuv.dtype)
    bd = w3.transpose(1, 0, 2)[:, :, None, :] * eye[:, None, :, None]
    return bd.reshape(N_MLA_HEADS * KV_RANK, N_MLA_HEADS * V_DIM).astype(BF16)


def _attention_keys(geo, ckvn, krr, cache_c, cache_kr):
    nc = geo.nc_tok
    pad = QK_W - KV_RANK - ROPE_DIM

    def keys(c, kr):
        kc = jnp.concatenate([c, kr, jnp.zeros(c.shape[:2] + (pad,), F32)], axis=-1).astype(BF16)
        return kc.transpose(0, 2, 1), c.astype(BF16)

    kr32 = krr[:, :ROPE_DIM]
    kct_c, cv_c = keys(ckvn[:nc].reshape(geo.n_ctx, geo.t_ctx, KV_RANK),
                       kr32[:nc].reshape(geo.n_ctx, geo.t_ctx, ROPE_DIM))
    c_l = jnp.concatenate([ckvn[nc:].reshape(geo.n_lat, geo.t_lat, KV_RANK), cache_c], axis=1)
    kr_l = jnp.concatenate([kr32[nc:].reshape(geo.n_lat, geo.t_lat, ROPE_DIM), cache_kr], axis=1)
    kct_l, cv_l = keys(c_l, kr_l)
    return kct_c, cv_c, kct_l, cv_l


def kernel(x_prompt, x_sample, cache_ckv, cache_krope, state_wkv, c, c_ctx, ada_w, ada_b, norm_mix_pre, norm_mix_post, norm_ffn_pre, norm_ffn_post, w_in, b_in, conv_w, conv_b, conv_ln_g, conv_ln_b, conv_wo, rwkv_mu, rwkv_w0, rwkv_bw, rwkv_a0, rwkv_ba, rwkv_bg, rwkv_xi, rwkv_alpha, rwkv_rho, rwkv_gn_g, rwkv_gn_b, rwkv_wo, mla_q_norm, mla_wuq, mla_kv_norm, mla_wuk, mla_wuv, mla_wo, w_out, ffn_w1, ffn_w3, ffn_w2, moe_router, moe_w1, moe_w3, moe_w2):
    P = dict(rwkv_mu=rwkv_mu, rwkv_w0=rwkv_w0, rwkv_bw=rwkv_bw, rwkv_a0=rwkv_a0, rwkv_ba=rwkv_ba,
             rwkv_bg=rwkv_bg, rwkv_xi=rwkv_xi, rwkv_alpha=rwkv_alpha, rwkv_rho=rwkv_rho)
    n_ctx, t_ctx, d = x_prompt.shape
    n_lat, t_lat, _ = x_sample.shape
    depth = ada_w.shape[0]
    past = cache_ckv.shape[2]
    geo = Geo(n_ctx, t_ctx, n_lat, t_lat, past)
    assert d == D_MODEL and (n_ctx * N_RWKV_HEADS) % LANES == 0 and LANES % (n_lat * 2 * N_RWKV_HEADS) == 0

    x = jnp.concatenate([x_prompt.reshape(-1, d), x_sample.reshape(-1, d)], axis=0)

    n_mod = 1 + n_lat
    n_mod_pad = -(-n_mod // SUBLANES) * SUBLANES
    c_all = jnp.zeros((n_mod_pad, d), F32).at[0].set(c_ctx).at[1:n_mod].set(c)
    head_id = np.arange(D_RWKV) // RWKV_HEAD
    ones_bd = jnp.asarray((head_id[:, None] == head_id[None, :]).astype(np.float32))
    cos_t, sin_t = _rope_tables(geo)

    ckv_out, kr_out, st_out = [], [], []
    for l in range(depth):
        mod = matmul_bias(c_all, ada_w[l].astype(BF16), ada_b[l], tm=n_mod_pad, tn=6 * d // 4,
                          pre="silu", name="ada_mod")
        mod = mod[:n_mod].reshape(n_mod * 6, 1, d)

        h = prenorm(x, norm_mix_pre[l], mod, geo, j_shift=0, j_scale=1)
        w_in_p, b_in_p = _prep_w_in(w_in[l], b_in[l])
        u = matmul_bias(h, w_in_p, b_in_p, tm=math.gcd(1024, geo.ntok), tn=1280, name="in_proj")

        y_conv = conv_branch(u, conv_w[l], conv_b[l], conv_ln_g[l], conv_ln_b[l], conv_wo[l], geo)

        ctx_block = (LANES // N_RWKV_HEADS, geo.ct)
        pre_c = rwkv_pre(u, P, l, geo, ones_bd, 0, geo.nct, seq_block=ctx_block)
        pre_l = rwkv_pre(u, P, l, geo, ones_bd, geo.nct, geo.nlt)
        oc_f, oc_b, sfin = _ctx_scan(geo, pre_c[:9])
        ol_f, ol_b = _lat_scan(geo, pre_l[:9], state_wkv[:, l])
        y_rwkv = rwkv_post(ol_f, ol_b, pre_l[10], pre_l[9], u, rwkv_gn_g[l], rwkv_gn_b[l], ones_bd,
                           rwkv_wo[l], None, geo.nct)
        y_rwkv = rwkv_post(oc_f, oc_b, pre_c[10], pre_c[9], u, rwkv_gn_g[l], rwkv_gn_b[l], ones_bd,
                           rwkv_wo[l], y_rwkv, 0, seq_block=ctx_block)

        wukt = mla_wuk[l].reshape(KV_RANK, N_MLA_HEADS, NOPE_DIM).transpose(1, 2, 0).astype(BF16)
        qf, ckvn, krr = mla_pre(u, cos_t, sin_t, mla_q_norm[l], mla_kv_norm[l],
                                _prep_wuq(mla_wuq[l]), wukt, geo)
        kct_c, cv_c, kct_l, cv_l = _attention_keys(geo, ckvn, krr, cache_ckv[:, l],
                                                   cache_krope[:, l][..., _DEINT])
        wuv_bd = _block_diag_heads(mla_wuv[l])
        wo_b = mla_wo[l].astype(BF16)
        y_mla = attention(qf, kct_l, cv_l, wuv_bd, wo_b, u, None, row0=geo.nc_tok, t_seq=t_lat,
                          tq=TILE, name="attention_lat")
        y_mla = attention(qf, kct_c, cv_c, wuv_bd, wo_b, u, y_mla, row0=0, t_seq=t_ctx,
                          tq=TILE, name="attention_ctx")

        moe = (l % 2 == 1)
        i = l // 2
        router = moe_router[i] if moe else jnp.zeros((d, N_EXPERTS), F32)
        x1, h2, comb = merge(x, y_conv, y_rwkv, y_mla, w_out[l], norm_mix_post[l], norm_ffn_pre[l],
                             mod, router, geo, moe)
        if moe:
            row_src, dest, te, nused = _route_plan(comb, MOE_TM)
            xs = gather_rows(h2, row_src, rows=TILE)
            ys = ffn_sorted(xs, te, nused, moe_w1[i].astype(BF16), moe_w3[i].astype(BF16),
                            moe_w2[i].astype(BF16), tm=MOE_TM, tf=D_FF // 2)
            x = moe_combine_post(x1, ys, dest, comb, norm_ffn_post[l], mod, geo)
        else:
            y = ffn(h2, comb, ffn_w1[i:i + 1].astype(BF16), ffn_w3[i:i + 1].astype(BF16),
                    ffn_w2[i:i + 1].astype(BF16), tm=512, tf=D_FF // 2)
            x = ffn_post(x1, y, norm_ffn_post[l], mod, geo)

        ckv_out.append(ckvn[:geo.nc_tok].reshape(n_ctx, t_ctx, KV_RANK))
        kr_out.append(u[:geo.nc_tok, C_KR:C_KR + ROPE_DIM].reshape(n_ctx, t_ctx, ROPE_DIM))
        st_out.append(sfin)

    y_prompt = x[:geo.nc_tok].reshape(n_ctx, t_ctx, d)
    y_sample = x[geo.nc_tok:].reshape(n_lat, t_lat, d)
    return (y_prompt, y_sample, jnp.stack(ckv_out, axis=1), jnp.stack(kr_out, axis=1),
            jnp.stack(st_out, axis=1))
```

```python
import functools
import math

import numpy as np
import jax
import jax.numpy as jnp
from jax import lax
from jax.experimental import pallas as pl
from jax.experimental.pallas import tpu as pltpu

F32 = jnp.float32
BF16 = jnp.bfloat16

D_MODEL = 1024
GRID_W = 64
D_CONV = 512
CONV_K = 31
D_RWKV = 512
RWKV_HEAD = 64
N_RWKV_HEADS = D_RWKV // RWKV_HEAD
W_RANK = 64
A_RANK = 64
G_RANK = 128
DECAY_SCALE = math.exp(-0.5)
GN_EPS = 64e-5
N_MLA_HEADS = 8
Q_RANK = 256
KV_RANK = 128
NOPE_DIM = 64
ROPE_DIM = 32
V_DIM = 64
ROPE_BASE = 10000.0
ATTN_SCALE = 1.0 / math.sqrt(NOPE_DIM + ROPE_DIM)
N_BRANCH = 3
D_FF = 2816
N_EXPERTS = 8
EPS = 1e-6

LANES = 128
SUBLANES = 8
VMEM_LIMIT_BYTES = 56 * 1024 * 1024

TILE = 256
CONV_HALO = 16
SHIFT_HALO = 8

C_CONV, C_G0, C_G1, C_G2 = 0, 1024, 2048, 3072
C_R, C_K, C_V = 4096, 4608, 5120
C_LORA = 5632
C_CQ = 5888
C_CKV = 6144
C_KR = 6272
IN_PAD = 6400


def _cparams(sem):
    return pltpu.CompilerParams(dimension_semantics=sem, vmem_limit_bytes=VMEM_LIMIT_BYTES)


def _sigmoid(x):
    return jax.nn.sigmoid(x)


def _rms(x, g):
    return (x * lax.rsqrt(jnp.mean(x * x, axis=-1, keepdims=True) + EPS)) * g


class Geo:
    def __init__(self, n_ctx, t_ctx, n_lat, t_lat, past):
        assert t_ctx % TILE == 0 and t_lat % TILE == 0
        self.n_ctx, self.t_ctx, self.n_lat, self.t_lat, self.past = n_ctx, t_ctx, n_lat, t_lat, past
        self.ct = t_ctx // TILE
        self.lt = t_lat // TILE
        self.nct = n_ctx * self.ct
        self.nlt = n_lat * self.lt
        self.ntile = self.nct + self.nlt
        self.nc_tok = n_ctx * t_ctx
        self.nl_tok = n_lat * t_lat
        self.ntok = self.nc_tok + self.nl_tok

    def pos(self, i):
        is_ctx = i < self.nct
        p = jnp.where(is_ctx, i % self.ct, (i - self.nct) % self.lt)
        n = jnp.where(is_ctx, self.ct, self.lt)
        return p, n

    def mod_row(self, i):
        return jnp.where(i < self.nct, 0, 1 + (i - self.nct) // self.lt)

    def rope_blk(self, i):
        return jnp.where(i < self.nct, 0, 1 + (i - self.nct) % self.lt)


def _mm_kernel(x_ref, w_ref, b_ref, o_ref, *, pre):
    x = x_ref[...]
    if pre == "silu":
        x = x.astype(F32)
        x = x * _sigmoid(x)
    acc = jnp.dot(x.astype(BF16), w_ref[...], preferred_element_type=F32)
    o_ref[...] = (acc + b_ref[...]).astype(o_ref.dtype)


def matmul_bias(x, w, b, *, tm, tn, pre=None, out_dtype=F32, name="matmul"):
    m, k = x.shape
    n = w.shape[1]
    assert m % tm == 0 and n % tn == 0
    return pl.pallas_call(
        functools.partial(_mm_kernel, pre=pre),
        grid=(n // tn, m // tm),
        in_specs=[
            pl.BlockSpec((tm, k), lambda j, i: (i, 0)),
            pl.BlockSpec((k, tn), lambda j, i: (0, j)),
            pl.BlockSpec((1, tn), lambda j, i: (0, j)),
        ],
        out_specs=pl.BlockSpec((tm, tn), lambda j, i: (i, j)),
        out_shape=jax.ShapeDtypeStruct((m, n), out_dtype),
        compiler_params=_cparams(("arbitrary", "arbitrary")),
        name=name,
    )(x, w, b.reshape(1, n).astype(F32))


def _prenorm_kernel(x_ref, g_ref, sc_ref, sh_ref, o_ref):
    h = _rms(x_ref[...], g_ref[...]) * (1.0 + sc_ref[0]) + sh_ref[0]
    o_ref[...] = h.astype(o_ref.dtype)


def prenorm(x, g, mod, geo, j_shift, j_scale):
    n, d = x.shape
    return pl.pallas_call(
        _prenorm_kernel,
        grid=(geo.ntile,),
        in_specs=[
            pl.BlockSpec((TILE, d), lambda i: (i, 0)),
            pl.BlockSpec((1, d), lambda i: (0, 0)),
            pl.BlockSpec((1, 1, d), lambda i: (geo.mod_row(i) * 6 + j_scale, 0, 0)),
            pl.BlockSpec((1, 1, d), lambda i: (geo.mod_row(i) * 6 + j_shift, 0, 0)),
        ],
        out_specs=pl.BlockSpec((TILE, d), lambda i: (i, 0)),
        out_shape=jax.ShapeDtypeStruct((n, d), BF16),
        compiler_params=_cparams(("arbitrary",)),
        name="prenorm",
    )(x, g.reshape(1, d), mod, mod)


def _conv_kernel(cur_ref, prev_ref, next_ref, gate_ref, cw_ref, cb_ref, lng_ref, lnb_ref,
                 wo_ref, o_ref, hp_ref, *, geo):
    i = pl.program_id(0)
    p, n = geo.pos(i)
    has_prev = p > 0
    has_next = p < n - 1

    def glu(z):
        return z[:, :D_CONV] * _sigmoid(z[:, D_CONV:])

    hp_ref[0, 0:CONV_HALO, :] = jnp.where(has_prev, glu(prev_ref[...]), 0.0)
    hp_ref[0, CONV_HALO:CONV_HALO + TILE, :] = glu(cur_ref[...])
    hp_ref[0, CONV_HALO + TILE:, :] = jnp.where(has_next, glu(next_ref[...]), 0.0)
    nrow = TILE + 2 * CONV_HALO - SUBLANES
    for b in range(1, SUBLANES):
        hp_ref[b, 0:nrow, :] = hp_ref[0, pl.ds(b, nrow), :]

    off = CONV_HALO - CONV_K // 2
    acc = None
    for j in range(CONV_K):
        q = off + j
        term = hp_ref[q % SUBLANES, pl.ds(q - q % SUBLANES, TILE), :] * cw_ref[j:j + 1, :]
        acc = term if acc is None else acc + term
    h = acc + cb_ref[...]
    mu = jnp.mean(h, axis=-1, keepdims=True)
    hc = h - mu
    var = jnp.mean(hc * hc, axis=-1, keepdims=True)
    h = hc * lax.rsqrt(var + EPS) * lng_ref[...] + lnb_ref[...]
    h = h * _sigmoid(h)
    y = jnp.dot(h.astype(BF16), wo_ref[...], preferred_element_type=F32)
    o_ref[...] = _sigmoid(gate_ref[...]) * y


def conv_branch(u, cw, cb, lng, lnb, wo, geo):
    n = u.shape[0]
    hb = TILE // CONV_HALO
    nhalo = n // CONV_HALO
    cwp = jnp.zeros((32, D_CONV), F32).at[:CONV_K].set(cw)
    return pl.pallas_call(
        functools.partial(_conv_kernel, geo=geo),
        grid=(geo.ntile,),
        in_specs=[
            pl.BlockSpec((TILE, 2 * D_CONV), lambda i: (i, C_CONV // (2 * D_CONV))),
            pl.BlockSpec((CONV_HALO, 2 * D_CONV), lambda i: (jnp.maximum(i * hb - 1, 0), 0)),
            pl.BlockSpec((CONV_HALO, 2 * D_CONV), lambda i: (jnp.minimum((i + 1) * hb, nhalo - 1), 0)),
            pl.BlockSpec((TILE, D_MODEL), lambda i: (i, C_G0 // D_MODEL)),
            pl.BlockSpec((32, D_CONV), lambda i: (0, 0)),
            pl.BlockSpec((1, D_CONV), lambda i: (0, 0)),
            pl.BlockSpec((1, D_CONV), lambda i: (0, 0)),
            pl.BlockSpec((1, D_CONV), lambda i: (0, 0)),
            pl.BlockSpec((D_CONV, D_MODEL), lambda i: (0, 0)),
        ],
        out_specs=pl.BlockSpec((TILE, D_MODEL), lambda i: (i, 0)),
        out_shape=jax.ShapeDtypeStruct((n, D_MODEL), F32),
        scratch_shapes=[pltpu.VMEM((SUBLANES, TILE + 2 * CONV_HALO, D_CONV), F32)],
        compiler_params=_cparams(("arbitrary",)),
        name="conv_branch",
    )(u, u, u, u, cwp, cb.reshape(1, -1), lng.reshape(1, -1), lnb.reshape(1, -1), wo.astype(BF16))


def _seg_sum(x, ones_bd):
    return jnp.dot(x, ones_bd, preferred_element_type=F32, precision=lax.Precision.HIGHEST)


def _rwkv_pre_kernel(r_ref, rp_ref, rn_ref, k_ref, kp_ref, kn_ref, v_ref, vp_ref, vn_ref,
                     lora_ref, mu_ref, w0_ref, bw_ref, a0_ref, ba_ref, bg_ref, xi_ref, al_ref,
                     rho_ref, ones_ref,
                     r_o, kk_o, v_o, w0_o, w1_o, ka0_o, ka1_o, kt0_o, kt1_o, g_o, bonus_o, *, geo, tile0):
    i = pl.program_id(0) + tile0
    p, n = geo.pos(i)
    has_prev = p > 0
    has_next = p < n - 1
    row = lax.broadcasted_iota(jnp.int32, (TILE, D_RWKV), 0)

    def shifted(c_ref, p_ref, n_ref, mu):
        cur = c_ref[...]
        pv = jnp.where(has_prev, p_ref[SHIFT_HALO - 1:SHIFT_HALO, :], 0.0)
        nx = jnp.where(has_next, n_ref[0:1, :], 0.0)
        prev = jnp.where(row == 0, pv, pltpu.roll(cur, 1, axis=0))
        nxt = jnp.where(row == TILE - 1, nx, pltpu.roll(cur, TILE - 1, axis=0))
        return cur + mu * (0.5 * (prev + nxt) - cur)

    r = shifted(r_ref, rp_ref, rn_ref, mu_ref[0:1, :])
    k = shifted(k_ref, kp_ref, kn_ref, mu_ref[1:2, :])
    v = shifted(v_ref, vp_ref, vn_ref, mu_ref[2:3, :])

    lora = lora_ref[...]
    dw = jnp.tanh(lora[:, :W_RANK]).astype(BF16)
    da = lora[:, W_RANK:W_RANK + A_RANK].astype(BF16)
    dg = _sigmoid(lora[:, W_RANK + A_RANK:]).astype(BF16)
    wl = jnp.dot(dw, bw_ref[...], preferred_element_type=F32)
    al = jnp.dot(da, ba_ref[...], preferred_element_type=F32)
    g = jnp.dot(dg, bg_ref[...], preferred_element_type=F32)

    ones_bd = ones_ref[...]
    kx = k * xi_ref[...]
    kk = kx * lax.rsqrt(_seg_sum(kx * kx, ones_bd) + EPS)
    alpha = al_ref[...]
    rho = rho_ref[...]
    bonus = jnp.zeros((TILE, D_RWKV), F32)
    for d, (w_o, ka_o, kt_o) in enumerate(((w0_o, ka0_o, kt0_o), (w1_o, ka1_o, kt1_o))):
        sl = slice(d * D_RWKV, (d + 1) * D_RWKV)
        w = jnp.exp(-DECAY_SCALE * _sigmoid(w0_ref[d:d + 1, :] + wl[:, sl]))
        a = _sigmoid(a0_ref[d:d + 1, :] + al[:, sl])
        kt = k * (1.0 + (a - 1.0) * alpha)
        w_o[...] = w
        ka_o[...] = kk * a
        kt_o[...] = kt
        bonus = bonus + _seg_sum(r * kt * rho, ones_bd) * v
    r_o[...] = r
    kk_o[...] = kk
    v_o[...] = v
    g_o[...] = g
    bonus_o[...] = bonus


def rwkv_pre(u, P, l, geo, ones_bd, tile0, ntiles, seq_block=None):
    hb = TILE // SHIFT_HALO
    nhalo = u.shape[0] // SHIFT_HALO

    def trio(c0):
        cb = c0 // D_RWKV
        return [
            pl.BlockSpec((TILE, D_RWKV), lambda i: (i + tile0, cb)),
            pl.BlockSpec((SHIFT_HALO, D_RWKV), lambda i: (jnp.maximum((i + tile0) * hb - 1, 0), cb)),
            pl.BlockSpec((SHIFT_HALO, D_RWKV), lambda i: (jnp.minimum((i + tile0 + 1) * hb, nhalo - 1), cb)),
        ]

    def full(shape):
        return pl.BlockSpec(shape, lambda i: tuple(0 for _ in shape))

    bw = jnp.concatenate([P["rwkv_bw"][l, 0], P["rwkv_bw"][l, 1]], axis=1).astype(BF16)
    ba = jnp.concatenate([P["rwkv_ba"][l, 0], P["rwkv_ba"][l, 1]], axis=1).astype(BF16)
    out = jax.ShapeDtypeStruct((ntiles * TILE, D_RWKV), F32)
    tok_spec = pl.BlockSpec((TILE, D_RWKV), lambda i: (i, 0))
    if seq_block is None:
        scan_spec, scan_out = tok_spec, out
    else:
        bl, ts = seq_block
        scan_out = jax.ShapeDtypeStruct((ntiles * TILE // bl, bl * D_RWKV), F32)
        scan_spec = pl.BlockSpec((TILE, D_RWKV),
                                 lambda i: ((i // ts // bl) * ts + i % ts, (i // ts) % bl))
    return pl.pallas_call(
        functools.partial(_rwkv_pre_kernel, geo=geo, tile0=tile0),
        grid=(ntiles,),
        in_specs=trio(C_R) + trio(C_K) + trio(C_V) + [
            pl.BlockSpec((TILE, 256), lambda i: (i + tile0, C_LORA // 256)),
            full((3, D_RWKV)), full((2, D_RWKV)), full((W_RANK, 2 * D_RWKV)),
            full((2, D_RWKV)), full((A_RANK, 2 * D_RWKV)), full((G_RANK, D_RWKV)),
            full((1, D_RWKV)), full((1, D_RWKV)), full((1, D_RWKV)), full((D_RWKV, D_RWKV)),
        ],
        out_specs=[scan_spec] * 9 + [tok_spec] * 2,
        out_shape=[scan_out] * 9 + [out] * 2,
        compiler_params=_cparams(("arbitrary",)),
        name="rwkv_pre",
    )(u, u, u, u, u, u, u, u, u, u,
      P["rwkv_mu"][l], P["rwkv_w0"][l], bw, P["rwkv_a0"][l], ba, P["rwkv_bg"][l].astype(BF16),
      P["rwkv_xi"][l].reshape(1, -1), P["rwkv_alpha"][l].reshape(1, -1),
      P["rwkv_rho"][l].reshape(1, -1), ones_bd)


N_VBLK = RWKV_HEAD // SUBLANES


def _scan_steps(xrow, vload, ostore, s_ref, tidx, *, tc, nk, kl, ngroups=2, unroll=1):
    gsz = N_VBLK // ngroups
    groups = tuple(tuple(range(g * gsz, (g + 1) * gsz)) for g in range(ngroups))

    def allred(a):
        out = a
        for j in range(1, kl):
            out = out + pltpu.roll(a, j * (LANES // kl), axis=1)
        return out

    def first_sa(grp, t):
        acc = [None] * len(grp)
        for kh in range(nk):
            kkb = xrow(t, 1, kh)
            for j, vb in enumerate(grp):
                pr = s_ref[vb, kh] * kkb
                acc[j] = pr if acc[j] is None else acc[j] + pr
        return tuple(allred(a) for a in acc)

    def fused(grp, s, t, t_next, sa):
        vv = [vload(t, vb) for vb in grp]
        oacc = [None] * len(grp)
        acc = [None] * len(grp)
        for kh in range(nk):
            wb, kab, ktb, rb = xrow(t, 0, kh), xrow(t, 2, kh), xrow(t, 3, kh), xrow(t, 4, kh)
            kkn = xrow(t_next, 1, kh)
            for j, vb in enumerate(grp):
                sn = s_ref[vb, kh] * wb - sa[j] * kab + vv[j] * ktb
                s_ref[vb, kh] = sn
                po = sn * rb
                pa = sn * kkn
                oacc[j] = po if oacc[j] is None else oacc[j] + po
                acc[j] = pa if acc[j] is None else acc[j] + pa
        for j, vb in enumerate(grp):
            ostore(s, t, vb, oacc[j])
        return tuple(allred(a) for a in acc)

    t0 = tidx(0)
    carry0 = tuple(first_sa(g, t0) for g in groups)

    def step(s, carry):
        t = tidx(s)
        t_next = tidx(jnp.minimum(s + 1, tc - 1))
        return tuple(fused(g, s, t, t_next, carry[i]) for i, g in enumerate(groups))

    lax.fori_loop(0, tc, step, carry0, unroll=unroll)


def _vrows(vb):
    return pl.ds(vb * SUBLANES, SUBLANES)


def _scan_ctx_kernel(w_ref, kk_ref, ka_ref, kt_ref, r_ref, v_ref, o_ref, sfin_ref, s_ref, *, tc, nk, rev):
    c = pl.program_id(1)
    x_refs = (w_ref, kk_ref, ka_ref, kt_ref, r_ref)

    @pl.when(c == 0)
    def _():
        s_ref[...] = jnp.zeros(s_ref.shape, F32)

    def xrow(t, a, kh):
        return jnp.broadcast_to(x_refs[a][0, t, pl.ds(kh, 1), :], (SUBLANES, LANES))

    def vload(t, vb):
        return v_ref[0, t, _vrows(vb), :]

    def ostore(s, t, vb, val):
        o_ref[0, t, _vrows(vb), :] = val

    def tidx(s):
        return tc - 1 - s if rev else s

    _scan_steps(xrow, vload, ostore, s_ref, tidx, tc=tc, nk=nk, kl=1)

    @pl.when(c == pl.num_programs(1) - 1)
    def _():
        sfin_ref[0] = s_ref[...]


def wkv_scan_ctx(w, kk, ka, kt, r, v, *, tc, rev):
    ngb, t_len, nk, _ = w.shape
    nch = t_len // tc
    blk = pl.BlockSpec((1, tc, nk, LANES), lambda g, c: (g, nch - 1 - c if rev else c, 0, 0))
    st = pl.BlockSpec((1, N_VBLK, nk, SUBLANES, LANES), lambda g, c: (g, 0, 0, 0, 0))
    return pl.pallas_call(
        functools.partial(_scan_ctx_kernel, tc=tc, nk=nk, rev=rev),
        grid=(ngb, nch),
        in_specs=[blk] * 6,
        out_specs=[blk, st],
        out_shape=[
            jax.ShapeDtypeStruct((ngb, t_len, RWKV_HEAD, LANES), F32),
            jax.ShapeDtypeStruct((ngb, N_VBLK, nk, SUBLANES, LANES), F32),
        ],
        scratch_shapes=[pltpu.VMEM((N_VBLK, nk, SUBLANES, LANES), F32)],
        compiler_params=_cparams(("arbitrary", "arbitrary")),
        name="wkv_scan_ctx",
    )(w, kk, ka, kt, r, v)


def _scan_lat_kernel(*refs, tc, nk, kl):
    xa_refs, xb_refs = refs[0:5], refs[5:10]
    va_ref, vb_ref, s0_ref, oa_ref, ob_ref, s_ref, xm_ref, vm_ref = refs[10:]
    c = pl.program_id(0)

    @pl.when(c == 0)
    def _():
        s_ref[...] = s0_ref[...]

    def is_bwd(shape):
        lane = lax.broadcasted_iota(jnp.int32, shape, len(shape) - 1)
        return (lane // N_RWKV_HEADS) % 2 == 1

    mx = is_bwd((nk, LANES))
    mv = is_bwd((RWKV_HEAD, LANES))

    def merge(s, carry):
        for a in range(5):
            xm_ref[a, s] = jnp.where(mx, xb_refs[a][tc - 1 - s], xa_refs[a][s])
        vm_ref[s] = jnp.where(mv, vb_ref[tc - 1 - s], va_ref[s])
        return carry

    lax.fori_loop(0, tc, merge, 0)

    def xrow(t, a, kh):
        return jnp.broadcast_to(xm_ref[a, t, pl.ds(kh, 1), :], (SUBLANES, LANES))

    def vload(t, vb):
        return vm_ref[t, _vrows(vb), :]

    def ostore(s, t, vb, val):
        oa_ref[s, _vrows(vb), :] = val
        ob_ref[tc - 1 - s, _vrows(vb), :] = val

    _scan_steps(xrow, vload, ostore, s_ref, lambda s: s, tc=tc, nk=nk, kl=kl, ngroups=2, unroll=2)


def wkv_scan_lat(xs, v, s0, *, tc, kl):
    t_len, nk, _ = xs[0].shape
    nch = t_len // tc
    o_sds = jax.ShapeDtypeStruct((t_len, RWKV_HEAD, LANES), F32)
    xa = pl.BlockSpec((tc, nk, LANES), lambda c: (c, 0, 0))
    xb = pl.BlockSpec((tc, nk, LANES), lambda c: (nch - 1 - c, 0, 0))
    va = pl.BlockSpec((tc, RWKV_HEAD, LANES), lambda c: (c, 0, 0))
    vb = pl.BlockSpec((tc, RWKV_HEAD, LANES), lambda c: (nch - 1 - c, 0, 0))
    return pl.pallas_call(
        functools.partial(_scan_lat_kernel, tc=tc, nk=nk, kl=kl),
        grid=(nch,),
        in_specs=[xa] * 5 + [xb] * 5 + [va, vb,
                                        pl.BlockSpec((N_VBLK, nk, SUBLANES, LANES), lambda c: (0, 0, 0, 0))],
        out_specs=[va, vb],
        out_shape=[o_sds, o_sds],
        scratch_shapes=[pltpu.VMEM((N_VBLK, nk, SUBLANES, LANES), F32),
                        pltpu.VMEM((5, tc, nk, LANES), F32),
                        pltpu.VMEM((tc, RWKV_HEAD, LANES), F32)],
        compiler_params=_cparams(("arbitrary",)),
        name="wkv_scan_lat",
    )(*xs, *xs, v, v, s0)


def _rwkv_post_kernel(of_ref, ob_ref, bonus_ref, g_ref, gate_ref, gng_ref, gnb_ref, ones_ref,
                      wo_ref, *rest):
    y_ref = rest[-1]
    o = of_ref[...] + ob_ref[...]
    ones_bd = ones_ref[...]
    mean = _seg_sum(o, ones_bd) * (1.0 / RWKV_HEAD)
    oc = o - mean
    var = _seg_sum(oc * oc, ones_bd) * (1.0 / RWKV_HEAD)
    gn = oc * lax.rsqrt(var + GN_EPS) * gng_ref[...] + gnb_ref[...]
    y = (gn + bonus_ref[...]) * g_ref[...]
    y = jnp.dot(y.astype(BF16), wo_ref[...], preferred_element_type=F32)
    y_ref[...] = _sigmoid(gate_ref[...]) * y


def rwkv_post(o_f, o_b, bonus, g, u, gng, gnb, ones_bd, wo, y_prev, tile0, seq_block=None):
    ntiles = bonus.shape[0] // TILE

    def full(shape):
        return pl.BlockSpec(shape, lambda i: tuple(0 for _ in shape))

    part = pl.BlockSpec((TILE, D_RWKV), lambda i: (i, 0))
    if seq_block is None:
        opart = part
    else:
        bl, ts = seq_block
        opart = pl.BlockSpec((TILE, D_RWKV), lambda i: ((i // ts // bl) * ts + i % ts, (i // ts) % bl))
    in_specs = [opart, opart, part, part,
                pl.BlockSpec((TILE, D_MODEL), lambda i: (i + tile0, C_G1 // D_MODEL)),
                full((1, D_RWKV)), full((1, D_RWKV)), full((D_RWKV, D_RWKV)), full((D_RWKV, D_MODEL))]
    args = [o_f, o_b, bonus, g, u, gng.reshape(1, -1), gnb.reshape(1, -1), ones_bd, wo.astype(BF16)]
    aliases = {}
    if y_prev is not None:
        in_specs.append(pl.BlockSpec(memory_space=pl.ANY))
        args.append(y_prev)
        aliases = {len(args) - 1: 0}
    return pl.pallas_call(
        _rwkv_post_kernel,
        grid=(ntiles,),
        in_specs=in_specs,
        out_specs=pl.BlockSpec((TILE, D_MODEL), lambda i: (i + tile0, 0)),
        out_shape=jax.ShapeDtypeStruct((u.shape[0], D_MODEL), F32),
        input_output_aliases=aliases,
        compiler_params=_cparams(("arbitrary",)),
        name="rwkv_post",
    )(*args)


QK_W = 2 * LANES
Q_SCALE = ATTN_SCALE * math.log2(math.e)


def _mla_pre_kernel(cq_ref, ckv_ref, kr_ref, cos_ref, sin_ref, qg_ref, kvg_ref, wuq_ref, wukt_ref,
                    qf_o, ckvn_o, krr_o):
    nq = N_MLA_HEADS * NOPE_DIM
    nrp = N_MLA_HEADS * LANES
    cq = _rms(cq_ref[...], qg_ref[...])
    q = jnp.dot(cq.astype(BF16), wuq_ref[...], preferred_element_type=F32)
    cos = cos_ref[...]
    sin = sin_ref[...]
    for h in range(N_MLA_HEADS):
        qn = q[:, h * NOPE_DIM:(h + 1) * NOPE_DIM].astype(BF16)
        qa = jnp.dot(qn, wukt_ref[h], preferred_element_type=F32)
        qr = (q[:, nq + h * LANES:nq + (h + 1) * LANES] * cos
              + q[:, nq + nrp + h * LANES:nq + nrp + (h + 1) * LANES] * sin)
        qf_o[:, h * QK_W:h * QK_W + LANES] = (qa * Q_SCALE).astype(qf_o.dtype)
        qf_o[:, h * QK_W + LANES:(h + 1) * QK_W] = (qr * Q_SCALE).astype(qf_o.dtype)
    ckvn_o[...] = _rms(ckv_ref[...], kvg_ref[...])
    kr = kr_ref[...]
    krr = kr[:, ROPE_DIM:2 * ROPE_DIM] * cos[:, :ROPE_DIM] + kr[:, 2 * ROPE_DIM:3 * ROPE_DIM] * sin[:, :ROPE_DIM]
    krr_o[...] = jnp.concatenate([krr, jnp.zeros((TILE, LANES - ROPE_DIM), F32)], axis=1)


def mla_pre(u, cos_t, sin_t, qg, kvg, wuq_p, wukt, geo):
    n = u.shape[0]

    def full(shape):
        return pl.BlockSpec(shape, lambda i: tuple(0 for _ in shape))

    return pl.pallas_call(
        _mla_pre_kernel,
        grid=(geo.ntile,),
        in_specs=[
            pl.BlockSpec((TILE, Q_RANK), lambda i: (i, C_CQ // Q_RANK)),
            pl.BlockSpec((TILE, KV_RANK), lambda i: (i, C_CKV // KV_RANK)),
            pl.BlockSpec((TILE, LANES), lambda i: (i, C_KR // LANES)),
            pl.BlockSpec((TILE, LANES), lambda i: (geo.rope_blk(i), 0)),
            pl.BlockSpec((TILE, LANES), lambda i: (geo.rope_blk(i), 0)),
            full((1, Q_RANK)), full((1, KV_RANK)),
            full(wuq_p.shape), full(wukt.shape),
        ],
        out_specs=[
            pl.BlockSpec((TILE, N_MLA_HEADS * QK_W), lambda i: (i, 0)),
            pl.BlockSpec((TILE, KV_RANK), lambda i: (i, 0)),
            pl.BlockSpec((TILE, LANES), lambda i: (i, 0)),
        ],
        out_shape=[
            jax.ShapeDtypeStruct((n, N_MLA_HEADS * QK_W), BF16),
            jax.ShapeDtypeStruct((n, KV_RANK), F32),
            jax.ShapeDtypeStruct((n, LANES), F32),
        ],
        compiler_params=_cparams(("arbitrary",)),
        name="mla_pre",
    )(u, u, u, cos_t, sin_t, qg.reshape(1, -1), kvg.reshape(1, -1), wuq_p, wukt)


ATT_KEY_CHUNK = 512
ATT_HEADS_PER_STEP = 4


def _attn_kernel(q_ref, kt_ref, c_ref, wuv_ref, wo_ref, gate_ref, *rest, ck):
    y_ref, pc_ref = rest[-2:]
    hg = pl.program_id(2)
    tq = q_ref.shape[0]
    t_k = kt_ref.shape[2]
    hp = ATT_HEADS_PER_STEP
    q = jnp.concatenate([q_ref[:, j * QK_W:(j + 1) * QK_W] for j in range(hp)], axis=0)
    m = jnp.full((hp * tq, 1), -jnp.inf, F32)
    l = jnp.zeros((hp * tq, 1), F32)
    acc = jnp.zeros((hp * tq, KV_RANK), F32)
    for c0 in range(0, t_k, ck):
        s = jnp.dot(q, kt_ref[0, :, c0:c0 + ck], preferred_element_type=F32)
        m_new = jnp.maximum(m, jnp.max(s, axis=-1, keepdims=True))
        alpha = jnp.exp2(m - m_new)
        p = jnp.exp2(s - m_new)
        l = alpha * l + jnp.sum(p, axis=-1, keepdims=True)
        acc = alpha * acc + jnp.dot(p.astype(BF16), c_ref[0, c0:c0 + ck, :], preferred_element_type=F32)
        m = m_new
    pc = (acc / l).astype(BF16)
    for j in range(hp):
        pc_ref[hg * hp + j] = pc[j * tq:(j + 1) * tq]

    @pl.when(hg == pl.num_programs(2) - 1)
    def _():
        pcs = jnp.concatenate([pc_ref[i] for i in range(N_MLA_HEADS)], axis=1)
        oh = jnp.dot(pcs, wuv_ref[...], preferred_element_type=F32)
        y = jnp.dot(oh.astype(BF16), wo_ref[...], preferred_element_type=F32)
        y_ref[...] = _sigmoid(gate_ref[...]) * y


def attention(qf, kct, cv, wuv_bd, wo, u, y_prev, *, row0, t_seq, tq, name):
    n = qf.shape[0]
    nseq, _, t_k = kct.shape
    qt = t_seq // tq
    rb0 = row0 // tq
    ck = math.gcd(ATT_KEY_CHUNK, t_k)

    def rows(w, col):
        return pl.BlockSpec((tq, w), lambda s, i, h: (rb0 + s * qt + i, col(h)))

    in_specs = [
        rows(ATT_HEADS_PER_STEP * QK_W, lambda h: h),
        pl.BlockSpec((1, QK_W, t_k), lambda s, i, h: (s, 0, 0)),
        pl.BlockSpec((1, t_k, KV_RANK), lambda s, i, h: (s, 0, 0)),
        pl.BlockSpec(wuv_bd.shape, lambda s, i, h: (0, 0)),
        pl.BlockSpec(wo.shape, lambda s, i, h: (0, 0)),
        rows(D_MODEL, lambda h: C_G2 // D_MODEL),
    ]
    args = [qf, kct, cv, wuv_bd, wo, u]
    aliases = {}
    if y_prev is not None:
        in_specs.append(pl.BlockSpec(memory_space=pl.ANY))
        args.append(y_prev)
        aliases = {len(args) - 1: 0}
    return pl.pallas_call(
        functools.partial(_attn_kernel, ck=ck),
        grid=(nseq, qt, N_MLA_HEADS // ATT_HEADS_PER_STEP),
        in_specs=in_specs,
        out_specs=rows(D_MODEL, lambda h: 0),
        out_shape=jax.ShapeDtypeStruct((n, D_MODEL), F32),
        scratch_shapes=[pltpu.VMEM((N_MLA_HEADS, tq, KV_RANK), BF16)],
        input_output_aliases=aliases,
        compiler_params=_cparams(("arbitrary", "arbitrary", "arbitrary")),
        name=name,
    )(*args)


R_I1, R_I2, R_G1, R_G2 = 0, 1, 2, 3


def _merge_kernel(x_ref, yc_ref, yr_ref, ym_ref, wout_ref, gpost_ref, g1_ref, gpre_ref, sc_ref,
                  sh_ref, router_ref, x1_o, h2_o, comb_o, *, moe):
    m = yc_ref[...] + yr_ref[...] + ym_ref[...]
    y = jnp.dot(m.astype(BF16), wout_ref[...], preferred_element_type=F32)
    x1 = x_ref[...] + g1_ref[0] * _rms(y, gpost_ref[...])
    x1_o[...] = x1
    h2 = _rms(x1, gpre_ref[...]) * (1.0 + sc_ref[0]) + sh_ref[0]
    h2_o[...] = h2.astype(h2_o.dtype)
    if moe:
        logits = jnp.dot(h2, router_ref[...], preferred_element_type=F32,
                         precision=lax.Precision.HIGHEST)
        lane = lax.broadcasted_iota(jnp.int32, logits.shape, 1)
        neg = jnp.float32(-jnp.inf)
        logits = jnp.where(lane < N_EXPERTS, logits, neg)
        m1 = jnp.max(logits, axis=-1, keepdims=True)
        i1 = jnp.min(jnp.where(logits == m1, lane, LANES), axis=-1, keepdims=True)
        rest = jnp.where(lane == i1, neg, logits)
        m2 = jnp.max(rest, axis=-1, keepdims=True)
        i2 = jnp.min(jnp.where(rest == m2, lane, LANES), axis=-1, keepdims=True)
        e2 = jnp.exp(m2 - m1)
        den = 1.0 + e2
        cols = ((R_I1, i1.astype(F32)), (R_I2, i2.astype(F32)), (R_G1, 1.0 / den), (R_G2, e2 / den))
        route = jnp.zeros(comb_o.shape, F32)
        for col, val in cols:
            route = jnp.where(lane == col, val, route)
        comb_o[...] = route
    else:
        comb_o[...] = jnp.ones(comb_o.shape, F32)


def merge(x, yc, yr, ym, wout, gpost, gpre, mod, router, geo, moe):
    n, d = x.shape

    def full(shape):
        return pl.BlockSpec(shape, lambda i: tuple(0 for _ in shape))

    def rows():
        return pl.BlockSpec((TILE, d), lambda i: (i, 0))

    def modspec(j):
        return pl.BlockSpec((1, 1, d), lambda i: (geo.mod_row(i) * 6 + j, 0, 0))

    router_p = jnp.zeros((d, LANES), F32).at[:, :N_EXPERTS].set(router)
    return pl.pallas_call(
        functools.partial(_merge_kernel, moe=moe),
        grid=(geo.ntile,),
        in_specs=[rows(), rows(), rows(), rows(), full((d, d)), full((1, d)), modspec(2),
                  full((1, d)), modspec(4), modspec(3), full((d, LANES))],
        out_specs=[rows(), rows(), pl.BlockSpec((TILE, LANES), lambda i: (i, 0))],
        out_shape=[jax.ShapeDtypeStruct((n, d), F32), jax.ShapeDtypeStruct((n, d), F32 if moe else BF16),
                   jax.ShapeDtypeStruct((n, LANES), F32)],
        compiler_params=_cparams(("arbitrary",)),
        name="merge",
    )(x, yc, yr, ym, wout.astype(BF16), gpost.reshape(1, d), mod, gpre.reshape(1, d), mod, mod,
      router_p)


def _ffn_kernel(h_ref, comb_ref, w1_ref, w3_ref, w2_ref, o_ref, acc_ref):
    e = pl.program_id(1)
    f = pl.program_id(2)

    @pl.when((e == 0) & (f == 0))
    def _():
        acc_ref[...] = jnp.zeros(acc_ref.shape, F32)

    h = h_ref[...]
    a = jnp.dot(h, w1_ref[0], preferred_element_type=F32)
    b = jnp.dot(h, w3_ref[0], preferred_element_type=F32)
    comb = comb_ref[...]
    lane = lax.broadcasted_iota(jnp.int32, comb.shape, 1)
    ce = jnp.sum(jnp.where(lane == e, comb, 0.0), axis=-1, keepdims=True)
    act = (a * _sigmoid(a) * b) * ce
    acc_ref[...] += jnp.dot(act.astype(BF16), w2_ref[0], preferred_element_type=F32)

    @pl.when((e == pl.num_programs(1) - 1) & (f == pl.num_programs(2) - 1))
    def _():
        o_ref[...] = acc_ref[...]


def ffn(h, comb, w1, w3, w2, *, tm, tf):
    n, d = h.shape
    ne, _, dff = w1.shape
    return pl.pallas_call(
        _ffn_kernel,
        grid=(n // tm, ne, dff // tf),
        in_specs=[
            pl.BlockSpec((tm, d), lambda i, e, f: (i, 0)),
            pl.BlockSpec((tm, LANES), lambda i, e, f: (i, 0)),
            pl.BlockSpec((1, d, tf), lambda i, e, f: (e, 0, f)),
            pl.BlockSpec((1, d, tf), lambda i, e, f: (e, 0, f)),
            pl.BlockSpec((1, tf, d), lambda i, e, f: (e, f, 0)),
        ],
        out_specs=pl.BlockSpec((tm, d), lambda i, e, f: (i, 0)),
        out_shape=jax.ShapeDtypeStruct((n, d), F32),
        scratch_shapes=[pltpu.VMEM((tm, d), F32)],
        compiler_params=_cparams(("arbitrary", "arbitrary", "arbitrary")),
        name="ffn",
    )(h, comb, w1, w3, w2)


MOE_TM = 512


def _row_copy(src_hbm, row, dst_vmem, r, sem):
    return pltpu.make_async_copy(src_hbm.at[pl.ds(row, 1)], dst_vmem.at[pl.ds(r, 1)], sem)


DMA_UNROLL = 8


def _ffn_sorted_kernel(te_ref, nused_ref, idx_ref, nxt_ref, h_hbm, w1_ref, w3_ref, w2_ref, o_ref,
                       acc_ref, xbuf_ref, sems):
    j = pl.program_id(0)
    f = pl.program_id(1)
    nj = pl.num_programs(0)
    tm = acc_ref.shape[0]
    slot = j % 2

    def issue(ids_ref, s):
        def body(i, c):
            for k in range(DMA_UNROLL):
                r = i * DMA_UNROLL + k
                _row_copy(h_hbm, ids_ref[0, 0, r], xbuf_ref.at[s], r, sems.at[s]).start(priority=k % 2)
            return c
        lax.fori_loop(0, tm // DMA_UNROLL, body, 0)

    def wait(s):
        def body(i, c):
            for k in range(DMA_UNROLL):
                _row_copy(h_hbm, 0, xbuf_ref.at[s], i * DMA_UNROLL + k, sems.at[s]).wait()
            return c
        lax.fori_loop(0, tm // DMA_UNROLL, body, 0)

    @pl.when(f == 0)
    def _():
        @pl.when(j == 0)
        def _():
            issue(idx_ref, 0)

        @pl.when(j + 1 < nj)
        def _():
            issue(nxt_ref, 1 - slot)

        wait(slot)

    @pl.when(j < nused_ref[0])
    def _():
        @pl.when(f == 0)
        def _():
            acc_ref[...] = jnp.zeros(acc_ref.shape, F32)

        h = xbuf_ref[slot].astype(BF16)
        a = jnp.dot(h, w1_ref[0], preferred_element_type=F32)
        b = jnp.dot(h, w3_ref[0], preferred_element_type=F32)
        act = a * _sigmoid(a) * b
        acc_ref[...] += jnp.dot(act.astype(BF16), w2_ref[0], preferred_element_type=F32)

        @pl.when(f == pl.num_programs(1) - 1)
        def _():
            o_ref[...] = acc_ref[...]


def ffn_sorted(h, row_src, te, nused, w1, w3, w2, *, tm, tf):
    p = row_src.shape[0]
    d = h.shape[1]
    dff = w1.shape[2]
    ntile = p // tm
    ids = row_src.reshape(ntile, 1, tm)
    return pl.pallas_call(
        _ffn_sorted_kernel,
        grid_spec=pltpu.PrefetchScalarGridSpec(
            num_scalar_prefetch=2,
            grid=(ntile, dff // tf),
            in_specs=[
                pl.BlockSpec((1, 1, tm), lambda j, f, te, nu: (j, 0, 0), memory_space=pltpu.SMEM),
                pl.BlockSpec((1, 1, tm), lambda j, f, te, nu: (jnp.minimum(j + 1, ntile - 1), 0, 0),
                             memory_space=pltpu.SMEM),
                pl.BlockSpec(memory_space=pl.ANY),
                pl.BlockSpec((1, d, tf), lambda j, f, te, nu: (te[j], 0, f)),
                pl.BlockSpec((1, d, tf), lambda j, f, te, nu: (te[j], 0, f)),
                pl.BlockSpec((1, tf, d), lambda j, f, te, nu: (te[j], f, 0)),
            ],
            out_specs=pl.BlockSpec((tm, d), lambda j, f, te, nu: (j, 0)),
            scratch_shapes=[pltpu.VMEM((tm, d), F32), pltpu.VMEM((2, tm, d), h.dtype),
                            pltpu.SemaphoreType.DMA((2,))],
        ),
        out_shape=jax.ShapeDtypeStruct((p, d), F32),
        compiler_params=_cparams(("arbitrary", "arbitrary")),
        name="ffn_sorted",
    )(te, nused, ids, ids, h, w1, w3, w2)


def _moe_combine_kernel(d0_ref, d1_ref, ys_hbm, route_ref, x_ref, g_ref, g2_ref, o_ref,
                        y0_ref, y1_ref, sem0, sem1):
    def start(i, c):
        for k in range(DMA_UNROLL):
            r = i * DMA_UNROLL + k
            _row_copy(ys_hbm, d0_ref[0, 0, r], y0_ref, r, sem0).start(priority=0)
            _row_copy(ys_hbm, d1_ref[0, 0, r], y1_ref, r, sem1).start(priority=1)
        return c

    def wait(i, c):
        for k in range(DMA_UNROLL):
            r = i * DMA_UNROLL + k
            _row_copy(ys_hbm, 0, y0_ref, r, sem0).wait()
            _row_copy(ys_hbm, 0, y1_ref, r, sem1).wait()
        return c

    lax.fori_loop(0, TILE // DMA_UNROLL, start, 0)
    lax.fori_loop(0, TILE // DMA_UNROLL, wait, 0)
    route = route_ref[...]
    lane = lax.broadcasted_iota(jnp.int32, route.shape, 1)
    ga = jnp.sum(jnp.where(lane == R_G1, route, 0.0), axis=-1, keepdims=True)
    gb = jnp.sum(jnp.where(lane == R_G2, route, 0.0), axis=-1, keepdims=True)
    y = ga * y0_ref[...] + gb * y1_ref[...]
    o_ref[...] = x_ref[...] + g2_ref[0] * _rms(y, g_ref[...])


def moe_combine_post(x1, ys, dest, route, gpost, mod, geo):
    n, d = x1.shape
    dd = dest.reshape(2, n // TILE, 1, TILE)
    return pl.pallas_call(
        _moe_combine_kernel,
        grid=(geo.ntile,),
        in_specs=[
            pl.BlockSpec((1, 1, TILE), lambda i: (i, 0, 0), memory_space=pltpu.SMEM),
            pl.BlockSpec((1, 1, TILE), lambda i: (i, 0, 0), memory_space=pltpu.SMEM),
            pl.BlockSpec(memory_space=pl.ANY),
            pl.BlockSpec((TILE, LANES), lambda i: (i, 0)),
            pl.BlockSpec((TILE, d), lambda i: (i, 0)),
            pl.BlockSpec((1, d), lambda i: (0, 0)),
            pl.BlockSpec((1, 1, d), lambda i: (geo.mod_row(i) * 6 + 5, 0, 0)),
        ],
        out_specs=pl.BlockSpec((TILE, d), lambda i: (i, 0)),
        out_shape=jax.ShapeDtypeStruct((n, d), F32),
        scratch_shapes=[pltpu.VMEM((TILE, d), F32), pltpu.VMEM((TILE, d), F32),
                        pltpu.SemaphoreType.DMA(()), pltpu.SemaphoreType.DMA(())],
        compiler_params=_cparams(("arbitrary",)),
        name="moe_combine",
    )(dd[0], dd[1], ys, route, x1, gpost.reshape(1, d), mod)


def _route_plan(route, tm):
    n = route.shape[0]
    e_flat = jnp.concatenate([route[:, R_I1], route[:, R_I2]]).astype(jnp.int32)
    onehot = (e_flat[:, None] == jnp.arange(N_EXPERTS, dtype=jnp.int32)[None, :]).astype(jnp.int32)
    csum = jnp.cumsum(onehot, axis=0)
    rank = jnp.take_along_axis(csum, e_flat[:, None], axis=1)[:, 0] - 1
    counts = csum[-1]
    ptiles = (counts + tm - 1) // tm
    tile_end = jnp.cumsum(ptiles)
    gstart = (tile_end - ptiles) * tm
    dest = jnp.take(gstart, e_flat) + rank
    p_max = 2 * n + N_EXPERTS * tm
    tok = jnp.tile(jnp.arange(n, dtype=jnp.int32), 2)
    row_src = jnp.zeros((p_max,), jnp.int32).at[dest].set(tok)
    tiles = jnp.arange(p_max // tm, dtype=jnp.int32)
    te = jnp.minimum(jnp.searchsorted(tile_end, tiles, side="right"), N_EXPERTS - 1).astype(jnp.int32)
    return row_src, dest.astype(jnp.int32).reshape(2, n), te, tile_end[-1:].astype(jnp.int32)


def _ffn_post_kernel(x_ref, y_ref, g_ref, g2_ref, o_ref):
    o_ref[...] = x_ref[...] + g2_ref[0] * _rms(y_ref[...], g_ref[...])


def ffn_post(x1, y, gpost, mod, geo):
    n, d = x1.shape
    return pl.pallas_call(
        _ffn_post_kernel,
        grid=(geo.ntile,),
        in_specs=[
            pl.BlockSpec((TILE, d), lambda i: (i, 0)),
            pl.BlockSpec((TILE, d), lambda i: (i, 0)),
            pl.BlockSpec((1, d), lambda i: (0, 0)),
            pl.BlockSpec((1, 1, d), lambda i: (geo.mod_row(i) * 6 + 5, 0, 0)),
        ],
        out_specs=pl.BlockSpec((TILE, d), lambda i: (i, 0)),
        out_shape=jax.ShapeDtypeStruct((n, d), F32),
        compiler_params=_cparams(("arbitrary",)),
        name="ffn_post",
    )(x1, y, gpost.reshape(1, d), mod)


def _ctx_scan(geo, pre):
    r, kk, v, w0, w1, ka0, ka1, kt0, kt1 = pre
    H, K = N_RWKV_HEADS, RWKV_HEAD
    bl = LANES // H
    nbh = geo.n_ctx // bl

    def to_scan(a):
        return jnp.swapaxes(a.reshape(nbh, geo.t_ctx, LANES, K), 2, 3)

    def from_scan(o):
        return jnp.swapaxes(o, 2, 3).reshape(nbh * geo.t_ctx, bl * H * K)

    def state(sf):
        sf = sf.reshape(nbh, K // SUBLANES, K, SUBLANES, bl, H).transpose(0, 4, 5, 1, 3, 2)
        return sf.reshape(geo.n_ctx, H, K, K)

    kk_s, r_s, v_s = to_scan(kk), to_scan(r), to_scan(v)
    tc = min(32, geo.t_ctx)
    o_f, sf_f = wkv_scan_ctx(to_scan(w0), kk_s, to_scan(ka0), to_scan(kt0), r_s, v_s, tc=tc, rev=False)
    o_b, sf_b = wkv_scan_ctx(to_scan(w1), kk_s, to_scan(ka1), to_scan(kt1), r_s, v_s, tc=tc, rev=True)
    return from_scan(o_f), from_scan(o_b), jnp.stack([state(sf_f), state(sf_b)], axis=1)


def _lat_scan(geo, pre, state_l):
    r, kk, v, w0, w1, ka0, ka1, kt0, kt1 = pre
    H, K = N_RWKV_HEADS, RWKV_HEAD
    kl = LANES // (geo.n_lat * 2 * H)
    nk = K // kl

    def pair(a0, a1):
        a0 = a0.reshape(geo.n_lat, geo.t_lat, H, K)
        a1 = a1.reshape(geo.n_lat, geo.t_lat, H, K)
        a = jnp.stack([a0, a1], axis=2).reshape(geo.n_lat, geo.t_lat, 2, H, kl, nk)
        return a.transpose(1, 5, 4, 0, 2, 3).reshape(geo.t_lat, nk, LANES)

    xs = (pair(w0, w1), pair(kk, kk), pair(ka0, ka1), pair(kt0, kt1), pair(r, r))
    vl = v.reshape(geo.n_lat, geo.t_lat, H, K)
    vl = jnp.stack([vl, vl], axis=2)
    vl = vl.transpose(1, 4, 0, 2, 3).reshape(geo.t_lat, K, LANES // kl)
    v_lat = jnp.tile(vl, (1, 1, kl))
    s0 = state_l.reshape(geo.n_lat, 2, H, K // SUBLANES, SUBLANES, kl, nk)
    s0 = s0.transpose(3, 6, 4, 5, 0, 1, 2).reshape(K // SUBLANES, nk, SUBLANES, LANES)
    oa, ob = wkv_scan_lat(xs, v_lat, s0, tc=min(64, geo.t_lat), kl=kl)

    def direction(o, d):
        o = o.reshape(geo.t_lat, K, kl, geo.n_lat, 2, H)[:, :, :, :, d].sum(axis=2)
        return o.transpose(2, 0, 3, 1).reshape(geo.nl_tok, H * K)

    return direction(oa, 0), direction(ob, 1)


def _rope_tables(geo):
    n_freq = ROPE_DIM // 4
    rows = geo.t_lat // GRID_W
    row = jnp.repeat(jnp.arange(rows, dtype=F32), GRID_W)
    col = jnp.tile(jnp.arange(GRID_W, dtype=F32), rows)
    inv = ROPE_BASE ** (-jnp.arange(n_freq, dtype=F32) / n_freq)
    ang = jnp.concatenate([row[:, None] * inv, col[:, None] * inv], axis=-1)
    cos, sin = jnp.cos(ang), jnp.sin(ang)
    cos32 = jnp.concatenate([cos, cos], axis=-1)
    sin32 = jnp.concatenate([-sin, sin], axis=-1)
    cos_t = jnp.concatenate([jnp.ones((TILE, ROPE_DIM), F32), cos32], axis=0)
    sin_t = jnp.concatenate([jnp.zeros((TILE, ROPE_DIM), F32), sin32], axis=0)
    pad = ((0, 0), (0, LANES - ROPE_DIM))
    return jnp.pad(cos_t, pad), jnp.pad(sin_t, pad)


_DEINT = np.concatenate([np.arange(0, ROPE_DIM, 2), np.arange(1, ROPE_DIM, 2)])
_DEINT_SW = np.concatenate([np.arange(1, ROPE_DIM, 2), np.arange(0, ROPE_DIM, 2)])


def _prep_w_in(w_in, b_in):
    offs = np.cumsum([0, 2 * D_CONV, D_RWKV, D_RWKV, D_RWKV, W_RANK, A_RANK, G_RANK, Q_RANK, KV_RANK,
                      ROPE_DIM, N_BRANCH * D_MODEL])
    o_conv, o_r, o_k, o_v, o_dw, o_da, o_dg, o_cq, o_ckv, o_kr, o_gate, _ = offs
    idx = np.zeros((IN_PAD,), np.int32)
    valid = np.zeros((IN_PAD,), bool)

    def put(dst, src):
        idx[dst:dst + len(src)] = src
        valid[dst:dst + len(src)] = True

    put(C_CONV, np.arange(o_conv, o_conv + 2 * D_CONV))
    for b, c in enumerate((C_G0, C_G1, C_G2)):
        put(c, np.arange(o_gate + b * D_MODEL, o_gate + (b + 1) * D_MODEL))
    put(C_R, np.arange(o_r, o_r + D_RWKV))
    put(C_K, np.arange(o_k, o_k + D_RWKV))
    put(C_V, np.arange(o_v, o_v + D_RWKV))
    put(C_LORA, np.arange(o_dw, o_dw + W_RANK + A_RANK + G_RANK))
    put(C_CQ, np.arange(o_cq, o_cq + Q_RANK))
    put(C_CKV, np.arange(o_ckv, o_ckv + KV_RANK))
    put(C_KR, np.arange(o_kr, o_kr + ROPE_DIM))
    put(C_KR + ROPE_DIM, o_kr + _DEINT)
    put(C_KR + 2 * ROPE_DIM, o_kr + _DEINT_SW)
    w = jnp.where(valid[None, :], jnp.take(w_in, idx, axis=1), 0.0).astype(BF16)
    b = jnp.where(valid, jnp.take(b_in, idx), 0.0)
    return w, b


def _prep_wuq(wuq):
    hd = NOPE_DIM + ROPE_DIM
    nq = N_MLA_HEADS * NOPE_DIM
    ncol = nq + 2 * N_MLA_HEADS * LANES
    idx = np.zeros((ncol,), np.int32)
    valid = np.zeros((ncol,), bool)
    for h in range(N_MLA_HEADS):
        idx[h * NOPE_DIM:(h + 1) * NOPE_DIM] = h * hd + np.arange(NOPE_DIM)
        valid[h * NOPE_DIM:(h + 1) * NOPE_DIM] = True
        for blk, perm in enumerate((_DEINT, _DEINT_SW)):
            c0 = nq + blk * N_MLA_HEADS * LANES + h * LANES
            idx[c0:c0 + ROPE_DIM] = h * hd + NOPE_DIM + perm
            valid[c0:c0 + ROPE_DIM] = True
    return jnp.where(valid[None, :], jnp.take(wuq, idx, axis=1), 0.0).astype(BF16)


def _block_diag_heads(wuv):
    w3 = wuv.reshape(KV_RANK, N_MLA_HEADS, V_DIM)
    eye = jnp.eye(N_MLA_HEADS, dtype=wuv.dtype)
    bd = w3.transpose(1, 0, 2)[:, :, None, :] * eye[:, None, :, None]
    return bd.reshape(N_MLA_HEADS * KV_RANK, N_MLA_HEADS * V_DIM).astype(BF16)


def _attention_keys(geo, ckvn, krr, cache_c, cache_kr):
    nc = geo.nc_tok
    pad = QK_W - KV_RANK - ROPE_DIM

    def keys(c, kr):
        kc = jnp.concatenate([c, kr, jnp.zeros(c.shape[:2] + (pad,), F32)], axis=-1).astype(BF16)
        return kc.transpose(0, 2, 1), c.astype(BF16)

    kr32 = krr[:, :ROPE_DIM]
    kct_c, cv_c = keys(ckvn[:nc].reshape(geo.n_ctx, geo.t_ctx, KV_RANK),
                       kr32[:nc].reshape(geo.n_ctx, geo.t_ctx, ROPE_DIM))
    c_l = jnp.concatenate([ckvn[nc:].reshape(geo.n_lat, geo.t_lat, KV_RANK), cache_c], axis=1)
    kr_l = jnp.concatenate([kr32[nc:].reshape(geo.n_lat, geo.t_lat, ROPE_DIM), cache_kr], axis=1)
    kct_l, cv_l = keys(c_l, kr_l)
    return kct_c, cv_c, kct_l, cv_l


def kernel(x_prompt, x_sample, cache_ckv, cache_krope, state_wkv, c, c_ctx, ada_w, ada_b, norm_mix_pre, norm_mix_post, norm_ffn_pre, norm_ffn_post, w_in, b_in, conv_w, conv_b, conv_ln_g, conv_ln_b, conv_wo, rwkv_mu, rwkv_w0, rwkv_bw, rwkv_a0, rwkv_ba, rwkv_bg, rwkv_xi, rwkv_alpha, rwkv_rho, rwkv_gn_g, rwkv_gn_b, rwkv_wo, mla_q_norm, mla_wuq, mla_kv_norm, mla_wuk, mla_wuv, mla_wo, w_out, ffn_w1, ffn_w3, ffn_w2, moe_router, moe_w1, moe_w3, moe_w2):
    P = dict(rwkv_mu=rwkv_mu, rwkv_w0=rwkv_w0, rwkv_bw=rwkv_bw, rwkv_a0=rwkv_a0, rwkv_ba=rwkv_ba,
             rwkv_bg=rwkv_bg, rwkv_xi=rwkv_xi, rwkv_alpha=rwkv_alpha, rwkv_rho=rwkv_rho)
    n_ctx, t_ctx, d = x_prompt.shape
    n_lat, t_lat, _ = x_sample.shape
    depth = ada_w.shape[0]
    past = cache_ckv.shape[2]
    geo = Geo(n_ctx, t_ctx, n_lat, t_lat, past)
    assert d == D_MODEL and (n_ctx * N_RWKV_HEADS) % LANES == 0 and LANES % (n_lat * 2 * N_RWKV_HEADS) == 0

    x = jnp.concatenate([x_prompt.reshape(-1, d), x_sample.reshape(-1, d)], axis=0)

    n_mod = 1 + n_lat
    n_mod_pad = -(-n_mod // SUBLANES) * SUBLANES
    c_all = jnp.zeros((n_mod_pad, d), F32).at[0].set(c_ctx).at[1:n_mod].set(c)
    head_id = np.arange(D_RWKV) // RWKV_HEAD
    ones_bd = jnp.asarray((head_id[:, None] == head_id[None, :]).astype(np.float32))
    cos_t, sin_t = _rope_tables(geo)

    ckv_out, kr_out, st_out = [], [], []
    for l in range(depth):
        mod = matmul_bias(c_all, ada_w[l].astype(BF16), ada_b[l], tm=n_mod_pad, tn=6 * d // 4,
                          pre="silu", name="ada_mod")
        mod = mod[:n_mod].reshape(n_mod * 6, 1, d)

        h = prenorm(x, norm_mix_pre[l], mod, geo, j_shift=0, j_scale=1)
        w_in_p, b_in_p = _prep_w_in(w_in[l], b_in[l])
        u = matmul_bias(h, w_in_p, b_in_p, tm=math.gcd(1024, geo.ntok), tn=1280, name="in_proj")

        y_conv = conv_branch(u, conv_w[l], conv_b[l], conv_ln_g[l], conv_ln_b[l], conv_wo[l], geo)

        ctx_block = (LANES // N_RWKV_HEADS, geo.ct)
        pre_c = rwkv_pre(u, P, l, geo, ones_bd, 0, geo.nct, seq_block=ctx_block)
        pre_l = rwkv_pre(u, P, l, geo, ones_bd, geo.nct, geo.nlt)
        oc_f, oc_b, sfin = _ctx_scan(geo, pre_c[:9])
        ol_f, ol_b = _lat_scan(geo, pre_l[:9], state_wkv[:, l])
        y_rwkv = rwkv_post(ol_f, ol_b, pre_l[10], pre_l[9], u, rwkv_gn_g[l], rwkv_gn_b[l], ones_bd,
                           rwkv_wo[l], None, geo.nct)
        y_rwkv = rwkv_post(oc_f, oc_b, pre_c[10], pre_c[9], u, rwkv_gn_g[l], rwkv_gn_b[l], ones_bd,
                           rwkv_wo[l], y_rwkv, 0, seq_block=ctx_block)

        wukt = mla_wuk[l].reshape(KV_RANK, N_MLA_HEADS, NOPE_DIM).transpose(1, 2, 0).astype(BF16)
        qf, ckvn, krr = mla_pre(u, cos_t, sin_t, mla_q_norm[l], mla_kv_norm[l],
                                _prep_wuq(mla_wuq[l]), wukt, geo)
        kct_c, cv_c, kct_l, cv_l = _attention_keys(geo, ckvn, krr, cache_ckv[:, l],
                                                   cache_krope[:, l][..., _DEINT])
        wuv_bd = _block_diag_heads(mla_wuv[l])
        wo_b = mla_wo[l].astype(BF16)
        y_mla = attention(qf, kct_l, cv_l, wuv_bd, wo_b, u, None, row0=geo.nc_tok, t_seq=t_lat,
                          tq=TILE, name="attention_lat")
        y_mla = attention(qf, kct_c, cv_c, wuv_bd, wo_b, u, y_mla, row0=0, t_seq=t_ctx,
                          tq=TILE, name="attention_ctx")

        moe = (l % 2 == 1)
        i = l // 2
        router = moe_router[i] if moe else jnp.zeros((d, N_EXPERTS), F32)
        x1, h2, comb = merge(x, y_conv, y_rwkv, y_mla, w_out[l], norm_mix_post[l], norm_ffn_pre[l],
                             mod, router, geo, moe)
        if moe:
            row_src, dest, te, nused = _route_plan(comb, MOE_TM)
            ys = ffn_sorted(h2, row_src, te, nused, moe_w1[i].astype(BF16), moe_w3[i].astype(BF16),
                            moe_w2[i].astype(BF16), tm=MOE_TM, tf=D_FF // 2)
            x = moe_combine_post(x1, ys, dest, comb, norm_ffn_post[l], mod, geo)
        else:
            y = ffn(h2, comb, ffn_w1[i:i + 1].astype(BF16), ffn_w3[i:i + 1].astype(BF16),
                    ffn_w2[i:i + 1].astype(BF16), tm=512, tf=D_FF // 2)
            x = ffn_post(x1, y, norm_ffn_post[l], mod, geo)

        ckv_out.append(ckvn[:geo.nc_tok].reshape(n_ctx, t_ctx, KV_RANK))
        kr_out.append(u[:geo.nc_tok, C_KR:C_KR + ROPE_DIM].reshape(n_ctx, t_ctx, ROPE_DIM))
        st_out.append(sfin)

    y_prompt = x[:geo.nc_tok].reshape(n_ctx, t_ctx, d)
    y_sample = x[geo.nc_tok:].reshape(n_lat, t_lat, d)
    return (y_prompt, y_sample, jnp.stack(ckv_out, axis=1), jnp.stack(kr_out, axis=1),
            jnp.stack(st_out, axis=1))
```

```python
import functools
import math

import numpy as np
import jax
import jax.numpy as jnp
from jax import lax
from jax.experimental import pallas as pl
from jax.experimental.pallas import tpu as pltpu

F32 = jnp.float32
BF16 = jnp.bfloat16

D_MODEL = 1024
GRID_W = 64
D_CONV = 512
CONV_K = 31
D_RWKV = 512
RWKV_HEAD = 64
N_RWKV_HEADS = D_RWKV // RWKV_HEAD
W_RANK = 64
A_RANK = 64
G_RANK = 128
DECAY_SCALE = math.exp(-0.5)
GN_EPS = 64e-5
N_MLA_HEADS = 8
Q_RANK = 256
KV_RANK = 128
NOPE_DIM = 64
ROPE_DIM = 32
V_DIM = 64
ROPE_BASE = 10000.0
ATTN_SCALE = 1.0 / math.sqrt(NOPE_DIM + ROPE_DIM)
N_BRANCH = 3
D_FF = 2816
N_EXPERTS = 8
EPS = 1e-6

LANES = 128
SUBLANES = 8
VMEM_LIMIT_BYTES = 56 * 1024 * 1024

TILE = 256
CONV_HALO = 16
SHIFT_HALO = 8

C_CONV, C_G0, C_G1, C_G2 = 0, 1024, 2048, 3072
C_R, C_K, C_V = 4096, 4608, 5120
C_LORA = 5632
C_CQ = 5888
C_CKV = 6144
C_KR = 6272
IN_PAD = 6400


def _cparams(sem):
    return pltpu.CompilerParams(dimension_semantics=sem, vmem_limit_bytes=VMEM_LIMIT_BYTES)


def _sigmoid(x):
    return jax.nn.sigmoid(x)


def _rms(x, g):
    return (x * lax.rsqrt(jnp.mean(x * x, axis=-1, keepdims=True) + EPS)) * g


class Geo:
    def __init__(self, n_ctx, t_ctx, n_lat, t_lat, past):
        assert t_ctx % TILE == 0 and t_lat % TILE == 0
        self.n_ctx, self.t_ctx, self.n_lat, self.t_lat, self.past = n_ctx, t_ctx, n_lat, t_lat, past
        self.ct = t_ctx // TILE
        self.lt = t_lat // TILE
        self.nct = n_ctx * self.ct
        self.nlt = n_lat * self.lt
        self.ntile = self.nct + self.nlt
        self.nc_tok = n_ctx * t_ctx
        self.nl_tok = n_lat * t_lat
        self.ntok = self.nc_tok + self.nl_tok

    def pos(self, i):
        is_ctx = i < self.nct
        p = jnp.where(is_ctx, i % self.ct, (i - self.nct) % self.lt)
        n = jnp.where(is_ctx, self.ct, self.lt)
        return p, n

    def mod_row(self, i):
        return jnp.where(i < self.nct, 0, 1 + (i - self.nct) // self.lt)

    def rope_blk(self, i):
        return jnp.where(i < self.nct, 0, 1 + (i - self.nct) % self.lt)


def _mm_kernel(x_ref, w_ref, b_ref, o_ref, *, pre):
    x = x_ref[...]
    if pre == "silu":
        x = x.astype(F32)
        x = x * _sigmoid(x)
    acc = jnp.dot(x.astype(BF16), w_ref[...], preferred_element_type=F32)
    o_ref[...] = (acc + b_ref[...]).astype(o_ref.dtype)


def matmul_bias(x, w, b, *, tm, tn, pre=None, out_dtype=F32, name="matmul"):
    m, k = x.shape
    n = w.shape[1]
    assert m % tm == 0 and n % tn == 0
    return pl.pallas_call(
        functools.partial(_mm_kernel, pre=pre),
        grid=(n // tn, m // tm),
        in_specs=[
            pl.BlockSpec((tm, k), lambda j, i: (i, 0)),
            pl.BlockSpec((k, tn), lambda j, i: (0, j)),
            pl.BlockSpec((1, tn), lambda j, i: (0, j)),
        ],
        out_specs=pl.BlockSpec((tm, tn), lambda j, i: (i, j)),
        out_shape=jax.ShapeDtypeStruct((m, n), out_dtype),
        compiler_params=_cparams(("arbitrary", "arbitrary")),
        name=name,
    )(x, w, b.reshape(1, n).astype(F32))


def _prenorm_kernel(x_ref, g_ref, sc_ref, sh_ref, o_ref):
    h = _rms(x_ref[...], g_ref[...]) * (1.0 + sc_ref[0]) + sh_ref[0]
    o_ref[...] = h.astype(o_ref.dtype)


def prenorm(x, g, mod, geo, j_shift, j_scale):
    n, d = x.shape
    return pl.pallas_call(
        _prenorm_kernel,
        grid=(geo.ntile,),
        in_specs=[
            pl.BlockSpec((TILE, d), lambda i: (i, 0)),
            pl.BlockSpec((1, d), lambda i: (0, 0)),
            pl.BlockSpec((1, 1, d), lambda i: (geo.mod_row(i) * 6 + j_scale, 0, 0)),
            pl.BlockSpec((1, 1, d), lambda i: (geo.mod_row(i) * 6 + j_shift, 0, 0)),
        ],
        out_specs=pl.BlockSpec((TILE, d), lambda i: (i, 0)),
        out_shape=jax.ShapeDtypeStruct((n, d), BF16),
        compiler_params=_cparams(("arbitrary",)),
        name="prenorm",
    )(x, g.reshape(1, d), mod, mod)


def _conv_kernel(cur_ref, prev_ref, next_ref, gate_ref, cw_ref, cb_ref, lng_ref, lnb_ref,
                 wo_ref, o_ref, hp_ref, *, geo):
    i = pl.program_id(0)
    p, n = geo.pos(i)
    has_prev = p > 0
    has_next = p < n - 1

    def glu(z):
        return z[:, :D_CONV] * _sigmoid(z[:, D_CONV:])

    hp_ref[0, 0:CONV_HALO, :] = jnp.where(has_prev, glu(prev_ref[...]), 0.0)
    hp_ref[0, CONV_HALO:CONV_HALO + TILE, :] = glu(cur_ref[...])
    hp_ref[0, CONV_HALO + TILE:, :] = jnp.where(has_next, glu(next_ref[...]), 0.0)
    nrow = TILE + 2 * CONV_HALO - SUBLANES
    for b in range(1, SUBLANES):
        hp_ref[b, 0:nrow, :] = hp_ref[0, pl.ds(b, nrow), :]

    off = CONV_HALO - CONV_K // 2
    acc = None
    for j in range(CONV_K):
        q = off + j
        term = hp_ref[q % SUBLANES, pl.ds(q - q % SUBLANES, TILE), :] * cw_ref[j:j + 1, :]
        acc = term if acc is None else acc + term
    h = acc + cb_ref[...]
    mu = jnp.mean(h, axis=-1, keepdims=True)
    hc = h - mu
    var = jnp.mean(hc * hc, axis=-1, keepdims=True)
    h = hc * lax.rsqrt(var + EPS) * lng_ref[...] + lnb_ref[...]
    h = h * _sigmoid(h)
    y = jnp.dot(h.astype(BF16), wo_ref[...], preferred_element_type=F32)
    o_ref[...] = _sigmoid(gate_ref[...]) * y


def conv_branch(u, cw, cb, lng, lnb, wo, geo):
    n = u.shape[0]
    hb = TILE // CONV_HALO
    nhalo = n // CONV_HALO
    cwp = jnp.zeros((32, D_CONV), F32).at[:CONV_K].set(cw)
    return pl.pallas_call(
        functools.partial(_conv_kernel, geo=geo),
        grid=(geo.ntile,),
        in_specs=[
            pl.BlockSpec((TILE, 2 * D_CONV), lambda i: (i, C_CONV // (2 * D_CONV))),
            pl.BlockSpec((CONV_HALO, 2 * D_CONV), lambda i: (jnp.maximum(i * hb - 1, 0), 0)),
            pl.BlockSpec((CONV_HALO, 2 * D_CONV), lambda i: (jnp.minimum((i + 1) * hb, nhalo - 1), 0)),
            pl.BlockSpec((TILE, D_MODEL), lambda i: (i, C_G0 // D_MODEL)),
            pl.BlockSpec((32, D_CONV), lambda i: (0, 0)),
            pl.BlockSpec((1, D_CONV), lambda i: (0, 0)),
            pl.BlockSpec((1, D_CONV), lambda i: (0, 0)),
            pl.BlockSpec((1, D_CONV), lambda i: (0, 0)),
            pl.BlockSpec((D_CONV, D_MODEL), lambda i: (0, 0)),
        ],
        out_specs=pl.BlockSpec((TILE, D_MODEL), lambda i: (i, 0)),
        out_shape=jax.ShapeDtypeStruct((n, D_MODEL), F32),
        scratch_shapes=[pltpu.VMEM((SUBLANES, TILE + 2 * CONV_HALO, D_CONV), F32)],
        compiler_params=_cparams(("arbitrary",)),
        name="conv_branch",
    )(u, u, u, u, cwp, cb.reshape(1, -1), lng.reshape(1, -1), lnb.reshape(1, -1), wo.astype(BF16))


def _seg_sum(x, ones_bd):
    return jnp.dot(x, ones_bd, preferred_element_type=F32, precision=lax.Precision.HIGHEST)


def _rwkv_pre_kernel(r_ref, rp_ref, rn_ref, k_ref, kp_ref, kn_ref, v_ref, vp_ref, vn_ref,
                     lora_ref, mu_ref, w0_ref, bw_ref, a0_ref, ba_ref, bg_ref, xi_ref, al_ref,
                     rho_ref, ones_ref,
                     r_o, kk_o, v_o, w0_o, w1_o, ka0_o, ka1_o, kt0_o, kt1_o, g_o, bonus_o, *, geo, tile0):
    i = pl.program_id(0) + tile0
    p, n = geo.pos(i)
    has_prev = p > 0
    has_next = p < n - 1
    row = lax.broadcasted_iota(jnp.int32, (TILE, D_RWKV), 0)

    def put(o_ref, val):
        if len(o_ref.shape) == 2:
            o_ref[...] = val
        else:
            for q in range(o_ref.shape[1]):
                o_ref[:, q, :] = val[:, q * LANES:(q + 1) * LANES]

    def shifted(c_ref, p_ref, n_ref, mu):
        cur = c_ref[...]
        pv = jnp.where(has_prev, p_ref[SHIFT_HALO - 1:SHIFT_HALO, :], 0.0)
        nx = jnp.where(has_next, n_ref[0:1, :], 0.0)
        prev = jnp.where(row == 0, pv, pltpu.roll(cur, 1, axis=0))
        nxt = jnp.where(row == TILE - 1, nx, pltpu.roll(cur, TILE - 1, axis=0))
        return cur + mu * (0.5 * (prev + nxt) - cur)

    r = shifted(r_ref, rp_ref, rn_ref, mu_ref[0:1, :])
    k = shifted(k_ref, kp_ref, kn_ref, mu_ref[1:2, :])
    v = shifted(v_ref, vp_ref, vn_ref, mu_ref[2:3, :])

    lora = lora_ref[...]
    dw = jnp.tanh(lora[:, :W_RANK]).astype(BF16)
    da = lora[:, W_RANK:W_RANK + A_RANK].astype(BF16)
    dg = _sigmoid(lora[:, W_RANK + A_RANK:]).astype(BF16)
    wl = jnp.dot(dw, bw_ref[...], preferred_element_type=F32)
    al = jnp.dot(da, ba_ref[...], preferred_element_type=F32)
    g = jnp.dot(dg, bg_ref[...], preferred_element_type=F32)

    ones_bd = ones_ref[...]
    kx = k * xi_ref[...]
    kk = kx * lax.rsqrt(_seg_sum(kx * kx, ones_bd) + EPS)
    alpha = al_ref[...]
    rho = rho_ref[...]
    bonus = jnp.zeros((TILE, D_RWKV), F32)
    for d, (w_o, ka_o, kt_o) in enumerate(((w0_o, ka0_o, kt0_o), (w1_o, ka1_o, kt1_o))):
        sl = slice(d * D_RWKV, (d + 1) * D_RWKV)
        w = jnp.exp(-DECAY_SCALE * _sigmoid(w0_ref[d:d + 1, :] + wl[:, sl]))
        a = _sigmoid(a0_ref[d:d + 1, :] + al[:, sl])
        kt = k * (1.0 + (a - 1.0) * alpha)
        put(w_o, w)
        put(ka_o, kk * a)
        put(kt_o, kt)
        bonus = bonus + _seg_sum(r * kt * rho, ones_bd) * v
    put(r_o, r)
    put(kk_o, kk)
    v_o[...] = v
    g_o[...] = g
    bonus_o[...] = bonus


def rwkv_pre(u, P, l, geo, ones_bd, tile0, ntiles, seq_block=None, lane_tiled=None):
    hb = TILE // SHIFT_HALO
    nhalo = u.shape[0] // SHIFT_HALO

    def trio(c0):
        cb = c0 // D_RWKV
        return [
            pl.BlockSpec((TILE, D_RWKV), lambda i: (i + tile0, cb)),
            pl.BlockSpec((SHIFT_HALO, D_RWKV), lambda i: (jnp.maximum((i + tile0) * hb - 1, 0), cb)),
            pl.BlockSpec((SHIFT_HALO, D_RWKV), lambda i: (jnp.minimum((i + tile0 + 1) * hb, nhalo - 1), cb)),
        ]

    def full(shape):
        return pl.BlockSpec(shape, lambda i: tuple(0 for _ in shape))

    bw = jnp.concatenate([P["rwkv_bw"][l, 0], P["rwkv_bw"][l, 1]], axis=1).astype(BF16)
    ba = jnp.concatenate([P["rwkv_ba"][l, 0], P["rwkv_ba"][l, 1]], axis=1).astype(BF16)
    out = jax.ShapeDtypeStruct((ntiles * TILE, D_RWKV), F32)
    tok_spec = pl.BlockSpec((TILE, D_RWKV), lambda i: (i, 0))
    if seq_block is None:
        scan_spec, scan_out = tok_spec, out
    else:
        bl, ts = seq_block
        scan_out = jax.ShapeDtypeStruct((ntiles * TILE // bl, bl * D_RWKV), F32)
        scan_spec = pl.BlockSpec((TILE, D_RWKV),
                                 lambda i: ((i // ts // bl) * ts + i % ts, (i // ts) % bl))
    out_specs = [scan_spec] * 9 + [tok_spec] * 2
    out_shape = [scan_out] * 9 + [out] * 2
    if lane_tiled is not None:
        nlt = D_RWKV // LANES
        nseq = ntiles // lane_tiled
        k_spec = pl.BlockSpec((TILE, nlt, LANES), lambda i: (i % lane_tiled, 0, i // lane_tiled))
        k_out = jax.ShapeDtypeStruct((lane_tiled * TILE, nlt, nseq * LANES), F32)
        for j in (0, 1, 3, 4, 5, 6, 7, 8):
            out_specs[j], out_shape[j] = k_spec, k_out
    return pl.pallas_call(
        functools.partial(_rwkv_pre_kernel, geo=geo, tile0=tile0),
        grid=(ntiles,),
        in_specs=trio(C_R) + trio(C_K) + trio(C_V) + [
            pl.BlockSpec((TILE, 256), lambda i: (i + tile0, C_LORA // 256)),
            full((3, D_RWKV)), full((2, D_RWKV)), full((W_RANK, 2 * D_RWKV)),
            full((2, D_RWKV)), full((A_RANK, 2 * D_RWKV)), full((G_RANK, D_RWKV)),
            full((1, D_RWKV)), full((1, D_RWKV)), full((1, D_RWKV)), full((D_RWKV, D_RWKV)),
        ],
        out_specs=out_specs,
        out_shape=out_shape,
        compiler_params=_cparams(("arbitrary",)),
        name="rwkv_pre",
    )(u, u, u, u, u, u, u, u, u, u,
      P["rwkv_mu"][l], P["rwkv_w0"][l], bw, P["rwkv_a0"][l], ba, P["rwkv_bg"][l].astype(BF16),
      P["rwkv_xi"][l].reshape(1, -1), P["rwkv_alpha"][l].reshape(1, -1),
      P["rwkv_rho"][l].reshape(1, -1), ones_bd)


N_VBLK = RWKV_HEAD // SUBLANES


def _scan_steps(xrow, vload, ostore, s_ref, tidx, *, tc, nk, kl, ngroups=2, unroll=1):
    gsz = N_VBLK // ngroups
    groups = tuple(tuple(range(g * gsz, (g + 1) * gsz)) for g in range(ngroups))

    def allred(a):
        if kl == 1:
            return a
        q = lax.broadcasted_iota(jnp.int32, a.shape, 1) % kl
        out = a
        for j in range(1, kl):
            out = out + jnp.where(q >= j, pltpu.roll(a, j, axis=1), 0.0)
            out = out + jnp.where(q < kl - j, pltpu.roll(a, LANES - j, axis=1), 0.0)
        return out

    def first_sa(grp, t):
        acc = [None] * len(grp)
        for kh in range(nk):
            kkb = xrow(t, 1, kh)
            for j, vb in enumerate(grp):
                pr = s_ref[vb, kh] * kkb
                acc[j] = pr if acc[j] is None else acc[j] + pr
        return tuple(allred(a) for a in acc)

    def fused(grp, s, t, t_next, sa):
        vv = [vload(t, vb) for vb in grp]
        oacc = [None] * len(grp)
        acc = [None] * len(grp)
        for kh in range(nk):
            wb, kab, ktb, rb = xrow(t, 0, kh), xrow(t, 2, kh), xrow(t, 3, kh), xrow(t, 4, kh)
            kkn = xrow(t_next, 1, kh)
            for j, vb in enumerate(grp):
                sn = s_ref[vb, kh] * wb - sa[j] * kab + vv[j] * ktb
                s_ref[vb, kh] = sn
                po = sn * rb
                pa = sn * kkn
                oacc[j] = po if oacc[j] is None else oacc[j] + po
                acc[j] = pa if acc[j] is None else acc[j] + pa
        for j, vb in enumerate(grp):
            ostore(s, t, vb, oacc[j])
        return tuple(allred(a) for a in acc)

    t0 = tidx(0)
    carry0 = tuple(first_sa(g, t0) for g in groups)

    def step(s, carry):
        t = tidx(s)
        t_next = tidx(jnp.minimum(s + 1, tc - 1))
        return tuple(fused(g, s, t, t_next, carry[i]) for i, g in enumerate(groups))

    lax.fori_loop(0, tc, step, carry0, unroll=unroll)


def _vrows(vb):
    return pl.ds(vb * SUBLANES, SUBLANES)


def _scan_ctx_kernel(w_ref, kk_ref, ka_ref, kt_ref, r_ref, v_ref, o_ref, sfin_ref, s_ref, *, tc, nk, rev):
    c = pl.program_id(1)
    x_refs = (w_ref, kk_ref, ka_ref, kt_ref, r_ref)

    @pl.when(c == 0)
    def _():
        s_ref[...] = jnp.zeros(s_ref.shape, F32)

    def xrow(t, a, kh):
        return jnp.broadcast_to(x_refs[a][0, t, pl.ds(kh, 1), :], (SUBLANES, LANES))

    def vload(t, vb):
        return v_ref[0, t, _vrows(vb), :]

    def ostore(s, t, vb, val):
        o_ref[0, t, _vrows(vb), :] = val

    def tidx(s):
        return tc - 1 - s if rev else s

    _scan_steps(xrow, vload, ostore, s_ref, tidx, tc=tc, nk=nk, kl=1)

    @pl.when(c == pl.num_programs(1) - 1)
    def _():
        sfin_ref[0] = s_ref[...]


def wkv_scan_ctx(w, kk, ka, kt, r, v, *, tc, rev):
    ngb, t_len, nk, _ = w.shape
    nch = t_len // tc
    blk = pl.BlockSpec((1, tc, nk, LANES), lambda g, c: (g, nch - 1 - c if rev else c, 0, 0))
    st = pl.BlockSpec((1, N_VBLK, nk, SUBLANES, LANES), lambda g, c: (g, 0, 0, 0, 0))
    return pl.pallas_call(
        functools.partial(_scan_ctx_kernel, tc=tc, nk=nk, rev=rev),
        grid=(ngb, nch),
        in_specs=[blk] * 6,
        out_specs=[blk, st],
        out_shape=[
            jax.ShapeDtypeStruct((ngb, t_len, RWKV_HEAD, LANES), F32),
            jax.ShapeDtypeStruct((ngb, N_VBLK, nk, SUBLANES, LANES), F32),
        ],
        scratch_shapes=[pltpu.VMEM((N_VBLK, nk, SUBLANES, LANES), F32)],
        compiler_params=_cparams(("arbitrary", "arbitrary")),
        name="wkv_scan_ctx",
    )(w, kk, ka, kt, r, v)


def _scan_lat_kernel(*refs, tc, nk, kl):
    xa_refs, xb_refs = refs[0:5], refs[5:10]
    va_ref, vb_ref, s0_ref, oa_ref, ob_ref, s_ref, xm_ref, vm_ref = refs[10:]
    c = pl.program_id(0)

    @pl.when(c == 0)
    def _():
        s_ref[...] = s0_ref[...]

    def is_bwd(shape):
        lane = lax.broadcasted_iota(jnp.int32, shape, len(shape) - 1)
        return (lane // N_RWKV_HEADS) % 2 == 1

    mx = is_bwd((nk, LANES))
    mv = is_bwd((RWKV_HEAD, LANES))

    def merge(s, carry):
        for a in range(5):
            xm_ref[a, s] = jnp.where(mx, xb_refs[a][tc - 1 - s], xa_refs[a][s])
        vm_ref[s] = jnp.where(mv, vb_ref[tc - 1 - s], va_ref[s])
        return carry

    lax.fori_loop(0, tc, merge, 0)

    def xrow(t, a, kh):
        return jnp.broadcast_to(xm_ref[a, t, pl.ds(kh, 1), :], (SUBLANES, LANES))

    def vload(t, vb):
        return vm_ref[t, _vrows(vb), :]

    def ostore(s, t, vb, val):
        oa_ref[s, _vrows(vb), :] = val
        ob_ref[tc - 1 - s, _vrows(vb), :] = val

    _scan_steps(xrow, vload, ostore, s_ref, lambda s: s, tc=tc, nk=nk, kl=kl, ngroups=2, unroll=2)


def wkv_scan_lat(xs, v, s0, *, tc, kl):
    t_len, nk, _ = xs[0].shape
    nch = t_len // tc
    o_sds = jax.ShapeDtypeStruct((t_len, RWKV_HEAD, LANES), F32)
    xa = pl.BlockSpec((tc, nk, LANES), lambda c: (c, 0, 0))
    xb = pl.BlockSpec((tc, nk, LANES), lambda c: (nch - 1 - c, 0, 0))
    va = pl.BlockSpec((tc, RWKV_HEAD, LANES), lambda c: (c, 0, 0))
    vb = pl.BlockSpec((tc, RWKV_HEAD, LANES), lambda c: (nch - 1 - c, 0, 0))
    return pl.pallas_call(
        functools.partial(_scan_lat_kernel, tc=tc, nk=nk, kl=kl),
        grid=(nch,),
        in_specs=[xa] * 5 + [xb] * 5 + [va, vb,
                                        pl.BlockSpec((N_VBLK, nk, SUBLANES, LANES), lambda c: (0, 0, 0, 0))],
        out_specs=[va, vb],
        out_shape=[o_sds, o_sds],
        scratch_shapes=[pltpu.VMEM((N_VBLK, nk, SUBLANES, LANES), F32),
                        pltpu.VMEM((5, tc, nk, LANES), F32),
                        pltpu.VMEM((tc, RWKV_HEAD, LANES), F32)],
        compiler_params=_cparams(("arbitrary",)),
        name="wkv_scan_lat",
    )(*xs, *xs, v, v, s0)


def _rwkv_post_kernel(of_ref, ob_ref, bonus_ref, g_ref, gate_ref, gng_ref, gnb_ref, ones_ref,
                      wo_ref, *rest):
    y_ref = rest[-1]
    o = of_ref[...] + ob_ref[...]
    ones_bd = ones_ref[...]
    mean = _seg_sum(o, ones_bd) * (1.0 / RWKV_HEAD)
    oc = o - mean
    var = _seg_sum(oc * oc, ones_bd) * (1.0 / RWKV_HEAD)
    gn = oc * lax.rsqrt(var + GN_EPS) * gng_ref[...] + gnb_ref[...]
    y = (gn + bonus_ref[...]) * g_ref[...]
    y = jnp.dot(y.astype(BF16), wo_ref[...], preferred_element_type=F32)
    y_ref[...] = _sigmoid(gate_ref[...]) * y


def rwkv_post(o_f, o_b, bonus, g, u, gng, gnb, ones_bd, wo, y_prev, tile0, seq_block=None):
    ntiles = bonus.shape[0] // TILE

    def full(shape):
        return pl.BlockSpec(shape, lambda i: tuple(0 for _ in shape))

    part = pl.BlockSpec((TILE, D_RWKV), lambda i: (i, 0))
    if seq_block is None:
        opart = part
    else:
        bl, ts = seq_block
        opart = pl.BlockSpec((TILE, D_RWKV), lambda i: ((i // ts // bl) * ts + i % ts, (i // ts) % bl))
    in_specs = [opart, opart, part, part,
                pl.BlockSpec((TILE, D_MODEL), lambda i: (i + tile0, C_G1 // D_MODEL)),
                full((1, D_RWKV)), full((1, D_RWKV)), full((D_RWKV, D_RWKV)), full((D_RWKV, D_MODEL))]
    args = [o_f, o_b, bonus, g, u, gng.reshape(1, -1), gnb.reshape(1, -1), ones_bd, wo.astype(BF16)]
    aliases = {}
    if y_prev is not None:
        in_specs.append(pl.BlockSpec(memory_space=pl.ANY))
        args.append(y_prev)
        aliases = {len(args) - 1: 0}
    return pl.pallas_call(
        _rwkv_post_kernel,
        grid=(ntiles,),
        in_specs=in_specs,
        out_specs=pl.BlockSpec((TILE, D_MODEL), lambda i: (i + tile0, 0)),
        out_shape=jax.ShapeDtypeStruct((u.shape[0], D_MODEL), F32),
        input_output_aliases=aliases,
        compiler_params=_cparams(("arbitrary",)),
        name="rwkv_post",
    )(*args)


QK_W = 2 * LANES
Q_SCALE = ATTN_SCALE * math.log2(math.e)


def _mla_pre_kernel(cq_ref, ckv_ref, kr_ref, cos_ref, sin_ref, qg_ref, kvg_ref, wuq_ref, wukt_ref,
                    qf_o, ckvn_o, krr_o):
    nq = N_MLA_HEADS * NOPE_DIM
    nrp = N_MLA_HEADS * LANES
    cq = _rms(cq_ref[...], qg_ref[...])
    q = jnp.dot(cq.astype(BF16), wuq_ref[...], preferred_element_type=F32)
    cos = cos_ref[...]
    sin = sin_ref[...]
    for h in range(N_MLA_HEADS):
        qn = q[:, h * NOPE_DIM:(h + 1) * NOPE_DIM].astype(BF16)
        qa = jnp.dot(qn, wukt_ref[h], preferred_element_type=F32)
        qr = (q[:, nq + h * LANES:nq + (h + 1) * LANES] * cos
              + q[:, nq + nrp + h * LANES:nq + nrp + (h + 1) * LANES] * sin)
        qf_o[:, h * QK_W:h * QK_W + LANES] = (qa * Q_SCALE).astype(qf_o.dtype)
        qf_o[:, h * QK_W + LANES:(h + 1) * QK_W] = (qr * Q_SCALE).astype(qf_o.dtype)
    ckvn_o[...] = _rms(ckv_ref[...], kvg_ref[...])
    kr = kr_ref[...]
    krr = kr[:, ROPE_DIM:2 * ROPE_DIM] * cos[:, :ROPE_DIM] + kr[:, 2 * ROPE_DIM:3 * ROPE_DIM] * sin[:, :ROPE_DIM]
    krr_o[...] = jnp.concatenate([krr, jnp.zeros((TILE, LANES - ROPE_DIM), F32)], axis=1)


def mla_pre(u, cos_t, sin_t, qg, kvg, wuq_p, wukt, geo):
    n = u.shape[0]

    def full(shape):
        return pl.BlockSpec(shape, lambda i: tuple(0 for _ in shape))

    return pl.pallas_call(
        _mla_pre_kernel,
        grid=(geo.ntile,),
        in_specs=[
            pl.BlockSpec((TILE, Q_RANK), lambda i: (i, C_CQ // Q_RANK)),
            pl.BlockSpec((TILE, KV_RANK), lambda i: (i, C_CKV // KV_RANK)),
            pl.BlockSpec((TILE, LANES), lambda i: (i, C_KR // LANES)),
            pl.BlockSpec((TILE, LANES), lambda i: (geo.rope_blk(i), 0)),
            pl.BlockSpec((TILE, LANES), lambda i: (geo.rope_blk(i), 0)),
            full((1, Q_RANK)), full((1, KV_RANK)),
            full(wuq_p.shape), full(wukt.shape),
        ],
        out_specs=[
            pl.BlockSpec((TILE, N_MLA_HEADS * QK_W), lambda i: (i, 0)),
            pl.BlockSpec((TILE, KV_RANK), lambda i: (i, 0)),
            pl.BlockSpec((TILE, LANES), lambda i: (i, 0)),
        ],
        out_shape=[
            jax.ShapeDtypeStruct((n, N_MLA_HEADS * QK_W), BF16),
            jax.ShapeDtypeStruct((n, KV_RANK), F32),
            jax.ShapeDtypeStruct((n, LANES), F32),
        ],
        compiler_params=_cparams(("arbitrary",)),
        name="mla_pre",
    )(u, u, u, cos_t, sin_t, qg.reshape(1, -1), kvg.reshape(1, -1), wuq_p, wukt)


ATT_KEY_CHUNK = 512
ATT_HEADS_PER_STEP = 4


def _attn_kernel(q_ref, kt_ref, c_ref, wuv_ref, wo_ref, gate_ref, *rest, ck):
    y_ref, pc_ref = rest[-2:]
    hg = pl.program_id(2)
    tq = q_ref.shape[0]
    t_k = kt_ref.shape[2]
    hp = ATT_HEADS_PER_STEP
    q = jnp.concatenate([q_ref[:, j * QK_W:(j + 1) * QK_W] for j in range(hp)], axis=0)
    m = jnp.full((hp * tq, 1), -jnp.inf, F32)
    l = jnp.zeros((hp * tq, 1), F32)
    acc = jnp.zeros((hp * tq, KV_RANK), F32)
    for c0 in range(0, t_k, ck):
        s = jnp.dot(q, kt_ref[0, :, c0:c0 + ck], preferred_element_type=F32)
        m_new = jnp.maximum(m, jnp.max(s, axis=-1, keepdims=True))
        alpha = jnp.exp2(m - m_new)
        p = jnp.exp2(s - m_new)
        l = alpha * l + jnp.sum(p, axis=-1, keepdims=True)
        acc = alpha * acc + jnp.dot(p.astype(BF16), c_ref[0, c0:c0 + ck, :], preferred_element_type=F32)
        m = m_new
    pc = (acc / l).astype(BF16)
    for j in range(hp):
        pc_ref[hg * hp + j] = pc[j * tq:(j + 1) * tq]

    @pl.when(hg == pl.num_programs(2) - 1)
    def _():
        pcs = jnp.concatenate([pc_ref[i] for i in range(N_MLA_HEADS)], axis=1)
        oh = jnp.dot(pcs, wuv_ref[...], preferred_element_type=F32)
        y = jnp.dot(oh.astype(BF16), wo_ref[...], preferred_element_type=F32)
        y_ref[...] = _sigmoid(gate_ref[...]) * y


def attention(qf, kct, cv, wuv_bd, wo, u, y_prev, *, row0, t_seq, tq, name):
    n = qf.shape[0]
    nseq, _, t_k = kct.shape
    qt = t_seq // tq
    rb0 = row0 // tq
    ck = math.gcd(ATT_KEY_CHUNK, t_k)

    def rows(w, col):
        return pl.BlockSpec((tq, w), lambda s, i, h: (rb0 + s * qt + i, col(h)))

    in_specs = [
        rows(ATT_HEADS_PER_STEP * QK_W, lambda h: h),
        pl.BlockSpec((1, QK_W, t_k), lambda s, i, h: (s, 0, 0)),
        pl.BlockSpec((1, t_k, KV_RANK), lambda s, i, h: (s, 0, 0)),
        pl.BlockSpec(wuv_bd.shape, lambda s, i, h: (0, 0)),
        pl.BlockSpec(wo.shape, lambda s, i, h: (0, 0)),
        rows(D_MODEL, lambda h: C_G2 // D_MODEL),
    ]
    args = [qf, kct, cv, wuv_bd, wo, u]
    aliases = {}
    if y_prev is not None:
        in_specs.append(pl.BlockSpec(memory_space=pl.ANY))
        args.append(y_prev)
        aliases = {len(args) - 1: 0}
    return pl.pallas_call(
        functools.partial(_attn_kernel, ck=ck),
        grid=(nseq, qt, N_MLA_HEADS // ATT_HEADS_PER_STEP),
        in_specs=in_specs,
        out_specs=rows(D_MODEL, lambda h: 0),
        out_shape=jax.ShapeDtypeStruct((n, D_MODEL), F32),
        scratch_shapes=[pltpu.VMEM((N_MLA_HEADS, tq, KV_RANK), BF16)],
        input_output_aliases=aliases,
        compiler_params=_cparams(("arbitrary", "arbitrary", "arbitrary")),
        name=name,
    )(*args)


R_I1, R_I2, R_G1, R_G2 = 0, 1, 2, 3


def _merge_kernel(x_ref, yc_ref, yr_ref, ym_ref, wout_ref, gpost_ref, g1_ref, gpre_ref, sc_ref,
                  sh_ref, router_ref, x1_o, h2_o, comb_o, *, moe):
    m = yc_ref[...] + yr_ref[...] + ym_ref[...]
    y = jnp.dot(m.astype(BF16), wout_ref[...], preferred_element_type=F32)
    x1 = x_ref[...] + g1_ref[0] * _rms(y, gpost_ref[...])
    x1_o[...] = x1
    h2 = _rms(x1, gpre_ref[...]) * (1.0 + sc_ref[0]) + sh_ref[0]
    h2_o[...] = h2.astype(h2_o.dtype)
    if moe:
        logits = jnp.dot(h2, router_ref[...], preferred_element_type=F32,
                         precision=lax.Precision.HIGHEST)
        lane = lax.broadcasted_iota(jnp.int32, logits.shape, 1)
        neg = jnp.float32(-jnp.inf)
        logits = jnp.where(lane < N_EXPERTS, logits, neg)
        m1 = jnp.max(logits, axis=-1, keepdims=True)
        i1 = jnp.min(jnp.where(logits == m1, lane, LANES), axis=-1, keepdims=True)
        rest = jnp.where(lane == i1, neg, logits)
        m2 = jnp.max(rest, axis=-1, keepdims=True)
        i2 = jnp.min(jnp.where(rest == m2, lane, LANES), axis=-1, keepdims=True)
        e2 = jnp.exp(m2 - m1)
        den = 1.0 + e2
        cols = ((R_I1, i1.astype(F32)), (R_I2, i2.astype(F32)), (R_G1, 1.0 / den), (R_G2, e2 / den))
        route = jnp.zeros(comb_o.shape, F32)
        for col, val in cols:
            route = jnp.where(lane == col, val, route)
        comb_o[...] = route
    else:
        comb_o[...] = jnp.ones(comb_o.shape, F32)


def merge(x, yc, yr, ym, wout, gpost, gpre, mod, router, geo, moe):
    n, d = x.shape

    def full(shape):
        return pl.BlockSpec(shape, lambda i: tuple(0 for _ in shape))

    def rows():
        return pl.BlockSpec((TILE, d), lambda i: (i, 0))

    def modspec(j):
        return pl.BlockSpec((1, 1, d), lambda i: (geo.mod_row(i) * 6 + j, 0, 0))

    router_p = jnp.zeros((d, LANES), F32).at[:, :N_EXPERTS].set(router)
    return pl.pallas_call(
        functools.partial(_merge_kernel, moe=moe),
        grid=(geo.ntile,),
        in_specs=[rows(), rows(), rows(), rows(), full((d, d)), full((1, d)), modspec(2),
                  full((1, d)), modspec(4), modspec(3), full((d, LANES))],
        out_specs=[rows(), rows(), pl.BlockSpec((TILE, LANES), lambda i: (i, 0))],
        out_shape=[jax.ShapeDtypeStruct((n, d), F32), jax.ShapeDtypeStruct((n, d), F32 if moe else BF16),
                   jax.ShapeDtypeStruct((n, LANES), F32)],
        compiler_params=_cparams(("arbitrary",)),
        name="merge",
    )(x, yc, yr, ym, wout.astype(BF16), gpost.reshape(1, d), mod, gpre.reshape(1, d), mod, mod,
      router_p)


def _ffn_kernel(h_ref, comb_ref, w1_ref, w3_ref, w2_ref, o_ref, acc_ref):
    e = pl.program_id(1)
    f = pl.program_id(2)

    @pl.when((e == 0) & (f == 0))
    def _():
        acc_ref[...] = jnp.zeros(acc_ref.shape, F32)

    h = h_ref[...]
    a = jnp.dot(h, w1_ref[0], preferred_element_type=F32)
    b = jnp.dot(h, w3_ref[0], preferred_element_type=F32)
    comb = comb_ref[...]
    lane = lax.broadcasted_iota(jnp.int32, comb.shape, 1)
    ce = jnp.sum(jnp.where(lane == e, comb, 0.0), axis=-1, keepdims=True)
    act = (a * _sigmoid(a) * b) * ce
    acc_ref[...] += jnp.dot(act.astype(BF16), w2_ref[0], preferred_element_type=F32)

    @pl.when((e == pl.num_programs(1) - 1) & (f == pl.num_programs(2) - 1))
    def _():
        o_ref[...] = acc_ref[...]


def ffn(h, comb, w1, w3, w2, *, tm, tf):
    n, d = h.shape
    ne, _, dff = w1.shape
    return pl.pallas_call(
        _ffn_kernel,
        grid=(n // tm, ne, dff // tf),
        in_specs=[
            pl.BlockSpec((tm, d), lambda i, e, f: (i, 0)),
            pl.BlockSpec((tm, LANES), lambda i, e, f: (i, 0)),
            pl.BlockSpec((1, d, tf), lambda i, e, f: (e, 0, f)),
            pl.BlockSpec((1, d, tf), lambda i, e, f: (e, 0, f)),
            pl.BlockSpec((1, tf, d), lambda i, e, f: (e, f, 0)),
        ],
        out_specs=pl.BlockSpec((tm, d), lambda i, e, f: (i, 0)),
        out_shape=jax.ShapeDtypeStruct((n, d), F32),
        scratch_shapes=[pltpu.VMEM((tm, d), F32)],
        compiler_params=_cparams(("arbitrary", "arbitrary", "arbitrary")),
        name="ffn",
    )(h, comb, w1, w3, w2)


MOE_TM = 512


def _row_copy(src_hbm, row, dst_vmem, r, sem):
    return pltpu.make_async_copy(src_hbm.at[pl.ds(row, 1)], dst_vmem.at[pl.ds(r, 1)], sem)


DMA_UNROLL = 8


def _ffn_sorted_kernel(te_ref, nused_ref, idx_ref, nxt_ref, h_hbm, w1_ref, w3_ref, w2_ref, o_ref,
                       acc_ref, xbuf_ref, sems):
    j = pl.program_id(0)
    f = pl.program_id(1)
    nj = pl.num_programs(0)
    tm = acc_ref.shape[0]
    slot = j % 2

    def issue(ids_ref, s):
        def body(i, c):
            for k in range(DMA_UNROLL):
                r = i * DMA_UNROLL + k
                _row_copy(h_hbm, ids_ref[0, 0, r], xbuf_ref.at[s], r, sems.at[s]).start(priority=k % 2)
            return c
        lax.fori_loop(0, tm // DMA_UNROLL, body, 0)

    def wait(s):
        def body(i, c):
            for k in range(DMA_UNROLL):
                _row_copy(h_hbm, 0, xbuf_ref.at[s], i * DMA_UNROLL + k, sems.at[s]).wait()
            return c
        lax.fori_loop(0, tm // DMA_UNROLL, body, 0)

    @pl.when(f == 0)
    def _():
        @pl.when(j == 0)
        def _():
            issue(idx_ref, 0)

        @pl.when(j + 1 < nj)
        def _():
            issue(nxt_ref, 1 - slot)

        wait(slot)

    @pl.when(j < nused_ref[0])
    def _():
        @pl.when(f == 0)
        def _():
            acc_ref[...] = jnp.zeros(acc_ref.shape, F32)

        h = xbuf_ref[slot].astype(BF16)
        a = jnp.dot(h, w1_ref[0], preferred_element_type=F32)
        b = jnp.dot(h, w3_ref[0], preferred_element_type=F32)
        act = a * _sigmoid(a) * b
        acc_ref[...] += jnp.dot(act.astype(BF16), w2_ref[0], preferred_element_type=F32)

        @pl.when(f == pl.num_programs(1) - 1)
        def _():
            o_ref[...] = acc_ref[...]


def ffn_sorted(h, row_src, te, nused, w1, w3, w2, *, tm, tf):
    p = row_src.shape[0]
    d = h.shape[1]
    dff = w1.shape[2]
    ntile = p // tm
    ids = row_src.reshape(ntile, 1, tm)
    return pl.pallas_call(
        _ffn_sorted_kernel,
        grid_spec=pltpu.PrefetchScalarGridSpec(
            num_scalar_prefetch=2,
            grid=(ntile, dff // tf),
            in_specs=[
                pl.BlockSpec((1, 1, tm), lambda j, f, te, nu: (j, 0, 0), memory_space=pltpu.SMEM),
                pl.BlockSpec((1, 1, tm), lambda j, f, te, nu: (jnp.minimum(j + 1, ntile - 1), 0, 0),
                             memory_space=pltpu.SMEM),
                pl.BlockSpec(memory_space=pl.ANY),
                pl.BlockSpec((1, d, tf), lambda j, f, te, nu: (te[j], 0, f)),
                pl.BlockSpec((1, d, tf), lambda j, f, te, nu: (te[j], 0, f)),
                pl.BlockSpec((1, tf, d), lambda j, f, te, nu: (te[j], f, 0)),
            ],
            out_specs=pl.BlockSpec((tm, d), lambda j, f, te, nu: (j, 0)),
            scratch_shapes=[pltpu.VMEM((tm, d), F32), pltpu.VMEM((2, tm, d), h.dtype),
                            pltpu.SemaphoreType.DMA((2,))],
        ),
        out_shape=jax.ShapeDtypeStruct((p, d), F32),
        compiler_params=_cparams(("arbitrary", "arbitrary")),
        name="ffn_sorted",
    )(te, nused, ids, ids, h, w1, w3, w2)


def _moe_combine_kernel(d0_ref, d1_ref, ys_hbm, route_ref, x_ref, g_ref, g2_ref, o_ref,
                        y0_ref, y1_ref, sem0, sem1):
    def start(i, c):
        for k in range(DMA_UNROLL):
            r = i * DMA_UNROLL + k
            _row_copy(ys_hbm, d0_ref[0, 0, r], y0_ref, r, sem0).start(priority=0)
            _row_copy(ys_hbm, d1_ref[0, 0, r], y1_ref, r, sem1).start(priority=1)
        return c

    def wait(i, c):
        for k in range(DMA_UNROLL):
            r = i * DMA_UNROLL + k
            _row_copy(ys_hbm, 0, y0_ref, r, sem0).wait()
            _row_copy(ys_hbm, 0, y1_ref, r, sem1).wait()
        return c

    lax.fori_loop(0, TILE // DMA_UNROLL, start, 0)
    lax.fori_loop(0, TILE // DMA_UNROLL, wait, 0)
    route = route_ref[...]
    lane = lax.broadcasted_iota(jnp.int32, route.shape, 1)
    ga = jnp.sum(jnp.where(lane == R_G1, route, 0.0), axis=-1, keepdims=True)
    gb = jnp.sum(jnp.where(lane == R_G2, route, 0.0), axis=-1, keepdims=True)
    y = ga * y0_ref[...] + gb * y1_ref[...]
    o_ref[...] = x_ref[...] + g2_ref[0] * _rms(y, g_ref[...])


def moe_combine_post(x1, ys, dest, route, gpost, mod, geo):
    n, d = x1.shape
    dd = dest.reshape(2, n // TILE, 1, TILE)
    return pl.pallas_call(
        _moe_combine_kernel,
        grid=(geo.ntile,),
        in_specs=[
            pl.BlockSpec((1, 1, TILE), lambda i: (i, 0, 0), memory_space=pltpu.SMEM),
            pl.BlockSpec((1, 1, TILE), lambda i: (i, 0, 0), memory_space=pltpu.SMEM),
            pl.BlockSpec(memory_space=pl.ANY),
            pl.BlockSpec((TILE, LANES), lambda i: (i, 0)),
            pl.BlockSpec((TILE, d), lambda i: (i, 0)),
            pl.BlockSpec((1, d), lambda i: (0, 0)),
            pl.BlockSpec((1, 1, d), lambda i: (geo.mod_row(i) * 6 + 5, 0, 0)),
        ],
        out_specs=pl.BlockSpec((TILE, d), lambda i: (i, 0)),
        out_shape=jax.ShapeDtypeStruct((n, d), F32),
        scratch_shapes=[pltpu.VMEM((TILE, d), F32), pltpu.VMEM((TILE, d), F32),
                        pltpu.SemaphoreType.DMA(()), pltpu.SemaphoreType.DMA(())],
        compiler_params=_cparams(("arbitrary",)),
        name="moe_combine",
    )(dd[0], dd[1], ys, route, x1, gpost.reshape(1, d), mod)


def _route_plan(route, tm):
    n = route.shape[0]
    e_flat = jnp.concatenate([route[:, R_I1], route[:, R_I2]]).astype(jnp.int32)
    onehot = (e_flat[:, None] == jnp.arange(N_EXPERTS, dtype=jnp.int32)[None, :]).astype(jnp.int32)
    csum = jnp.cumsum(onehot, axis=0)
    rank = jnp.take_along_axis(csum, e_flat[:, None], axis=1)[:, 0] - 1
    counts = csum[-1]
    ptiles = (counts + tm - 1) // tm
    tile_end = jnp.cumsum(ptiles)
    gstart = (tile_end - ptiles) * tm
    dest = jnp.take(gstart, e_flat) + rank
    p_max = 2 * n + N_EXPERTS * tm
    tok = jnp.tile(jnp.arange(n, dtype=jnp.int32), 2)
    row_src = jnp.zeros((p_max,), jnp.int32).at[dest].set(tok)
    tiles = jnp.arange(p_max // tm, dtype=jnp.int32)
    te = jnp.minimum(jnp.searchsorted(tile_end, tiles, side="right"), N_EXPERTS - 1).astype(jnp.int32)
    return row_src, dest.astype(jnp.int32).reshape(2, n), te, tile_end[-1:].astype(jnp.int32)


def _ffn_post_kernel(x_ref, y_ref, g_ref, g2_ref, o_ref):
    o_ref[...] = x_ref[...] + g2_ref[0] * _rms(y_ref[...], g_ref[...])


def ffn_post(x1, y, gpost, mod, geo):
    n, d = x1.shape
    return pl.pallas_call(
        _ffn_post_kernel,
        grid=(geo.ntile,),
        in_specs=[
            pl.BlockSpec((TILE, d), lambda i: (i, 0)),
            pl.BlockSpec((TILE, d), lambda i: (i, 0)),
            pl.BlockSpec((1, d), lambda i: (0, 0)),
            pl.BlockSpec((1, 1, d), lambda i: (geo.mod_row(i) * 6 + 5, 0, 0)),
        ],
        out_specs=pl.BlockSpec((TILE, d), lambda i: (i, 0)),
        out_shape=jax.ShapeDtypeStruct((n, d), F32),
        compiler_params=_cparams(("arbitrary",)),
        name="ffn_post",
    )(x1, y, gpost.reshape(1, d), mod)


def _ctx_scan(geo, pre):
    r, kk, v, w0, w1, ka0, ka1, kt0, kt1 = pre
    H, K = N_RWKV_HEADS, RWKV_HEAD
    bl = LANES // H
    nbh = geo.n_ctx // bl

    def to_scan(a):
        return jnp.swapaxes(a.reshape(nbh, geo.t_ctx, LANES, K), 2, 3)

    def from_scan(o):
        return jnp.swapaxes(o, 2, 3).reshape(nbh * geo.t_ctx, bl * H * K)

    def state(sf):
        sf = sf.reshape(nbh, K // SUBLANES, K, SUBLANES, bl, H).transpose(0, 4, 5, 1, 3, 2)
        return sf.reshape(geo.n_ctx, H, K, K)

    kk_s, r_s, v_s = to_scan(kk), to_scan(r), to_scan(v)
    tc = min(32, geo.t_ctx)
    o_f, sf_f = wkv_scan_ctx(to_scan(w0), kk_s, to_scan(ka0), to_scan(kt0), r_s, v_s, tc=tc, rev=False)
    o_b, sf_b = wkv_scan_ctx(to_scan(w1), kk_s, to_scan(ka1), to_scan(kt1), r_s, v_s, tc=tc, rev=True)
    return from_scan(o_f), from_scan(o_b), jnp.stack([state(sf_f), state(sf_b)], axis=1)


def _lat_scan(geo, pre, state_l):
    r, kk, v, w0, w1, ka0, ka1, kt0, kt1 = pre
    H, K = N_RWKV_HEADS, RWKV_HEAD
    nb, t_len = geo.n_lat, geo.t_lat
    hp = H // 2
    kl = LANES // (nb * 2 * H)
    nk = K // kl

    def pair(a0, a1):
        a = jnp.stack([a0.reshape(t_len, hp, nb, LANES), a1.reshape(t_len, hp, nb, LANES)], axis=3)
        return jnp.swapaxes(a.reshape(t_len, LANES, nk), 1, 2)

    xs = (pair(w0, w1), pair(kk, kk), pair(ka0, ka1), pair(kt0, kt1), pair(r, r))
    vl = v.reshape(nb, t_len, hp, 1, 2, 1, K)
    vl = jnp.broadcast_to(vl, (nb, t_len, hp, 2, 2, kl, K))
    v_lat = vl.transpose(1, 6, 2, 0, 3, 4, 5).reshape(t_len, K, LANES)
    s0 = state_l.reshape(nb, 2, hp, 2, K // SUBLANES, SUBLANES, kl, nk)
    s0 = s0.transpose(4, 7, 5, 2, 0, 1, 3, 6).reshape(K // SUBLANES, nk, SUBLANES, LANES)
    oa, ob = wkv_scan_lat(xs, v_lat, s0, tc=min(64, t_len), kl=kl)

    def direction(o, d):
        o = o.reshape(t_len, K, hp, nb, 2, 2, kl)[:, :, :, :, d].sum(axis=-1)
        return o.transpose(3, 0, 2, 4, 1).reshape(geo.nl_tok, H * K)

    return direction(oa, 0), direction(ob, 1)


def _rope_tables(geo):
    n_freq = ROPE_DIM // 4
    rows = geo.t_lat // GRID_W
    row = jnp.repeat(jnp.arange(rows, dtype=F32), GRID_W)
    col = jnp.tile(jnp.arange(GRID_W, dtype=F32), rows)
    inv = ROPE_BASE ** (-jnp.arange(n_freq, dtype=F32) / n_freq)
    ang = jnp.concatenate([row[:, None] * inv, col[:, None] * inv], axis=-1)
    cos, sin = jnp.cos(ang), jnp.sin(ang)
    cos32 = jnp.concatenate([cos, cos], axis=-1)
    sin32 = jnp.concatenate([-sin, sin], axis=-1)
    cos_t = jnp.concatenate([jnp.ones((TILE, ROPE_DIM), F32), cos32], axis=0)
    sin_t = jnp.concatenate([jnp.zeros((TILE, ROPE_DIM), F32), sin32], axis=0)
    pad = ((0, 0), (0, LANES - ROPE_DIM))
    return jnp.pad(cos_t, pad), jnp.pad(sin_t, pad)


_DEINT = np.concatenate([np.arange(0, ROPE_DIM, 2), np.arange(1, ROPE_DIM, 2)])
_DEINT_SW = np.concatenate([np.arange(1, ROPE_DIM, 2), np.arange(0, ROPE_DIM, 2)])


def _prep_w_in(w_in, b_in):
    offs = np.cumsum([0, 2 * D_CONV, D_RWKV, D_RWKV, D_RWKV, W_RANK, A_RANK, G_RANK, Q_RANK, KV_RANK,
                      ROPE_DIM, N_BRANCH * D_MODEL])
    o_conv, o_r, o_k, o_v, o_dw, o_da, o_dg, o_cq, o_ckv, o_kr, o_gate, _ = offs
    idx = np.zeros((IN_PAD,), np.int32)
    valid = np.zeros((IN_PAD,), bool)

    def put(dst, src):
        idx[dst:dst + len(src)] = src
        valid[dst:dst + len(src)] = True

    put(C_CONV, np.arange(o_conv, o_conv + 2 * D_CONV))
    for b, c in enumerate((C_G0, C_G1, C_G2)):
        put(c, np.arange(o_gate + b * D_MODEL, o_gate + (b + 1) * D_MODEL))
    put(C_R, np.arange(o_r, o_r + D_RWKV))
    put(C_K, np.arange(o_k, o_k + D_RWKV))
    put(C_V, np.arange(o_v, o_v + D_RWKV))
    put(C_LORA, np.arange(o_dw, o_dw + W_RANK + A_RANK + G_RANK))
    put(C_CQ, np.arange(o_cq, o_cq + Q_RANK))
    put(C_CKV, np.arange(o_ckv, o_ckv + KV_RANK))
    put(C_KR, np.arange(o_kr, o_kr + ROPE_DIM))
    put(C_KR + ROPE_DIM, o_kr + _DEINT)
    put(C_KR + 2 * ROPE_DIM, o_kr + _DEINT_SW)
    w = jnp.where(valid[None, :], jnp.take(w_in, idx, axis=1), 0.0).astype(BF16)
    b = jnp.where(valid, jnp.take(b_in, idx), 0.0)
    return w, b


def _prep_wuq(wuq):
    hd = NOPE_DIM + ROPE_DIM
    nq = N_MLA_HEADS * NOPE_DIM
    ncol = nq + 2 * N_MLA_HEADS * LANES
    idx = np.zeros((ncol,), np.int32)
    valid = np.zeros((ncol,), bool)
    for h in range(N_MLA_HEADS):
        idx[h * NOPE_DIM:(h + 1) * NOPE_DIM] = h * hd + np.arange(NOPE_DIM)
        valid[h * NOPE_DIM:(h + 1) * NOPE_DIM] = True
        for blk, perm in enumerate((_DEINT, _DEINT_SW)):
            c0 = nq + blk * N_MLA_HEADS * LANES + h * LANES
            idx[c0:c0 + ROPE_DIM] = h * hd + NOPE_DIM + perm
            valid[c0:c0 + ROPE_DIM] = True
    return jnp.where(valid[None, :], jnp.take(wuq, idx, axis=1), 0.0).astype(BF16)


def _block_diag_heads(wuv):
    w3 = wuv.reshape(KV_RANK, N_MLA_HEADS, V_DIM)
    eye = jnp.eye(N_MLA_HEADS, dtype=wuv.dtype)
    bd = w3.transpose(1, 0, 2)[:, :, None, :] * eye[:, None, :, None]
    return bd.reshape(N_MLA_HEADS * KV_RANK, N_MLA_HEADS * V_DIM).astype(BF16)


def _attention_keys(geo, ckvn, krr, cache_c, cache_kr):
    nc = geo.nc_tok
    pad = QK_W - KV_RANK - ROPE_DIM

    def keys(c, kr):
        kc = jnp.concatenate([c, kr, jnp.zeros(c.shape[:2] + (pad,), F32)], axis=-1).astype(BF16)
        return kc.transpose(0, 2, 1), c.astype(BF16)

    kr32 = krr[:, :ROPE_DIM]
    kct_c, cv_c = keys(ckvn[:nc].reshape(geo.n_ctx, geo.t_ctx, KV_RANK),
                       kr32[:nc].reshape(geo.n_ctx, geo.t_ctx, ROPE_DIM))
    c_l = jnp.concatenate([ckvn[nc:].reshape(geo.n_lat, geo.t_lat, KV_RANK), cache_c], axis=1)
    kr_l = jnp.concatenate([kr32[nc:].reshape(geo.n_lat, geo.t_lat, ROPE_DIM), cache_kr], axis=1)
    kct_l, cv_l = keys(c_l, kr_l)
    return kct_c, cv_c, kct_l, cv_l


def kernel(x_prompt, x_sample, cache_ckv, cache_krope, state_wkv, c, c_ctx, ada_w, ada_b, norm_mix_pre, norm_mix_post, norm_ffn_pre, norm_ffn_post, w_in, b_in, conv_w, conv_b, conv_ln_g, conv_ln_b, conv_wo, rwkv_mu, rwkv_w0, rwkv_bw, rwkv_a0, rwkv_ba, rwkv_bg, rwkv_xi, rwkv_alpha, rwkv_rho, rwkv_gn_g, rwkv_gn_b, rwkv_wo, mla_q_norm, mla_wuq, mla_kv_norm, mla_wuk, mla_wuv, mla_wo, w_out, ffn_w1, ffn_w3, ffn_w2, moe_router, moe_w1, moe_w3, moe_w2):
    P = dict(rwkv_mu=rwkv_mu, rwkv_w0=rwkv_w0, rwkv_bw=rwkv_bw, rwkv_a0=rwkv_a0, rwkv_ba=rwkv_ba,
             rwkv_bg=rwkv_bg, rwkv_xi=rwkv_xi, rwkv_alpha=rwkv_alpha, rwkv_rho=rwkv_rho)
    n_ctx, t_ctx, d = x_prompt.shape
    n_lat, t_lat, _ = x_sample.shape
    depth = ada_w.shape[0]
    past = cache_ckv.shape[2]
    geo = Geo(n_ctx, t_ctx, n_lat, t_lat, past)
    assert d == D_MODEL and (n_ctx * N_RWKV_HEADS) % LANES == 0 and LANES % (n_lat * 2 * N_RWKV_HEADS) == 0

    x = jnp.concatenate([x_prompt.reshape(-1, d), x_sample.reshape(-1, d)], axis=0)

    n_mod = 1 + n_lat
    n_mod_pad = -(-n_mod // SUBLANES) * SUBLANES
    c_all = jnp.zeros((n_mod_pad, d), F32).at[0].set(c_ctx).at[1:n_mod].set(c)
    head_id = np.arange(D_RWKV) // RWKV_HEAD
    ones_bd = jnp.asarray((head_id[:, None] == head_id[None, :]).astype(np.float32))
    cos_t, sin_t = _rope_tables(geo)

    ckv_out, kr_out, st_out = [], [], []
    for l in range(depth):
        mod = matmul_bias(c_all, ada_w[l].astype(BF16), ada_b[l], tm=n_mod_pad, tn=6 * d // 4,
                          pre="silu", name="ada_mod")
        mod = mod[:n_mod].reshape(n_mod * 6, 1, d)

        h = prenorm(x, norm_mix_pre[l], mod, geo, j_shift=0, j_scale=1)
        w_in_p, b_in_p = _prep_w_in(w_in[l], b_in[l])
        u = matmul_bias(h, w_in_p, b_in_p, tm=math.gcd(1024, geo.ntok), tn=1280, name="in_proj")

        y_conv = conv_branch(u, conv_w[l], conv_b[l], conv_ln_g[l], conv_ln_b[l], conv_wo[l], geo)

        ctx_block = (LANES // N_RWKV_HEADS, geo.ct)
        pre_c = rwkv_pre(u, P, l, geo, ones_bd, 0, geo.nct, seq_block=ctx_block)
        pre_l = rwkv_pre(u, P, l, geo, ones_bd, geo.nct, geo.nlt, lane_tiled=geo.lt)
        oc_f, oc_b, sfin = _ctx_scan(geo, pre_c[:9])
        ol_f, ol_b = _lat_scan(geo, pre_l[:9], state_wkv[:, l])
        y_rwkv = rwkv_post(ol_f, ol_b, pre_l[10], pre_l[9], u, rwkv_gn_g[l], rwkv_gn_b[l], ones_bd,
                           rwkv_wo[l], None, geo.nct)
        y_rwkv = rwkv_post(oc_f, oc_b, pre_c[10], pre_c[9], u, rwkv_gn_g[l], rwkv_gn_b[l], ones_bd,
                           rwkv_wo[l], y_rwkv, 0, seq_block=ctx_block)

        wukt = mla_wuk[l].reshape(KV_RANK, N_MLA_HEADS, NOPE_DIM).transpose(1, 2, 0).astype(BF16)
        qf, ckvn, krr = mla_pre(u, cos_t, sin_t, mla_q_norm[l], mla_kv_norm[l],
                                _prep_wuq(mla_wuq[l]), wukt, geo)
        kct_c, cv_c, kct_l, cv_l = _attention_keys(geo, ckvn, krr, cache_ckv[:, l],
                                                   cache_krope[:, l][..., _DEINT])
        wuv_bd = _block_diag_heads(mla_wuv[l])
        wo_b = mla_wo[l].astype(BF16)
        y_mla = attention(qf, kct_l, cv_l, wuv_bd, wo_b, u, None, row0=geo.nc_tok, t_seq=t_lat,
                          tq=TILE, name="attention_lat")
        y_mla = attention(qf, kct_c, cv_c, wuv_bd, wo_b, u, y_mla, row0=0, t_seq=t_ctx,
                          tq=TILE, name="attention_ctx")

        moe = (l % 2 == 1)
        i = l // 2
        router = moe_router[i] if moe else jnp.zeros((d, N_EXPERTS), F32)
        x1, h2, comb = merge(x, y_conv, y_rwkv, y_mla, w_out[l], norm_mix_post[l], norm_ffn_pre[l],
                             mod, router, geo, moe)
        if moe:
            row_src, dest, te, nused = _route_plan(comb, MOE_TM)
            ys = ffn_sorted(h2, row_src, te, nused, moe_w1[i].astype(BF16), moe_w3[i].astype(BF16),
                            moe_w2[i].astype(BF16), tm=MOE_TM, tf=D_FF // 2)
            x = moe_combine_post(x1, ys, dest, comb, norm_ffn_post[l], mod, geo)
        else:
            y = ffn(h2, comb, ffn_w1[i:i + 1].astype(BF16), ffn_w3[i:i + 1].astype(BF16),
                    ffn_w2[i:i + 1].astype(BF16), tm=512, tf=D_FF // 2)
            x = ffn_post(x1, y, norm_ffn_post[l], mod, geo)

        ckv_out.append(ckvn[:geo.nc_tok].reshape(n_ctx, t_ctx, KV_RANK))
        kr_out.append(u[:geo.nc_tok, C_KR:C_KR + ROPE_DIM].reshape(n_ctx, t_ctx, ROPE_DIM))
        st_out.append(sfin)

    y_prompt = x[:geo.nc_tok].reshape(n_ctx, t_ctx, d)
    y_sample = x[geo.nc_tok:].reshape(n_lat, t_lat, d)
    return (y_prompt, y_sample, jnp.stack(ckv_out, axis=1), jnp.stack(kr_out, axis=1),
            jnp.stack(st_out, axis=1))
```

```python
import functools
import math

import numpy as np
import jax
import jax.numpy as jnp
from jax import lax
from jax.experimental import pallas as pl
from jax.experimental.pallas import tpu as pltpu

F32 = jnp.float32
BF16 = jnp.bfloat16

D_MODEL = 1024
GRID_W = 64
D_CONV = 512
CONV_K = 31
D_RWKV = 512
RWKV_HEAD = 64
N_RWKV_HEADS = D_RWKV // RWKV_HEAD
W_RANK = 64
A_RANK = 64
G_RANK = 128
DECAY_SCALE = math.exp(-0.5)
GN_EPS = 64e-5
N_MLA_HEADS = 8
Q_RANK = 256
KV_RANK = 128
NOPE_DIM = 64
ROPE_DIM = 32
V_DIM = 64
ROPE_BASE = 10000.0
ATTN_SCALE = 1.0 / math.sqrt(NOPE_DIM + ROPE_DIM)
N_BRANCH = 3
D_FF = 2816
N_EXPERTS = 8
EPS = 1e-6

LANES = 128
SUBLANES = 8
VMEM_LIMIT_BYTES = 56 * 1024 * 1024

TILE = 256
CONV_HALO = 16
SHIFT_HALO = 8

C_CONV, C_G0, C_G1, C_G2 = 0, 1024, 2048, 3072
C_R, C_K, C_V = 4096, 4608, 5120
C_LORA = 5632
C_CQ = 5888
C_CKV = 6144
C_KR = 6272
IN_PAD = 6400


def _cparams(sem):
    return pltpu.CompilerParams(dimension_semantics=sem, vmem_limit_bytes=VMEM_LIMIT_BYTES)


def _sigmoid(x):
    return jax.nn.sigmoid(x)


def _rms(x, g):
    return (x * lax.rsqrt(jnp.mean(x * x, axis=-1, keepdims=True) + EPS)) * g


class Geo:
    def __init__(self, n_ctx, t_ctx, n_lat, t_lat, past):
        assert t_ctx % TILE == 0 and t_lat % TILE == 0
        self.n_ctx, self.t_ctx, self.n_lat, self.t_lat, self.past = n_ctx, t_ctx, n_lat, t_lat, past
        self.ct = t_ctx // TILE
        self.lt = t_lat // TILE
        self.nct = n_ctx * self.ct
        self.nlt = n_lat * self.lt
        self.ntile = self.nct + self.nlt
        self.nc_tok = n_ctx * t_ctx
        self.nl_tok = n_lat * t_lat
        self.ntok = self.nc_tok + self.nl_tok

    def pos(self, i):
        is_ctx = i < self.nct
        p = jnp.where(is_ctx, i % self.ct, (i - self.nct) % self.lt)
        n = jnp.where(is_ctx, self.ct, self.lt)
        return p, n

    def mod_row(self, i):
        return jnp.where(i < self.nct, 0, 1 + (i - self.nct) // self.lt)

    def rope_blk(self, i):
        return jnp.where(i < self.nct, 0, 1 + (i - self.nct) % self.lt)


def _mm_kernel(x_ref, w_ref, b_ref, o_ref, *, pre):
    x = x_ref[...]
    if pre == "silu":
        x = x.astype(F32)
        x = x * _sigmoid(x)
    acc = jnp.dot(x.astype(BF16), w_ref[...], preferred_element_type=F32)
    o_ref[...] = (acc + b_ref[...]).astype(o_ref.dtype)


def matmul_bias(x, w, b, *, tm, tn, pre=None, out_dtype=F32, name="matmul"):
    m, k = x.shape
    n = w.shape[1]
    assert m % tm == 0 and n % tn == 0
    return pl.pallas_call(
        functools.partial(_mm_kernel, pre=pre),
        grid=(n // tn, m // tm),
        in_specs=[
            pl.BlockSpec((tm, k), lambda j, i: (i, 0)),
            pl.BlockSpec((k, tn), lambda j, i: (0, j)),
            pl.BlockSpec((1, tn), lambda j, i: (0, j)),
        ],
        out_specs=pl.BlockSpec((tm, tn), lambda j, i: (i, j)),
        out_shape=jax.ShapeDtypeStruct((m, n), out_dtype),
        compiler_params=_cparams(("arbitrary", "arbitrary")),
        name=name,
    )(x, w, b.reshape(1, n).astype(F32))


def _prenorm_kernel(x_ref, g_ref, sc_ref, sh_ref, o_ref):
    h = _rms(x_ref[...], g_ref[...]) * (1.0 + sc_ref[0]) + sh_ref[0]
    o_ref[...] = h.astype(o_ref.dtype)


def prenorm(x, g, mod, geo, j_shift, j_scale):
    n, d = x.shape
    return pl.pallas_call(
        _prenorm_kernel,
        grid=(geo.ntile,),
        in_specs=[
            pl.BlockSpec((TILE, d), lambda i: (i, 0)),
            pl.BlockSpec((1, d), lambda i: (0, 0)),
            pl.BlockSpec((1, 1, d), lambda i: (geo.mod_row(i) * 6 + j_scale, 0, 0)),
            pl.BlockSpec((1, 1, d), lambda i: (geo.mod_row(i) * 6 + j_shift, 0, 0)),
        ],
        out_specs=pl.BlockSpec((TILE, d), lambda i: (i, 0)),
        out_shape=jax.ShapeDtypeStruct((n, d), BF16),
        compiler_params=_cparams(("arbitrary",)),
        name="prenorm",
    )(x, g.reshape(1, d), mod, mod)


def _conv_kernel(cur_ref, prev_ref, next_ref, gate_ref, cw_ref, cb_ref, lng_ref, lnb_ref,
                 wo_ref, o_ref, hp_ref, *, geo):
    i = pl.program_id(0)
    p, n = geo.pos(i)
    has_prev = p > 0
    has_next = p < n - 1

    def glu(z):
        return z[:, :D_CONV] * _sigmoid(z[:, D_CONV:])

    hp_ref[0, 0:CONV_HALO, :] = jnp.where(has_prev, glu(prev_ref[...]), 0.0)
    hp_ref[0, CONV_HALO:CONV_HALO + TILE, :] = glu(cur_ref[...])
    hp_ref[0, CONV_HALO + TILE:, :] = jnp.where(has_next, glu(next_ref[...]), 0.0)
    nrow = TILE + 2 * CONV_HALO - SUBLANES
    for b in range(1, SUBLANES):
        hp_ref[b, 0:nrow, :] = hp_ref[0, pl.ds(b, nrow), :]

    off = CONV_HALO - CONV_K // 2
    acc = None
    for j in range(CONV_K):
        q = off + j
        term = hp_ref[q % SUBLANES, pl.ds(q - q % SUBLANES, TILE), :] * cw_ref[j:j + 1, :]
        acc = term if acc is None else acc + term
    h = acc + cb_ref[...]
    mu = jnp.mean(h, axis=-1, keepdims=True)
    hc = h - mu
    var = jnp.mean(hc * hc, axis=-1, keepdims=True)
    h = hc * lax.rsqrt(var + EPS) * lng_ref[...] + lnb_ref[...]
    h = h * _sigmoid(h)
    y = jnp.dot(h.astype(BF16), wo_ref[...], preferred_element_type=F32)
    o_ref[...] = _sigmoid(gate_ref[...]) * y


def conv_branch(u, cw, cb, lng, lnb, wo, geo):
    n = u.shape[0]
    hb = TILE // CONV_HALO
    nhalo = n // CONV_HALO
    cwp = jnp.zeros((32, D_CONV), F32).at[:CONV_K].set(cw)
    return pl.pallas_call(
        functools.partial(_conv_kernel, geo=geo),
        grid=(geo.ntile,),
        in_specs=[
            pl.BlockSpec((TILE, 2 * D_CONV), lambda i: (i, C_CONV // (2 * D_CONV))),
            pl.BlockSpec((CONV_HALO, 2 * D_CONV), lambda i: (jnp.maximum(i * hb - 1, 0), 0)),
            pl.BlockSpec((CONV_HALO, 2 * D_CONV), lambda i: (jnp.minimum((i + 1) * hb, nhalo - 1), 0)),
            pl.BlockSpec((TILE, D_MODEL), lambda i: (i, C_G0 // D_MODEL)),
            pl.BlockSpec((32, D_CONV), lambda i: (0, 0)),
            pl.BlockSpec((1, D_CONV), lambda i: (0, 0)),
            pl.BlockSpec((1, D_CONV), lambda i: (0, 0)),
            pl.BlockSpec((1, D_CONV), lambda i: (0, 0)),
            pl.BlockSpec((D_CONV, D_MODEL), lambda i: (0, 0)),
        ],
        out_specs=pl.BlockSpec((TILE, D_MODEL), lambda i: (i, 0)),
        out_shape=jax.ShapeDtypeStruct((n, D_MODEL), F32),
        scratch_shapes=[pltpu.VMEM((SUBLANES, TILE + 2 * CONV_HALO, D_CONV), F32)],
        compiler_params=_cparams(("arbitrary",)),
        name="conv_branch",
    )(u, u, u, u, cwp, cb.reshape(1, -1), lng.reshape(1, -1), lnb.reshape(1, -1), wo.astype(BF16))


def _seg_sum(x, ones_bd):
    return jnp.dot(x, ones_bd, preferred_element_type=F32, precision=lax.Precision.HIGHEST)


def _rwkv_pre_kernel(r_ref, rp_ref, rn_ref, k_ref, kp_ref, kn_ref, v_ref, vp_ref, vn_ref,
                     lora_ref, mu_ref, w0_ref, bw_ref, a0_ref, ba_ref, bg_ref, xi_ref, al_ref,
                     rho_ref, ones_ref, *outs, geo, tile0):
    if len(outs) == 11:
        r_o, kk_o, v_o, w0_o, w1_o, ka0_o, ka1_o, kt0_o, kt1_o, g_o, bonus_o = outs
    else:
        r_o, kk_o, v_o, w0_o, ka0_o, kt0_o, g_o, bonus_o = outs
        w1_o, ka1_o, kt1_o = w0_o, ka0_o, kt0_o
    i = pl.program_id(0) + tile0
    p, n = geo.pos(i)
    has_prev = p > 0
    has_next = p < n - 1
    row = lax.broadcasted_iota(jnp.int32, (TILE, D_RWKV), 0)

    def put(o_ref, val, dirs=(0, 1)):
        if len(o_ref.shape) == 2:
            o_ref[...] = val
        else:
            for q in range(o_ref.shape[1]):
                for d in dirs:
                    o_ref[:, q, d * LANES:(d + 1) * LANES] = val[:, q * LANES:(q + 1) * LANES]

    def shifted(c_ref, p_ref, n_ref, mu):
        cur = c_ref[...]
        pv = jnp.where(has_prev, p_ref[SHIFT_HALO - 1:SHIFT_HALO, :], 0.0)
        nx = jnp.where(has_next, n_ref[0:1, :], 0.0)
        prev = jnp.where(row == 0, pv, pltpu.roll(cur, 1, axis=0))
        nxt = jnp.where(row == TILE - 1, nx, pltpu.roll(cur, TILE - 1, axis=0))
        return cur + mu * (0.5 * (prev + nxt) - cur)

    r = shifted(r_ref, rp_ref, rn_ref, mu_ref[0:1, :])
    k = shifted(k_ref, kp_ref, kn_ref, mu_ref[1:2, :])
    v = shifted(v_ref, vp_ref, vn_ref, mu_ref[2:3, :])

    lora = lora_ref[...]
    dw = jnp.tanh(lora[:, :W_RANK]).astype(BF16)
    da = lora[:, W_RANK:W_RANK + A_RANK].astype(BF16)
    dg = _sigmoid(lora[:, W_RANK + A_RANK:]).astype(BF16)
    wl = jnp.dot(dw, bw_ref[...], preferred_element_type=F32)
    al = jnp.dot(da, ba_ref[...], preferred_element_type=F32)
    g = jnp.dot(dg, bg_ref[...], preferred_element_type=F32)

    ones_bd = ones_ref[...]
    kx = k * xi_ref[...]
    kk = kx * lax.rsqrt(_seg_sum(kx * kx, ones_bd) + EPS)
    alpha = al_ref[...]
    rho = rho_ref[...]
    bonus = jnp.zeros((TILE, D_RWKV), F32)
    for d, (w_o, ka_o, kt_o) in enumerate(((w0_o, ka0_o, kt0_o), (w1_o, ka1_o, kt1_o))):
        sl = slice(d * D_RWKV, (d + 1) * D_RWKV)
        w = jnp.exp(-DECAY_SCALE * _sigmoid(w0_ref[d:d + 1, :] + wl[:, sl]))
        a = _sigmoid(a0_ref[d:d + 1, :] + al[:, sl])
        kt = k * (1.0 + (a - 1.0) * alpha)
        put(w_o, w, (d,))
        put(ka_o, kk * a, (d,))
        put(kt_o, kt, (d,))
        bonus = bonus + _seg_sum(r * kt * rho, ones_bd) * v
    put(r_o, r)
    put(kk_o, kk)
    v_o[...] = v
    g_o[...] = g
    bonus_o[...] = bonus


def rwkv_pre(u, P, l, geo, ones_bd, tile0, ntiles, seq_block=None, lane_tiled=None):
    hb = TILE // SHIFT_HALO
    nhalo = u.shape[0] // SHIFT_HALO

    def trio(c0):
        cb = c0 // D_RWKV
        return [
            pl.BlockSpec((TILE, D_RWKV), lambda i: (i + tile0, cb)),
            pl.BlockSpec((SHIFT_HALO, D_RWKV), lambda i: (jnp.maximum((i + tile0) * hb - 1, 0), cb)),
            pl.BlockSpec((SHIFT_HALO, D_RWKV), lambda i: (jnp.minimum((i + tile0 + 1) * hb, nhalo - 1), cb)),
        ]

    def full(shape):
        return pl.BlockSpec(shape, lambda i: tuple(0 for _ in shape))

    bw = jnp.concatenate([P["rwkv_bw"][l, 0], P["rwkv_bw"][l, 1]], axis=1).astype(BF16)
    ba = jnp.concatenate([P["rwkv_ba"][l, 0], P["rwkv_ba"][l, 1]], axis=1).astype(BF16)
    out = jax.ShapeDtypeStruct((ntiles * TILE, D_RWKV), F32)
    tok_spec = pl.BlockSpec((TILE, D_RWKV), lambda i: (i, 0))
    if seq_block is None:
        scan_spec, scan_out = tok_spec, out
    else:
        bl, ts = seq_block
        scan_out = jax.ShapeDtypeStruct((ntiles * TILE // bl, bl * D_RWKV), F32)
        scan_spec = pl.BlockSpec((TILE, D_RWKV),
                                 lambda i: ((i // ts // bl) * ts + i % ts, (i // ts) % bl))
    out_specs = [scan_spec] * 9 + [tok_spec] * 2
    out_shape = [scan_out] * 9 + [out] * 2
    if lane_tiled is not None:
        nlt = D_RWKV // LANES
        nseq = ntiles // lane_tiled
        k_spec = pl.BlockSpec((TILE, nlt, 2 * LANES), lambda i: (i % lane_tiled, 0, i // lane_tiled))
        k_out = jax.ShapeDtypeStruct((lane_tiled * TILE, nlt, nseq * 2 * LANES), F32)
        out_specs = [k_spec, k_spec, tok_spec, k_spec, k_spec, k_spec, tok_spec, tok_spec]
        out_shape = [k_out, k_out, out, k_out, k_out, k_out, out, out]
    return pl.pallas_call(
        functools.partial(_rwkv_pre_kernel, geo=geo, tile0=tile0),
        grid=(ntiles,),
        in_specs=trio(C_R) + trio(C_K) + trio(C_V) + [
            pl.BlockSpec((TILE, 256), lambda i: (i + tile0, C_LORA // 256)),
            full((3, D_RWKV)), full((2, D_RWKV)), full((W_RANK, 2 * D_RWKV)),
            full((2, D_RWKV)), full((A_RANK, 2 * D_RWKV)), full((G_RANK, D_RWKV)),
            full((1, D_RWKV)), full((1, D_RWKV)), full((1, D_RWKV)), full((D_RWKV, D_RWKV)),
        ],
        out_specs=out_specs,
        out_shape=out_shape,
        compiler_params=_cparams(("arbitrary",)),
        name="rwkv_pre",
    )(u, u, u, u, u, u, u, u, u, u,
      P["rwkv_mu"][l], P["rwkv_w0"][l], bw, P["rwkv_a0"][l], ba, P["rwkv_bg"][l].astype(BF16),
      P["rwkv_xi"][l].reshape(1, -1), P["rwkv_alpha"][l].reshape(1, -1),
      P["rwkv_rho"][l].reshape(1, -1), ones_bd)


N_VBLK = RWKV_HEAD // SUBLANES


def _scan_steps(xrow, vload, ostore, s_ref, tidx, *, tc, nk, kl, ngroups=2, unroll=1):
    gsz = N_VBLK // ngroups
    groups = tuple(tuple(range(g * gsz, (g + 1) * gsz)) for g in range(ngroups))

    def allred(a):
        if kl == 1:
            return a
        q = lax.broadcasted_iota(jnp.int32, a.shape, 1) % kl
        sh = 1
        while sh < kl:
            a = a + jnp.where((q // sh) % 2 == 1, pltpu.roll(a, sh, axis=1),
                              pltpu.roll(a, LANES - sh, axis=1))
            sh *= 2
        return a

    def first_sa(grp, t):
        acc = [None] * len(grp)
        for kh in range(nk):
            kkb = xrow(t, 1, kh)
            for j, vb in enumerate(grp):
                pr = s_ref[vb, kh] * kkb
                acc[j] = pr if acc[j] is None else acc[j] + pr
        return tuple(allred(a) for a in acc)

    def fused(grp, s, t, t_next, sa):
        vv = [vload(t, vb) for vb in grp]
        oacc = [None] * len(grp)
        acc = [None] * len(grp)
        for kh in range(nk):
            wb, kab, ktb, rb = xrow(t, 0, kh), xrow(t, 2, kh), xrow(t, 3, kh), xrow(t, 4, kh)
            kkn = xrow(t_next, 1, kh)
            for j, vb in enumerate(grp):
                sn = s_ref[vb, kh] * wb - sa[j] * kab + vv[j] * ktb
                s_ref[vb, kh] = sn
                po = sn * rb
                pa = sn * kkn
                oacc[j] = po if oacc[j] is None else oacc[j] + po
                acc[j] = pa if acc[j] is None else acc[j] + pa
        for j, vb in enumerate(grp):
            ostore(s, t, vb, oacc[j])
        return tuple(allred(a) for a in acc)

    t0 = tidx(0)
    carry0 = tuple(first_sa(g, t0) for g in groups)

    def step(s, carry):
        t = tidx(s)
        t_next = tidx(jnp.minimum(s + 1, tc - 1))
        return tuple(fused(g, s, t, t_next, carry[i]) for i, g in enumerate(groups))

    lax.fori_loop(0, tc, step, carry0, unroll=unroll)


def _vrows(vb):
    return pl.ds(vb * SUBLANES, SUBLANES)


def _scan_ctx_kernel(w_ref, kk_ref, ka_ref, kt_ref, r_ref, v_ref, o_ref, sfin_ref, s_ref, *, tc, nk, rev):
    c = pl.program_id(1)
    x_refs = (w_ref, kk_ref, ka_ref, kt_ref, r_ref)

    @pl.when(c == 0)
    def _():
        s_ref[...] = jnp.zeros(s_ref.shape, F32)

    def xrow(t, a, kh):
        return jnp.broadcast_to(x_refs[a][0, t, pl.ds(kh, 1), :], (SUBLANES, LANES))

    def vload(t, vb):
        return v_ref[0, t, _vrows(vb), :]

    def ostore(s, t, vb, val):
        o_ref[0, t, _vrows(vb), :] = val

    def tidx(s):
        return tc - 1 - s if rev else s

    _scan_steps(xrow, vload, ostore, s_ref, tidx, tc=tc, nk=nk, kl=1)

    @pl.when(c == pl.num_programs(1) - 1)
    def _():
        sfin_ref[0] = s_ref[...]


def wkv_scan_ctx(w, kk, ka, kt, r, v, *, tc, rev):
    ngb, t_len, nk, _ = w.shape
    nch = t_len // tc
    blk = pl.BlockSpec((1, tc, nk, LANES), lambda g, c: (g, nch - 1 - c if rev else c, 0, 0))
    st = pl.BlockSpec((1, N_VBLK, nk, SUBLANES, LANES), lambda g, c: (g, 0, 0, 0, 0))
    return pl.pallas_call(
        functools.partial(_scan_ctx_kernel, tc=tc, nk=nk, rev=rev),
        grid=(ngb, nch),
        in_specs=[blk] * 6,
        out_specs=[blk, st],
        out_shape=[
            jax.ShapeDtypeStruct((ngb, t_len, RWKV_HEAD, LANES), F32),
            jax.ShapeDtypeStruct((ngb, N_VBLK, nk, SUBLANES, LANES), F32),
        ],
        scratch_shapes=[pltpu.VMEM((N_VBLK, nk, SUBLANES, LANES), F32)],
        compiler_params=_cparams(("arbitrary", "arbitrary")),
        name="wkv_scan_ctx",
    )(w, kk, ka, kt, r, v)


def _scan_lat_kernel(*refs, tc, nk, kl):
    xa_refs, xb_refs = refs[0:5], refs[5:10]
    va_ref, vb_ref, s0_ref, oa_ref, ob_ref, s_ref, xm_ref, vm_ref = refs[10:]
    c = pl.program_id(0)

    @pl.when(c == 0)
    def _():
        s_ref[...] = s0_ref[...]

    def is_bwd(shape):
        lane = lax.broadcasted_iota(jnp.int32, shape, len(shape) - 1)
        return (lane // N_RWKV_HEADS) % 2 == 1

    mx = is_bwd((nk, LANES))
    mv = is_bwd((RWKV_HEAD, LANES))

    def merge(s, carry):
        for a in range(5):
            xm_ref[a, s] = jnp.where(mx, xb_refs[a][tc - 1 - s], xa_refs[a][s])
        vm_ref[s] = jnp.where(mv, vb_ref[tc - 1 - s], va_ref[s])
        return carry

    lax.fori_loop(0, tc, merge, 0)

    def xrow(t, a, kh):
        return jnp.broadcast_to(xm_ref[a, t, pl.ds(kh, 1), :], (SUBLANES, LANES))

    def vload(t, vb):
        return vm_ref[t, _vrows(vb), :]

    def ostore(s, t, vb, val):
        oa_ref[s, _vrows(vb), :] = val
        ob_ref[tc - 1 - s, _vrows(vb), :] = val

    _scan_steps(xrow, vload, ostore, s_ref, lambda s: s, tc=tc, nk=nk, kl=kl, ngroups=2, unroll=2)


def wkv_scan_lat(xs, v, s0, *, tc, kl):
    t_len, nk, _ = xs[0].shape
    nch = t_len // tc
    o_sds = jax.ShapeDtypeStruct((t_len, RWKV_HEAD, LANES), F32)
    xa = pl.BlockSpec((tc, nk, LANES), lambda c: (c, 0, 0))
    xb = pl.BlockSpec((tc, nk, LANES), lambda c: (nch - 1 - c, 0, 0))
    va = pl.BlockSpec((tc, RWKV_HEAD, LANES), lambda c: (c, 0, 0))
    vb = pl.BlockSpec((tc, RWKV_HEAD, LANES), lambda c: (nch - 1 - c, 0, 0))
    return pl.pallas_call(
        functools.partial(_scan_lat_kernel, tc=tc, nk=nk, kl=kl),
        grid=(nch,),
        in_specs=[xa] * 5 + [xb] * 5 + [va, vb,
                                        pl.BlockSpec((N_VBLK, nk, SUBLANES, LANES), lambda c: (0, 0, 0, 0))],
        out_specs=[va, vb],
        out_shape=[o_sds, o_sds],
        scratch_shapes=[pltpu.VMEM((N_VBLK, nk, SUBLANES, LANES), F32),
                        pltpu.VMEM((5, tc, nk, LANES), F32),
                        pltpu.VMEM((tc, RWKV_HEAD, LANES), F32)],
        compiler_params=_cparams(("arbitrary",)),
        name="wkv_scan_lat",
    )(*xs, *xs, v, v, s0)


def _rwkv_post_kernel(of_ref, ob_ref, bonus_ref, g_ref, gate_ref, gng_ref, gnb_ref, ones_ref,
                      wo_ref, *rest):
    y_ref = rest[-1]
    o = of_ref[...] + ob_ref[...]
    ones_bd = ones_ref[...]
    mean = _seg_sum(o, ones_bd) * (1.0 / RWKV_HEAD)
    oc = o - mean
    var = _seg_sum(oc * oc, ones_bd) * (1.0 / RWKV_HEAD)
    gn = oc * lax.rsqrt(var + GN_EPS) * gng_ref[...] + gnb_ref[...]
    y = (gn + bonus_ref[...]) * g_ref[...]
    y = jnp.dot(y.astype(BF16), wo_ref[...], preferred_element_type=F32)
    y_ref[...] = _sigmoid(gate_ref[...]) * y


def rwkv_post(o_f, o_b, bonus, g, u, gng, gnb, ones_bd, wo, y_prev, tile0, seq_block=None):
    ntiles = bonus.shape[0] // TILE

    def full(shape):
        return pl.BlockSpec(shape, lambda i: tuple(0 for _ in shape))

    part = pl.BlockSpec((TILE, D_RWKV), lambda i: (i, 0))
    if seq_block is None:
        opart = part
    else:
        bl, ts = seq_block
        opart = pl.BlockSpec((TILE, D_RWKV), lambda i: ((i // ts // bl) * ts + i % ts, (i // ts) % bl))
    in_specs = [opart, opart, part, part,
                pl.BlockSpec((TILE, D_MODEL), lambda i: (i + tile0, C_G1 // D_MODEL)),
                full((1, D_RWKV)), full((1, D_RWKV)), full((D_RWKV, D_RWKV)), full((D_RWKV, D_MODEL))]
    args = [o_f, o_b, bonus, g, u, gng.reshape(1, -1), gnb.reshape(1, -1), ones_bd, wo.astype(BF16)]
    aliases = {}
    if y_prev is not None:
        in_specs.append(pl.BlockSpec(memory_space=pl.ANY))
        args.append(y_prev)
        aliases = {len(args) - 1: 0}
    return pl.pallas_call(
        _rwkv_post_kernel,
        grid=(ntiles,),
        in_specs=in_specs,
        out_specs=pl.BlockSpec((TILE, D_MODEL), lambda i: (i + tile0, 0)),
        out_shape=jax.ShapeDtypeStruct((u.shape[0], D_MODEL), F32),
        input_output_aliases=aliases,
        compiler_params=_cparams(("arbitrary",)),
        name="rwkv_post",
    )(*args)


QK_W = 2 * LANES
Q_SCALE = ATTN_SCALE * math.log2(math.e)


def _mla_pre_kernel(cq_ref, ckv_ref, kr_ref, cos_ref, sin_ref, qg_ref, kvg_ref, wuq_ref, wukt_ref,
                    qf_o, ckvn_o, krr_o):
    nq = N_MLA_HEADS * NOPE_DIM
    nrp = N_MLA_HEADS * LANES
    cq = _rms(cq_ref[...], qg_ref[...])
    q = jnp.dot(cq.astype(BF16), wuq_ref[...], preferred_element_type=F32)
    cos = cos_ref[...]
    sin = sin_ref[...]
    for h in range(N_MLA_HEADS):
        qn = q[:, h * NOPE_DIM:(h + 1) * NOPE_DIM].astype(BF16)
        qa = jnp.dot(qn, wukt_ref[h], preferred_element_type=F32)
        qr = (q[:, nq + h * LANES:nq + (h + 1) * LANES] * cos
              + q[:, nq + nrp + h * LANES:nq + nrp + (h + 1) * LANES] * sin)
        qf_o[:, h * QK_W:h * QK_W + LANES] = (qa * Q_SCALE).astype(qf_o.dtype)
        qf_o[:, h * QK_W + LANES:(h + 1) * QK_W] = (qr * Q_SCALE).astype(qf_o.dtype)
    ckvn_o[...] = _rms(ckv_ref[...], kvg_ref[...])
    kr = kr_ref[...]
    krr = kr[:, ROPE_DIM:2 * ROPE_DIM] * cos[:, :ROPE_DIM] + kr[:, 2 * ROPE_DIM:3 * ROPE_DIM] * sin[:, :ROPE_DIM]
    krr_o[...] = jnp.concatenate([krr, jnp.zeros((TILE, LANES - ROPE_DIM), F32)], axis=1)


def mla_pre(u, cos_t, sin_t, qg, kvg, wuq_p, wukt, geo):
    n = u.shape[0]

    def full(shape):
        return pl.BlockSpec(shape, lambda i: tuple(0 for _ in shape))

    return pl.pallas_call(
        _mla_pre_kernel,
        grid=(geo.ntile,),
        in_specs=[
            pl.BlockSpec((TILE, Q_RANK), lambda i: (i, C_CQ // Q_RANK)),
            pl.BlockSpec((TILE, KV_RANK), lambda i: (i, C_CKV // KV_RANK)),
            pl.BlockSpec((TILE, LANES), lambda i: (i, C_KR // LANES)),
            pl.BlockSpec((TILE, LANES), lambda i: (geo.rope_blk(i), 0)),
            pl.BlockSpec((TILE, LANES), lambda i: (geo.rope_blk(i), 0)),
            full((1, Q_RANK)), full((1, KV_RANK)),
            full(wuq_p.shape), full(wukt.shape),
        ],
        out_specs=[
            pl.BlockSpec((TILE, N_MLA_HEADS * QK_W), lambda i: (i, 0)),
            pl.BlockSpec((TILE, KV_RANK), lambda i: (i, 0)),
            pl.BlockSpec((TILE, LANES), lambda i: (i, 0)),
        ],
        out_shape=[
            jax.ShapeDtypeStruct((n, N_MLA_HEADS * QK_W), BF16),
            jax.ShapeDtypeStruct((n, KV_RANK), F32),
            jax.ShapeDtypeStruct((n, LANES), F32),
        ],
        compiler_params=_cparams(("arbitrary",)),
        name="mla_pre",
    )(u, u, u, cos_t, sin_t, qg.reshape(1, -1), kvg.reshape(1, -1), wuq_p, wukt)


ATT_KEY_CHUNK = 512
ATT_HEADS_PER_STEP = 4


def _attn_kernel(q_ref, kt_ref, c_ref, wuv_ref, wo_ref, gate_ref, *rest, ck):
    y_ref, pc_ref = rest[-2:]
    hg = pl.program_id(2)
    tq = q_ref.shape[0]
    t_k = kt_ref.shape[2]
    hp = ATT_HEADS_PER_STEP
    q = jnp.concatenate([q_ref[:, j * QK_W:(j + 1) * QK_W] for j in range(hp)], axis=0)
    m = jnp.full((hp * tq, 1), -jnp.inf, F32)
    l = jnp.zeros((hp * tq, 1), F32)
    acc = jnp.zeros((hp * tq, KV_RANK), F32)
    for c0 in range(0, t_k, ck):
        s = jnp.dot(q, kt_ref[0, :, c0:c0 + ck], preferred_element_type=F32)
        m_new = jnp.maximum(m, jnp.max(s, axis=-1, keepdims=True))
        alpha = jnp.exp2(m - m_new)
        p = jnp.exp2(s - m_new)
        l = alpha * l + jnp.sum(p, axis=-1, keepdims=True)
        acc = alpha * acc + jnp.dot(p.astype(BF16), c_ref[0, c0:c0 + ck, :], preferred_element_type=F32)
        m = m_new
    pc = (acc / l).astype(BF16)
    for j in range(hp):
        pc_ref[hg * hp + j] = pc[j * tq:(j + 1) * tq]

    @pl.when(hg == pl.num_programs(2) - 1)
    def _():
        pcs = jnp.concatenate([pc_ref[i] for i in range(N_MLA_HEADS)], axis=1)
        oh = jnp.dot(pcs, wuv_ref[...], preferred_element_type=F32)
        y = jnp.dot(oh.astype(BF16), wo_ref[...], preferred_element_type=F32)
        y_ref[...] = _sigmoid(gate_ref[...]) * y


def attention(qf, kct, cv, wuv_bd, wo, u, y_prev, *, row0, t_seq, tq, name):
    n = qf.shape[0]
    nseq, _, t_k = kct.shape
    qt = t_seq // tq
    rb0 = row0 // tq
    ck = math.gcd(ATT_KEY_CHUNK, t_k)

    def rows(w, col):
        return pl.BlockSpec((tq, w), lambda s, i, h: (rb0 + s * qt + i, col(h)))

    in_specs = [
        rows(ATT_HEADS_PER_STEP * QK_W, lambda h: h),
        pl.BlockSpec((1, QK_W, t_k), lambda s, i, h: (s, 0, 0)),
        pl.BlockSpec((1, t_k, KV_RANK), lambda s, i, h: (s, 0, 0)),
        pl.BlockSpec(wuv_bd.shape, lambda s, i, h: (0, 0)),
        pl.BlockSpec(wo.shape, lambda s, i, h: (0, 0)),
        rows(D_MODEL, lambda h: C_G2 // D_MODEL),
    ]
    args = [qf, kct, cv, wuv_bd, wo, u]
    aliases = {}
    if y_prev is not None:
        in_specs.append(pl.BlockSpec(memory_space=pl.ANY))
        args.append(y_prev)
        aliases = {len(args) - 1: 0}
    return pl.pallas_call(
        functools.partial(_attn_kernel, ck=ck),
        grid=(nseq, qt, N_MLA_HEADS // ATT_HEADS_PER_STEP),
        in_specs=in_specs,
        out_specs=rows(D_MODEL, lambda h: 0),
        out_shape=jax.ShapeDtypeStruct((n, D_MODEL), F32),
        scratch_shapes=[pltpu.VMEM((N_MLA_HEADS, tq, KV_RANK), BF16)],
        input_output_aliases=aliases,
        compiler_params=_cparams(("arbitrary", "arbitrary", "arbitrary")),
        name=name,
    )(*args)


R_I1, R_I2, R_G1, R_G2 = 0, 1, 2, 3


def _merge_kernel(x_ref, yc_ref, yr_ref, ym_ref, wout_ref, gpost_ref, g1_ref, gpre_ref, sc_ref,
                  sh_ref, router_ref, x1_o, h2_o, comb_o, *, moe):
    m = yc_ref[...] + yr_ref[...] + ym_ref[...]
    y = jnp.dot(m.astype(BF16), wout_ref[...], preferred_element_type=F32)
    x1 = x_ref[...] + g1_ref[0] * _rms(y, gpost_ref[...])
    x1_o[...] = x1
    h2 = _rms(x1, gpre_ref[...]) * (1.0 + sc_ref[0]) + sh_ref[0]
    h2_o[...] = h2.astype(h2_o.dtype)
    if moe:
        logits = jnp.dot(h2, router_ref[...], preferred_element_type=F32,
                         precision=lax.Precision.HIGHEST)
        lane = lax.broadcasted_iota(jnp.int32, logits.shape, 1)
        neg = jnp.float32(-jnp.inf)
        logits = jnp.where(lane < N_EXPERTS, logits, neg)
        m1 = jnp.max(logits, axis=-1, keepdims=True)
        i1 = jnp.min(jnp.where(logits == m1, lane, LANES), axis=-1, keepdims=True)
        rest = jnp.where(lane == i1, neg, logits)
        m2 = jnp.max(rest, axis=-1, keepdims=True)
        i2 = jnp.min(jnp.where(rest == m2, lane, LANES), axis=-1, keepdims=True)
        e2 = jnp.exp(m2 - m1)
        den = 1.0 + e2
        cols = ((R_I1, i1.astype(F32)), (R_I2, i2.astype(F32)), (R_G1, 1.0 / den), (R_G2, e2 / den))
        route = jnp.zeros(comb_o.shape, F32)
        for col, val in cols:
            route = jnp.where(lane == col, val, route)
        comb_o[...] = route
    else:
        comb_o[...] = jnp.ones(comb_o.shape, F32)


def merge(x, yc, yr, ym, wout, gpost, gpre, mod, router, geo, moe):
    n, d = x.shape

    def full(shape):
        return pl.BlockSpec(shape, lambda i: tuple(0 for _ in shape))

    def rows():
        return pl.BlockSpec((TILE, d), lambda i: (i, 0))

    def modspec(j):
        return pl.BlockSpec((1, 1, d), lambda i: (geo.mod_row(i) * 6 + j, 0, 0))

    router_p = jnp.zeros((d, LANES), F32).at[:, :N_EXPERTS].set(router)
    return pl.pallas_call(
        functools.partial(_merge_kernel, moe=moe),
        grid=(geo.ntile,),
        in_specs=[rows(), rows(), rows(), rows(), full((d, d)), full((1, d)), modspec(2),
                  full((1, d)), modspec(4), modspec(3), full((d, LANES))],
        out_specs=[rows(), rows(), pl.BlockSpec((TILE, LANES), lambda i: (i, 0))],
        out_shape=[jax.ShapeDtypeStruct((n, d), F32), jax.ShapeDtypeStruct((n, d), F32 if moe else BF16),
                   jax.ShapeDtypeStruct((n, LANES), F32)],
        compiler_params=_cparams(("arbitrary",)),
        name="merge",
    )(x, yc, yr, ym, wout.astype(BF16), gpost.reshape(1, d), mod, gpre.reshape(1, d), mod, mod,
      router_p)


def _ffn_kernel(h_ref, comb_ref, w1_ref, w3_ref, w2_ref, o_ref, acc_ref):
    e = pl.program_id(1)
    f = pl.program_id(2)

    @pl.when((e == 0) & (f == 0))
    def _():
        acc_ref[...] = jnp.zeros(acc_ref.shape, F32)

    h = h_ref[...]
    a = jnp.dot(h, w1_ref[0], preferred_element_type=F32)
    b = jnp.dot(h, w3_ref[0], preferred_element_type=F32)
    comb = comb_ref[...]
    lane = lax.broadcasted_iota(jnp.int32, comb.shape, 1)
    ce = jnp.sum(jnp.where(lane == e, comb, 0.0), axis=-1, keepdims=True)
    act = (a * _sigmoid(a) * b) * ce
    acc_ref[...] += jnp.dot(act.astype(BF16), w2_ref[0], preferred_element_type=F32)

    @pl.when((e == pl.num_programs(1) - 1) & (f == pl.num_programs(2) - 1))
    def _():
        o_ref[...] = acc_ref[...]


def ffn(h, comb, w1, w3, w2, *, tm, tf):
    n, d = h.shape
    ne, _, dff = w1.shape
    return pl.pallas_call(
        _ffn_kernel,
        grid=(n // tm, ne, dff // tf),
        in_specs=[
            pl.BlockSpec((tm, d), lambda i, e, f: (i, 0)),
            pl.BlockSpec((tm, LANES), lambda i, e, f: (i, 0)),
            pl.BlockSpec((1, d, tf), lambda i, e, f: (e, 0, f)),
            pl.BlockSpec((1, d, tf), lambda i, e, f: (e, 0, f)),
            pl.BlockSpec((1, tf, d), lambda i, e, f: (e, f, 0)),
        ],
        out_specs=pl.BlockSpec((tm, d), lambda i, e, f: (i, 0)),
        out_shape=jax.ShapeDtypeStruct((n, d), F32),
        scratch_shapes=[pltpu.VMEM((tm, d), F32)],
        compiler_params=_cparams(("arbitrary", "arbitrary", "arbitrary")),
        name="ffn",
    )(h, comb, w1, w3, w2)


MOE_TM = 512


def _row_copy(src_hbm, row, dst_vmem, r, sem):
    return pltpu.make_async_copy(src_hbm.at[pl.ds(row, 1)], dst_vmem.at[pl.ds(r, 1)], sem)


DMA_UNROLL = 8


def _ffn_sorted_kernel(te_ref, nused_ref, idx_ref, nxt_ref, h_hbm, w1_ref, w3_ref, w2_ref, o_ref,
                       acc_ref, xbuf_ref, sems):
    j = pl.program_id(0)
    f = pl.program_id(1)
    nj = pl.num_programs(0)
    tm = acc_ref.shape[0]
    slot = j % 2

    def issue(ids_ref, s):
        def body(i, c):
            for k in range(DMA_UNROLL):
                r = i * DMA_UNROLL + k
                _row_copy(h_hbm, ids_ref[0, 0, r], xbuf_ref.at[s], r, sems.at[s]).start(priority=k % 2)
            return c
        lax.fori_loop(0, tm // DMA_UNROLL, body, 0)

    def wait(s):
        def body(i, c):
            for k in range(DMA_UNROLL):
                _row_copy(h_hbm, 0, xbuf_ref.at[s], i * DMA_UNROLL + k, sems.at[s]).wait()
            return c
        lax.fori_loop(0, tm // DMA_UNROLL, body, 0)

    @pl.when(f == 0)
    def _():
        @pl.when(j == 0)
        def _():
            issue(idx_ref, 0)

        @pl.when(j + 1 < nj)
        def _():
            issue(nxt_ref, 1 - slot)

        wait(slot)

    @pl.when(j < nused_ref[0])
    def _():
        @pl.when(f == 0)
        def _():
            acc_ref[...] = jnp.zeros(acc_ref.shape, F32)

        h = xbuf_ref[slot].astype(BF16)
        a = jnp.dot(h, w1_ref[0], preferred_element_type=F32)
        b = jnp.dot(h, w3_ref[0], preferred_element_type=F32)
        act = a * _sigmoid(a) * b
        acc_ref[...] += jnp.dot(act.astype(BF16), w2_ref[0], preferred_element_type=F32)

        @pl.when(f == pl.num_programs(1) - 1)
        def _():
            o_ref[...] = acc_ref[...]


def ffn_sorted(h, row_src, te, nused, w1, w3, w2, *, tm, tf):
    p = row_src.shape[0]
    d = h.shape[1]
    dff = w1.shape[2]
    ntile = p // tm
    ids = row_src.reshape(ntile, 1, tm)
    return pl.pallas_call(
        _ffn_sorted_kernel,
        grid_spec=pltpu.PrefetchScalarGridSpec(
            num_scalar_prefetch=2,
            grid=(ntile, dff // tf),
            in_specs=[
                pl.BlockSpec((1, 1, tm), lambda j, f, te, nu: (j, 0, 0), memory_space=pltpu.SMEM),
                pl.BlockSpec((1, 1, tm), lambda j, f, te, nu: (jnp.minimum(j + 1, ntile - 1), 0, 0),
                             memory_space=pltpu.SMEM),
                pl.BlockSpec(memory_space=pl.ANY),
                pl.BlockSpec((1, d, tf), lambda j, f, te, nu: (te[j], 0, f)),
                pl.BlockSpec((1, d, tf), lambda j, f, te, nu: (te[j], 0, f)),
                pl.BlockSpec((1, tf, d), lambda j, f, te, nu: (te[j], f, 0)),
            ],
            out_specs=pl.BlockSpec((tm, d), lambda j, f, te, nu: (j, 0)),
            scratch_shapes=[pltpu.VMEM((tm, d), F32), pltpu.VMEM((2, tm, d), h.dtype),
                            pltpu.SemaphoreType.DMA((2,))],
        ),
        out_shape=jax.ShapeDtypeStruct((p, d), F32),
        compiler_params=_cparams(("arbitrary", "arbitrary")),
        name="ffn_sorted",
    )(te, nused, ids, ids, h, w1, w3, w2)


def _moe_combine_kernel(d0_ref, d1_ref, ys_hbm, route_ref, x_ref, g_ref, g2_ref, o_ref,
                        y0_ref, y1_ref, sem0, sem1):
    def start(i, c):
        for k in range(DMA_UNROLL):
            r = i * DMA_UNROLL + k
            _row_copy(ys_hbm, d0_ref[0, 0, r], y0_ref, r, sem0).start(priority=0)
            _row_copy(ys_hbm, d1_ref[0, 0, r], y1_ref, r, sem1).start(priority=1)
        return c

    def wait(i, c):
        for k in range(DMA_UNROLL):
            r = i * DMA_UNROLL + k
            _row_copy(ys_hbm, 0, y0_ref, r, sem0).wait()
            _row_copy(ys_hbm, 0, y1_ref, r, sem1).wait()
        return c

    lax.fori_loop(0, TILE // DMA_UNROLL, start, 0)
    lax.fori_loop(0, TILE // DMA_UNROLL, wait, 0)
    route = route_ref[...]
    lane = lax.broadcasted_iota(jnp.int32, route.shape, 1)
    ga = jnp.sum(jnp.where(lane == R_G1, route, 0.0), axis=-1, keepdims=True)
    gb = jnp.sum(jnp.where(lane == R_G2, route, 0.0), axis=-1, keepdims=True)
    y = ga * y0_ref[...] + gb * y1_ref[...]
    o_ref[...] = x_ref[...] + g2_ref[0] * _rms(y, g_ref[...])


def moe_combine_post(x1, ys, dest, route, gpost, mod, geo):
    n, d = x1.shape
    dd = dest.reshape(2, n // TILE, 1, TILE)
    return pl.pallas_call(
        _moe_combine_kernel,
        grid=(geo.ntile,),
        in_specs=[
            pl.BlockSpec((1, 1, TILE), lambda i: (i, 0, 0), memory_space=pltpu.SMEM),
            pl.BlockSpec((1, 1, TILE), lambda i: (i, 0, 0), memory_space=pltpu.SMEM),
            pl.BlockSpec(memory_space=pl.ANY),
            pl.BlockSpec((TILE, LANES), lambda i: (i, 0)),
            pl.BlockSpec((TILE, d), lambda i: (i, 0)),
            pl.BlockSpec((1, d), lambda i: (0, 0)),
            pl.BlockSpec((1, 1, d), lambda i: (geo.mod_row(i) * 6 + 5, 0, 0)),
        ],
        out_specs=pl.BlockSpec((TILE, d), lambda i: (i, 0)),
        out_shape=jax.ShapeDtypeStruct((n, d), F32),
        scratch_shapes=[pltpu.VMEM((TILE, d), F32), pltpu.VMEM((TILE, d), F32),
                        pltpu.SemaphoreType.DMA(()), pltpu.SemaphoreType.DMA(())],
        compiler_params=_cparams(("arbitrary",)),
        name="moe_combine",
    )(dd[0], dd[1], ys, route, x1, gpost.reshape(1, d), mod)


def _route_plan(route, tm):
    n = route.shape[0]
    e_flat = jnp.concatenate([route[:, R_I1], route[:, R_I2]]).astype(jnp.int32)
    onehot = (e_flat[:, None] == jnp.arange(N_EXPERTS, dtype=jnp.int32)[None, :]).astype(jnp.int32)
    csum = jnp.cumsum(onehot, axis=0)
    rank = jnp.take_along_axis(csum, e_flat[:, None], axis=1)[:, 0] - 1
    counts = csum[-1]
    ptiles = (counts + tm - 1) // tm
    tile_end = jnp.cumsum(ptiles)
    gstart = (tile_end - ptiles) * tm
    dest = jnp.take(gstart, e_flat) + rank
    p_max = 2 * n + N_EXPERTS * tm
    tok = jnp.tile(jnp.arange(n, dtype=jnp.int32), 2)
    row_src = jnp.zeros((p_max,), jnp.int32).at[dest].set(tok)
    tiles = jnp.arange(p_max // tm, dtype=jnp.int32)
    te = jnp.minimum(jnp.searchsorted(tile_end, tiles, side="right"), N_EXPERTS - 1).astype(jnp.int32)
    return row_src, dest.astype(jnp.int32).reshape(2, n), te, tile_end[-1:].astype(jnp.int32)


def _ffn_post_kernel(x_ref, y_ref, g_ref, g2_ref, o_ref):
    o_ref[...] = x_ref[...] + g2_ref[0] * _rms(y_ref[...], g_ref[...])


def ffn_post(x1, y, gpost, mod, geo):
    n, d = x1.shape
    return pl.pallas_call(
        _ffn_post_kernel,
        grid=(geo.ntile,),
        in_specs=[
            pl.BlockSpec((TILE, d), lambda i: (i, 0)),
            pl.BlockSpec((TILE, d), lambda i: (i, 0)),
            pl.BlockSpec((1, d), lambda i: (0, 0)),
            pl.BlockSpec((1, 1, d), lambda i: (geo.mod_row(i) * 6 + 5, 0, 0)),
        ],
        out_specs=pl.BlockSpec((TILE, d), lambda i: (i, 0)),
        out_shape=jax.ShapeDtypeStruct((n, d), F32),
        compiler_params=_cparams(("arbitrary",)),
        name="ffn_post",
    )(x1, y, gpost.reshape(1, d), mod)


def _ctx_scan(geo, pre):
    r, kk, v, w0, w1, ka0, ka1, kt0, kt1 = pre
    H, K = N_RWKV_HEADS, RWKV_HEAD
    bl = LANES // H
    nbh = geo.n_ctx // bl

    def to_scan(a):
        return jnp.swapaxes(a.reshape(nbh, geo.t_ctx, LANES, K), 2, 3)

    def from_scan(o):
        return jnp.swapaxes(o, 2, 3).reshape(nbh * geo.t_ctx, bl * H * K)

    def state(sf):
        sf = sf.reshape(nbh, K // SUBLANES, K, SUBLANES, bl, H).transpose(0, 4, 5, 1, 3, 2)
        return sf.reshape(geo.n_ctx, H, K, K)

    kk_s, r_s, v_s = to_scan(kk), to_scan(r), to_scan(v)
    tc = min(32, geo.t_ctx)
    o_f, sf_f = wkv_scan_ctx(to_scan(w0), kk_s, to_scan(ka0), to_scan(kt0), r_s, v_s, tc=tc, rev=False)
    o_b, sf_b = wkv_scan_ctx(to_scan(w1), kk_s, to_scan(ka1), to_scan(kt1), r_s, v_s, tc=tc, rev=True)
    return from_scan(o_f), from_scan(o_b), jnp.stack([state(sf_f), state(sf_b)], axis=1)


def _lat_scan(geo, pre, state_l):
    r, kk, v, w, ka, kt = pre
    H, K = N_RWKV_HEADS, RWKV_HEAD
    nb, t_len = geo.n_lat, geo.t_lat
    hp = H // 2
    kl = LANES // (nb * 2 * H)
    nk = K // kl

    def to_scan(a):
        return jnp.swapaxes(a.reshape(t_len, LANES, nk), 1, 2)

    xs = (to_scan(w), to_scan(kk), to_scan(ka), to_scan(kt), to_scan(r))
    vl = v.reshape(nb, t_len, hp, 1, 2, 1, K)
    vl = jnp.broadcast_to(vl, (nb, t_len, hp, 2, 2, kl, K))
    v_lat = vl.transpose(1, 6, 2, 0, 3, 4, 5).reshape(t_len, K, LANES)
    s0 = state_l.reshape(nb, 2, hp, 2, K // SUBLANES, SUBLANES, kl, nk)
    s0 = s0.transpose(4, 7, 5, 2, 0, 1, 3, 6).reshape(K // SUBLANES, nk, SUBLANES, LANES)
    oa, ob = wkv_scan_lat(xs, v_lat, s0, tc=min(64, t_len), kl=kl)

    def direction(o, d):
        o = o.reshape(t_len, K, hp, nb, 2, 2, kl)[:, :, :, :, d].sum(axis=-1)
        return o.transpose(3, 0, 2, 4, 1).reshape(geo.nl_tok, H * K)

    return direction(oa, 0), direction(ob, 1)


def _rope_tables(geo):
    n_freq = ROPE_DIM // 4
    rows = geo.t_lat // GRID_W
    row = jnp.repeat(jnp.arange(rows, dtype=F32), GRID_W)
    col = jnp.tile(jnp.arange(GRID_W, dtype=F32), rows)
    inv = ROPE_BASE ** (-jnp.arange(n_freq, dtype=F32) / n_freq)
    ang = jnp.concatenate([row[:, None] * inv, col[:, None] * inv], axis=-1)
    cos, sin = jnp.cos(ang), jnp.sin(ang)
    cos32 = jnp.concatenate([cos, cos], axis=-1)
    sin32 = jnp.concatenate([-sin, sin], axis=-1)
    cos_t = jnp.concatenate([jnp.ones((TILE, ROPE_DIM), F32), cos32], axis=0)
    sin_t = jnp.concatenate([jnp.zeros((TILE, ROPE_DIM), F32), sin32], axis=0)
    pad = ((0, 0), (0, LANES - ROPE_DIM))
    return jnp.pad(cos_t, pad), jnp.pad(sin_t, pad)


_DEINT = np.concatenate([np.arange(0, ROPE_DIM, 2), np.arange(1, ROPE_DIM, 2)])
_DEINT_SW = np.concatenate([np.arange(1, ROPE_DIM, 2), np.arange(0, ROPE_DIM, 2)])


def _prep_w_in(w_in, b_in):
    offs = np.cumsum([0, 2 * D_CONV, D_RWKV, D_RWKV, D_RWKV, W_RANK, A_RANK, G_RANK, Q_RANK, KV_RANK,
                      ROPE_DIM, N_BRANCH * D_MODEL])
    o_conv, o_r, o_k, o_v, o_dw, o_da, o_dg, o_cq, o_ckv, o_kr, o_gate, _ = offs
    idx = np.zeros((IN_PAD,), np.int32)
    valid = np.zeros((IN_PAD,), bool)

    def put(dst, src):
        idx[dst:dst + len(src)] = src
        valid[dst:dst + len(src)] = True

    put(C_CONV, np.arange(o_conv, o_conv + 2 * D_CONV))
    for b, c in enumerate((C_G0, C_G1, C_G2)):
        put(c, np.arange(o_gate + b * D_MODEL, o_gate + (b + 1) * D_MODEL))
    put(C_R, np.arange(o_r, o_r + D_RWKV))
    put(C_K, np.arange(o_k, o_k + D_RWKV))
    put(C_V, np.arange(o_v, o_v + D_RWKV))
    put(C_LORA, np.arange(o_dw, o_dw + W_RANK + A_RANK + G_RANK))
    put(C_CQ, np.arange(o_cq, o_cq + Q_RANK))
    put(C_CKV, np.arange(o_ckv, o_ckv + KV_RANK))
    put(C_KR, np.arange(o_kr, o_kr + ROPE_DIM))
    put(C_KR + ROPE_DIM, o_kr + _DEINT)
    put(C_KR + 2 * ROPE_DIM, o_kr + _DEINT_SW)
    w = jnp.where(valid[None, :], jnp.take(w_in, idx, axis=1), 0.0).astype(BF16)
    b = jnp.where(valid, jnp.take(b_in, idx), 0.0)
    return w, b


def _prep_wuq(wuq):
    hd = NOPE_DIM + ROPE_DIM
    nq = N_MLA_HEADS * NOPE_DIM
    ncol = nq + 2 * N_MLA_HEADS * LANES
    idx = np.zeros((ncol,), np.int32)
    valid = np.zeros((ncol,), bool)
    for h in range(N_MLA_HEADS):
        idx[h * NOPE_DIM:(h + 1) * NOPE_DIM] = h * hd + np.arange(NOPE_DIM)
        valid[h * NOPE_DIM:(h + 1) * NOPE_DIM] = True
        for blk, perm in enumerate((_DEINT, _DEINT_SW)):
            c0 = nq + blk * N_MLA_HEADS * LANES + h * LANES
            idx[c0:c0 + ROPE_DIM] = h * hd + NOPE_DIM + perm
            valid[c0:c0 + ROPE_DIM] = True
    return jnp.where(valid[None, :], jnp.take(wuq, idx, axis=1), 0.0).astype(BF16)


def _block_diag_heads(wuv):
    w3 = wuv.reshape(KV_RANK, N_MLA_HEADS, V_DIM)
    eye = jnp.eye(N_MLA_HEADS, dtype=wuv.dtype)
    bd = w3.transpose(1, 0, 2)[:, :, None, :] * eye[:, None, :, None]
    return bd.reshape(N_MLA_HEADS * KV_RANK, N_MLA_HEADS * V_DIM).astype(BF16)


def _attention_keys(geo, ckvn, krr, cache_c, cache_kr):
    nc = geo.nc_tok
    pad = QK_W - KV_RANK - ROPE_DIM

    def keys(c, kr):
        kc = jnp.concatenate([c, kr, jnp.zeros(c.shape[:2] + (pad,), F32)], axis=-1).astype(BF16)
        return kc.transpose(0, 2, 1), c.astype(BF16)

    kr32 = krr[:, :ROPE_DIM]
    kct_c, cv_c = keys(ckvn[:nc].reshape(geo.n_ctx, geo.t_ctx, KV_RANK),
                       kr32[:nc].reshape(geo.n_ctx, geo.t_ctx, ROPE_DIM))
    c_l = jnp.concatenate([ckvn[nc:].reshape(geo.n_lat, geo.t_lat, KV_RANK), cache_c], axis=1)
    kr_l = jnp.concatenate([kr32[nc:].reshape(geo.n_lat, geo.t_lat, ROPE_DIM), cache_kr], axis=1)
    kct_l, cv_l = keys(c_l, kr_l)
    return kct_c, cv_c, kct_l, cv_l


def kernel(x_prompt, x_sample, cache_ckv, cache_krope, state_wkv, c, c_ctx, ada_w, ada_b, norm_mix_pre, norm_mix_post, norm_ffn_pre, norm_ffn_post, w_in, b_in, conv_w, conv_b, conv_ln_g, conv_ln_b, conv_wo, rwkv_mu, rwkv_w0, rwkv_bw, rwkv_a0, rwkv_ba, rwkv_bg, rwkv_xi, rwkv_alpha, rwkv_rho, rwkv_gn_g, rwkv_gn_b, rwkv_wo, mla_q_norm, mla_wuq, mla_kv_norm, mla_wuk, mla_wuv, mla_wo, w_out, ffn_w1, ffn_w3, ffn_w2, moe_router, moe_w1, moe_w3, moe_w2):
    P = dict(rwkv_mu=rwkv_mu, rwkv_w0=rwkv_w0, rwkv_bw=rwkv_bw, rwkv_a0=rwkv_a0, rwkv_ba=rwkv_ba,
             rwkv_bg=rwkv_bg, rwkv_xi=rwkv_xi, rwkv_alpha=rwkv_alpha, rwkv_rho=rwkv_rho)
    n_ctx, t_ctx, d = x_prompt.shape
    n_lat, t_lat, _ = x_sample.shape
    depth = ada_w.shape[0]
    past = cache_ckv.shape[2]
    geo = Geo(n_ctx, t_ctx, n_lat, t_lat, past)
    assert d == D_MODEL and (n_ctx * N_RWKV_HEADS) % LANES == 0 and LANES % (n_lat * 2 * N_RWKV_HEADS) == 0

    x = jnp.concatenate([x_prompt.reshape(-1, d), x_sample.reshape(-1, d)], axis=0)

    n_mod = 1 + n_lat
    n_mod_pad = -(-n_mod // SUBLANES) * SUBLANES
    c_all = jnp.zeros((n_mod_pad, d), F32).at[0].set(c_ctx).at[1:n_mod].set(c)
    head_id = np.arange(D_RWKV) // RWKV_HEAD
    ones_bd = jnp.asarray((head_id[:, None] == head_id[None, :]).astype(np.float32))
    cos_t, sin_t = _rope_tables(geo)

    ckv_out, kr_out, st_out = [], [], []
    for l in range(depth):
        mod = matmul_bias(c_all, ada_w[l].astype(BF16), ada_b[l], tm=n_mod_pad, tn=6 * d // 4,
                          pre="silu", name="ada_mod")
        mod = mod[:n_mod].reshape(n_mod * 6, 1, d)

        h = prenorm(x, norm_mix_pre[l], mod, geo, j_shift=0, j_scale=1)
        w_in_p, b_in_p = _prep_w_in(w_in[l], b_in[l])
        u = matmul_bias(h, w_in_p, b_in_p, tm=math.gcd(1024, geo.ntok), tn=1280, name="in_proj")

        y_conv = conv_branch(u, conv_w[l], conv_b[l], conv_ln_g[l], conv_ln_b[l], conv_wo[l], geo)

        ctx_block = (LANES // N_RWKV_HEADS, geo.ct)
        pre_c = rwkv_pre(u, P, l, geo, ones_bd, 0, geo.nct, seq_block=ctx_block)
        pre_l = rwkv_pre(u, P, l, geo, ones_bd, geo.nct, geo.nlt, lane_tiled=geo.lt)
        oc_f, oc_b, sfin = _ctx_scan(geo, pre_c[:9])
        ol_f, ol_b = _lat_scan(geo, pre_l[:6], state_wkv[:, l])
        y_rwkv = rwkv_post(ol_f, ol_b, pre_l[7], pre_l[6], u, rwkv_gn_g[l], rwkv_gn_b[l], ones_bd,
                           rwkv_wo[l], None, geo.nct)
        y_rwkv = rwkv_post(oc_f, oc_b, pre_c[10], pre_c[9], u, rwkv_gn_g[l], rwkv_gn_b[l], ones_bd,
                           rwkv_wo[l], y_rwkv, 0, seq_block=ctx_block)

        wukt = mla_wuk[l].reshape(KV_RANK, N_MLA_HEADS, NOPE_DIM).transpose(1, 2, 0).astype(BF16)
        qf, ckvn, krr = mla_pre(u, cos_t, sin_t, mla_q_norm[l], mla_kv_norm[l],
                                _prep_wuq(mla_wuq[l]), wukt, geo)
        kct_c, cv_c, kct_l, cv_l = _attention_keys(geo, ckvn, krr, cache_ckv[:, l],
                                                   cache_krope[:, l][..., _DEINT])
        wuv_bd = _block_diag_heads(mla_wuv[l])
        wo_b = mla_wo[l].astype(BF16)
        y_mla = attention(qf, kct_l, cv_l, wuv_bd, wo_b, u, None, row0=geo.nc_tok, t_seq=t_lat,
                          tq=TILE, name="attention_lat")
        y_mla = attention(qf, kct_c, cv_c, wuv_bd, wo_b, u, y_mla, row0=0, t_seq=t_ctx,
                          tq=TILE, name="attention_ctx")

        moe = (l % 2 == 1)
        i = l // 2
        router = moe_router[i] if moe else jnp.zeros((d, N_EXPERTS), F32)
        x1, h2, comb = merge(x, y_conv, y_rwkv, y_mla, w_out[l], norm_mix_post[l], norm_ffn_pre[l],
                             mod, router, geo, moe)
        if moe:
            row_src, dest, te, nused = _route_plan(comb, MOE_TM)
            ys = ffn_sorted(h2, row_src, te, nused, moe_w1[i].astype(BF16), moe_w3[i].astype(BF16),
                            moe_w2[i].astype(BF16), tm=MOE_TM, tf=D_FF // 2)
            x = moe_combine_post(x1, ys, dest, comb, norm_ffn_post[l], mod, geo)
        else:
            y = ffn(h2, comb, ffn_w1[i:i + 1].astype(BF16), ffn_w3[i:i + 1].astype(BF16),
                    ffn_w2[i:i + 1].astype(BF16), tm=512, tf=D_FF // 2)
            x = ffn_post(x1, y, norm_ffn_post[l], mod, geo)

        ckv_out.append(ckvn[:geo.nc_tok].reshape(n_ctx, t_ctx, KV_RANK))
        kr_out.append(u[:geo.nc_tok, C_KR:C_KR + ROPE_DIM].reshape(n_ctx, t_ctx, ROPE_DIM))
        st_out.append(sfin)

    y_prompt = x[:geo.nc_tok].reshape(n_ctx, t_ctx, d)
    y_sample = x[geo.nc_tok:].reshape(n_lat, t_lat, d)
    return (y_prompt, y_sample, jnp.stack(ckv_out, axis=1), jnp.stack(kr_out, axis=1),
            jnp.stack(st_out, axis=1))
```

```python
import functools
import math

import numpy as np
import jax
import jax.numpy as jnp
from jax import lax
from jax.experimental import pallas as pl
from jax.experimental.pallas import tpu as pltpu

F32 = jnp.float32
BF16 = jnp.bfloat16

D_MODEL = 1024
GRID_W = 64
D_CONV = 512
CONV_K = 31
D_RWKV = 512
RWKV_HEAD = 64
N_RWKV_HEADS = D_RWKV // RWKV_HEAD
W_RANK = 64
A_RANK = 64
G_RANK = 128
DECAY_SCALE = math.exp(-0.5)
GN_EPS = 64e-5
N_MLA_HEADS = 8
Q_RANK = 256
KV_RANK = 128
NOPE_DIM = 64
ROPE_DIM = 32
V_DIM = 64
ROPE_BASE = 10000.0
ATTN_SCALE = 1.0 / math.sqrt(NOPE_DIM + ROPE_DIM)
N_BRANCH = 3
D_FF = 2816
N_EXPERTS = 8
EPS = 1e-6

LANES = 128
SUBLANES = 8
VMEM_LIMIT_BYTES = 56 * 1024 * 1024

TILE = 256
CONV_HALO = 16
SHIFT_HALO = 8

C_CONV, C_G0, C_G1, C_G2 = 0, 1024, 2048, 3072
C_R, C_K, C_V = 4096, 4608, 5120
C_LORA = 5632
C_CQ = 5888
C_CKV = 6144
C_KR = 6272
IN_PAD = 6400


def _cparams(sem):
    return pltpu.CompilerParams(dimension_semantics=sem, vmem_limit_bytes=VMEM_LIMIT_BYTES)


def _sigmoid(x):
    return jax.nn.sigmoid(x)


def _rms(x, g):
    return (x * lax.rsqrt(jnp.mean(x * x, axis=-1, keepdims=True) + EPS)) * g


class Geo:
    def __init__(self, n_ctx, t_ctx, n_lat, t_lat, past):
        assert t_ctx % TILE == 0 and t_lat % TILE == 0
        self.n_ctx, self.t_ctx, self.n_lat, self.t_lat, self.past = n_ctx, t_ctx, n_lat, t_lat, past
        self.ct = t_ctx // TILE
        self.lt = t_lat // TILE
        self.nct = n_ctx * self.ct
        self.nlt = n_lat * self.lt
        self.ntile = self.nct + self.nlt
        self.nc_tok = n_ctx * t_ctx
        self.nl_tok = n_lat * t_lat
        self.ntok = self.nc_tok + self.nl_tok

    def pos(self, i):
        is_ctx = i < self.nct
        p = jnp.where(is_ctx, i % self.ct, (i - self.nct) % self.lt)
        n = jnp.where(is_ctx, self.ct, self.lt)
        return p, n

    def mod_row(self, i):
        return jnp.where(i < self.nct, 0, 1 + (i - self.nct) // self.lt)

    def rope_blk(self, i):
        return jnp.where(i < self.nct, 0, 1 + (i - self.nct) % self.lt)


def _mm_kernel(x_ref, w_ref, b_ref, o_ref, *, pre):
    x = x_ref[...]
    if pre == "silu":
        x = x.astype(F32)
        x = x * _sigmoid(x)
    acc = jnp.dot(x.astype(BF16), w_ref[...], preferred_element_type=F32)
    o_ref[...] = (acc + b_ref[...]).astype(o_ref.dtype)


def matmul_bias(x, w, b, *, tm, tn, pre=None, out_dtype=F32, name="matmul"):
    m, k = x.shape
    n = w.shape[1]
    assert m % tm == 0 and n % tn == 0
    return pl.pallas_call(
        functools.partial(_mm_kernel, pre=pre),
        grid=(n // tn, m // tm),
        in_specs=[
            pl.BlockSpec((tm, k), lambda j, i: (i, 0)),
            pl.BlockSpec((k, tn), lambda j, i: (0, j)),
            pl.BlockSpec((1, tn), lambda j, i: (0, j)),
        ],
        out_specs=pl.BlockSpec((tm, tn), lambda j, i: (i, j)),
        out_shape=jax.ShapeDtypeStruct((m, n), out_dtype),
        compiler_params=_cparams(("arbitrary", "arbitrary")),
        name=name,
    )(x, w, b.reshape(1, n).astype(F32))


def _prenorm_kernel(x_ref, g_ref, sc_ref, sh_ref, o_ref):
    h = _rms(x_ref[...], g_ref[...]) * (1.0 + sc_ref[0]) + sh_ref[0]
    o_ref[...] = h.astype(o_ref.dtype)


def prenorm(x, g, mod, geo, j_shift, j_scale):
    n, d = x.shape
    return pl.pallas_call(
        _prenorm_kernel,
        grid=(geo.ntile,),
        in_specs=[
            pl.BlockSpec((TILE, d), lambda i: (i, 0)),
            pl.BlockSpec((1, d), lambda i: (0, 0)),
            pl.BlockSpec((1, 1, d), lambda i: (geo.mod_row(i) * 6 + j_scale, 0, 0)),
            pl.BlockSpec((1, 1, d), lambda i: (geo.mod_row(i) * 6 + j_shift, 0, 0)),
        ],
        out_specs=pl.BlockSpec((TILE, d), lambda i: (i, 0)),
        out_shape=jax.ShapeDtypeStruct((n, d), BF16),
        compiler_params=_cparams(("arbitrary",)),
        name="prenorm",
    )(x, g.reshape(1, d), mod, mod)


def _conv_kernel(cur_ref, prev_ref, next_ref, gate_ref, cw_ref, cb_ref, lng_ref, lnb_ref,
                 wo_ref, o_ref, hp_ref, *, geo):
    i = pl.program_id(0)
    p, n = geo.pos(i)
    has_prev = p > 0
    has_next = p < n - 1

    def glu(z):
        return z[:, :D_CONV] * _sigmoid(z[:, D_CONV:])

    hp_ref[0, 0:CONV_HALO, :] = jnp.where(has_prev, glu(prev_ref[...]), 0.0)
    hp_ref[0, CONV_HALO:CONV_HALO + TILE, :] = glu(cur_ref[...])
    hp_ref[0, CONV_HALO + TILE:, :] = jnp.where(has_next, glu(next_ref[...]), 0.0)
    nrow = TILE + 2 * CONV_HALO - SUBLANES
    for b in range(1, SUBLANES):
        hp_ref[b, 0:nrow, :] = hp_ref[0, pl.ds(b, nrow), :]

    off = CONV_HALO - CONV_K // 2
    acc = None
    for j in range(CONV_K):
        q = off + j
        term = hp_ref[q % SUBLANES, pl.ds(q - q % SUBLANES, TILE), :] * cw_ref[j:j + 1, :]
        acc = term if acc is None else acc + term
    h = acc + cb_ref[...]
    mu = jnp.mean(h, axis=-1, keepdims=True)
    hc = h - mu
    var = jnp.mean(hc * hc, axis=-1, keepdims=True)
    h = hc * lax.rsqrt(var + EPS) * lng_ref[...] + lnb_ref[...]
    h = h * _sigmoid(h)
    y = jnp.dot(h.astype(BF16), wo_ref[...], preferred_element_type=F32)
    o_ref[...] = _sigmoid(gate_ref[...]) * y


def conv_branch(u, cw, cb, lng, lnb, wo, geo):
    n = u.shape[0]
    hb = TILE // CONV_HALO
    nhalo = n // CONV_HALO
    cwp = jnp.zeros((32, D_CONV), F32).at[:CONV_K].set(cw)
    return pl.pallas_call(
        functools.partial(_conv_kernel, geo=geo),
        grid=(geo.ntile,),
        in_specs=[
            pl.BlockSpec((TILE, 2 * D_CONV), lambda i: (i, C_CONV // (2 * D_CONV))),
            pl.BlockSpec((CONV_HALO, 2 * D_CONV), lambda i: (jnp.maximum(i * hb - 1, 0), 0)),
            pl.BlockSpec((CONV_HALO, 2 * D_CONV), lambda i: (jnp.minimum((i + 1) * hb, nhalo - 1), 0)),
            pl.BlockSpec((TILE, D_MODEL), lambda i: (i, C_G0 // D_MODEL)),
            pl.BlockSpec((32, D_CONV), lambda i: (0, 0)),
            pl.BlockSpec((1, D_CONV), lambda i: (0, 0)),
            pl.BlockSpec((1, D_CONV), lambda i: (0, 0)),
            pl.BlockSpec((1, D_CONV), lambda i: (0, 0)),
            pl.BlockSpec((D_CONV, D_MODEL), lambda i: (0, 0)),
        ],
        out_specs=pl.BlockSpec((TILE, D_MODEL), lambda i: (i, 0)),
        out_shape=jax.ShapeDtypeStruct((n, D_MODEL), F32),
        scratch_shapes=[pltpu.VMEM((SUBLANES, TILE + 2 * CONV_HALO, D_CONV), F32)],
        compiler_params=_cparams(("arbitrary",)),
        name="conv_branch",
    )(u, u, u, u, cwp, cb.reshape(1, -1), lng.reshape(1, -1), lnb.reshape(1, -1), wo.astype(BF16))


def _seg_sum(x, ones_bd):
    return jnp.dot(x, ones_bd, preferred_element_type=F32, precision=lax.Precision.HIGHEST)


def _rwkv_pre_kernel(r_ref, rp_ref, rn_ref, k_ref, kp_ref, kn_ref, v_ref, vp_ref, vn_ref,
                     lora_ref, mu_ref, w0_ref, bw_ref, a0_ref, ba_ref, bg_ref, xi_ref, al_ref,
                     rho_ref, ones_ref, *outs, geo, tile0):
    if len(outs) == 11:
        r_o, kk_o, v_o, w0_o, w1_o, ka0_o, ka1_o, kt0_o, kt1_o, g_o, bonus_o = outs
    else:
        r_o, kk_o, v_o, w0_o, ka0_o, kt0_o, g_o, bonus_o = outs
        w1_o, ka1_o, kt1_o = w0_o, ka0_o, kt0_o
    i = pl.program_id(0) + tile0
    p, n = geo.pos(i)
    has_prev = p > 0
    has_next = p < n - 1
    row = lax.broadcasted_iota(jnp.int32, (TILE, D_RWKV), 0)

    def put(o_ref, val, dirs=(0, 1)):
        if len(o_ref.shape) == 2:
            o_ref[...] = val
        else:
            for q in range(o_ref.shape[1]):
                for d in dirs:
                    o_ref[:, q, d * LANES:(d + 1) * LANES] = val[:, q * LANES:(q + 1) * LANES]

    def shifted(c_ref, p_ref, n_ref, mu):
        cur = c_ref[...]
        pv = jnp.where(has_prev, p_ref[SHIFT_HALO - 1:SHIFT_HALO, :], 0.0)
        nx = jnp.where(has_next, n_ref[0:1, :], 0.0)
        prev = jnp.where(row == 0, pv, pltpu.roll(cur, 1, axis=0))
        nxt = jnp.where(row == TILE - 1, nx, pltpu.roll(cur, TILE - 1, axis=0))
        return cur + mu * (0.5 * (prev + nxt) - cur)

    r = shifted(r_ref, rp_ref, rn_ref, mu_ref[0:1, :])
    k = shifted(k_ref, kp_ref, kn_ref, mu_ref[1:2, :])
    v = shifted(v_ref, vp_ref, vn_ref, mu_ref[2:3, :])

    lora = lora_ref[...]
    dw = jnp.tanh(lora[:, :W_RANK]).astype(BF16)
    da = lora[:, W_RANK:W_RANK + A_RANK].astype(BF16)
    dg = _sigmoid(lora[:, W_RANK + A_RANK:]).astype(BF16)
    wl = jnp.dot(dw, bw_ref[...], preferred_element_type=F32)
    al = jnp.dot(da, ba_ref[...], preferred_element_type=F32)
    g = jnp.dot(dg, bg_ref[...], preferred_element_type=F32)

    ones_bd = ones_ref[...]
    kx = k * xi_ref[...]
    kk = kx * lax.rsqrt(_seg_sum(kx * kx, ones_bd) + EPS)
    alpha = al_ref[...]
    rho = rho_ref[...]
    bonus = jnp.zeros((TILE, D_RWKV), F32)
    for d, (w_o, ka_o, kt_o) in enumerate(((w0_o, ka0_o, kt0_o), (w1_o, ka1_o, kt1_o))):
        sl = slice(d * D_RWKV, (d + 1) * D_RWKV)
        w = jnp.exp(-DECAY_SCALE * _sigmoid(w0_ref[d:d + 1, :] + wl[:, sl]))
        a = _sigmoid(a0_ref[d:d + 1, :] + al[:, sl])
        kt = k * (1.0 + (a - 1.0) * alpha)
        put(w_o, w, (d,))
        put(ka_o, kk * a, (d,))
        put(kt_o, kt, (d,))
        bonus = bonus + _seg_sum(r * kt * rho, ones_bd) * v
    put(r_o, r)
    put(kk_o, kk)
    v_o[...] = v
    g_o[...] = g
    bonus_o[...] = bonus


def rwkv_pre(u, P, l, geo, ones_bd, tile0, ntiles, seq_block=None, lane_tiled=None):
    hb = TILE // SHIFT_HALO
    nhalo = u.shape[0] // SHIFT_HALO

    def trio(c0):
        cb = c0 // D_RWKV
        return [
            pl.BlockSpec((TILE, D_RWKV), lambda i: (i + tile0, cb)),
            pl.BlockSpec((SHIFT_HALO, D_RWKV), lambda i: (jnp.maximum((i + tile0) * hb - 1, 0), cb)),
            pl.BlockSpec((SHIFT_HALO, D_RWKV), lambda i: (jnp.minimum((i + tile0 + 1) * hb, nhalo - 1), cb)),
        ]

    def full(shape):
        return pl.BlockSpec(shape, lambda i: tuple(0 for _ in shape))

    bw = jnp.concatenate([P["rwkv_bw"][l, 0], P["rwkv_bw"][l, 1]], axis=1).astype(BF16)
    ba = jnp.concatenate([P["rwkv_ba"][l, 0], P["rwkv_ba"][l, 1]], axis=1).astype(BF16)
    out = jax.ShapeDtypeStruct((ntiles * TILE, D_RWKV), F32)
    tok_spec = pl.BlockSpec((TILE, D_RWKV), lambda i: (i, 0))
    if seq_block is None:
        scan_spec, scan_out = tok_spec, out
    else:
        bl, ts = seq_block
        scan_out = jax.ShapeDtypeStruct((ntiles * TILE // bl, bl * D_RWKV), F32)
        scan_spec = pl.BlockSpec((TILE, D_RWKV),
                                 lambda i: ((i // ts // bl) * ts + i % ts, (i // ts) % bl))
    out_specs = [scan_spec] * 9 + [tok_spec] * 2
    out_shape = [scan_out] * 9 + [out] * 2
    if lane_tiled is not None:
        nlt = D_RWKV // LANES
        nseq = ntiles // lane_tiled
        k_spec = pl.BlockSpec((TILE, nlt, 2 * LANES), lambda i: (i % lane_tiled, 0, i // lane_tiled))
        k_out = jax.ShapeDtypeStruct((lane_tiled * TILE, nlt, nseq * 2 * LANES), F32)
        out_specs = [k_spec, k_spec, tok_spec, k_spec, k_spec, k_spec, tok_spec, tok_spec]
        out_shape = [k_out, k_out, out, k_out, k_out, k_out, out, out]
    return pl.pallas_call(
        functools.partial(_rwkv_pre_kernel, geo=geo, tile0=tile0),
        grid=(ntiles,),
        in_specs=trio(C_R) + trio(C_K) + trio(C_V) + [
            pl.BlockSpec((TILE, 256), lambda i: (i + tile0, C_LORA // 256)),
            full((3, D_RWKV)), full((2, D_RWKV)), full((W_RANK, 2 * D_RWKV)),
            full((2, D_RWKV)), full((A_RANK, 2 * D_RWKV)), full((G_RANK, D_RWKV)),
            full((1, D_RWKV)), full((1, D_RWKV)), full((1, D_RWKV)), full((D_RWKV, D_RWKV)),
        ],
        out_specs=out_specs,
        out_shape=out_shape,
        compiler_params=_cparams(("arbitrary",)),
        name="rwkv_pre",
    )(u, u, u, u, u, u, u, u, u, u,
      P["rwkv_mu"][l], P["rwkv_w0"][l], bw, P["rwkv_a0"][l], ba, P["rwkv_bg"][l].astype(BF16),
      P["rwkv_xi"][l].reshape(1, -1), P["rwkv_alpha"][l].reshape(1, -1),
      P["rwkv_rho"][l].reshape(1, -1), ones_bd)


N_VBLK = RWKV_HEAD // SUBLANES


def _scan_steps(xrow, vload, ostore, s_ref, tidx, *, tc, nk, kl, ngroups=2, unroll=1):
    gsz = N_VBLK // ngroups
    groups = tuple(tuple(range(g * gsz, (g + 1) * gsz)) for g in range(ngroups))

    def allred(a):
        if kl == 1:
            return a
        q = lax.broadcasted_iota(jnp.int32, a.shape, 1) % kl
        out = a
        for j in range(1, kl):
            out = out + jnp.where(q >= j, pltpu.roll(a, j, axis=1), 0.0)
            out = out + jnp.where(q < kl - j, pltpu.roll(a, LANES - j, axis=1), 0.0)
        return out

    def first_sa(grp, t):
        acc = [None] * len(grp)
        for kh in range(nk):
            kkb = xrow(t, 1, kh)
            for j, vb in enumerate(grp):
                pr = s_ref[vb, kh] * kkb
                acc[j] = pr if acc[j] is None else acc[j] + pr
        return tuple(allred(a) for a in acc)

    def fused(grp, s, t, t_next, sa):
        vv = [vload(t, vb) for vb in grp]
        oacc = [None] * len(grp)
        acc = [None] * len(grp)
        for kh in range(nk):
            wb, kab, ktb, rb = xrow(t, 0, kh), xrow(t, 2, kh), xrow(t, 3, kh), xrow(t, 4, kh)
            kkn = xrow(t_next, 1, kh)
            for j, vb in enumerate(grp):
                sn = s_ref[vb, kh] * wb - sa[j] * kab + vv[j] * ktb
                s_ref[vb, kh] = sn
                po = sn * rb
                pa = sn * kkn
                oacc[j] = po if oacc[j] is None else oacc[j] + po
                acc[j] = pa if acc[j] is None else acc[j] + pa
        for j, vb in enumerate(grp):
            ostore(s, t, vb, oacc[j])
        return tuple(allred(a) for a in acc)

    t0 = tidx(0)
    carry0 = tuple(first_sa(g, t0) for g in groups)

    def step(s, carry):
        t = tidx(s)
        t_next = tidx(jnp.minimum(s + 1, tc - 1))
        return tuple(fused(g, s, t, t_next, carry[i]) for i, g in enumerate(groups))

    lax.fori_loop(0, tc, step, carry0, unroll=unroll)


def _vrows(vb):
    return pl.ds(vb * SUBLANES, SUBLANES)


def _scan_ctx_kernel(w_ref, kk_ref, ka_ref, kt_ref, r_ref, v_ref, o_ref, sfin_ref, s_ref, *, tc, nk, rev):
    c = pl.program_id(1)
    x_refs = (w_ref, kk_ref, ka_ref, kt_ref, r_ref)

    @pl.when(c == 0)
    def _():
        s_ref[...] = jnp.zeros(s_ref.shape, F32)

    def xrow(t, a, kh):
        return jnp.broadcast_to(x_refs[a][0, t, pl.ds(kh, 1), :], (SUBLANES, LANES))

    def vload(t, vb):
        return v_ref[0, t, _vrows(vb), :]

    def ostore(s, t, vb, val):
        o_ref[0, t, _vrows(vb), :] = val

    def tidx(s):
        return tc - 1 - s if rev else s

    _scan_steps(xrow, vload, ostore, s_ref, tidx, tc=tc, nk=nk, kl=1)

    @pl.when(c == pl.num_programs(1) - 1)
    def _():
        sfin_ref[0] = s_ref[...]


def wkv_scan_ctx(w, kk, ka, kt, r, v, *, tc, rev):
    ngb, t_len, nk, _ = w.shape
    nch = t_len // tc
    blk = pl.BlockSpec((1, tc, nk, LANES), lambda g, c: (g, nch - 1 - c if rev else c, 0, 0))
    st = pl.BlockSpec((1, N_VBLK, nk, SUBLANES, LANES), lambda g, c: (g, 0, 0, 0, 0))
    return pl.pallas_call(
        functools.partial(_scan_ctx_kernel, tc=tc, nk=nk, rev=rev),
        grid=(ngb, nch),
        in_specs=[blk] * 6,
        out_specs=[blk, st],
        out_shape=[
            jax.ShapeDtypeStruct((ngb, t_len, RWKV_HEAD, LANES), F32),
            jax.ShapeDtypeStruct((ngb, N_VBLK, nk, SUBLANES, LANES), F32),
        ],
        scratch_shapes=[pltpu.VMEM((N_VBLK, nk, SUBLANES, LANES), F32)],
        compiler_params=_cparams(("arbitrary", "arbitrary")),
        name="wkv_scan_ctx",
    )(w, kk, ka, kt, r, v)


def _scan_lat_kernel(*refs, tc, nk, kl):
    xa_refs, xb_refs = refs[0:5], refs[5:10]
    va_ref, vb_ref, s0_ref, oa_ref, ob_ref, s_ref, xm_ref, vm_ref = refs[10:]
    c = pl.program_id(0)

    @pl.when(c == 0)
    def _():
        s_ref[...] = s0_ref[...]

    def is_bwd(shape):
        lane = lax.broadcasted_iota(jnp.int32, shape, len(shape) - 1)
        return (lane // N_RWKV_HEADS) % 2 == 1

    mx = is_bwd((nk, LANES))
    mv = is_bwd((RWKV_HEAD, LANES))

    def merge(s, carry):
        for a in range(5):
            xm_ref[a, s] = jnp.where(mx, xb_refs[a][tc - 1 - s], xa_refs[a][s])
        vm_ref[s] = jnp.where(mv, vb_ref[tc - 1 - s], va_ref[s])
        return carry

    lax.fori_loop(0, tc, merge, 0)

    def xrow(t, a, kh):
        return jnp.broadcast_to(xm_ref[a, t, pl.ds(kh, 1), :], (SUBLANES, LANES))

    def vload(t, vb):
        return vm_ref[t, _vrows(vb), :]

    def ostore(s, t, vb, val):
        oa_ref[s, _vrows(vb), :] = val
        ob_ref[tc - 1 - s, _vrows(vb), :] = val

    _scan_steps(xrow, vload, ostore, s_ref, lambda s: s, tc=tc, nk=nk, kl=kl, ngroups=2, unroll=2)


def wkv_scan_lat(xs, v, s0, *, tc, kl):
    t_len, nk, _ = xs[0].shape
    nch = t_len // tc
    o_sds = jax.ShapeDtypeStruct((t_len, RWKV_HEAD, LANES), F32)
    xa = pl.BlockSpec((tc, nk, LANES), lambda c: (c, 0, 0))
    xb = pl.BlockSpec((tc, nk, LANES), lambda c: (nch - 1 - c, 0, 0))
    va = pl.BlockSpec((tc, RWKV_HEAD, LANES), lambda c: (c, 0, 0))
    vb = pl.BlockSpec((tc, RWKV_HEAD, LANES), lambda c: (nch - 1 - c, 0, 0))
    return pl.pallas_call(
        functools.partial(_scan_lat_kernel, tc=tc, nk=nk, kl=kl),
        grid=(nch,),
        in_specs=[xa] * 5 + [xb] * 5 + [va, vb,
                                        pl.BlockSpec((N_VBLK, nk, SUBLANES, LANES), lambda c: (0, 0, 0, 0))],
        out_specs=[va, vb],
        out_shape=[o_sds, o_sds],
        scratch_shapes=[pltpu.VMEM((N_VBLK, nk, SUBLANES, LANES), F32),
                        pltpu.VMEM((5, tc, nk, LANES), F32),
                        pltpu.VMEM((tc, RWKV_HEAD, LANES), F32)],
        compiler_params=_cparams(("arbitrary",)),
        name="wkv_scan_lat",
    )(*xs, *xs, v, v, s0)


def _rwkv_post_kernel(of_ref, ob_ref, bonus_ref, g_ref, gate_ref, gng_ref, gnb_ref, ones_ref,
                      wo_ref, *rest):
    y_ref = rest[-1]
    o = of_ref[...] + ob_ref[...]
    ones_bd = ones_ref[...]
    mean = _seg_sum(o, ones_bd) * (1.0 / RWKV_HEAD)
    oc = o - mean
    var = _seg_sum(oc * oc, ones_bd) * (1.0 / RWKV_HEAD)
    gn = oc * lax.rsqrt(var + GN_EPS) * gng_ref[...] + gnb_ref[...]
    y = (gn + bonus_ref[...]) * g_ref[...]
    y = jnp.dot(y.astype(BF16), wo_ref[...], preferred_element_type=F32)
    y_ref[...] = _sigmoid(gate_ref[...]) * y


def rwkv_post(o_f, o_b, bonus, g, u, gng, gnb, ones_bd, wo, y_prev, tile0, seq_block=None):
    ntiles = bonus.shape[0] // TILE

    def full(shape):
        return pl.BlockSpec(shape, lambda i: tuple(0 for _ in shape))

    part = pl.BlockSpec((TILE, D_RWKV), lambda i: (i, 0))
    if seq_block is None:
        opart = part
    else:
        bl, ts = seq_block
        opart = pl.BlockSpec((TILE, D_RWKV), lambda i: ((i // ts // bl) * ts + i % ts, (i // ts) % bl))
    in_specs = [opart, opart, part, part,
                pl.BlockSpec((TILE, D_MODEL), lambda i: (i + tile0, C_G1 // D_MODEL)),
                full((1, D_RWKV)), full((1, D_RWKV)), full((D_RWKV, D_RWKV)), full((D_RWKV, D_MODEL))]
    args = [o_f, o_b, bonus, g, u, gng.reshape(1, -1), gnb.reshape(1, -1), ones_bd, wo.astype(BF16)]
    aliases = {}
    if y_prev is not None:
        in_specs.append(pl.BlockSpec(memory_space=pl.ANY))
        args.append(y_prev)
        aliases = {len(args) - 1: 0}
    return pl.pallas_call(
        _rwkv_post_kernel,
        grid=(ntiles,),
        in_specs=in_specs,
        out_specs=pl.BlockSpec((TILE, D_MODEL), lambda i: (i + tile0, 0)),
        out_shape=jax.ShapeDtypeStruct((u.shape[0], D_MODEL), F32),
        input_output_aliases=aliases,
        compiler_params=_cparams(("arbitrary",)),
        name="rwkv_post",
    )(*args)


QK_W = 2 * LANES
Q_SCALE = ATTN_SCALE * math.log2(math.e)


def _mla_pre_kernel(cq_ref, ckv_ref, kr_ref, cos_ref, sin_ref, qg_ref, kvg_ref, wuq_ref, wukt_ref,
                    qf_o, ckvn_o, krr_o):
    nq = N_MLA_HEADS * NOPE_DIM
    nrp = N_MLA_HEADS * LANES
    cq = _rms(cq_ref[...], qg_ref[...])
    q = jnp.dot(cq.astype(BF16), wuq_ref[...], preferred_element_type=F32)
    cos = cos_ref[...]
    sin = sin_ref[...]
    for h in range(N_MLA_HEADS):
        qn = q[:, h * NOPE_DIM:(h + 1) * NOPE_DIM].astype(BF16)
        qa = jnp.dot(qn, wukt_ref[h], preferred_element_type=F32)
        qr = (q[:, nq + h * LANES:nq + (h + 1) * LANES] * cos
              + q[:, nq + nrp + h * LANES:nq + nrp + (h + 1) * LANES] * sin)
        qf_o[:, h * QK_W:h * QK_W + LANES] = (qa * Q_SCALE).astype(qf_o.dtype)
        qf_o[:, h * QK_W + LANES:(h + 1) * QK_W] = (qr * Q_SCALE).astype(qf_o.dtype)
    ckvn_o[...] = _rms(ckv_ref[...], kvg_ref[...])
    kr = kr_ref[...]
    krr = kr[:, ROPE_DIM:2 * ROPE_DIM] * cos[:, :ROPE_DIM] + kr[:, 2 * ROPE_DIM:3 * ROPE_DIM] * sin[:, :ROPE_DIM]
    krr_o[...] = jnp.concatenate([krr, jnp.zeros((TILE, LANES - ROPE_DIM), F32)], axis=1)


def mla_pre(u, cos_t, sin_t, qg, kvg, wuq_p, wukt, geo):
    n = u.shape[0]

    def full(shape):
        return pl.BlockSpec(shape, lambda i: tuple(0 for _ in shape))

    return pl.pallas_call(
        _mla_pre_kernel,
        grid=(geo.ntile,),
        in_specs=[
            pl.BlockSpec((TILE, Q_RANK), lambda i: (i, C_CQ // Q_RANK)),
            pl.BlockSpec((TILE, KV_RANK), lambda i: (i, C_CKV // KV_RANK)),
            pl.BlockSpec((TILE, LANES), lambda i: (i, C_KR // LANES)),
            pl.BlockSpec((TILE, LANES), lambda i: (geo.rope_blk(i), 0)),
            pl.BlockSpec((TILE, LANES), lambda i: (geo.rope_blk(i), 0)),
            full((1, Q_RANK)), full((1, KV_RANK)),
            full(wuq_p.shape), full(wukt.shape),
        ],
        out_specs=[
            pl.BlockSpec((TILE, N_MLA_HEADS * QK_W), lambda i: (i, 0)),
            pl.BlockSpec((TILE, KV_RANK), lambda i: (i, 0)),
            pl.BlockSpec((TILE, LANES), lambda i: (i, 0)),
        ],
        out_shape=[
            jax.ShapeDtypeStruct((n, N_MLA_HEADS * QK_W), BF16),
            jax.ShapeDtypeStruct((n, KV_RANK), F32),
            jax.ShapeDtypeStruct((n, LANES), F32),
        ],
        compiler_params=_cparams(("arbitrary",)),
        name="mla_pre",
    )(u, u, u, cos_t, sin_t, qg.reshape(1, -1), kvg.reshape(1, -1), wuq_p, wukt)


ATT_KEY_CHUNK = 512
ATT_HEADS_PER_STEP = 4


def _attn_kernel(q_ref, kt_ref, c_ref, wuv_ref, wo_ref, gate_ref, *rest, ck):
    y_ref, pc_ref = rest[-2:]
    hg = pl.program_id(2)
    tq = q_ref.shape[0]
    t_k = kt_ref.shape[2]
    hp = ATT_HEADS_PER_STEP
    q = jnp.concatenate([q_ref[:, j * QK_W:(j + 1) * QK_W] for j in range(hp)], axis=0)
    m = jnp.full((hp * tq, 1), -jnp.inf, F32)
    l = jnp.zeros((hp * tq, 1), F32)
    acc = jnp.zeros((hp * tq, KV_RANK), F32)
    for c0 in range(0, t_k, ck):
        s = jnp.dot(q, kt_ref[0, :, c0:c0 + ck], preferred_element_type=F32)
        m_new = jnp.maximum(m, jnp.max(s, axis=-1, keepdims=True))
        alpha = jnp.exp2(m - m_new)
        p = jnp.exp2(s - m_new)
        l = alpha * l + jnp.sum(p, axis=-1, keepdims=True)
        acc = alpha * acc + jnp.dot(p.astype(BF16), c_ref[0, c0:c0 + ck, :], preferred_element_type=F32)
        m = m_new
    pc = (acc / l).astype(BF16)
    for j in range(hp):
        pc_ref[hg * hp + j] = pc[j * tq:(j + 1) * tq]

    @pl.when(hg == pl.num_programs(2) - 1)
    def _():
        pcs = jnp.concatenate([pc_ref[i] for i in range(N_MLA_HEADS)], axis=1)
        oh = jnp.dot(pcs, wuv_ref[...], preferred_element_type=F32)
        y = jnp.dot(oh.astype(BF16), wo_ref[...], preferred_element_type=F32)
        y_ref[...] = _sigmoid(gate_ref[...]) * y


def attention(qf, kct, cv, wuv_bd, wo, u, y_prev, *, row0, t_seq, tq, name):
    n = qf.shape[0]
    nseq, _, t_k = kct.shape
    qt = t_seq // tq
    rb0 = row0 // tq
    ck = math.gcd(ATT_KEY_CHUNK, t_k)

    def rows(w, col):
        return pl.BlockSpec((tq, w), lambda s, i, h: (rb0 + s * qt + i, col(h)))

    in_specs = [
        rows(ATT_HEADS_PER_STEP * QK_W, lambda h: h),
        pl.BlockSpec((1, QK_W, t_k), lambda s, i, h: (s, 0, 0)),
        pl.BlockSpec((1, t_k, KV_RANK), lambda s, i, h: (s, 0, 0)),
        pl.BlockSpec(wuv_bd.shape, lambda s, i, h: (0, 0)),
        pl.BlockSpec(wo.shape, lambda s, i, h: (0, 0)),
        rows(D_MODEL, lambda h: C_G2 // D_MODEL),
    ]
    args = [qf, kct, cv, wuv_bd, wo, u]
    aliases = {}
    if y_prev is not None:
        in_specs.append(pl.BlockSpec(memory_space=pl.ANY))
        args.append(y_prev)
        aliases = {len(args) - 1: 0}
    return pl.pallas_call(
        functools.partial(_attn_kernel, ck=ck),
        grid=(nseq, qt, N_MLA_HEADS // ATT_HEADS_PER_STEP),
        in_specs=in_specs,
        out_specs=rows(D_MODEL, lambda h: 0),
        out_shape=jax.ShapeDtypeStruct((n, D_MODEL), F32),
        scratch_shapes=[pltpu.VMEM((N_MLA_HEADS, tq, KV_RANK), BF16)],
        input_output_aliases=aliases,
        compiler_params=_cparams(("arbitrary", "arbitrary", "arbitrary")),
        name=name,
    )(*args)


R_I1, R_I2, R_G1, R_G2 = 0, 1, 2, 3


def _merge_kernel(x_ref, yc_ref, yr_ref, ym_ref, wout_ref, gpost_ref, g1_ref, gpre_ref, sc_ref,
                  sh_ref, router_ref, x1_o, h2_o, comb_o, *, moe):
    m = yc_ref[...] + yr_ref[...] + ym_ref[...]
    y = jnp.dot(m.astype(BF16), wout_ref[...], preferred_element_type=F32)
    x1 = x_ref[...] + g1_ref[0] * _rms(y, gpost_ref[...])
    x1_o[...] = x1
    h2 = _rms(x1, gpre_ref[...]) * (1.0 + sc_ref[0]) + sh_ref[0]
    h2_o[...] = h2.astype(h2_o.dtype)
    if moe:
        logits = jnp.dot(h2, router_ref[...], preferred_element_type=F32,
                         precision=lax.Precision.HIGHEST)
        lane = lax.broadcasted_iota(jnp.int32, logits.shape, 1)
        neg = jnp.float32(-jnp.inf)
        logits = jnp.where(lane < N_EXPERTS, logits, neg)
        m1 = jnp.max(logits, axis=-1, keepdims=True)
        i1 = jnp.min(jnp.where(logits == m1, lane, LANES), axis=-1, keepdims=True)
        rest = jnp.where(lane == i1, neg, logits)
        m2 = jnp.max(rest, axis=-1, keepdims=True)
        i2 = jnp.min(jnp.where(rest == m2, lane, LANES), axis=-1, keepdims=True)
        e2 = jnp.exp(m2 - m1)
        den = 1.0 + e2
        cols = ((R_I1, i1.astype(F32)), (R_I2, i2.astype(F32)), (R_G1, 1.0 / den), (R_G2, e2 / den))
        route = jnp.zeros(comb_o.shape, F32)
        for col, val in cols:
            route = jnp.where(lane == col, val, route)
        comb_o[...] = route
    else:
        comb_o[...] = jnp.ones(comb_o.shape, F32)


def merge(x, yc, yr, ym, wout, gpost, gpre, mod, router, geo, moe):
    n, d = x.shape

    def full(shape):
        return pl.BlockSpec(shape, lambda i: tuple(0 for _ in shape))

    def rows():
        return pl.BlockSpec((TILE, d), lambda i: (i, 0))

    def modspec(j):
        return pl.BlockSpec((1, 1, d), lambda i: (geo.mod_row(i) * 6 + j, 0, 0))

    router_p = jnp.zeros((d, LANES), F32).at[:, :N_EXPERTS].set(router)
    return pl.pallas_call(
        functools.partial(_merge_kernel, moe=moe),
        grid=(geo.ntile,),
        in_specs=[rows(), rows(), rows(), rows(), full((d, d)), full((1, d)), modspec(2),
                  full((1, d)), modspec(4), modspec(3), full((d, LANES))],
        out_specs=[rows(), rows(), pl.BlockSpec((TILE, LANES), lambda i: (i, 0))],
        out_shape=[jax.ShapeDtypeStruct((n, d), F32), jax.ShapeDtypeStruct((n, d), F32 if moe else BF16),
                   jax.ShapeDtypeStruct((n, LANES), F32)],
        compiler_params=_cparams(("arbitrary",)),
        name="merge",
    )(x, yc, yr, ym, wout.astype(BF16), gpost.reshape(1, d), mod, gpre.reshape(1, d), mod, mod,
      router_p)


def _ffn_kernel(h_ref, comb_ref, w1_ref, w3_ref, w2_ref, o_ref, acc_ref):
    e = pl.program_id(1)
    f = pl.program_id(2)

    @pl.when((e == 0) & (f == 0))
    def _():
        acc_ref[...] = jnp.zeros(acc_ref.shape, F32)

    h = h_ref[...]
    a = jnp.dot(h, w1_ref[0], preferred_element_type=F32)
    b = jnp.dot(h, w3_ref[0], preferred_element_type=F32)
    comb = comb_ref[...]
    lane = lax.broadcasted_iota(jnp.int32, comb.shape, 1)
    ce = jnp.sum(jnp.where(lane == e, comb, 0.0), axis=-1, keepdims=True)
    act = (a * _sigmoid(a) * b) * ce
    acc_ref[...] += jnp.dot(act.astype(BF16), w2_ref[0], preferred_element_type=F32)

    @pl.when((e == pl.num_programs(1) - 1) & (f == pl.num_programs(2) - 1))
    def _():
        o_ref[...] = acc_ref[...]


def ffn(h, comb, w1, w3, w2, *, tm, tf):
    n, d = h.shape
    ne, _, dff = w1.shape
    return pl.pallas_call(
        _ffn_kernel,
        grid=(n // tm, ne, dff // tf),
        in_specs=[
            pl.BlockSpec((tm, d), lambda i, e, f: (i, 0)),
            pl.BlockSpec((tm, LANES), lambda i, e, f: (i, 0)),
            pl.BlockSpec((1, d, tf), lambda i, e, f: (e, 0, f)),
            pl.BlockSpec((1, d, tf), lambda i, e, f: (e, 0, f)),
            pl.BlockSpec((1, tf, d), lambda i, e, f: (e, f, 0)),
        ],
        out_specs=pl.BlockSpec((tm, d), lambda i, e, f: (i, 0)),
        out_shape=jax.ShapeDtypeStruct((n, d), F32),
        scratch_shapes=[pltpu.VMEM((tm, d), F32)],
        compiler_params=_cparams(("arbitrary", "arbitrary", "arbitrary")),
        name="ffn",
    )(h, comb, w1, w3, w2)


MOE_TM = 512


def _row_copy(src_hbm, row, dst_vmem, r, sem):
    return pltpu.make_async_copy(src_hbm.at[pl.ds(row, 1)], dst_vmem.at[pl.ds(r, 1)], sem)


DMA_UNROLL = 8


def _ffn_sorted_kernel(te_ref, nused_ref, idx_ref, nxt_ref, h_hbm, w1_ref, w3_ref, w2_ref, o_ref,
                       acc_ref, xbuf_ref, sems):
    j = pl.program_id(0)
    f = pl.program_id(1)
    nj = pl.num_programs(0)
    tm = acc_ref.shape[0]
    slot = j % 2

    def issue(ids_ref, s):
        def body(i, c):
            for k in range(DMA_UNROLL):
                r = i * DMA_UNROLL + k
                _row_copy(h_hbm, ids_ref[0, 0, r], xbuf_ref.at[s], r, sems.at[s]).start(priority=k % 2)
            return c
        lax.fori_loop(0, tm // DMA_UNROLL, body, 0)

    def wait(s):
        def body(i, c):
            for k in range(DMA_UNROLL):
                _row_copy(h_hbm, 0, xbuf_ref.at[s], i * DMA_UNROLL + k, sems.at[s]).wait()
            return c
        lax.fori_loop(0, tm // DMA_UNROLL, body, 0)

    @pl.when(f == 0)
    def _():
        @pl.when(j == 0)
        def _():
            issue(idx_ref, 0)

        @pl.when(j + 1 < nj)
        def _():
            issue(nxt_ref, 1 - slot)

        wait(slot)

    @pl.when(j < nused_ref[0])
    def _():
        @pl.when(f == 0)
        def _():
            acc_ref[...] = jnp.zeros(acc_ref.shape, F32)

        h = xbuf_ref[slot].astype(BF16)
        a = jnp.dot(h, w1_ref[0], preferred_element_type=F32)
        b = jnp.dot(h, w3_ref[0], preferred_element_type=F32)
        act = a * _sigmoid(a) * b
        acc_ref[...] += jnp.dot(act.astype(BF16), w2_ref[0], preferred_element_type=F32)

        @pl.when(f == pl.num_programs(1) - 1)
        def _():
            o_ref[...] = acc_ref[...]


def ffn_sorted(h, row_src, te, nused, w1, w3, w2, *, tm, tf):
    p = row_src.shape[0]
    d = h.shape[1]
    dff = w1.shape[2]
    ntile = p // tm
    ids = row_src.reshape(ntile, 1, tm)
    return pl.pallas_call(
        _ffn_sorted_kernel,
        grid_spec=pltpu.PrefetchScalarGridSpec(
            num_scalar_prefetch=2,
            grid=(ntile, dff // tf),
            in_specs=[
                pl.BlockSpec((1, 1, tm), lambda j, f, te, nu: (j, 0, 0), memory_space=pltpu.SMEM),
                pl.BlockSpec((1, 1, tm), lambda j, f, te, nu: (jnp.minimum(j + 1, ntile - 1), 0, 0),
                             memory_space=pltpu.SMEM),
                pl.BlockSpec(memory_space=pl.ANY),
                pl.BlockSpec((1, d, tf), lambda j, f, te, nu: (te[j], 0, f)),
                pl.BlockSpec((1, d, tf), lambda j, f, te, nu: (te[j], 0, f)),
                pl.BlockSpec((1, tf, d), lambda j, f, te, nu: (te[j], f, 0)),
            ],
            out_specs=pl.BlockSpec((tm, d), lambda j, f, te, nu: (j, 0)),
            scratch_shapes=[pltpu.VMEM((tm, d), F32), pltpu.VMEM((2, tm, d), h.dtype),
                            pltpu.SemaphoreType.DMA((2,))],
        ),
        out_shape=jax.ShapeDtypeStruct((p, d), F32),
        compiler_params=_cparams(("arbitrary", "arbitrary")),
        name="ffn_sorted",
    )(te, nused, ids, ids, h, w1, w3, w2)


def _moe_combine_kernel(d0_ref, d1_ref, ys_hbm, route_ref, x_ref, g_ref, g2_ref, o_ref,
                        y0_ref, y1_ref, sem0, sem1):
    def start(i, c):
        for k in range(DMA_UNROLL):
            r = i * DMA_UNROLL + k
            _row_copy(ys_hbm, d0_ref[0, 0, r], y0_ref, r, sem0).start(priority=0)
            _row_copy(ys_hbm, d1_ref[0, 0, r], y1_ref, r, sem1).start(priority=1)
        return c

    def wait(i, c):
        for k in range(DMA_UNROLL):
            r = i * DMA_UNROLL + k
            _row_copy(ys_hbm, 0, y0_ref, r, sem0).wait()
            _row_copy(ys_hbm, 0, y1_ref, r, sem1).wait()
        return c

    lax.fori_loop(0, TILE // DMA_UNROLL, start, 0)
    lax.fori_loop(0, TILE // DMA_UNROLL, wait, 0)
    route = route_ref[...]
    lane = lax.broadcasted_iota(jnp.int32, route.shape, 1)
    ga = jnp.sum(jnp.where(lane == R_G1, route, 0.0), axis=-1, keepdims=True)
    gb = jnp.sum(jnp.where(lane == R_G2, route, 0.0), axis=-1, keepdims=True)
    y = ga * y0_ref[...] + gb * y1_ref[...]
    o_ref[...] = x_ref[...] + g2_ref[0] * _rms(y, g_ref[...])


def moe_combine_post(x1, ys, dest, route, gpost, mod, geo):
    n, d = x1.shape
    dd = dest.reshape(2, n // TILE, 1, TILE)
    return pl.pallas_call(
        _moe_combine_kernel,
        grid=(geo.ntile,),
        in_specs=[
            pl.BlockSpec((1, 1, TILE), lambda i: (i, 0, 0), memory_space=pltpu.SMEM),
            pl.BlockSpec((1, 1, TILE), lambda i: (i, 0, 0), memory_space=pltpu.SMEM),
            pl.BlockSpec(memory_space=pl.ANY),
            pl.BlockSpec((TILE, LANES), lambda i: (i, 0)),
            pl.BlockSpec((TILE, d), lambda i: (i, 0)),
            pl.BlockSpec((1, d), lambda i: (0, 0)),
            pl.BlockSpec((1, 1, d), lambda i: (geo.mod_row(i) * 6 + 5, 0, 0)),
        ],
        out_specs=pl.BlockSpec((TILE, d), lambda i: (i, 0)),
        out_shape=jax.ShapeDtypeStruct((n, d), F32),
        scratch_shapes=[pltpu.VMEM((TILE, d), F32), pltpu.VMEM((TILE, d), F32),
                        pltpu.SemaphoreType.DMA(()), pltpu.SemaphoreType.DMA(())],
        compiler_params=_cparams(("arbitrary",)),
        name="moe_combine",
    )(dd[0], dd[1], ys, route, x1, gpost.reshape(1, d), mod)


def _route_plan(route, tm):
    n = route.shape[0]
    e_flat = jnp.concatenate([route[:, R_I1], route[:, R_I2]]).astype(jnp.int32)
    onehot = (e_flat[:, None] == jnp.arange(N_EXPERTS, dtype=jnp.int32)[None, :]).astype(jnp.int32)
    csum = jnp.cumsum(onehot, axis=0)
    rank = jnp.take_along_axis(csum, e_flat[:, None], axis=1)[:, 0] - 1
    counts = csum[-1]
    ptiles = (counts + tm - 1) // tm
    tile_end = jnp.cumsum(ptiles)
    gstart = (tile_end - ptiles) * tm
    dest = jnp.take(gstart, e_flat) + rank
    p_max = 2 * n + N_EXPERTS * tm
    tok = jnp.tile(jnp.arange(n, dtype=jnp.int32), 2)
    row_src = jnp.zeros((p_max,), jnp.int32).at[dest].set(tok)
    tiles = jnp.arange(p_max // tm, dtype=jnp.int32)
    te = jnp.minimum(jnp.searchsorted(tile_end, tiles, side="right"), N_EXPERTS - 1).astype(jnp.int32)
    return row_src, dest.astype(jnp.int32).reshape(2, n), te, tile_end[-1:].astype(jnp.int32)


def _ffn_post_kernel(x_ref, y_ref, g_ref, g2_ref, o_ref):
    o_ref[...] = x_ref[...] + g2_ref[0] * _rms(y_ref[...], g_ref[...])


def ffn_post(x1, y, gpost, mod, geo):
    n, d = x1.shape
    return pl.pallas_call(
        _ffn_post_kernel,
        grid=(geo.ntile,),
        in_specs=[
            pl.BlockSpec((TILE, d), lambda i: (i, 0)),
            pl.BlockSpec((TILE, d), lambda i: (i, 0)),
            pl.BlockSpec((1, d), lambda i: (0, 0)),
            pl.BlockSpec((1, 1, d), lambda i: (geo.mod_row(i) * 6 + 5, 0, 0)),
        ],
        out_specs=pl.BlockSpec((TILE, d), lambda i: (i, 0)),
        out_shape=jax.ShapeDtypeStruct((n, d), F32),
        compiler_params=_cparams(("arbitrary",)),
        name="ffn_post",
    )(x1, y, gpost.reshape(1, d), mod)


def _ctx_scan(geo, pre):
    r, kk, v, w0, w1, ka0, ka1, kt0, kt1 = pre
    H, K = N_RWKV_HEADS, RWKV_HEAD
    bl = LANES // H
    nbh = geo.n_ctx // bl

    def to_scan(a):
        return jnp.swapaxes(a.reshape(nbh, geo.t_ctx, LANES, K), 2, 3)

    def from_scan(o):
        return jnp.swapaxes(o, 2, 3).reshape(nbh * geo.t_ctx, bl * H * K)

    def state(sf):
        sf = sf.reshape(nbh, K // SUBLANES, K, SUBLANES, bl, H).transpose(0, 4, 5, 1, 3, 2)
        return sf.reshape(geo.n_ctx, H, K, K)

    kk_s, r_s, v_s = to_scan(kk), to_scan(r), to_scan(v)
    tc = min(32, geo.t_ctx)
    o_f, sf_f = wkv_scan_ctx(to_scan(w0), kk_s, to_scan(ka0), to_scan(kt0), r_s, v_s, tc=tc, rev=False)
    o_b, sf_b = wkv_scan_ctx(to_scan(w1), kk_s, to_scan(ka1), to_scan(kt1), r_s, v_s, tc=tc, rev=True)
    return from_scan(o_f), from_scan(o_b), jnp.stack([state(sf_f), state(sf_b)], axis=1)


def _lat_scan(geo, pre, state_l):
    r, kk, v, w, ka, kt = pre
    H, K = N_RWKV_HEADS, RWKV_HEAD
    nb, t_len = geo.n_lat, geo.t_lat
    hp = H // 2
    kl = LANES // (nb * 2 * H)
    nk = K // kl

    def to_scan(a):
        return jnp.swapaxes(a.reshape(t_len, LANES, nk), 1, 2)

    xs = (to_scan(w), to_scan(kk), to_scan(ka), to_scan(kt), to_scan(r))
    vl = v.reshape(nb, t_len, hp, 1, 2, 1, K)
    vl = jnp.broadcast_to(vl, (nb, t_len, hp, 2, 2, kl, K))
    v_lat = vl.transpose(1, 6, 2, 0, 3, 4, 5).reshape(t_len, K, LANES)
    s0 = state_l.reshape(nb, 2, hp, 2, K // SUBLANES, SUBLANES, kl, nk)
    s0 = s0.transpose(4, 7, 5, 2, 0, 1, 3, 6).reshape(K // SUBLANES, nk, SUBLANES, LANES)
    oa, ob = wkv_scan_lat(xs, v_lat, s0, tc=min(64, t_len), kl=kl)

    def direction(o, d):
        o = o.reshape(t_len, K, hp, nb, 2, 2, kl)[:, :, :, :, d].sum(axis=-1)
        return o.transpose(3, 0, 2, 4, 1).reshape(geo.nl_tok, H * K)

    return direction(oa, 0), direction(ob, 1)


def _rope_tables(geo):
    n_freq = ROPE_DIM // 4
    rows = geo.t_lat // GRID_W
    row = jnp.repeat(jnp.arange(rows, dtype=F32), GRID_W)
    col = jnp.tile(jnp.arange(GRID_W, dtype=F32), rows)
    inv = ROPE_BASE ** (-jnp.arange(n_freq, dtype=F32) / n_freq)
    ang = jnp.concatenate([row[:, None] * inv, col[:, None] * inv], axis=-1)
    cos, sin = jnp.cos(ang), jnp.sin(ang)
    cos32 = jnp.concatenate([cos, cos], axis=-1)
    sin32 = jnp.concatenate([-sin, sin], axis=-1)
    cos_t = jnp.concatenate([jnp.ones((TILE, ROPE_DIM), F32), cos32], axis=0)
    sin_t = jnp.concatenate([jnp.zeros((TILE, ROPE_DIM), F32), sin32], axis=0)
    pad = ((0, 0), (0, LANES - ROPE_DIM))
    return jnp.pad(cos_t, pad), jnp.pad(sin_t, pad)


_DEINT = np.concatenate([np.arange(0, ROPE_DIM, 2), np.arange(1, ROPE_DIM, 2)])
_DEINT_SW = np.concatenate([np.arange(1, ROPE_DIM, 2), np.arange(0, ROPE_DIM, 2)])


def _prep_w_in(w_in, b_in):
    offs = np.cumsum([0, 2 * D_CONV, D_RWKV, D_RWKV, D_RWKV, W_RANK, A_RANK, G_RANK, Q_RANK, KV_RANK,
                      ROPE_DIM, N_BRANCH * D_MODEL])
    o_conv, o_r, o_k, o_v, o_dw, o_da, o_dg, o_cq, o_ckv, o_kr, o_gate, _ = offs
    idx = np.zeros((IN_PAD,), np.int32)
    valid = np.zeros((IN_PAD,), bool)

    def put(dst, src):
        idx[dst:dst + len(src)] = src
        valid[dst:dst + len(src)] = True

    put(C_CONV, np.arange(o_conv, o_conv + 2 * D_CONV))
    for b, c in enumerate((C_G0, C_G1, C_G2)):
        put(c, np.arange(o_gate + b * D_MODEL, o_gate + (b + 1) * D_MODEL))
    put(C_R, np.arange(o_r, o_r + D_RWKV))
    put(C_K, np.arange(o_k, o_k + D_RWKV))
    put(C_V, np.arange(o_v, o_v + D_RWKV))
    put(C_LORA, np.arange(o_dw, o_dw + W_RANK + A_RANK + G_RANK))
    put(C_CQ, np.arange(o_cq, o_cq + Q_RANK))
    put(C_CKV, np.arange(o_ckv, o_ckv + KV_RANK))
    put(C_KR, np.arange(o_kr, o_kr + ROPE_DIM))
    put(C_KR + ROPE_DIM, o_kr + _DEINT)
    put(C_KR + 2 * ROPE_DIM, o_kr + _DEINT_SW)
    w = jnp.where(valid[None, :], jnp.take(w_in, idx, axis=1), 0.0).astype(BF16)
    b = jnp.where(valid, jnp.take(b_in, idx), 0.0)
    return w, b


def _prep_wuq(wuq):
    hd = NOPE_DIM + ROPE_DIM
    nq = N_MLA_HEADS * NOPE_DIM
    ncol = nq + 2 * N_MLA_HEADS * LANES
    idx = np.zeros((ncol,), np.int32)
    valid = np.zeros((ncol,), bool)
    for h in range(N_MLA_HEADS):
        idx[h * NOPE_DIM:(h + 1) * NOPE_DIM] = h * hd + np.arange(NOPE_DIM)
        valid[h * NOPE_DIM:(h + 1) * NOPE_DIM] = True
        for blk, perm in enumerate((_DEINT, _DEINT_SW)):
            c0 = nq + blk * N_MLA_HEADS * LANES + h * LANES
            idx[c0:c0 + ROPE_DIM] = h * hd + NOPE_DIM + perm
            valid[c0:c0 + ROPE_DIM] = True
    return jnp.where(valid[None, :], jnp.take(wuq, idx, axis=1), 0.0).astype(BF16)


def _block_diag_heads(wuv):
    w3 = wuv.reshape(KV_RANK, N_MLA_HEADS, V_DIM)
    eye = jnp.eye(N_MLA_HEADS, dtype=wuv.dtype)
    bd = w3.transpose(1, 0, 2)[:, :, None, :] * eye[:, None, :, None]
    return bd.reshape(N_MLA_HEADS * KV_RANK, N_MLA_HEADS * V_DIM).astype(BF16)


def _attention_keys(geo, ckvn, krr, cache_c, cache_kr):
    nc = geo.nc_tok
    pad = QK_W - KV_RANK - ROPE_DIM

    def keys(c, kr):
        kc = jnp.concatenate([c, kr, jnp.zeros(c.shape[:2] + (pad,), F32)], axis=-1).astype(BF16)
        return kc.transpose(0, 2, 1), c.astype(BF16)

    kr32 = krr[:, :ROPE_DIM]
    kct_c, cv_c = keys(ckvn[:nc].reshape(geo.n_ctx, geo.t_ctx, KV_RANK),
                       kr32[:nc].reshape(geo.n_ctx, geo.t_ctx, ROPE_DIM))
    c_l = jnp.concatenate([ckvn[nc:].reshape(geo.n_lat, geo.t_lat, KV_RANK), cache_c], axis=1)
    kr_l = jnp.concatenate([kr32[nc:].reshape(geo.n_lat, geo.t_lat, ROPE_DIM), cache_kr], axis=1)
    kct_l, cv_l = keys(c_l, kr_l)
    return kct_c, cv_c, kct_l, cv_l


def kernel(x_prompt, x_sample, cache_ckv, cache_krope, state_wkv, c, c_ctx, ada_w, ada_b, norm_mix_pre, norm_mix_post, norm_ffn_pre, norm_ffn_post, w_in, b_in, conv_w, conv_b, conv_ln_g, conv_ln_b, conv_wo, rwkv_mu, rwkv_w0, rwkv_bw, rwkv_a0, rwkv_ba, rwkv_bg, rwkv_xi, rwkv_alpha, rwkv_rho, rwkv_gn_g, rwkv_gn_b, rwkv_wo, mla_q_norm, mla_wuq, mla_kv_norm, mla_wuk, mla_wuv, mla_wo, w_out, ffn_w1, ffn_w3, ffn_w2, moe_router, moe_w1, moe_w3, moe_w2):
    P = dict(rwkv_mu=rwkv_mu, rwkv_w0=rwkv_w0, rwkv_bw=rwkv_bw, rwkv_a0=rwkv_a0, rwkv_ba=rwkv_ba,
             rwkv_bg=rwkv_bg, rwkv_xi=rwkv_xi, rwkv_alpha=rwkv_alpha, rwkv_rho=rwkv_rho)
    n_ctx, t_ctx, d = x_prompt.shape
    n_lat, t_lat, _ = x_sample.shape
    depth = ada_w.shape[0]
    past = cache_ckv.shape[2]
    geo = Geo(n_ctx, t_ctx, n_lat, t_lat, past)
    assert d == D_MODEL and (n_ctx * N_RWKV_HEADS) % LANES == 0 and LANES % (n_lat * 2 * N_RWKV_HEADS) == 0

    x = jnp.concatenate([x_prompt.reshape(-1, d), x_sample.reshape(-1, d)], axis=0)

    n_mod = 1 + n_lat
    n_mod_pad = -(-n_mod // SUBLANES) * SUBLANES
    c_all = jnp.zeros((n_mod_pad, d), F32).at[0].set(c_ctx).at[1:n_mod].set(c)
    head_id = np.arange(D_RWKV) // RWKV_HEAD
    ones_bd = jnp.asarray((head_id[:, None] == head_id[None, :]).astype(np.float32))
    cos_t, sin_t = _rope_tables(geo)

    ckv_out, kr_out, st_out = [], [], []
    for l in range(depth):
        mod = matmul_bias(c_all, ada_w[l].astype(BF16), ada_b[l], tm=n_mod_pad, tn=6 * d // 4,
                          pre="silu", name="ada_mod")
        mod = mod[:n_mod].reshape(n_mod * 6, 1, d)

        h = prenorm(x, norm_mix_pre[l], mod, geo, j_shift=0, j_scale=1)
        w_in_p, b_in_p = _prep_w_in(w_in[l], b_in[l])
        u = matmul_bias(h, w_in_p, b_in_p, tm=math.gcd(1024, geo.ntok), tn=1280, name="in_proj")

        y_conv = conv_branch(u, conv_w[l], conv_b[l], conv_ln_g[l], conv_ln_b[l], conv_wo[l], geo)

        ctx_block = (LANES // N_RWKV_HEADS, geo.ct)
        pre_c = rwkv_pre(u, P, l, geo, ones_bd, 0, geo.nct, seq_block=ctx_block)
        pre_l = rwkv_pre(u, P, l, geo, ones_bd, geo.nct, geo.nlt, lane_tiled=geo.lt)
        oc_f, oc_b, sfin = _ctx_scan(geo, pre_c[:9])
        ol_f, ol_b = _lat_scan(geo, pre_l[:6], state_wkv[:, l])
        y_rwkv = rwkv_post(ol_f, ol_b, pre_l[7], pre_l[6], u, rwkv_gn_g[l], rwkv_gn_b[l], ones_bd,
                           rwkv_wo[l], None, geo.nct)
        y_rwkv = rwkv_post(oc_f, oc_b, pre_c[10], pre_c[9], u, rwkv_gn_g[l], rwkv_gn_b[l], ones_bd,
                           rwkv_wo[l], y_rwkv, 0, seq_block=ctx_block)

        wukt = mla_wuk[l].reshape(KV_RANK, N_MLA_HEADS, NOPE_DIM).transpose(1, 2, 0).astype(BF16)
        qf, ckvn, krr = mla_pre(u, cos_t, sin_t, mla_q_norm[l], mla_kv_norm[l],
                                _prep_wuq(mla_wuq[l]), wukt, geo)
        kct_c, cv_c, kct_l, cv_l = _attention_keys(geo, ckvn, krr, cache_ckv[:, l],
                                                   cache_krope[:, l][..., _DEINT])
        wuv_bd = _block_diag_heads(mla_wuv[l])
        wo_b = mla_wo[l].astype(BF16)
        y_mla = attention(qf, kct_l, cv_l, wuv_bd, wo_b, u, None, row0=geo.nc_tok, t_seq=t_lat,
                          tq=TILE, name="attention_lat")
        y_mla = attention(qf, kct_c, cv_c, wuv_bd, wo_b, u, y_mla, row0=0, t_seq=t_ctx,
                          tq=TILE, name="attention_ctx")

        moe = (l % 2 == 1)
        i = l // 2
        router = moe_router[i] if moe else jnp.zeros((d, N_EXPERTS), F32)
        x1, h2, comb = merge(x, y_conv, y_rwkv, y_mla, w_out[l], norm_mix_post[l], norm_ffn_pre[l],
                             mod, router, geo, moe)
        if moe:
            row_src, dest, te, nused = _route_plan(comb, MOE_TM)
            ys = ffn_sorted(h2, row_src, te, nused, moe_w1[i].astype(BF16), moe_w3[i].astype(BF16),
                            moe_w2[i].astype(BF16), tm=MOE_TM, tf=D_FF // 2)
            x = moe_combine_post(x1, ys, dest, comb, norm_ffn_post[l], mod, geo)
        else:
            y = ffn(h2, comb, ffn_w1[i:i + 1].astype(BF16), ffn_w3[i:i + 1].astype(BF16),
                    ffn_w2[i:i + 1].astype(BF16), tm=512, tf=D_FF // 2)
            x = ffn_post(x1, y, norm_ffn_post[l], mod, geo)

        ckv_out.append(ckvn[:geo.nc_tok].reshape(n_ctx, t_ctx, KV_RANK))
        kr_out.append(u[:geo.nc_tok, C_KR:C_KR + ROPE_DIM].reshape(n_ctx, t_ctx, ROPE_DIM))
        st_out.append(sfin)

    y_prompt = x[:geo.nc_tok].reshape(n_ctx, t_ctx, d)
    y_sample = x[geo.nc_tok:].reshape(n_lat, t_lat, d)
    return (y_prompt, y_sample, jnp.stack(ckv_out, axis=1), jnp.stack(kr_out, axis=1),
            jnp.stack(st_out, axis=1))
```

```python
import functools
import math

import numpy as np
import jax
import jax.numpy as jnp
from jax import lax
from jax.experimental import pallas as pl
from jax.experimental.pallas import tpu as pltpu

F32 = jnp.float32
BF16 = jnp.bfloat16

D_MODEL = 1024
GRID_W = 64
D_CONV = 512
CONV_K = 31
D_RWKV = 512
RWKV_HEAD = 64
N_RWKV_HEADS = D_RWKV // RWKV_HEAD
W_RANK = 64
A_RANK = 64
G_RANK = 128
DECAY_SCALE = math.exp(-0.5)
GN_EPS = 64e-5
N_MLA_HEADS = 8
Q_RANK = 256
KV_RANK = 128
NOPE_DIM = 64
ROPE_DIM = 32
V_DIM = 64
ROPE_BASE = 10000.0
ATTN_SCALE = 1.0 / math.sqrt(NOPE_DIM + ROPE_DIM)
N_BRANCH = 3
D_FF = 2816
N_EXPERTS = 8
EPS = 1e-6

LANES = 128
SUBLANES = 8
VMEM_LIMIT_BYTES = 56 * 1024 * 1024

TILE = 256
CONV_HALO = 16
SHIFT_HALO = 8

C_CONV, C_G0, C_G1, C_G2 = 0, 1024, 2048, 3072
C_R, C_K, C_V = 4096, 4608, 5120
C_LORA = 5632
C_CQ = 5888
C_CKV = 6144
C_KR = 6272
IN_PAD = 6400


def _cparams(sem):
    return pltpu.CompilerParams(dimension_semantics=sem, vmem_limit_bytes=VMEM_LIMIT_BYTES)


def _sigmoid(x):
    return jax.nn.sigmoid(x)


def _rms(x, g):
    return (x * lax.rsqrt(jnp.mean(x * x, axis=-1, keepdims=True) + EPS)) * g


class Geo:
    def __init__(self, n_ctx, t_ctx, n_lat, t_lat, past):
        assert t_ctx % TILE == 0 and t_lat % TILE == 0
        self.n_ctx, self.t_ctx, self.n_lat, self.t_lat, self.past = n_ctx, t_ctx, n_lat, t_lat, past
        self.ct = t_ctx // TILE
        self.lt = t_lat // TILE
        self.nct = n_ctx * self.ct
        self.nlt = n_lat * self.lt
        self.ntile = self.nct + self.nlt
        self.nc_tok = n_ctx * t_ctx
        self.nl_tok = n_lat * t_lat
        self.ntok = self.nc_tok + self.nl_tok

    def pos(self, i):
        is_ctx = i < self.nct
        p = jnp.where(is_ctx, i % self.ct, (i - self.nct) % self.lt)
        n = jnp.where(is_ctx, self.ct, self.lt)
        return p, n

    def mod_row(self, i):
        return jnp.where(i < self.nct, 0, 1 + (i - self.nct) // self.lt)

    def rope_blk(self, i):
        return jnp.where(i < self.nct, 0, 1 + (i - self.nct) % self.lt)


def _mm_kernel(x_ref, w_ref, b_ref, o_ref, *, pre):
    x = x_ref[...]
    if pre == "silu":
        x = x.astype(F32)
        x = x * _sigmoid(x)
    acc = jnp.dot(x.astype(BF16), w_ref[...], preferred_element_type=F32)
    o_ref[...] = (acc + b_ref[...]).astype(o_ref.dtype)


def matmul_bias(x, w, b, *, tm, tn, pre=None, out_dtype=F32, name="matmul"):
    m, k = x.shape
    n = w.shape[1]
    assert m % tm == 0 and n % tn == 0
    return pl.pallas_call(
        functools.partial(_mm_kernel, pre=pre),
        grid=(n // tn, m // tm),
        in_specs=[
            pl.BlockSpec((tm, k), lambda j, i: (i, 0)),
            pl.BlockSpec((k, tn), lambda j, i: (0, j)),
            pl.BlockSpec((1, tn), lambda j, i: (0, j)),
        ],
        out_specs=pl.BlockSpec((tm, tn), lambda j, i: (i, j)),
        out_shape=jax.ShapeDtypeStruct((m, n), out_dtype),
        compiler_params=_cparams(("arbitrary", "arbitrary")),
        name=name,
    )(x, w, b.reshape(1, n).astype(F32))


def _prenorm_kernel(x_ref, g_ref, sc_ref, sh_ref, o_ref):
    h = _rms(x_ref[...], g_ref[...]) * (1.0 + sc_ref[0]) + sh_ref[0]
    o_ref[...] = h.astype(o_ref.dtype)


def prenorm(x, g, mod, geo, j_shift, j_scale):
    n, d = x.shape
    return pl.pallas_call(
        _prenorm_kernel,
        grid=(geo.ntile,),
        in_specs=[
            pl.BlockSpec((TILE, d), lambda i: (i, 0)),
            pl.BlockSpec((1, d), lambda i: (0, 0)),
            pl.BlockSpec((1, 1, d), lambda i: (geo.mod_row(i) * 6 + j_scale, 0, 0)),
            pl.BlockSpec((1, 1, d), lambda i: (geo.mod_row(i) * 6 + j_shift, 0, 0)),
        ],
        out_specs=pl.BlockSpec((TILE, d), lambda i: (i, 0)),
        out_shape=jax.ShapeDtypeStruct((n, d), BF16),
        compiler_params=_cparams(("arbitrary",)),
        name="prenorm",
    )(x, g.reshape(1, d), mod, mod)


def _conv_kernel(cur_ref, prev_ref, next_ref, gate_ref, cw_ref, cb_ref, lng_ref, lnb_ref,
                 wo_ref, o_ref, hp_ref, *, geo):
    i = pl.program_id(0)
    p, n = geo.pos(i)
    has_prev = p > 0
    has_next = p < n - 1

    def glu(z):
        return z[:, :D_CONV] * _sigmoid(z[:, D_CONV:])

    hp_ref[0, 0:CONV_HALO, :] = jnp.where(has_prev, glu(prev_ref[...]), 0.0)
    hp_ref[0, CONV_HALO:CONV_HALO + TILE, :] = glu(cur_ref[...])
    hp_ref[0, CONV_HALO + TILE:, :] = jnp.where(has_next, glu(next_ref[...]), 0.0)
    nrow = TILE + 2 * CONV_HALO - SUBLANES
    for b in range(1, SUBLANES):
        hp_ref[b, 0:nrow, :] = hp_ref[0, pl.ds(b, nrow), :]

    off = CONV_HALO - CONV_K // 2
    acc = None
    for j in range(CONV_K):
        q = off + j
        term = hp_ref[q % SUBLANES, pl.ds(q - q % SUBLANES, TILE), :] * cw_ref[j:j + 1, :]
        acc = term if acc is None else acc + term
    h = acc + cb_ref[...]
    mu = jnp.mean(h, axis=-1, keepdims=True)
    hc = h - mu
    var = jnp.mean(hc * hc, axis=-1, keepdims=True)
    h = hc * lax.rsqrt(var + EPS) * lng_ref[...] + lnb_ref[...]
    h = h * _sigmoid(h)
    y = jnp.dot(h.astype(BF16), wo_ref[...], preferred_element_type=F32)
    o_ref[...] = _sigmoid(gate_ref[...]) * y


def conv_branch(u, cw, cb, lng, lnb, wo, geo):
    n = u.shape[0]
    hb = TILE // CONV_HALO
    nhalo = n // CONV_HALO
    cwp = jnp.zeros((32, D_CONV), F32).at[:CONV_K].set(cw)
    return pl.pallas_call(
        functools.partial(_conv_kernel, geo=geo),
        grid=(geo.ntile,),
        in_specs=[
            pl.BlockSpec((TILE, 2 * D_CONV), lambda i: (i, C_CONV // (2 * D_CONV))),
            pl.BlockSpec((CONV_HALO, 2 * D_CONV), lambda i: (jnp.maximum(i * hb - 1, 0), 0)),
            pl.BlockSpec((CONV_HALO, 2 * D_CONV), lambda i: (jnp.minimum((i + 1) * hb, nhalo - 1), 0)),
            pl.BlockSpec((TILE, D_MODEL), lambda i: (i, C_G0 // D_MODEL)),
            pl.BlockSpec((32, D_CONV), lambda i: (0, 0)),
            pl.BlockSpec((1, D_CONV), lambda i: (0, 0)),
            pl.BlockSpec((1, D_CONV), lambda i: (0, 0)),
            pl.BlockSpec((1, D_CONV), lambda i: (0, 0)),
            pl.BlockSpec((D_CONV, D_MODEL), lambda i: (0, 0)),
        ],
        out_specs=pl.BlockSpec((TILE, D_MODEL), lambda i: (i, 0)),
        out_shape=jax.ShapeDtypeStruct((n, D_MODEL), F32),
        scratch_shapes=[pltpu.VMEM((SUBLANES, TILE + 2 * CONV_HALO, D_CONV), F32)],
        compiler_params=_cparams(("arbitrary",)),
        name="conv_branch",
    )(u, u, u, u, cwp, cb.reshape(1, -1), lng.reshape(1, -1), lnb.reshape(1, -1), wo.astype(BF16))


def _seg_sum(x, ones_bd):
    return jnp.dot(x, ones_bd, preferred_element_type=F32, precision=lax.Precision.HIGHEST)


def _rwkv_pre_kernel(r_ref, rp_ref, rn_ref, k_ref, kp_ref, kn_ref, v_ref, vp_ref, vn_ref,
                     lora_ref, mu_ref, w0_ref, bw_ref, a0_ref, ba_ref, bg_ref, xi_ref, al_ref,
                     rho_ref, ones_ref, *outs, geo, tile0):
    if len(outs) == 11:
        r_o, kk_o, v_o, w0_o, w1_o, ka0_o, ka1_o, kt0_o, kt1_o, g_o, bonus_o = outs
    else:
        r_o, kk_o, v_o, w0_o, ka0_o, kt0_o, g_o, bonus_o = outs
        w1_o, ka1_o, kt1_o = w0_o, ka0_o, kt0_o
    i = pl.program_id(0) + tile0
    p, n = geo.pos(i)
    has_prev = p > 0
    has_next = p < n - 1
    row = lax.broadcasted_iota(jnp.int32, (TILE, D_RWKV), 0)

    def put(o_ref, val, dirs=(0, 1)):
        if len(o_ref.shape) == 2:
            o_ref[...] = val
        else:
            for q in range(o_ref.shape[1]):
                for d in dirs:
                    o_ref[:, q, d * LANES:(d + 1) * LANES] = val[:, q * LANES:(q + 1) * LANES]

    def shifted(c_ref, p_ref, n_ref, mu):
        cur = c_ref[...]
        pv = jnp.where(has_prev, p_ref[SHIFT_HALO - 1:SHIFT_HALO, :], 0.0)
        nx = jnp.where(has_next, n_ref[0:1, :], 0.0)
        prev = jnp.where(row == 0, pv, pltpu.roll(cur, 1, axis=0))
        nxt = jnp.where(row == TILE - 1, nx, pltpu.roll(cur, TILE - 1, axis=0))
        return cur + mu * (0.5 * (prev + nxt) - cur)

    r = shifted(r_ref, rp_ref, rn_ref, mu_ref[0:1, :])
    k = shifted(k_ref, kp_ref, kn_ref, mu_ref[1:2, :])
    v = shifted(v_ref, vp_ref, vn_ref, mu_ref[2:3, :])

    lora = lora_ref[...]
    dw = jnp.tanh(lora[:, :W_RANK]).astype(BF16)
    da = lora[:, W_RANK:W_RANK + A_RANK].astype(BF16)
    dg = _sigmoid(lora[:, W_RANK + A_RANK:]).astype(BF16)
    wl = jnp.dot(dw, bw_ref[...], preferred_element_type=F32)
    al = jnp.dot(da, ba_ref[...], preferred_element_type=F32)
    g = jnp.dot(dg, bg_ref[...], preferred_element_type=F32)

    ones_bd = ones_ref[...]
    kx = k * xi_ref[...]
    kk = kx * lax.rsqrt(_seg_sum(kx * kx, ones_bd) + EPS)
    alpha = al_ref[...]
    rho = rho_ref[...]
    bonus = jnp.zeros((TILE, D_RWKV), F32)
    for d, (w_o, ka_o, kt_o) in enumerate(((w0_o, ka0_o, kt0_o), (w1_o, ka1_o, kt1_o))):
        sl = slice(d * D_RWKV, (d + 1) * D_RWKV)
        w = jnp.exp(-DECAY_SCALE * _sigmoid(w0_ref[d:d + 1, :] + wl[:, sl]))
        a = _sigmoid(a0_ref[d:d + 1, :] + al[:, sl])
        kt = k * (1.0 + (a - 1.0) * alpha)
        put(w_o, w, (d,))
        put(ka_o, kk * a, (d,))
        put(kt_o, kt, (d,))
        bonus = bonus + _seg_sum(r * kt * rho, ones_bd) * v
    put(r_o, r)
    put(kk_o, kk)
    v_o[...] = v
    g_o[...] = g
    bonus_o[...] = bonus


def rwkv_pre(u, P, l, geo, ones_bd, tile0, ntiles, seq_block=None, lane_tiled=None):
    hb = TILE // SHIFT_HALO
    nhalo = u.shape[0] // SHIFT_HALO

    def trio(c0):
        cb = c0 // D_RWKV
        return [
            pl.BlockSpec((TILE, D_RWKV), lambda i: (i + tile0, cb)),
            pl.BlockSpec((SHIFT_HALO, D_RWKV), lambda i: (jnp.maximum((i + tile0) * hb - 1, 0), cb)),
            pl.BlockSpec((SHIFT_HALO, D_RWKV), lambda i: (jnp.minimum((i + tile0 + 1) * hb, nhalo - 1), cb)),
        ]

    def full(shape):
        return pl.BlockSpec(shape, lambda i: tuple(0 for _ in shape))

    bw = jnp.concatenate([P["rwkv_bw"][l, 0], P["rwkv_bw"][l, 1]], axis=1).astype(BF16)
    ba = jnp.concatenate([P["rwkv_ba"][l, 0], P["rwkv_ba"][l, 1]], axis=1).astype(BF16)
    out = jax.ShapeDtypeStruct((ntiles * TILE, D_RWKV), F32)
    tok_spec = pl.BlockSpec((TILE, D_RWKV), lambda i: (i, 0))
    if seq_block is None:
        scan_spec, scan_out = tok_spec, out
    else:
        bl, ts = seq_block
        scan_out = jax.ShapeDtypeStruct((ntiles * TILE // bl, bl * D_RWKV), F32)
        scan_spec = pl.BlockSpec((TILE, D_RWKV),
                                 lambda i: ((i // ts // bl) * ts + i % ts, (i // ts) % bl))
    out_specs = [scan_spec] * 9 + [tok_spec] * 2
    out_shape = [scan_out] * 9 + [out] * 2
    if lane_tiled is not None:
        nlt = D_RWKV // LANES
        nseq = ntiles // lane_tiled
        k_spec = pl.BlockSpec((TILE, nlt, 2 * LANES), lambda i: (i % lane_tiled, 0, i // lane_tiled))
        k_out = jax.ShapeDtypeStruct((lane_tiled * TILE, nlt, nseq * 2 * LANES), F32)
        out_specs = [k_spec, k_spec, tok_spec, k_spec, k_spec, k_spec, tok_spec, tok_spec]
        out_shape = [k_out, k_out, out, k_out, k_out, k_out, out, out]
    return pl.pallas_call(
        functools.partial(_rwkv_pre_kernel, geo=geo, tile0=tile0),
        grid=(ntiles,),
        in_specs=trio(C_R) + trio(C_K) + trio(C_V) + [
            pl.BlockSpec((TILE, 256), lambda i: (i + tile0, C_LORA // 256)),
            full((3, D_RWKV)), full((2, D_RWKV)), full((W_RANK, 2 * D_RWKV)),
            full((2, D_RWKV)), full((A_RANK, 2 * D_RWKV)), full((G_RANK, D_RWKV)),
            full((1, D_RWKV)), full((1, D_RWKV)), full((1, D_RWKV)), full((D_RWKV, D_RWKV)),
        ],
        out_specs=out_specs,
        out_shape=out_shape,
        compiler_params=_cparams(("arbitrary",)),
        name="rwkv_pre",
    )(u, u, u, u, u, u, u, u, u, u,
      P["rwkv_mu"][l], P["rwkv_w0"][l], bw, P["rwkv_a0"][l], ba, P["rwkv_bg"][l].astype(BF16),
      P["rwkv_xi"][l].reshape(1, -1), P["rwkv_alpha"][l].reshape(1, -1),
      P["rwkv_rho"][l].reshape(1, -1), ones_bd)


N_VBLK = RWKV_HEAD // SUBLANES


def _scan_steps(xrow, vload, ostore, s_ref, tidx, *, tc, nk, kl, ngroups=2, unroll=1):
    gsz = N_VBLK // ngroups
    groups = tuple(tuple(range(g * gsz, (g + 1) * gsz)) for g in range(ngroups))

    def allred(a):
        if kl == 1:
            return a
        q = lax.broadcasted_iota(jnp.int32, a.shape, 1) % kl
        out = a
        for j in range(1, kl):
            out = out + jnp.where(q >= j, pltpu.roll(a, j, axis=1), 0.0)
            out = out + jnp.where(q < kl - j, pltpu.roll(a, LANES - j, axis=1), 0.0)
        return out

    def first_sa(grp, t):
        acc = [None] * len(grp)
        for kh in range(nk):
            kkb = xrow(t, 1, kh)
            for j, vb in enumerate(grp):
                pr = s_ref[vb, kh] * kkb
                acc[j] = pr if acc[j] is None else acc[j] + pr
        return tuple(allred(a) for a in acc)

    def fused(grp, s, t, t_next, sa):
        vv = [vload(t, vb) for vb in grp]
        oacc = [None] * len(grp)
        acc = [None] * len(grp)
        for kh in range(nk):
            wb, kab, ktb, rb = xrow(t, 0, kh), xrow(t, 2, kh), xrow(t, 3, kh), xrow(t, 4, kh)
            kkn = xrow(t_next, 1, kh)
            for j, vb in enumerate(grp):
                sn = s_ref[vb, kh] * wb - sa[j] * kab + vv[j] * ktb
                s_ref[vb, kh] = sn
                po = sn * rb
                pa = sn * kkn
                oacc[j] = po if oacc[j] is None else oacc[j] + po
                acc[j] = pa if acc[j] is None else acc[j] + pa
        for j, vb in enumerate(grp):
            ostore(s, t, vb, oacc[j])
        return tuple(allred(a) for a in acc)

    t0 = tidx(0)
    carry0 = tuple(first_sa(g, t0) for g in groups)

    def step(s, carry):
        t = tidx(s)
        t_next = tidx(jnp.minimum(s + 1, tc - 1))
        return tuple(fused(g, s, t, t_next, carry[i]) for i, g in enumerate(groups))

    lax.fori_loop(0, tc, step, carry0, unroll=unroll)


def _vrows(vb):
    return pl.ds(vb * SUBLANES, SUBLANES)


def _scan_ctx_kernel(w_ref, kk_ref, ka_ref, kt_ref, r_ref, v_ref, o_ref, sfin_ref, s_ref, *, tc, nk, rev):
    c = pl.program_id(1)
    x_refs = (w_ref, kk_ref, ka_ref, kt_ref, r_ref)

    @pl.when(c == 0)
    def _():
        s_ref[...] = jnp.zeros(s_ref.shape, F32)

    def xrow(t, a, kh):
        return jnp.broadcast_to(x_refs[a][0, t, pl.ds(kh, 1), :], (SUBLANES, LANES))

    def vload(t, vb):
        return v_ref[0, t, _vrows(vb), :]

    def ostore(s, t, vb, val):
        o_ref[0, t, _vrows(vb), :] = val

    def tidx(s):
        return tc - 1 - s if rev else s

    _scan_steps(xrow, vload, ostore, s_ref, tidx, tc=tc, nk=nk, kl=1)

    @pl.when(c == pl.num_programs(1) - 1)
    def _():
        sfin_ref[0] = s_ref[...]


def wkv_scan_ctx(w, kk, ka, kt, r, v, *, tc, rev):
    ngb, t_len, nk, _ = w.shape
    nch = t_len // tc
    blk = pl.BlockSpec((1, tc, nk, LANES), lambda g, c: (g, nch - 1 - c if rev else c, 0, 0))
    st = pl.BlockSpec((1, N_VBLK, nk, SUBLANES, LANES), lambda g, c: (g, 0, 0, 0, 0))
    return pl.pallas_call(
        functools.partial(_scan_ctx_kernel, tc=tc, nk=nk, rev=rev),
        grid=(ngb, nch),
        in_specs=[blk] * 6,
        out_specs=[blk, st],
        out_shape=[
            jax.ShapeDtypeStruct((ngb, t_len, RWKV_HEAD, LANES), F32),
            jax.ShapeDtypeStruct((ngb, N_VBLK, nk, SUBLANES, LANES), F32),
        ],
        scratch_shapes=[pltpu.VMEM((N_VBLK, nk, SUBLANES, LANES), F32)],
        compiler_params=_cparams(("arbitrary", "arbitrary")),
        name="wkv_scan_ctx",
    )(w, kk, ka, kt, r, v)


def _scan_lat_kernel(*refs, tc, nk, kl):
    xa_refs, xb_refs = refs[0:5], refs[5:10]
    va_ref, vb_ref, s0_ref, oa_ref, ob_ref, s_ref, xm_ref, vm_ref = refs[10:]
    c = pl.program_id(0)

    @pl.when(c == 0)
    def _():
        s_ref[...] = s0_ref[...]

    def is_bwd(shape):
        lane = lax.broadcasted_iota(jnp.int32, shape, len(shape) - 1)
        return (lane // N_RWKV_HEADS) % 2 == 1

    mx = is_bwd((nk, LANES))
    mv = is_bwd((RWKV_HEAD, LANES))

    def merge(s, carry):
        for a in range(5):
            xm_ref[a, s] = jnp.where(mx, xb_refs[a][tc - 1 - s], xa_refs[a][s])
        vm_ref[s] = jnp.where(mv, vb_ref[tc - 1 - s], va_ref[s])
        return carry

    lax.fori_loop(0, tc, merge, 0)

    def xrow(t, a, kh):
        return jnp.broadcast_to(xm_ref[a, t, pl.ds(kh, 1), :], (SUBLANES, LANES))

    def vload(t, vb):
        return vm_ref[t, _vrows(vb), :]

    def ostore(s, t, vb, val):
        oa_ref[s, _vrows(vb), :] = val
        ob_ref[tc - 1 - s, _vrows(vb), :] = val

    _scan_steps(xrow, vload, ostore, s_ref, lambda s: s, tc=tc, nk=nk, kl=kl, ngroups=2, unroll=2)


def wkv_scan_lat(xs, v, s0, *, tc, kl):
    t_len, nk, _ = xs[0].shape
    nch = t_len // tc
    o_sds = jax.ShapeDtypeStruct((t_len, RWKV_HEAD, LANES), F32)
    xa = pl.BlockSpec((tc, nk, LANES), lambda c: (c, 0, 0))
    xb = pl.BlockSpec((tc, nk, LANES), lambda c: (nch - 1 - c, 0, 0))
    va = pl.BlockSpec((tc, RWKV_HEAD, LANES), lambda c: (c, 0, 0))
    vb = pl.BlockSpec((tc, RWKV_HEAD, LANES), lambda c: (nch - 1 - c, 0, 0))
    return pl.pallas_call(
        functools.partial(_scan_lat_kernel, tc=tc, nk=nk, kl=kl),
        grid=(nch,),
        in_specs=[xa] * 5 + [xb] * 5 + [va, vb,
                                        pl.BlockSpec((N_VBLK, nk, SUBLANES, LANES), lambda c: (0, 0, 0, 0))],
        out_specs=[va, vb],
        out_shape=[o_sds, o_sds],
        scratch_shapes=[pltpu.VMEM((N_VBLK, nk, SUBLANES, LANES), F32),
                        pltpu.VMEM((5, tc, nk, LANES), F32),
                        pltpu.VMEM((tc, RWKV_HEAD, LANES), F32)],
        compiler_params=_cparams(("arbitrary",)),
        name="wkv_scan_lat",
    )(*xs, *xs, v, v, s0)


def _rwkv_post_kernel(of_ref, ob_ref, bonus_ref, g_ref, gate_ref, gng_ref, gnb_ref, ones_ref,
                      wo_ref, *rest):
    y_ref = rest[-1]
    if len(of_ref.shape) == 2:
        o = of_ref[...] + ob_ref[...]
    else:
        o = jnp.concatenate([of_ref[0, :, h, :] + ob_ref[0, :, h, :] for h in range(N_RWKV_HEADS)], axis=1)
    ones_bd = ones_ref[...]
    mean = _seg_sum(o, ones_bd) * (1.0 / RWKV_HEAD)
    oc = o - mean
    var = _seg_sum(oc * oc, ones_bd) * (1.0 / RWKV_HEAD)
    gn = oc * lax.rsqrt(var + GN_EPS) * gng_ref[...] + gnb_ref[...]
    y = (gn + bonus_ref[...]) * g_ref[...]
    y = jnp.dot(y.astype(BF16), wo_ref[...], preferred_element_type=F32)
    y_ref[...] = _sigmoid(gate_ref[...]) * y


def rwkv_post(o_f, o_b, bonus, g, u, gng, gnb, ones_bd, wo, y_prev, tile0, seq_block=None):
    ntiles = bonus.shape[0] // TILE

    def full(shape):
        return pl.BlockSpec(shape, lambda i: tuple(0 for _ in shape))

    part = pl.BlockSpec((TILE, D_RWKV), lambda i: (i, 0))
    if o_f.ndim == 4:
        ts = o_f.shape[1] // TILE
        opart = pl.BlockSpec((1, TILE) + o_f.shape[2:], lambda i: (i // ts, i % ts, 0, 0))
    elif seq_block is None:
        opart = part
    else:
        bl, ts = seq_block
        opart = pl.BlockSpec((TILE, D_RWKV), lambda i: ((i // ts // bl) * ts + i % ts, (i // ts) % bl))
    in_specs = [opart, opart, part, part,
                pl.BlockSpec((TILE, D_MODEL), lambda i: (i + tile0, C_G1 // D_MODEL)),
                full((1, D_RWKV)), full((1, D_RWKV)), full((D_RWKV, D_RWKV)), full((D_RWKV, D_MODEL))]
    args = [o_f, o_b, bonus, g, u, gng.reshape(1, -1), gnb.reshape(1, -1), ones_bd, wo.astype(BF16)]
    aliases = {}
    if y_prev is not None:
        in_specs.append(pl.BlockSpec(memory_space=pl.ANY))
        args.append(y_prev)
        aliases = {len(args) - 1: 0}
    return pl.pallas_call(
        _rwkv_post_kernel,
        grid=(ntiles,),
        in_specs=in_specs,
        out_specs=pl.BlockSpec((TILE, D_MODEL), lambda i: (i + tile0, 0)),
        out_shape=jax.ShapeDtypeStruct((u.shape[0], D_MODEL), F32),
        input_output_aliases=aliases,
        compiler_params=_cparams(("arbitrary",)),
        name="rwkv_post",
    )(*args)


QK_W = 2 * LANES
Q_SCALE = ATTN_SCALE * math.log2(math.e)


def _mla_pre_kernel(cq_ref, ckv_ref, kr_ref, cos_ref, sin_ref, qg_ref, kvg_ref, wuq_ref, wukt_ref,
                    qf_o, ckvn_o, krr_o):
    nq = N_MLA_HEADS * NOPE_DIM
    nrp = N_MLA_HEADS * LANES
    cq = _rms(cq_ref[...], qg_ref[...])
    q = jnp.dot(cq.astype(BF16), wuq_ref[...], preferred_element_type=F32)
    cos = cos_ref[...]
    sin = sin_ref[...]
    for h in range(N_MLA_HEADS):
        qn = q[:, h * NOPE_DIM:(h + 1) * NOPE_DIM].astype(BF16)
        qa = jnp.dot(qn, wukt_ref[h], preferred_element_type=F32)
        qr = (q[:, nq + h * LANES:nq + (h + 1) * LANES] * cos
              + q[:, nq + nrp + h * LANES:nq + nrp + (h + 1) * LANES] * sin)
        qf_o[:, h * QK_W:h * QK_W + LANES] = (qa * Q_SCALE).astype(qf_o.dtype)
        qf_o[:, h * QK_W + LANES:(h + 1) * QK_W] = (qr * Q_SCALE).astype(qf_o.dtype)
    ckvn_o[...] = _rms(ckv_ref[...], kvg_ref[...])
    kr = kr_ref[...]
    krr = kr[:, ROPE_DIM:2 * ROPE_DIM] * cos[:, :ROPE_DIM] + kr[:, 2 * ROPE_DIM:3 * ROPE_DIM] * sin[:, :ROPE_DIM]
    krr_o[...] = jnp.concatenate([krr, jnp.zeros((TILE, LANES - ROPE_DIM), F32)], axis=1)


def mla_pre(u, cos_t, sin_t, qg, kvg, wuq_p, wukt, geo):
    n = u.shape[0]

    def full(shape):
        return pl.BlockSpec(shape, lambda i: tuple(0 for _ in shape))

    return pl.pallas_call(
        _mla_pre_kernel,
        grid=(geo.ntile,),
        in_specs=[
            pl.BlockSpec((TILE, Q_RANK), lambda i: (i, C_CQ // Q_RANK)),
            pl.BlockSpec((TILE, KV_RANK), lambda i: (i, C_CKV // KV_RANK)),
            pl.BlockSpec((TILE, LANES), lambda i: (i, C_KR // LANES)),
            pl.BlockSpec((TILE, LANES), lambda i: (geo.rope_blk(i), 0)),
            pl.BlockSpec((TILE, LANES), lambda i: (geo.rope_blk(i), 0)),
            full((1, Q_RANK)), full((1, KV_RANK)),
            full(wuq_p.shape), full(wukt.shape),
        ],
        out_specs=[
            pl.BlockSpec((TILE, N_MLA_HEADS * QK_W), lambda i: (i, 0)),
            pl.BlockSpec((TILE, KV_RANK), lambda i: (i, 0)),
            pl.BlockSpec((TILE, LANES), lambda i: (i, 0)),
        ],
        out_shape=[
            jax.ShapeDtypeStruct((n, N_MLA_HEADS * QK_W), BF16),
            jax.ShapeDtypeStruct((n, KV_RANK), F32),
            jax.ShapeDtypeStruct((n, LANES), F32),
        ],
        compiler_params=_cparams(("arbitrary",)),
        name="mla_pre",
    )(u, u, u, cos_t, sin_t, qg.reshape(1, -1), kvg.reshape(1, -1), wuq_p, wukt)


ATT_KEY_CHUNK = 512
ATT_HEADS_PER_STEP = 4


def _attn_kernel(q_ref, kt_ref, c_ref, wuv_ref, wo_ref, gate_ref, *rest, ck):
    y_ref, pc_ref = rest[-2:]
    hg = pl.program_id(2)
    tq = q_ref.shape[0]
    t_k = kt_ref.shape[2]
    hp = ATT_HEADS_PER_STEP
    q = jnp.concatenate([q_ref[:, j * QK_W:(j + 1) * QK_W] for j in range(hp)], axis=0)
    m = jnp.full((hp * tq, 1), -jnp.inf, F32)
    l = jnp.zeros((hp * tq, 1), F32)
    acc = jnp.zeros((hp * tq, KV_RANK), F32)
    for c0 in range(0, t_k, ck):
        s = jnp.dot(q, kt_ref[0, :, c0:c0 + ck], preferred_element_type=F32)
        m_new = jnp.maximum(m, jnp.max(s, axis=-1, keepdims=True))
        alpha = jnp.exp2(m - m_new)
        p = jnp.exp2(s - m_new)
        l = alpha * l + jnp.sum(p, axis=-1, keepdims=True)
        acc = alpha * acc + jnp.dot(p.astype(BF16), c_ref[0, c0:c0 + ck, :], preferred_element_type=F32)
        m = m_new
    pc = (acc / l).astype(BF16)
    for j in range(hp):
        pc_ref[hg * hp + j] = pc[j * tq:(j + 1) * tq]

    @pl.when(hg == pl.num_programs(2) - 1)
    def _():
        pcs = jnp.concatenate([pc_ref[i] for i in range(N_MLA_HEADS)], axis=1)
        oh = jnp.dot(pcs, wuv_ref[...], preferred_element_type=F32)
        y = jnp.dot(oh.astype(BF16), wo_ref[...], preferred_element_type=F32)
        y_ref[...] = _sigmoid(gate_ref[...]) * y


def attention(qf, kct, cv, wuv_bd, wo, u, y_prev, *, row0, t_seq, tq, name):
    n = qf.shape[0]
    nseq, _, t_k = kct.shape
    qt = t_seq // tq
    rb0 = row0 // tq
    ck = math.gcd(ATT_KEY_CHUNK, t_k)

    def rows(w, col):
        return pl.BlockSpec((tq, w), lambda s, i, h: (rb0 + s * qt + i, col(h)))

    in_specs = [
        rows(ATT_HEADS_PER_STEP * QK_W, lambda h: h),
        pl.BlockSpec((1, QK_W, t_k), lambda s, i, h: (s, 0, 0)),
        pl.BlockSpec((1, t_k, KV_RANK), lambda s, i, h: (s, 0, 0)),
        pl.BlockSpec(wuv_bd.shape, lambda s, i, h: (0, 0)),
        pl.BlockSpec(wo.shape, lambda s, i, h: (0, 0)),
        rows(D_MODEL, lambda h: C_G2 // D_MODEL),
    ]
    args = [qf, kct, cv, wuv_bd, wo, u]
    aliases = {}
    if y_prev is not None:
        in_specs.append(pl.BlockSpec(memory_space=pl.ANY))
        args.append(y_prev)
        aliases = {len(args) - 1: 0}
    return pl.pallas_call(
        functools.partial(_attn_kernel, ck=ck),
        grid=(nseq, qt, N_MLA_HEADS // ATT_HEADS_PER_STEP),
        in_specs=in_specs,
        out_specs=rows(D_MODEL, lambda h: 0),
        out_shape=jax.ShapeDtypeStruct((n, D_MODEL), F32),
        scratch_shapes=[pltpu.VMEM((N_MLA_HEADS, tq, KV_RANK), BF16)],
        input_output_aliases=aliases,
        compiler_params=_cparams(("arbitrary", "arbitrary", "arbitrary")),
        name=name,
    )(*args)


R_I1, R_I2, R_G1, R_G2 = 0, 1, 2, 3


def _merge_kernel(x_ref, yc_ref, yr_ref, ym_ref, wout_ref, gpost_ref, g1_ref, gpre_ref, sc_ref,
                  sh_ref, router_ref, x1_o, h2_o, comb_o, *, moe):
    m = yc_ref[...] + yr_ref[...] + ym_ref[...]
    y = jnp.dot(m.astype(BF16), wout_ref[...], preferred_element_type=F32)
    x1 = x_ref[...] + g1_ref[0] * _rms(y, gpost_ref[...])
    x1_o[...] = x1
    h2 = _rms(x1, gpre_ref[...]) * (1.0 + sc_ref[0]) + sh_ref[0]
    h2_o[...] = h2.astype(h2_o.dtype)
    if moe:
        logits = jnp.dot(h2, router_ref[...], preferred_element_type=F32,
                         precision=lax.Precision.HIGHEST)
        lane = lax.broadcasted_iota(jnp.int32, logits.shape, 1)
        neg = jnp.float32(-jnp.inf)
        logits = jnp.where(lane < N_EXPERTS, logits, neg)
        m1 = jnp.max(logits, axis=-1, keepdims=True)
        i1 = jnp.min(jnp.where(logits == m1, lane, LANES), axis=-1, keepdims=True)
        rest = jnp.where(lane == i1, neg, logits)
        m2 = jnp.max(rest, axis=-1, keepdims=True)
        i2 = jnp.min(jnp.where(rest == m2, lane, LANES), axis=-1, keepdims=True)
        e2 = jnp.exp(m2 - m1)
        den = 1.0 + e2
        cols = ((R_I1, i1.astype(F32)), (R_I2, i2.astype(F32)), (R_G1, 1.0 / den), (R_G2, e2 / den))
        route = jnp.zeros(comb_o.shape, F32)
        for col, val in cols:
            route = jnp.where(lane == col, val, route)
        comb_o[...] = route
    else:
        comb_o[...] = jnp.ones(comb_o.shape, F32)


def merge(x, yc, yr, ym, wout, gpost, gpre, mod, router, geo, moe):
    n, d = x.shape

    def full(shape):
        return pl.BlockSpec(shape, lambda i: tuple(0 for _ in shape))

    def rows():
        return pl.BlockSpec((TILE, d), lambda i: (i, 0))

    def modspec(j):
        return pl.BlockSpec((1, 1, d), lambda i: (geo.mod_row(i) * 6 + j, 0, 0))

    router_p = jnp.zeros((d, LANES), F32).at[:, :N_EXPERTS].set(router)
    return pl.pallas_call(
        functools.partial(_merge_kernel, moe=moe),
        grid=(geo.ntile,),
        in_specs=[rows(), rows(), rows(), rows(), full((d, d)), full((1, d)), modspec(2),
                  full((1, d)), modspec(4), modspec(3), full((d, LANES))],
        out_specs=[rows(), rows(), pl.BlockSpec((TILE, LANES), lambda i: (i, 0))],
        out_shape=[jax.ShapeDtypeStruct((n, d), F32), jax.ShapeDtypeStruct((n, d), F32 if moe else BF16),
                   jax.ShapeDtypeStruct((n, LANES), F32)],
        compiler_params=_cparams(("arbitrary",)),
        name="merge",
    )(x, yc, yr, ym, wout.astype(BF16), gpost.reshape(1, d), mod, gpre.reshape(1, d), mod, mod,
      router_p)


def _ffn_kernel(h_ref, comb_ref, w1_ref, w3_ref, w2_ref, o_ref, acc_ref):
    e = pl.program_id(1)
    f = pl.program_id(2)

    @pl.when((e == 0) & (f == 0))
    def _():
        acc_ref[...] = jnp.zeros(acc_ref.shape, F32)

    h = h_ref[...]
    a = jnp.dot(h, w1_ref[0], preferred_element_type=F32)
    b = jnp.dot(h, w3_ref[0], preferred_element_type=F32)
    comb = comb_ref[...]
    lane = lax.broadcasted_iota(jnp.int32, comb.shape, 1)
    ce = jnp.sum(jnp.where(lane == e, comb, 0.0), axis=-1, keepdims=True)
    act = (a * _sigmoid(a) * b) * ce
    acc_ref[...] += jnp.dot(act.astype(BF16), w2_ref[0], preferred_element_type=F32)

    @pl.when((e == pl.num_programs(1) - 1) & (f == pl.num_programs(2) - 1))
    def _():
        o_ref[...] = acc_ref[...]


def ffn(h, comb, w1, w3, w2, *, tm, tf):
    n, d = h.shape
    ne, _, dff = w1.shape
    return pl.pallas_call(
        _ffn_kernel,
        grid=(n // tm, ne, dff // tf),
        in_specs=[
            pl.BlockSpec((tm, d), lambda i, e, f: (i, 0)),
            pl.BlockSpec((tm, LANES), lambda i, e, f: (i, 0)),
            pl.BlockSpec((1, d, tf), lambda i, e, f: (e, 0, f)),
            pl.BlockSpec((1, d, tf), lambda i, e, f: (e, 0, f)),
            pl.BlockSpec((1, tf, d), lambda i, e, f: (e, f, 0)),
        ],
        out_specs=pl.BlockSpec((tm, d), lambda i, e, f: (i, 0)),
        out_shape=jax.ShapeDtypeStruct((n, d), F32),
        scratch_shapes=[pltpu.VMEM((tm, d), F32)],
        compiler_params=_cparams(("arbitrary", "arbitrary", "arbitrary")),
        name="ffn",
    )(h, comb, w1, w3, w2)


MOE_TM = 512


def _row_copy(src_hbm, row, dst_vmem, r, sem):
    return pltpu.make_async_copy(src_hbm.at[pl.ds(row, 1)], dst_vmem.at[pl.ds(r, 1)], sem)


DMA_UNROLL = 8


def _ffn_sorted_kernel(te_ref, nused_ref, idx_ref, nxt_ref, h_hbm, w1_ref, w3_ref, w2_ref, o_ref,
                       acc_ref, xbuf_ref, sems):
    j = pl.program_id(0)
    f = pl.program_id(1)
    nj = pl.num_programs(0)
    tm = acc_ref.shape[0]
    slot = j % 2

    def issue(ids_ref, s):
        def body(i, c):
            for k in range(DMA_UNROLL):
                r = i * DMA_UNROLL + k
                _row_copy(h_hbm, ids_ref[0, 0, r], xbuf_ref.at[s], r, sems.at[s]).start(priority=k % 2)
            return c
        lax.fori_loop(0, tm // DMA_UNROLL, body, 0)

    def wait(s):
        def body(i, c):
            for k in range(DMA_UNROLL):
                _row_copy(h_hbm, 0, xbuf_ref.at[s], i * DMA_UNROLL + k, sems.at[s]).wait()
            return c
        lax.fori_loop(0, tm // DMA_UNROLL, body, 0)

    @pl.when(f == 0)
    def _():
        @pl.when(j == 0)
        def _():
            issue(idx_ref, 0)

        @pl.when(j + 1 < nj)
        def _():
            issue(nxt_ref, 1 - slot)

        wait(slot)

    @pl.when(j < nused_ref[0])
    def _():
        @pl.when(f == 0)
        def _():
            acc_ref[...] = jnp.zeros(acc_ref.shape, F32)

        h = xbuf_ref[slot].astype(BF16)
        a = jnp.dot(h, w1_ref[0], preferred_element_type=F32)
        b = jnp.dot(h, w3_ref[0], preferred_element_type=F32)
        act = a * _sigmoid(a) * b
        acc_ref[...] += jnp.dot(act.astype(BF16), w2_ref[0], preferred_element_type=F32)

        @pl.when(f == pl.num_programs(1) - 1)
        def _():
            o_ref[...] = acc_ref[...]


def ffn_sorted(h, row_src, te, nused, w1, w3, w2, *, tm, tf):
    p = row_src.shape[0]
    d = h.shape[1]
    dff = w1.shape[2]
    ntile = p // tm
    ids = row_src.reshape(ntile, 1, tm)
    return pl.pallas_call(
        _ffn_sorted_kernel,
        grid_spec=pltpu.PrefetchScalarGridSpec(
            num_scalar_prefetch=2,
            grid=(ntile, dff // tf),
            in_specs=[
                pl.BlockSpec((1, 1, tm), lambda j, f, te, nu: (j, 0, 0), memory_space=pltpu.SMEM),
                pl.BlockSpec((1, 1, tm), lambda j, f, te, nu: (jnp.minimum(j + 1, ntile - 1), 0, 0),
                             memory_space=pltpu.SMEM),
                pl.BlockSpec(memory_space=pl.ANY),
                pl.BlockSpec((1, d, tf), lambda j, f, te, nu: (te[j], 0, f)),
                pl.BlockSpec((1, d, tf), lambda j, f, te, nu: (te[j], 0, f)),
                pl.BlockSpec((1, tf, d), lambda j, f, te, nu: (te[j], f, 0)),
            ],
            out_specs=pl.BlockSpec((tm, d), lambda j, f, te, nu: (j, 0)),
            scratch_shapes=[pltpu.VMEM((tm, d), F32), pltpu.VMEM((2, tm, d), h.dtype),
                            pltpu.SemaphoreType.DMA((2,))],
        ),
        out_shape=jax.ShapeDtypeStruct((p, d), F32),
        compiler_params=_cparams(("arbitrary", "arbitrary")),
        name="ffn_sorted",
    )(te, nused, ids, ids, h, w1, w3, w2)


def _moe_combine_kernel(d0_ref, d1_ref, ys_hbm, route_ref, x_ref, g_ref, g2_ref, o_ref,
                        y0_ref, y1_ref, sem0, sem1):
    def start(i, c):
        for k in range(DMA_UNROLL):
            r = i * DMA_UNROLL + k
            _row_copy(ys_hbm, d0_ref[0, 0, r], y0_ref, r, sem0).start(priority=0)
            _row_copy(ys_hbm, d1_ref[0, 0, r], y1_ref, r, sem1).start(priority=1)
        return c

    def wait(i, c):
        for k in range(DMA_UNROLL):
            r = i * DMA_UNROLL + k
            _row_copy(ys_hbm, 0, y0_ref, r, sem0).wait()
            _row_copy(ys_hbm, 0, y1_ref, r, sem1).wait()
        return c

    lax.fori_loop(0, TILE // DMA_UNROLL, start, 0)
    lax.fori_loop(0, TILE // DMA_UNROLL, wait, 0)
    route = route_ref[...]
    lane = lax.broadcasted_iota(jnp.int32, route.shape, 1)
    ga = jnp.sum(jnp.where(lane == R_G1, route, 0.0), axis=-1, keepdims=True)
    gb = jnp.sum(jnp.where(lane == R_G2, route, 0.0), axis=-1, keepdims=True)
    y = ga * y0_ref[...] + gb * y1_ref[...]
    o_ref[...] = x_ref[...] + g2_ref[0] * _rms(y, g_ref[...])


def moe_combine_post(x1, ys, dest, route, gpost, mod, geo):
    n, d = x1.shape
    dd = dest.reshape(2, n // TILE, 1, TILE)
    return pl.pallas_call(
        _moe_combine_kernel,
        grid=(geo.ntile,),
        in_specs=[
            pl.BlockSpec((1, 1, TILE), lambda i: (i, 0, 0), memory_space=pltpu.SMEM),
            pl.BlockSpec((1, 1, TILE), lambda i: (i, 0, 0), memory_space=pltpu.SMEM),
            pl.BlockSpec(memory_space=pl.ANY),
            pl.BlockSpec((TILE, LANES), lambda i: (i, 0)),
            pl.BlockSpec((TILE, d), lambda i: (i, 0)),
            pl.BlockSpec((1, d), lambda i: (0, 0)),
            pl.BlockSpec((1, 1, d), lambda i: (geo.mod_row(i) * 6 + 5, 0, 0)),
        ],
        out_specs=pl.BlockSpec((TILE, d), lambda i: (i, 0)),
        out_shape=jax.ShapeDtypeStruct((n, d), F32),
        scratch_shapes=[pltpu.VMEM((TILE, d), F32), pltpu.VMEM((TILE, d), F32),
                        pltpu.SemaphoreType.DMA(()), pltpu.SemaphoreType.DMA(())],
        compiler_params=_cparams(("arbitrary",)),
        name="moe_combine",
    )(dd[0], dd[1], ys, route, x1, gpost.reshape(1, d), mod)


def _route_plan(route, tm):
    n = route.shape[0]
    e_flat = jnp.concatenate([route[:, R_I1], route[:, R_I2]]).astype(jnp.int32)
    onehot = (e_flat[:, None] == jnp.arange(N_EXPERTS, dtype=jnp.int32)[None, :]).astype(jnp.int32)
    csum = jnp.cumsum(onehot, axis=0)
    rank = jnp.take_along_axis(csum, e_flat[:, None], axis=1)[:, 0] - 1
    counts = csum[-1]
    ptiles = (counts + tm - 1) // tm
    tile_end = jnp.cumsum(ptiles)
    gstart = (tile_end - ptiles) * tm
    dest = jnp.take(gstart, e_flat) + rank
    p_max = 2 * n + N_EXPERTS * tm
    tok = jnp.tile(jnp.arange(n, dtype=jnp.int32), 2)
    row_src = jnp.zeros((p_max,), jnp.int32).at[dest].set(tok)
    tiles = jnp.arange(p_max // tm, dtype=jnp.int32)
    te = jnp.minimum(jnp.searchsorted(tile_end, tiles, side="right"), N_EXPERTS - 1).astype(jnp.int32)
    return row_src, dest.astype(jnp.int32).reshape(2, n), te, tile_end[-1:].astype(jnp.int32)


def _ffn_post_kernel(x_ref, y_ref, g_ref, g2_ref, o_ref):
    o_ref[...] = x_ref[...] + g2_ref[0] * _rms(y_ref[...], g_ref[...])


def ffn_post(x1, y, gpost, mod, geo):
    n, d = x1.shape
    return pl.pallas_call(
        _ffn_post_kernel,
        grid=(geo.ntile,),
        in_specs=[
            pl.BlockSpec((TILE, d), lambda i: (i, 0)),
            pl.BlockSpec((TILE, d), lambda i: (i, 0)),
            pl.BlockSpec((1, d), lambda i: (0, 0)),
            pl.BlockSpec((1, 1, d), lambda i: (geo.mod_row(i) * 6 + 5, 0, 0)),
        ],
        out_specs=pl.BlockSpec((TILE, d), lambda i: (i, 0)),
        out_shape=jax.ShapeDtypeStruct((n, d), F32),
        compiler_params=_cparams(("arbitrary",)),
        name="ffn_post",
    )(x1, y, gpost.reshape(1, d), mod)


def _ctx_scan(geo, pre):
    r, kk, v, w0, w1, ka0, ka1, kt0, kt1 = pre
    H, K = N_RWKV_HEADS, RWKV_HEAD
    bl = LANES // H
    nbh = geo.n_ctx // bl

    def to_scan(a):
        return jnp.swapaxes(a.reshape(nbh, geo.t_ctx, LANES, K), 2, 3)

    def from_scan(o):
        return jnp.swapaxes(o, 2, 3).reshape(nbh * geo.t_ctx, bl * H * K)

    def state(sf):
        sf = sf.reshape(nbh, K // SUBLANES, K, SUBLANES, bl, H).transpose(0, 4, 5, 1, 3, 2)
        return sf.reshape(geo.n_ctx, H, K, K)

    kk_s, r_s, v_s = to_scan(kk), to_scan(r), to_scan(v)
    tc = min(32, geo.t_ctx)
    o_f, sf_f = wkv_scan_ctx(to_scan(w0), kk_s, to_scan(ka0), to_scan(kt0), r_s, v_s, tc=tc, rev=False)
    o_b, sf_b = wkv_scan_ctx(to_scan(w1), kk_s, to_scan(ka1), to_scan(kt1), r_s, v_s, tc=tc, rev=True)
    return from_scan(o_f), from_scan(o_b), jnp.stack([state(sf_f), state(sf_b)], axis=1)


def _lat_scan(geo, pre, state_l):
    r, kk, v, w, ka, kt = pre
    H, K = N_RWKV_HEADS, RWKV_HEAD
    nb, t_len = geo.n_lat, geo.t_lat
    hp = H // 2
    kl = LANES // (nb * 2 * H)
    nk = K // kl

    def to_scan(a):
        return jnp.swapaxes(a.reshape(t_len, LANES, nk), 1, 2)

    xs = (to_scan(w), to_scan(kk), to_scan(ka), to_scan(kt), to_scan(r))
    vl = v.reshape(nb, t_len, hp, 1, 2, 1, K)
    vl = jnp.broadcast_to(vl, (nb, t_len, hp, 2, 2, kl, K))
    v_lat = vl.transpose(1, 6, 2, 0, 3, 4, 5).reshape(t_len, K, LANES)
    s0 = state_l.reshape(nb, 2, hp, 2, K // SUBLANES, SUBLANES, kl, nk)
    s0 = s0.transpose(4, 7, 5, 2, 0, 1, 3, 6).reshape(K // SUBLANES, nk, SUBLANES, LANES)
    oa, ob = wkv_scan_lat(xs, v_lat, s0, tc=min(64, t_len), kl=kl)

    def direction(o, d):
        o = o.reshape(t_len, K, hp, nb, 2, 2, kl)[:, :, :, :, d].sum(axis=-1)
        return o.transpose(3, 0, 2, 4, 1).reshape(nb, t_len, H, K)

    return direction(oa, 0), direction(ob, 1)


def _rope_tables(geo):
    n_freq = ROPE_DIM // 4
    rows = geo.t_lat // GRID_W
    row = jnp.repeat(jnp.arange(rows, dtype=F32), GRID_W)
    col = jnp.tile(jnp.arange(GRID_W, dtype=F32), rows)
    inv = ROPE_BASE ** (-jnp.arange(n_freq, dtype=F32) / n_freq)
    ang = jnp.concatenate([row[:, None] * inv, col[:, None] * inv], axis=-1)
    cos, sin = jnp.cos(ang), jnp.sin(ang)
    cos32 = jnp.concatenate([cos, cos], axis=-1)
    sin32 = jnp.concatenate([-sin, sin], axis=-1)
    cos_t = jnp.concatenate([jnp.ones((TILE, ROPE_DIM), F32), cos32], axis=0)
    sin_t = jnp.concatenate([jnp.zeros((TILE, ROPE_DIM), F32), sin32], axis=0)
    pad = ((0, 0), (0, LANES - ROPE_DIM))
    return jnp.pad(cos_t, pad), jnp.pad(sin_t, pad)


_DEINT = np.concatenate([np.arange(0, ROPE_DIM, 2), np.arange(1, ROPE_DIM, 2)])
_DEINT_SW = np.concatenate([np.arange(1, ROPE_DIM, 2), np.arange(0, ROPE_DIM, 2)])


def _prep_w_in(w_in, b_in):
    offs = np.cumsum([0, 2 * D_CONV, D_RWKV, D_RWKV, D_RWKV, W_RANK, A_RANK, G_RANK, Q_RANK, KV_RANK,
                      ROPE_DIM, N_BRANCH * D_MODEL])
    o_conv, o_r, o_k, o_v, o_dw, o_da, o_dg, o_cq, o_ckv, o_kr, o_gate, _ = offs
    idx = np.zeros((IN_PAD,), np.int32)
    valid = np.zeros((IN_PAD,), bool)

    def put(dst, src):
        idx[dst:dst + len(src)] = src
        valid[dst:dst + len(src)] = True

    put(C_CONV, np.arange(o_conv, o_conv + 2 * D_CONV))
    for b, c in enumerate((C_G0, C_G1, C_G2)):
        put(c, np.arange(o_gate + b * D_MODEL, o_gate + (b + 1) * D_MODEL))
    put(C_R, np.arange(o_r, o_r + D_RWKV))
    put(C_K, np.arange(o_k, o_k + D_RWKV))
    put(C_V, np.arange(o_v, o_v + D_RWKV))
    put(C_LORA, np.arange(o_dw, o_dw + W_RANK + A_RANK + G_RANK))
    put(C_CQ, np.arange(o_cq, o_cq + Q_RANK))
    put(C_CKV, np.arange(o_ckv, o_ckv + KV_RANK))
    put(C_KR, np.arange(o_kr, o_kr + ROPE_DIM))
    put(C_KR + ROPE_DIM, o_kr + _DEINT)
    put(C_KR + 2 * ROPE_DIM, o_kr + _DEINT_SW)
    w = jnp.where(valid[None, :], jnp.take(w_in, idx, axis=1), 0.0).astype(BF16)
    b = jnp.where(valid, jnp.take(b_in, idx), 0.0)
    return w, b


def _prep_wuq(wuq):
    hd = NOPE_DIM + ROPE_DIM
    nq = N_MLA_HEADS * NOPE_DIM
    ncol = nq + 2 * N_MLA_HEADS * LANES
    idx = np.zeros((ncol,), np.int32)
    valid = np.zeros((ncol,), bool)
    for h in range(N_MLA_HEADS):
        idx[h * NOPE_DIM:(h + 1) * NOPE_DIM] = h * hd + np.arange(NOPE_DIM)
        valid[h * NOPE_DIM:(h + 1) * NOPE_DIM] = True
        for blk, perm in enumerate((_DEINT, _DEINT_SW)):
            c0 = nq + blk * N_MLA_HEADS * LANES + h * LANES
            idx[c0:c0 + ROPE_DIM] = h * hd + NOPE_DIM + perm
            valid[c0:c0 + ROPE_DIM] = True
    return jnp.where(valid[None, :], jnp.take(wuq, idx, axis=1), 0.0).astype(BF16)


def _block_diag_heads(wuv):
    w3 = wuv.reshape(KV_RANK, N_MLA_HEADS, V_DIM)
    eye = jnp.eye(N_MLA_HEADS, dtype=wuv.dtype)
    bd = w3.transpose(1, 0, 2)[:, :, None, :] * eye[:, None, :, None]
    return bd.reshape(N_MLA_HEADS * KV_RANK, N_MLA_HEADS * V_DIM).astype(BF16)


def _attention_keys(geo, ckvn, krr, cache_c, cache_kr):
    nc = geo.nc_tok
    pad = QK_W - KV_RANK - ROPE_DIM

    def keys(c, kr):
        kc = jnp.concatenate([c, kr, jnp.zeros(c.shape[:2] + (pad,), F32)], axis=-1).astype(BF16)
        return kc.transpose(0, 2, 1), c.astype(BF16)

    kr32 = krr[:, :ROPE_DIM]
    kct_c, cv_c = keys(ckvn[:nc].reshape(geo.n_ctx, geo.t_ctx, KV_RANK),
                       kr32[:nc].reshape(geo.n_ctx, geo.t_ctx, ROPE_DIM))
    c_l = jnp.concatenate([ckvn[nc:].reshape(geo.n_lat, geo.t_lat, KV_RANK), cache_c], axis=1)
    kr_l = jnp.concatenate([kr32[nc:].reshape(geo.n_lat, geo.t_lat, ROPE_DIM), cache_kr], axis=1)
    kct_l, cv_l = keys(c_l, kr_l)
    return kct_c, cv_c, kct_l, cv_l


def kernel(x_prompt, x_sample, cache_ckv, cache_krope, state_wkv, c, c_ctx, ada_w, ada_b, norm_mix_pre, norm_mix_post, norm_ffn_pre, norm_ffn_post, w_in, b_in, conv_w, conv_b, conv_ln_g, conv_ln_b, conv_wo, rwkv_mu, rwkv_w0, rwkv_bw, rwkv_a0, rwkv_ba, rwkv_bg, rwkv_xi, rwkv_alpha, rwkv_rho, rwkv_gn_g, rwkv_gn_b, rwkv_wo, mla_q_norm, mla_wuq, mla_kv_norm, mla_wuk, mla_wuv, mla_wo, w_out, ffn_w1, ffn_w3, ffn_w2, moe_router, moe_w1, moe_w3, moe_w2):
    P = dict(rwkv_mu=rwkv_mu, rwkv_w0=rwkv_w0, rwkv_bw=rwkv_bw, rwkv_a0=rwkv_a0, rwkv_ba=rwkv_ba,
             rwkv_bg=rwkv_bg, rwkv_xi=rwkv_xi, rwkv_alpha=rwkv_alpha, rwkv_rho=rwkv_rho)
    n_ctx, t_ctx, d = x_prompt.shape
    n_lat, t_lat, _ = x_sample.shape
    depth = ada_w.shape[0]
    past = cache_ckv.shape[2]
    geo = Geo(n_ctx, t_ctx, n_lat, t_lat, past)
    assert d == D_MODEL and (n_ctx * N_RWKV_HEADS) % LANES == 0 and LANES % (n_lat * 2 * N_RWKV_HEADS) == 0

    x = jnp.concatenate([x_prompt.reshape(-1, d), x_sample.reshape(-1, d)], axis=0)

    n_mod = 1 + n_lat
    n_mod_pad = -(-n_mod // SUBLANES) * SUBLANES
    c_all = jnp.zeros((n_mod_pad, d), F32).at[0].set(c_ctx).at[1:n_mod].set(c)
    head_id = np.arange(D_RWKV) // RWKV_HEAD
    ones_bd = jnp.asarray((head_id[:, None] == head_id[None, :]).astype(np.float32))
    cos_t, sin_t = _rope_tables(geo)

    ckv_out, kr_out, st_out = [], [], []
    for l in range(depth):
        mod = matmul_bias(c_all, ada_w[l].astype(BF16), ada_b[l], tm=n_mod_pad, tn=6 * d // 4,
                          pre="silu", name="ada_mod")
        mod = mod[:n_mod].reshape(n_mod * 6, 1, d)

        h = prenorm(x, norm_mix_pre[l], mod, geo, j_shift=0, j_scale=1)
        w_in_p, b_in_p = _prep_w_in(w_in[l], b_in[l])
        u = matmul_bias(h, w_in_p, b_in_p, tm=math.gcd(1024, geo.ntok), tn=1280, name="in_proj")

        y_conv = conv_branch(u, conv_w[l], conv_b[l], conv_ln_g[l], conv_ln_b[l], conv_wo[l], geo)

        ctx_block = (LANES // N_RWKV_HEADS, geo.ct)
        pre_c = rwkv_pre(u, P, l, geo, ones_bd, 0, geo.nct, seq_block=ctx_block)
        pre_l = rwkv_pre(u, P, l, geo, ones_bd, geo.nct, geo.nlt, lane_tiled=geo.lt)
        oc_f, oc_b, sfin = _ctx_scan(geo, pre_c[:9])
        ol_f, ol_b = _lat_scan(geo, pre_l[:6], state_wkv[:, l])
        y_rwkv = rwkv_post(ol_f, ol_b, pre_l[7], pre_l[6], u, rwkv_gn_g[l], rwkv_gn_b[l], ones_bd,
                           rwkv_wo[l], None, geo.nct)
        y_rwkv = rwkv_post(oc_f, oc_b, pre_c[10], pre_c[9], u, rwkv_gn_g[l], rwkv_gn_b[l], ones_bd,
                           rwkv_wo[l], y_rwkv, 0, seq_block=ctx_block)

        wukt = mla_wuk[l].reshape(KV_RANK, N_MLA_HEADS, NOPE_DIM).transpose(1, 2, 0).astype(BF16)
        qf, ckvn, krr = mla_pre(u, cos_t, sin_t, mla_q_norm[l], mla_kv_norm[l],
                                _prep_wuq(mla_wuq[l]), wukt, geo)
        kct_c, cv_c, kct_l, cv_l = _attention_keys(geo, ckvn, krr, cache_ckv[:, l],
                                                   cache_krope[:, l][..., _DEINT])
        wuv_bd = _block_diag_heads(mla_wuv[l])
        wo_b = mla_wo[l].astype(BF16)
        y_mla = attention(qf, kct_l, cv_l, wuv_bd, wo_b, u, None, row0=geo.nc_tok, t_seq=t_lat,
                          tq=TILE, name="attention_lat")
        y_mla = attention(qf, kct_c, cv_c, wuv_bd, wo_b, u, y_mla, row0=0, t_seq=t_ctx,
                          tq=TILE, name="attention_ctx")

        moe = (l % 2 == 1)
        i = l // 2
        router = moe_router[i] if moe else jnp.zeros((d, N_EXPERTS), F32)
        x1, h2, comb = merge(x, y_conv, y_rwkv, y_mla, w_out[l], norm_mix_post[l], norm_ffn_pre[l],
                             mod, router, geo, moe)
        if moe:
            row_src, dest, te, nused = _route_plan(comb, MOE_TM)
            ys = ffn_sorted(h2, row_src, te, nused, moe_w1[i].astype(BF16), moe_w3[i].astype(BF16),
                            moe_w2[i].astype(BF16), tm=MOE_TM, tf=D_FF // 2)
            x = moe_combine_post(x1, ys, dest, comb, norm_ffn_post[l], mod, geo)
        else:
            y = ffn(h2, comb, ffn_w1[i:i + 1].astype(BF16), ffn_w3[i:i + 1].astype(BF16),
                    ffn_w2[i:i + 1].astype(BF16), tm=512, tf=D_FF // 2)
            x = ffn_post(x1, y, norm_ffn_post[l], mod, geo)

        ckv_out.append(ckvn[:geo.nc_tok].reshape(n_ctx, t_ctx, KV_RANK))
        kr_out.append(u[:geo.nc_tok, C_KR:C_KR + ROPE_DIM].reshape(n_ctx, t_ctx, ROPE_DIM))
        st_out.append(sfin)

    y_prompt = x[:geo.nc_tok].reshape(n_ctx, t_ctx, d)
    y_sample = x[geo.nc_tok:].reshape(n_lat, t_lat, d)
    return (y_prompt, y_sample, jnp.stack(ckv_out, axis=1), jnp.stack(kr_out, axis=1),
            jnp.stack(st_out, axis=1))
```
